```python
import math
import jax
import jax.numpy as jnp
from jax import lax
import numpy as np

D_MODEL = 1024
BATCH = 8
SEQ = 16384
DEPTH = 4

N_A_LAYERS = DEPTH // 2
N_B_LAYERS = DEPTH - N_A_LAYERS
EPS = 1e-6

GDN_HEADS = 6
GDN_DK = 128
GDN_DV = 128
GDN_QK_W = GDN_HEADS * GDN_DK
GDN_V_W = GDN_HEADS * GDN_DV
CONV_K = 4
CHUNK = 64

SWA_HEADS = 12
SWA_KV_HEADS = 2
SWA_DH = 64
SWA_GROUP = SWA_HEADS // SWA_KV_HEADS
SWA_Q_W = SWA_HEADS * SWA_DH
KV_W = SWA_KV_HEADS * SWA_DH
WINDOW = 128
SWA_BLOCK = 128
ROPE_THETA = 500000.0
ROT_DIM = SWA_DH // 4

MEM_LEN = 256
MEM_HEADS = 4
MEM_DH = 64
MEM_W = MEM_HEADS * MEM_DH

D_MIX = GDN_V_W + MEM_W
GDN_IN = 2 * GDN_QK_W + 2 * GDN_V_W + 2 * GDN_HEADS + MEM_W
SWA_IN = SWA_Q_W + MEM_W
D_FF = -(-8 * D_MODEL // (3 * 256)) * 256

kernel_name = "yoco_gdn_swa_sink_memory_trunk"


def rms_norm(x, g):
    xf = x.astype(jnp.float32)
    y = xf * lax.rsqrt(jnp.mean(xf * xf, axis=-1, keepdims=True) + EPS)
    return (y * g.astype(jnp.float32)).astype(x.dtype)


def l2_normalize(x):
    xf = x.astype(jnp.float32)
    return xf * lax.rsqrt(jnp.sum(xf * xf, axis=-1, keepdims=True) + EPS)


def rope_tables(positions):
    inv = ROPE_THETA ** (-jnp.arange(0, ROT_DIM, 2, dtype=jnp.float32) / ROT_DIM)
    ang = positions.astype(jnp.float32)[..., None] * inv
    return jnp.cos(ang), jnp.sin(ang)


def apply_partial_rope(x, cos, sin):
    half = ROT_DIM // 2
    xf = x.astype(jnp.float32)
    x1, x2 = xf[..., :half], xf[..., half:ROT_DIM]
    c, s = cos[:, :, None, :], sin[:, :, None, :]
    out = jnp.concatenate([x1 * c - x2 * s, x2 * c + x1 * s, xf[..., ROT_DIM:]], axis=-1)
    return out.astype(x.dtype)


def causal_depthwise_conv(x, w):
    c = x.shape[-1]
    return lax.conv_general_dilated(
        x, w[:, None, :].astype(x.dtype), window_strides=(1,), padding=[(CONV_K - 1, 0)],
        dimension_numbers=("NWC", "WIO", "NWC"), feature_group_count=c)


def swiglu(h, w_gate_up, w_down):
    gu = h @ w_gate_up
    return (jax.nn.silu(gu[..., :D_FF]) * gu[..., D_FF:]) @ w_down


def gated_delta_rule_chunked(q, k, v, g, beta):
    b_sz, s_len, n_h, dk = q.shape
    dv = v.shape[-1]
    n_ch = s_len // CHUNK

    def chunks(t):
        t = t.reshape((b_sz, n_ch, CHUNK, n_h) + t.shape[3:])
        return jnp.moveaxis(t, 3, 1)

    q = chunks(q) * (dk ** -0.5)
    k = chunks(k)
    v = chunks(v)
    beta = chunks(beta)
    gc = jnp.cumsum(chunks(g), axis=-1)
    tril = jnp.tril(jnp.ones((CHUNK, CHUNK), dtype=bool))
    strict = jnp.tril(jnp.ones((CHUNK, CHUNK), dtype=bool), -1)
    decay = jnp.exp(jnp.where(tril, gc[..., :, None] - gc[..., None, :], -jnp.inf))
    kb = k * beta[..., None]
    lower = jnp.where(strict, jnp.einsum("bhncd,bhnkd->bhnck", kb, k) * decay, 0.0)
    rhs = jnp.concatenate([v * beta[..., None], kb * jnp.exp(gc)[..., None]], axis=-1)
    sol = lax.linalg.triangular_solve(lower, rhs, left_side=True, lower=True, unit_diagonal=True)
    u, w = sol[..., :dv], sol[..., dv:]
    intra = jnp.einsum("bhncd,bhnkd->bhnck", q, k) * decay
    q_g = q * jnp.exp(gc)[..., None]
    k_g = k * jnp.exp(gc[..., -1:] - gc)[..., None]
    g_last = jnp.exp(gc[..., -1])
    xs = tuple(jnp.moveaxis(t, 2, 0) for t in (u, w, intra, q_g, k_g, g_last))

    def step(state, inp):
        u_n, w_n, a_n, qg_n, kg_n, gl_n = inp
        v_new = u_n - jnp.einsum("bhck,bhkv->bhcv", w_n, state)
        o_n = jnp.einsum("bhck,bhkv->bhcv", qg_n, state) + jnp.einsum("bhcs,bhsv->bhcv", a_n, v_new)
        state = state * gl_n[..., None, None] + jnp.einsum("bhck,bhcv->bhkv", kg_n, v_new)
        return state, o_n

    s0 = jnp.zeros((b_sz, n_h, dk, dv), jnp.float32)
    _, o = lax.scan(step, s0, xs)
    return jnp.transpose(o, (1, 0, 3, 2, 4)).reshape(b_sz, s_len, n_h, dv)


def gated_deltanet_mixer(h, w_in, conv_w, a_log, dt_bias, norm_g):
    b_sz, s_len, _ = h.shape
    proj = h @ w_in
    o1 = 2 * GDN_QK_W + GDN_V_W
    qkv = jax.nn.silu(causal_depthwise_conv(proj[..., :o1], conv_w))
    z = proj[..., o1:o1 + GDN_V_W]
    o2 = o1 + GDN_V_W
    b_logit = proj[..., o2:o2 + GDN_HEADS].astype(jnp.float32)
    a_logit = proj[..., o2 + GDN_HEADS:o2 + 2 * GDN_HEADS].astype(jnp.float32)
    mem_q = proj[..., o2 + 2 * GDN_HEADS:]
    q = l2_normalize(qkv[..., :GDN_QK_W].reshape(b_sz, s_len, GDN_HEADS, GDN_DK))
    k = l2_normalize(qkv[..., GDN_QK_W:2 * GDN_QK_W].reshape(b_sz, s_len, GDN_HEADS, GDN_DK))
    v = qkv[..., 2 * GDN_QK_W:].reshape(b_sz, s_len, GDN_HEADS, GDN_DV).astype(jnp.float32)
    beta = jax.nn.sigmoid(b_logit)
    g = -jnp.exp(a_log.astype(jnp.float32)) * jax.nn.softplus(a_logit + dt_bias.astype(jnp.float32))
    o = gated_delta_rule_chunked(q, k, v, g, beta)
    o = rms_norm(o, norm_g) * jax.nn.silu(z.reshape(b_sz, s_len, GDN_HEADS, GDN_DV).astype(jnp.float32))
    return o.reshape(b_sz, s_len, GDN_V_W).astype(h.dtype), mem_q


def sliding_window_attention(q, k, v, sinks):
    b_sz, s_len = q.shape[:2]
    nb = s_len // SWA_BLOCK
    qb = q.reshape(b_sz, nb, SWA_BLOCK, SWA_KV_HEADS, SWA_GROUP, SWA_DH)
    kb = k.reshape(b_sz, nb, SWA_BLOCK, SWA_KV_HEADS, SWA_DH)
    vb = v.reshape(b_sz, nb, SWA_BLOCK, SWA_KV_HEADS, SWA_DH)
    pad = jnp.zeros_like(kb[:, :1])
    kw = jnp.concatenate([jnp.concatenate([pad, kb[:, :-1]], axis=1), kb], axis=2)
    vw = jnp.concatenate([jnp.concatenate([pad, vb[:, :-1]], axis=1), vb], axis=2)
    s = jnp.einsum("bnqhgd,bnkhd->bnhgqk", qb, kw).astype(jnp.float32) * (SWA_DH ** -0.5)
    qi = jnp.arange(SWA_BLOCK)[:, None] + SWA_BLOCK
    ki = jnp.arange(2 * SWA_BLOCK)[None, :]
    diff = qi - ki
    band = (diff >= 0) & (diff < WINDOW)
    has_prev = jnp.arange(nb) > 0
    mask = band[None] & (has_prev[:, None, None] | (ki >= SWA_BLOCK)[None])
    s = jnp.where(mask[None, :, None, None], s, -jnp.inf)
    sink = sinks.astype(jnp.float32).reshape(SWA_KV_HEADS, SWA_GROUP)[None, None, :, :, None, None]
    m = jnp.maximum(jnp.max(s, axis=-1, keepdims=True), sink)
    p = jnp.exp(s - m)
    p = (p / (jnp.sum(p, axis=-1, keepdims=True) + jnp.exp(sink - m))).astype(v.dtype)
    o = jnp.einsum("bnhgqk,bnkhd->bnqhgd", p, vw)
    return o.reshape(b_sz, s_len, SWA_Q_W)


def memory_attention(q, mem_k, mem_v):
    s = jnp.einsum("bshd,bmhd->bhsm", q, mem_k).astype(jnp.float32) * (MEM_DH ** -0.5)
    p = jax.nn.softmax(s, axis=-1).astype(q.dtype)
    o = jnp.einsum("bhsm,bmhd->bshd", p, mem_v)
    return o.reshape(q.shape[0], q.shape[1], MEM_W)


def _fwd_setup_inputs(seed: int = 0) -> dict:
    key = jax.random.key(seed)
    ks = jax.random.split(key, 24)
    f32 = jnp.float32

    def dense(k, shape, fan_in):
        return jax.random.normal(k, shape, f32) * (fan_in ** -0.5)

    def gain(k, shape):
        return 1.0 + 0.02 * jax.random.normal(k, shape, f32)

    x = jax.random.normal(ks[0], (BATCH, SEQ, D_MODEL), f32)
    mem = jax.random.normal(ks[1], (BATCH, MEM_LEN, D_MODEL), f32)
    positions = (jnp.arange(SEQ, dtype=jnp.int32)[None, :]
                 + jax.random.randint(ks[2], (BATCH, 1), 0, 4096, dtype=jnp.int32))
    dt0 = jnp.exp(jax.random.uniform(ks[15], (N_A_LAYERS, GDN_HEADS), f32,
                                     math.log(1e-3), math.log(1e-1)))
    return {
        "x": x,
        "mem": mem,
        "positions": positions,
        "ln_mix": gain(ks[3], (DEPTH, D_MODEL)),
        "ln_ffn": gain(ks[4], (DEPTH, D_MODEL)),
        "ln_mem": gain(ks[5], (D_MODEL,)),
        "w_mem_kv": dense(ks[6], (DEPTH, D_MODEL, 2 * MEM_W), D_MODEL),
        "w_out": dense(ks[7], (DEPTH, D_MIX, D_MODEL), D_MIX),
        "w_gate_up": dense(ks[8], (DEPTH, D_MODEL, 2 * D_FF), D_MODEL),
        "w_down": dense(ks[9], (DEPTH, D_FF, D_MODEL), D_FF),
        "gdn_w_in": dense(ks[10], (N_A_LAYERS, D_MODEL, GDN_IN), D_MODEL),
        "gdn_conv": dense(ks[11], (N_A_LAYERS, CONV_K, 2 * GDN_QK_W + GDN_V_W), CONV_K),
        "gdn_A_log": jnp.log(jax.random.uniform(ks[12], (N_A_LAYERS, GDN_HEADS), f32, 1.0, 16.0)),
        "gdn_dt_bias": dt0 + jnp.log(-jnp.expm1(-dt0)),
        "gdn_norm": gain(ks[13], (N_A_LAYERS, GDN_DV)),
        "swa_w_q": dense(ks[14], (N_B_LAYERS, D_MODEL, SWA_IN), D_MODEL),
        "swa_sinks": 0.5 * jax.random.normal(ks[16], (N_B_LAYERS, SWA_HEADS), f32),
        "ln_kv": gain(ks[17], (D_MODEL,)),
        "w_kv": dense(ks[18], (D_MODEL, 2 * KV_W), D_MODEL),
        "ln_final": gain(ks[19], (D_MODEL,)),
    }


def _fwd_reference(x, mem, positions, ln_mix, ln_ffn, ln_mem, w_mem_kv, w_out, w_gate_up, w_down,
              gdn_w_in, gdn_conv, gdn_A_log, gdn_dt_bias, gdn_norm,
              swa_w_q, swa_sinks, ln_kv, w_kv, ln_final):
    b_sz, s_len, _ = x.shape
    cos, sin = rope_tables(positions)
    mem_n = rms_norm(mem, ln_mem)
    shared_k = None
    shared_v = None
    for layer in range(DEPTH):
        h = rms_norm(x, ln_mix[layer])
        mkv = mem_n @ w_mem_kv[layer]
        mem_k = mkv[..., :MEM_W].reshape(b_sz, MEM_LEN, MEM_HEADS, MEM_DH)
        mem_v = mkv[..., MEM_W:].reshape(b_sz, MEM_LEN, MEM_HEADS, MEM_DH)
        if layer < N_A_LAYERS:
            a = layer
            mix_out, mem_q = gated_deltanet_mixer(h, gdn_w_in[a], gdn_conv[a], gdn_A_log[a],
                                                  gdn_dt_bias[a], gdn_norm[a])
        else:
            bl = layer - N_A_LAYERS
            proj = h @ swa_w_q[bl]
            q = apply_partial_rope(proj[..., :SWA_Q_W].reshape(b_sz, s_len, SWA_HEADS, SWA_DH), cos, sin)
            mix_out = sliding_window_attention(q, shared_k, shared_v, swa_sinks[bl])
            mem_q = proj[..., SWA_Q_W:]
        mem_o = memory_attention(mem_q.reshape(b_sz, s_len, MEM_HEADS, MEM_DH), mem_k, mem_v)
        x = x + jnp.concatenate([mix_out.astype(x.dtype), mem_o.astype(x.dtype)], axis=-1) @ w_out[layer]
        x = x + swiglu(rms_norm(x, ln_ffn[layer]), w_gate_up[layer], w_down[layer])
        if layer == N_A_LAYERS - 1:
            kv = rms_norm(x, ln_kv) @ w_kv
            shared_k = apply_partial_rope(kv[..., :KV_W].reshape(b_sz, s_len, SWA_KV_HEADS, SWA_DH), cos, sin)
            shared_v = kv[..., KV_W:].reshape(b_sz, s_len, SWA_KV_HEADS, SWA_DH)
    return rms_norm(x, ln_final)


import jax as _jax
import jax.numpy as _jnp

TWIN_FORMAT = 'train_step'
FWD_PARAMS = ['x', 'mem', 'positions', 'ln_mix', 'ln_ffn', 'ln_mem', 'w_mem_kv', 'w_out', 'w_gate_up', 'w_down', 'gdn_w_in', 'gdn_conv', 'gdn_A_log', 'gdn_dt_bias', 'gdn_norm', 'swa_w_q', 'swa_sinks', 'ln_kv', 'w_kv', 'ln_final']
TWIN_WEIGHTS = ['ln_mix', 'ln_ffn', 'ln_mem', 'w_mem_kv', 'w_out', 'w_gate_up', 'w_down', 'gdn_w_in', 'gdn_conv', 'gdn_A_log', 'gdn_dt_bias', 'gdn_norm', 'swa_w_q', 'swa_sinks', 'ln_kv', 'w_kv', 'ln_final']
TWIN_DIFF_INPUT = 'x'
TWIN_INPUTS = ['x', 'mem', 'positions', 'ln_mix', 'ln_ffn', 'ln_mem', 'w_mem_kv', 'w_out', 'w_gate_up', 'w_down', 'gdn_w_in', 'gdn_conv', 'gdn_A_log', 'gdn_dt_bias', 'gdn_norm', 'swa_w_q', 'swa_sinks', 'ln_kv', 'w_kv', 'ln_final', 'loss_target', 'm_ln_mix', 'm_ln_ffn', 'm_ln_mem', 'm_w_mem_kv', 'm_w_out', 'm_w_gate_up', 'm_w_down', 'm_gdn_w_in', 'm_gdn_conv', 'm_gdn_A_log', 'm_gdn_dt_bias', 'm_gdn_norm', 'm_swa_w_q', 'm_swa_sinks', 'm_ln_kv', 'm_w_kv', 'm_ln_final', 'v_ln_mix', 'v_ln_ffn', 'v_ln_mem', 'v_w_mem_kv', 'v_w_out', 'v_w_gate_up', 'v_w_down', 'v_gdn_w_in', 'v_gdn_conv', 'v_gdn_A_log', 'v_gdn_dt_bias', 'v_gdn_norm', 'v_swa_w_q', 'v_swa_sinks', 'v_ln_kv', 'v_w_kv', 'v_ln_final']
TWIN_OUTPUTS = ['loss', 'grad_x', 'grad_ln_mix', 'grad_ln_ffn', 'grad_ln_mem', 'grad_w_mem_kv', 'grad_w_out', 'grad_w_gate_up', 'grad_w_down', 'grad_gdn_w_in', 'grad_gdn_conv', 'grad_gdn_A_log', 'grad_gdn_dt_bias', 'grad_gdn_norm', 'grad_swa_w_q', 'grad_swa_sinks', 'grad_ln_kv', 'grad_w_kv', 'grad_ln_final', 'delta_ln_mix', 'delta_ln_ffn', 'delta_ln_mem', 'delta_w_mem_kv', 'delta_w_out', 'delta_w_gate_up', 'delta_w_down', 'delta_gdn_w_in', 'delta_gdn_conv', 'delta_gdn_A_log', 'delta_gdn_dt_bias', 'delta_gdn_norm', 'delta_swa_w_q', 'delta_swa_sinks', 'delta_ln_kv', 'delta_w_kv', 'delta_ln_final', 'new_m_ln_mix', 'new_m_ln_ffn', 'new_m_ln_mem', 'new_m_w_mem_kv', 'new_m_w_out', 'new_m_w_gate_up', 'new_m_w_down', 'new_m_gdn_w_in', 'new_m_gdn_conv', 'new_m_gdn_A_log', 'new_m_gdn_dt_bias', 'new_m_gdn_norm', 'new_m_swa_w_q', 'new_m_swa_sinks', 'new_m_ln_kv', 'new_m_w_kv', 'new_m_ln_final', 'new_v_ln_mix', 'new_v_ln_ffn', 'new_v_ln_mem', 'new_v_w_mem_kv', 'new_v_w_out', 'new_v_w_gate_up', 'new_v_w_down', 'new_v_gdn_w_in', 'new_v_gdn_conv', 'new_v_gdn_A_log', 'new_v_gdn_dt_bias', 'new_v_gdn_norm', 'new_v_swa_w_q', 'new_v_swa_sinks', 'new_v_ln_kv', 'new_v_w_kv', 'new_v_ln_final']
TWIN_LEAF_KINDS = {'loss': 'loss', 'grad_x': 'grad_x', 'grad_ln_mix': 'grad_w', 'grad_ln_ffn': 'grad_w', 'grad_ln_mem': 'grad_w', 'grad_w_mem_kv': 'grad_w', 'grad_w_out': 'grad_w', 'grad_w_gate_up': 'grad_w', 'grad_w_down': 'grad_w', 'grad_gdn_w_in': 'grad_w', 'grad_gdn_conv': 'grad_w', 'grad_gdn_A_log': 'grad_w', 'grad_gdn_dt_bias': 'grad_w', 'grad_gdn_norm': 'grad_w', 'grad_swa_w_q': 'grad_w', 'grad_swa_sinks': 'grad_w', 'grad_ln_kv': 'grad_w', 'grad_w_kv': 'grad_w', 'grad_ln_final': 'grad_w', 'delta_ln_mix': 'delta_w', 'delta_ln_ffn': 'delta_w', 'delta_ln_mem': 'delta_w', 'delta_w_mem_kv': 'delta_w', 'delta_w_out': 'delta_w', 'delta_w_gate_up': 'delta_w', 'delta_w_down': 'delta_w', 'delta_gdn_w_in': 'delta_w', 'delta_gdn_conv': 'delta_w', 'delta_gdn_A_log': 'delta_w', 'delta_gdn_dt_bias': 'delta_w', 'delta_gdn_norm': 'delta_w', 'delta_swa_w_q': 'delta_w', 'delta_swa_sinks': 'delta_w', 'delta_ln_kv': 'delta_w', 'delta_w_kv': 'delta_w', 'delta_ln_final': 'delta_w', 'new_m_ln_mix': 'new_m', 'new_m_ln_ffn': 'new_m', 'new_m_ln_mem': 'new_m', 'new_m_w_mem_kv': 'new_m', 'new_m_w_out': 'new_m', 'new_m_w_gate_up': 'new_m', 'new_m_w_down': 'new_m', 'new_m_gdn_w_in': 'new_m', 'new_m_gdn_conv': 'new_m', 'new_m_gdn_A_log': 'new_m', 'new_m_gdn_dt_bias': 'new_m', 'new_m_gdn_norm': 'new_m', 'new_m_swa_w_q': 'new_m', 'new_m_swa_sinks': 'new_m', 'new_m_ln_kv': 'new_m', 'new_m_w_kv': 'new_m', 'new_m_ln_final': 'new_m', 'new_v_ln_mix': 'new_v', 'new_v_ln_ffn': 'new_v', 'new_v_ln_mem': 'new_v', 'new_v_w_mem_kv': 'new_v', 'new_v_w_out': 'new_v', 'new_v_w_gate_up': 'new_v', 'new_v_w_down': 'new_v', 'new_v_gdn_w_in': 'new_v', 'new_v_gdn_conv': 'new_v', 'new_v_gdn_A_log': 'new_v', 'new_v_gdn_dt_bias': 'new_v', 'new_v_gdn_norm': 'new_v', 'new_v_swa_w_q': 'new_v', 'new_v_swa_sinks': 'new_v', 'new_v_ln_kv': 'new_v', 'new_v_w_kv': 'new_v', 'new_v_ln_final': 'new_v'}


def _forward(args):
    return _fwd_reference(*[args[k] for k in FWD_PARAMS])


def _output_shape():
    def fwd():
        inp = _fwd_setup_inputs(0)
        return _fwd_reference(*[inp[k] for k in FWD_PARAMS])
    out = _jax.eval_shape(fwd)
    return out.shape, out.dtype

N_MICROBATCH = 1
ADAM_LR = 0.001
ADAM_B1 = 0.9
ADAM_B2 = 0.999
ADAM_EPS = 1e-08
ADAM_WD = 0.01
ADAM_STEP = 10
PER_EXAMPLE_BATCH_AXIS = {'x': 0, 'mem': 0, 'positions': 0, 'loss_target': 0}
SHARED_INPUTS = []
_WEIGHT_DTYPES = {'ln_mix': _jnp.float32, 'ln_ffn': _jnp.float32, 'ln_mem': _jnp.float32, 'w_mem_kv': _jnp.float32, 'w_out': _jnp.float32, 'w_gate_up': _jnp.float32, 'w_down': _jnp.float32, 'gdn_w_in': _jnp.float32, 'gdn_conv': _jnp.float32, 'gdn_A_log': _jnp.float32, 'gdn_dt_bias': _jnp.float32, 'gdn_norm': _jnp.float32, 'swa_w_q': _jnp.float32, 'swa_sinks': _jnp.float32, 'ln_kv': _jnp.float32, 'w_kv': _jnp.float32, 'ln_final': _jnp.float32}
MOMENT_SCALE = {'ln_mix': 2.497225e-01, 'ln_ffn': 2.408096e-01, 'ln_mem': 5.973748e-02, 'w_mem_kv': 3.689548e-02, 'w_out': 1.529227e-01, 'w_gate_up': 1.031305e-01, 'w_down': 1.682057e-01, 'gdn_w_in': 1.946973e-01, 'gdn_conv': 1.845031e-01, 'gdn_A_log': 8.879155e-01, 'gdn_dt_bias': 8.647446e-01, 'gdn_norm': 6.351424e-01, 'swa_w_q': 4.000444e-02, 'swa_sinks': 5.102546e-02, 'ln_kv': 8.643936e-02, 'w_kv': 1.767014e-01, 'ln_final': 1.278823e+02}


def _to_microbatches(a, axis):
    t = _jnp.moveaxis(a, axis, 0)
    t = t.reshape((N_MICROBATCH, t.shape[0] // N_MICROBATCH) + t.shape[1:])
    return _jnp.moveaxis(t, 1, axis + 1)


def setup_inputs(seed: int = 0) -> dict:
    inp = _fwd_setup_inputs(seed)
    key = _jax.random.fold_in(_jax.random.key(seed), 7919)
    shape, _ = _output_shape()
    out = dict(inp)
    out["loss_target"] = _jax.random.normal(_jax.random.fold_in(key, 0), shape, _jnp.float32)
    for i, name in enumerate(TWIN_WEIGHTS):
        w = inp[name].astype(_jnp.float32)
        if MOMENT_SCALE is None:
            s = _jnp.sqrt(_jnp.mean(_jnp.square(w)) + 1e-30)
        else:
            s = MOMENT_SCALE[name]
        km, kv = _jax.random.split(_jax.random.fold_in(key, i + 1))
        out[name] = w
        out["m_" + name] = s * _jax.random.normal(km, w.shape, _jnp.float32)
        out["v_" + name] = (s * s) * _jax.random.uniform(kv, w.shape, _jnp.float32, 0.5, 1.5)
    if N_MICROBATCH > 1:
        for name, axis in PER_EXAMPLE_BATCH_AXIS.items():
            out[name] = _to_microbatches(out[name], axis)
    return {'x': out['x'], 'mem': out['mem'], 'positions': out['positions'], 'ln_mix': out['ln_mix'], 'ln_ffn': out['ln_ffn'], 'ln_mem': out['ln_mem'], 'w_mem_kv': out['w_mem_kv'], 'w_out': out['w_out'], 'w_gate_up': out['w_gate_up'], 'w_down': out['w_down'], 'gdn_w_in': out['gdn_w_in'], 'gdn_conv': out['gdn_conv'], 'gdn_A_log': out['gdn_A_log'], 'gdn_dt_bias': out['gdn_dt_bias'], 'gdn_norm': out['gdn_norm'], 'swa_w_q': out['swa_w_q'], 'swa_sinks': out['swa_sinks'], 'ln_kv': out['ln_kv'], 'w_kv': out['w_kv'], 'ln_final': out['ln_final'], 'loss_target': out['loss_target'], 'm_ln_mix': out['m_ln_mix'], 'm_ln_ffn': out['m_ln_ffn'], 'm_ln_mem': out['m_ln_mem'], 'm_w_mem_kv': out['m_w_mem_kv'], 'm_w_out': out['m_w_out'], 'm_w_gate_up': out['m_w_gate_up'], 'm_w_down': out['m_w_down'], 'm_gdn_w_in': out['m_gdn_w_in'], 'm_gdn_conv': out['m_gdn_conv'], 'm_gdn_A_log': out['m_gdn_A_log'], 'm_gdn_dt_bias': out['m_gdn_dt_bias'], 'm_gdn_norm': out['m_gdn_norm'], 'm_swa_w_q': out['m_swa_w_q'], 'm_swa_sinks': out['m_swa_sinks'], 'm_ln_kv': out['m_ln_kv'], 'm_w_kv': out['m_w_kv'], 'm_ln_final': out['m_ln_final'], 'v_ln_mix': out['v_ln_mix'], 'v_ln_ffn': out['v_ln_ffn'], 'v_ln_mem': out['v_ln_mem'], 'v_w_mem_kv': out['v_w_mem_kv'], 'v_w_out': out['v_w_out'], 'v_w_gate_up': out['v_w_gate_up'], 'v_w_down': out['v_w_down'], 'v_gdn_w_in': out['v_gdn_w_in'], 'v_gdn_conv': out['v_gdn_conv'], 'v_gdn_A_log': out['v_gdn_A_log'], 'v_gdn_dt_bias': out['v_gdn_dt_bias'], 'v_gdn_norm': out['v_gdn_norm'], 'v_swa_w_q': out['v_swa_w_q'], 'v_swa_sinks': out['v_swa_sinks'], 'v_ln_kv': out['v_ln_kv'], 'v_w_kv': out['v_w_kv'], 'v_ln_final': out['v_ln_final']}


def _loss(weights, diff, rest, loss_target):
    with _jax.named_scope("forward"):
        args = {**rest, TWIN_DIFF_INPUT: diff, **{k: w.astype(_WEIGHT_DTYPES[k]) for k, w in weights.items()}}
        y = _forward(args)
    with _jax.named_scope("loss_head"):
        err = _jnp.square(y.astype(_jnp.float32) - loss_target)
        return 0.5 * _jnp.sum(_jnp.mean(err, axis=-1)) if err.ndim else 0.5 * err


def _adamw(w, g, m, v):
    m = ADAM_B1 * m + (1.0 - ADAM_B1) * g
    v = ADAM_B2 * v + (1.0 - ADAM_B2) * _jnp.square(g)
    m_hat = m / (1.0 - ADAM_B1 ** ADAM_STEP)
    v_hat = v / (1.0 - ADAM_B2 ** ADAM_STEP)
    delta = -ADAM_LR * (m_hat / (_jnp.sqrt(v_hat) + ADAM_EPS) + ADAM_WD * w)
    return delta, m, v


def reference(x, mem, positions, ln_mix, ln_ffn, ln_mem, w_mem_kv, w_out, w_gate_up, w_down, gdn_w_in, gdn_conv, gdn_A_log, gdn_dt_bias, gdn_norm, swa_w_q, swa_sinks, ln_kv, w_kv, ln_final, loss_target, m_ln_mix, m_ln_ffn, m_ln_mem, m_w_mem_kv, m_w_out, m_w_gate_up, m_w_down, m_gdn_w_in, m_gdn_conv, m_gdn_A_log, m_gdn_dt_bias, m_gdn_norm, m_swa_w_q, m_swa_sinks, m_ln_kv, m_w_kv, m_ln_final, v_ln_mix, v_ln_ffn, v_ln_mem, v_w_mem_kv, v_w_out, v_w_gate_up, v_w_down, v_gdn_w_in, v_gdn_conv, v_gdn_A_log, v_gdn_dt_bias, v_gdn_norm, v_swa_w_q, v_swa_sinks, v_ln_kv, v_w_kv, v_ln_final):
    given = dict(x=x, mem=mem, positions=positions, ln_mix=ln_mix, ln_ffn=ln_ffn, ln_mem=ln_mem, w_mem_kv=w_mem_kv, w_out=w_out, w_gate_up=w_gate_up, w_down=w_down, gdn_w_in=gdn_w_in, gdn_conv=gdn_conv, gdn_A_log=gdn_A_log, gdn_dt_bias=gdn_dt_bias, gdn_norm=gdn_norm, swa_w_q=swa_w_q, swa_sinks=swa_sinks, ln_kv=ln_kv, w_kv=w_kv, ln_final=ln_final, loss_target=loss_target, m_ln_mix=m_ln_mix, m_ln_ffn=m_ln_ffn, m_ln_mem=m_ln_mem, m_w_mem_kv=m_w_mem_kv, m_w_out=m_w_out, m_w_gate_up=m_w_gate_up, m_w_down=m_w_down, m_gdn_w_in=m_gdn_w_in, m_gdn_conv=m_gdn_conv, m_gdn_A_log=m_gdn_A_log, m_gdn_dt_bias=m_gdn_dt_bias, m_gdn_norm=m_gdn_norm, m_swa_w_q=m_swa_w_q, m_swa_sinks=m_swa_sinks, m_ln_kv=m_ln_kv, m_w_kv=m_w_kv, m_ln_final=m_ln_final, v_ln_mix=v_ln_mix, v_ln_ffn=v_ln_ffn, v_ln_mem=v_ln_mem, v_w_mem_kv=v_w_mem_kv, v_w_out=v_w_out, v_w_gate_up=v_w_gate_up, v_w_down=v_w_down, v_gdn_w_in=v_gdn_w_in, v_gdn_conv=v_gdn_conv, v_gdn_A_log=v_gdn_A_log, v_gdn_dt_bias=v_gdn_dt_bias, v_gdn_norm=v_gdn_norm, v_swa_w_q=v_swa_w_q, v_swa_sinks=v_swa_sinks, v_ln_kv=v_ln_kv, v_w_kv=v_w_kv, v_ln_final=v_ln_final)
    weights = {n: given[n] for n in TWIN_WEIGHTS}
    shared = {n: given[n] for n in SHARED_INPUTS}
    per_example = {n: given[n] for n in ['x', 'mem', 'positions']}
    grad_fn = _jax.value_and_grad(_loss, argnums=(0, 1))

    def one_microbatch(ex, loss_target):
        ex = dict(ex)
        diff = ex.pop(TWIN_DIFF_INPUT)
        return grad_fn(weights, diff, {**shared, **ex}, loss_target)

    if N_MICROBATCH == 1:
        loss, (grad_w, grad_x) = one_microbatch(per_example, given["loss_target"])
    else:
        def body(carry, xs):
            loss_sum, grad_sum = carry
            l_k, (gw_k, gx_k) = one_microbatch(xs[0], xs[1])
            with _jax.named_scope("update"):
                return (loss_sum + l_k, _jax.tree.map(_jnp.add, grad_sum, gw_k)), gx_k

        init = (_jnp.zeros((), _jnp.float32), _jax.tree.map(_jnp.zeros_like, weights))
        (loss, grad_w), grad_x = _jax.lax.scan(body, init, (per_example, given["loss_target"]))
    with _jax.named_scope("update"):
        delta_w, new_m, new_v = {}, {}, {}
        for n in TWIN_WEIGHTS:
            delta_w[n], new_m[n], new_v[n] = _adamw(weights[n], grad_w[n], given["m_" + n], given["v_" + n])
    return (loss, grad_x, *[grad_w[n] for n in TWIN_WEIGHTS], *[delta_w[n] for n in TWIN_WEIGHTS],
            *[new_m[n] for n in TWIN_WEIGHTS], *[new_v[n] for n in TWIN_WEIGHTS])
```

```python
import functools
import math

import jax
import jax.numpy as jnp
from jax import lax
from jax.experimental import pallas as pl
from jax.experimental.pallas import tpu as pltpu

F32 = jnp.float32
BF16 = jnp.bfloat16
HI = lax.Precision.HIGHEST
MESH = pl.DeviceIdType.MESH

D_MODEL = 1024
DEPTH = 4
N_A = 2
N_B = 2
EPS = 1e-6
GDN_HEADS = 6
GDN_DK = 128
GDN_W = 768
CONV_K = 4
CHUNK = 64
SWA_HEADS = 12
SWA_KV_HEADS = 2
SWA_DH = 64
SWA_GROUP = 6
SWA_GW = SWA_GROUP * SWA_DH
SWA_BLOCK = 128
ROPE_THETA = 500000.0
ROT_DIM = 16
MEM_LEN = 256
MEM_HEADS = 4
MEM_DH = 64
MEM_W = 256
D_FF = 2816
GDN_IN = 3340
GDN_IN_PAD = 3456
ADAM_LR = 0.001
ADAM_B1 = 0.9
ADAM_B2 = 0.999
ADAM_EPS = 1e-08
ADAM_WD = 0.01
ADAM_STEP = 10

N_DEV = 8
LANES = 128
SUBLANES = 8
V7X_VMEM_LIMIT = 56 * 2**20

SHARDED = ("w_mem_kv", "w_out", "w_gate_up", "w_down", "gdn_w_in", "swa_w_q", "w_kv")
SMALL = ("ln_mix", "ln_ffn", "ln_mem", "gdn_A_log", "gdn_dt_bias", "gdn_norm", "swa_sinks", "ln_kv", "ln_final")
WEIGHTS = ("ln_mix", "ln_ffn", "ln_mem", "w_mem_kv", "w_out", "w_gate_up", "w_down", "gdn_w_in", "gdn_conv",
           "gdn_A_log", "gdn_dt_bias", "gdn_norm", "swa_w_q", "swa_sinks", "ln_kv", "w_kv", "ln_final")


def _params(sem=None, **kw):
    return pltpu.CompilerParams(dimension_semantics=sem, vmem_limit_bytes=V7X_VMEM_LIMIT, **kw)


def _dot(a, b, dims=(((1,), (0,)), ((), ())), precision=None):
    return lax.dot_general(a, b, dims, precision=precision, preferred_element_type=F32)


NT = (((1,), (1,)), ((), ()))
TN = (((0,), (0,)), ((), ()))


def _fold8(v):
    r, w = v.shape
    return v.reshape(r // SUBLANES, SUBLANES, w).sum(axis=0)


def _row(a, w=None, cb=0):
    return ("row", a, a.shape[1] if w is None else w, cb)


def _full(a):
    return ("full", a, None, None)


def _prev8(a, w, cb=0):
    return ("prev8", a, w, cb)


def _next8(a, w, cb=0):
    return ("next8", a, w, cb)


def _rowcall(fn, name, rows, bm, ins, outs, accs=()):
    bm = min(bm, rows)
    assert rows % bm == 0 and bm % SUBLANES == 0
    steps = rows // bm
    r8 = bm // SUBLANES
    in_specs, arrays = [], []
    for kind, a, w, cb in ins:
        arrays.append(a)
        if kind == "row":
            in_specs.append(pl.BlockSpec((bm, w), lambda i, cb=cb: (i, cb)))
        elif kind == "full":
            in_specs.append(pl.BlockSpec(a.shape, lambda i, nd=a.ndim: (0,) * nd))
        elif kind == "prev8":
            in_specs.append(pl.BlockSpec((SUBLANES, w), lambda i, cb=cb: (jnp.maximum(i * r8 - 1, 0), cb)))
        else:
            last = rows // SUBLANES - 1
            in_specs.append(pl.BlockSpec((SUBLANES, w), lambda i, cb=cb: (jnp.minimum((i + 1) * r8, last), cb)))
    out_shape = [jax.ShapeDtypeStruct((rows, w), dt) for w, dt in outs]
    out_specs = [pl.BlockSpec((bm, w), lambda i: (i, 0)) for w, _ in outs]
    out_shape += [jax.ShapeDtypeStruct(s, F32) for s in accs]
    out_specs += [pl.BlockSpec(s, lambda i: (0, 0)) for s in accs]
    n_in, n_out = len(ins), len(outs)

    def body(*refs):
        i = pl.program_id(0)
        res = fn(i, *[r[...] for r in refs[:n_in]])
        if not isinstance(res, (tuple, list)):
            res = (res,)
        for r, v in zip(refs[n_in:n_in + n_out], res[:n_out]):
            r[...] = v.astype(r.dtype)
        if accs:
            @pl.when(i == 0)
            def _():
                for r in refs[n_in + n_out:]:
                    r[...] = jnp.zeros(r.shape, F32)
            for r, v in zip(refs[n_in + n_out:], res[n_out:]):
                r[...] += v

    res = pl.pallas_call(
        body, name=name, grid=(steps,), in_specs=in_specs, out_specs=out_specs, out_shape=out_shape,
        compiler_params=_params(("arbitrary",)))(*arrays)
    return res


def _tile(n, cap):
    for t in (1408, 1152, 1024, 896, 768, 640, 512, 384, 256, 128):
        if t <= cap and n % t == 0:
            return t
    return n


def _mm(a, b, mode, name, out_dtype=F32, add=None):
    if mode == "tn":
        s, m = a.shape
        n = b.shape[1]
        bm, bn, bk = _tile(m, 1408), _tile(n, 512), min(s, 2048)
        nk = s // bk

        def body(a_ref, b_ref, o_ref, acc_ref):
            k = pl.program_id(2)

            @pl.when(k == 0)
            def _():
                acc_ref[...] = jnp.zeros(acc_ref.shape, F32)
            acc_ref[...] += _dot(a_ref[...].astype(BF16), b_ref[...].astype(BF16), TN)

            @pl.when(k == nk - 1)
            def _():
                o_ref[...] = acc_ref[...].astype(o_ref.dtype)

        return pl.pallas_call(
            body, name=name, grid=(m // bm, n // bn, nk),
            in_specs=[pl.BlockSpec((bk, bm), lambda i, j, k: (k, i)), pl.BlockSpec((bk, bn), lambda i, j, k: (k, j))],
            out_specs=pl.BlockSpec((bm, bn), lambda i, j, k: (i, j)),
            out_shape=jax.ShapeDtypeStruct((m, n), out_dtype),
            scratch_shapes=[pltpu.VMEM((bm, bn), F32)],
            compiler_params=_params(("parallel", "parallel", "arbitrary")))(a, b)

    m, k = a.shape
    n = b.shape[1] if mode == "nn" else b.shape[0]
    big = k > 2048
    bm = min(m, 512 if big else 1024)
    bn = _tile(n, 512 if big else 1024)
    dims = NT if mode == "nt" else (((1,), (0,)), ((), ()))
    b_spec = (pl.BlockSpec((k, bn), lambda i, j: (0, j)) if mode == "nn" else pl.BlockSpec((bn, k), lambda i, j: (j, 0)))
    in_specs = [pl.BlockSpec((bm, k), lambda i, j: (i, 0)), b_spec]
    args = [a, b]
    if add is not None:
        in_specs.append(pl.BlockSpec((bm, bn), lambda i, j: (i, j)))
        args.append(add)

    def body(a_ref, b_ref, *rest):
        o_ref = rest[-1]
        acc = _dot(a_ref[...].astype(BF16), b_ref[...].astype(BF16), dims)
        if add is not None:
            acc = acc + rest[0][...]
        o_ref[...] = acc.astype(o_ref.dtype)

    return pl.pallas_call(
        body, name=name, grid=(m // bm, n // bn), in_specs=in_specs,
        out_specs=pl.BlockSpec((bm, bn), lambda i, j: (i, j)),
        out_shape=jax.ShapeDtypeStruct((m, n), out_dtype),
        compiler_params=_params(("parallel", "parallel")))(*args)


def _sigmoid(x):
    return 1.0 / (1.0 + jnp.exp(-x))


def _softplus(x):
    return jnp.maximum(x, 0.0) + jnp.log(1.0 + jnp.exp(-jnp.abs(x)))


def _silu_and_grad(x):
    s = _sigmoid(x)
    return x * s, s * (1.0 + x * (1.0 - s))


def _rms_stats(x):
    r = lax.rsqrt(jnp.mean(x * x, axis=-1, keepdims=True) + EPS)
    return r, x * r


def _rms_fwd(x, g, name, out_dtype=BF16, bm=512):
    def fn(i, x, g):
        _, xn = _rms_stats(x)
        return xn * g
    return _rowcall(fn, name, x.shape[0], bm, [_row(x), _full(g)], [(x.shape[1], out_dtype)])[0]


def _rms_bwd_math(x, g, dy):
    r, xn = _rms_stats(x)
    dxn = dy * g
    dx = r * (dxn - xn * jnp.mean(dxn * xn, axis=-1, keepdims=True))
    return dx, dy * xn


def _rms_bwd(x, g, dy, res, name, bm=256):
    d = x.shape[1]

    def fn(i, x, g, dy, *res_):
        dx, dg = _rms_bwd_math(x, g, dy.astype(F32))
        if res_:
            dx = dx + res_[0]
        return dx, dx, _fold8(dg)
    ins = [_row(x), _full(g), _row(dy)] + ([_row(res)] if res is not None else [])
    return _rowcall(fn, name, x.shape[0], bm, ins, [(d, F32), (d, BF16)], [(SUBLANES, d)])


def _final_loss(x, g, target, name, bm=256):
    d = x.shape[1]

    def fn(i, x, g, t):
        r, xn = _rms_stats(x)
        err = xn * g - t
        dy = err * (1.0 / d)
        dxn = dy * g
        dx = r * (dxn - xn * jnp.mean(dxn * xn, axis=-1, keepdims=True))
        e2 = _fold8(err * err)
        lp = e2[:, 0:LANES]
        for c in range(1, d // LANES):
            lp = lp + e2[:, c * LANES:(c + 1) * LANES]
        return dx, dx, lp * (0.5 / d), _fold8(dy * xn)
    return _rowcall(fn, name, x.shape[0], bm, [_row(x), _full(g), _row(target)], [(d, F32), (d, BF16)],
                    [(SUBLANES, LANES), (SUBLANES, d)])


def _swiglu_fwd(gu, name, bm=256):
    def fn(i, g, u):
        return _silu_and_grad(g)[0] * u
    return _rowcall(fn, name, gu.shape[0], bm, [_row(gu, D_FF, 0), _row(gu, D_FF, 1)], [(D_FF, BF16)])[0]


def _swiglu_bwd(gu, dact, name, bm=256):
    def fn(i, g, u, da):
        da = da.astype(F32)
        s, ds = _silu_and_grad(g)
        return jnp.concatenate([da * u * ds, da * s], axis=1)
    return _rowcall(fn, name, gu.shape[0], bm, [_row(gu, D_FF, 0), _row(gu, D_FF, 1), _row(dact)], [(2 * D_FF, BF16)])[0]


def _rope_apply(x, tabs, sign):
    cos, ta, tb = tabs
    outs = []
    for c in range(x.shape[1] // LANES):
        xc = x[:, c * LANES:(c + 1) * LANES]
        if sign > 0:
            o = xc * cos + pltpu.roll(xc, LANES - 8, 1) * ta + pltpu.roll(xc, 8, 1) * tb
        else:
            o = xc * cos + pltpu.roll(xc * ta, 8, 1) + pltpu.roll(xc * tb, LANES - 8, 1)
        outs.append(o)
    return outs[0] if len(outs) == 1 else jnp.concatenate(outs, axis=1)


def _rope(x, w, cb, tabs, sign, name, out_dtype, bm=512):
    def fn(i, x, c, a, b):
        return _rope_apply(x.astype(F32), (c, a, b), sign)
    return _rowcall(fn, name, x.shape[0], bm, [_row(x, w, cb)] + [_row(t) for t in tabs], [(w, out_dtype)])[0]


def _shift_down(x, prev8, s, first):
    xs = pltpu.roll(x, s, 0)
    rp = pltpu.roll(prev8, s, 0) * jnp.where(first, 0.0, 1.0)
    rid = lax.broadcasted_iota(jnp.int32, rp.shape, 0)
    top = jnp.where(rid < s, rp, xs[0:SUBLANES])
    return jnp.concatenate([top, xs[SUBLANES:]], axis=0)


def _shift_up(x, next8, s, last):
    n = x.shape[0]
    xs = pltpu.roll(x, n - s, 0)
    rn = pltpu.roll(next8, SUBLANES - s, 0) * jnp.where(last, 0.0, 1.0)
    rid = lax.broadcasted_iota(jnp.int32, rn.shape, 0)
    bot = jnp.where(rid >= SUBLANES - s, rn, xs[n - SUBLANES:])
    return jnp.concatenate([xs[:n - SUBLANES], bot], axis=0)


def _conv_fwd(x, prev8, w, first):
    acc = x * w[CONV_K - 1:CONV_K]
    shifted = []
    for s in range(1, CONV_K):
        xs = _shift_down(x, prev8, s, first)
        shifted.append(xs)
        acc = acc + xs * w[CONV_K - 1 - s:CONV_K - s]
    return acc, shifted


def _l2n(x):
    outs, rs = [], []
    for h in range(x.shape[1] // LANES):
        xh = x[:, h * LANES:(h + 1) * LANES]
        r = lax.rsqrt(jnp.sum(xh * xh, axis=-1, keepdims=True) + EPS)
        outs.append(xh * r)
        rs.append(r)
    return jnp.concatenate(outs, axis=1), rs


def _gate_math(ba, sel_b, sel_a, a_row, dt_row):
    bl = _dot(ba, sel_b, precision=HI)
    al = _dot(ba, sel_a, precision=HI) + dt_row
    beta = _sigmoid(bl)
    ea = jnp.exp(a_row)
    g = -ea * _softplus(al)
    return bl, al, beta, g, ea


def _gdn_pre_fwd(proj, conv_w, sel_b, sel_a, a_row, dt_row, name, bm=256):
    rows = proj.shape[0]
    w3 = 3 * GDN_W

    def fn(i, x, p8, ba, w, sel_b, sel_a, a_row, dt_row):
        conv, _ = _conv_fwd(x, p8, w, i == 0)
        act = _silu_and_grad(conv)[0]
        qk, _ = _l2n(act[:, :2 * GDN_W])
        _, _, beta, g, _ = _gate_math(ba, sel_b, sel_a, a_row, dt_row)
        return qk[:, :GDN_W], qk[:, GDN_W:], act[:, 2 * GDN_W:], g, beta
    ins = [_row(proj, w3, 0), _prev8(proj, w3, 0), _row(proj, LANES, (GDN_IN_PAD - LANES) // LANES),
           _full(conv_w), _full(sel_b), _full(sel_a), _full(a_row), _full(dt_row)]
    return _rowcall(fn, name, rows, bm, ins, [(GDN_W, F32)] * 5)


def _gdn_pre_bwd(proj, conv_w, sel_b, sel_a, a_row, dt_row, dq, dk, dv, dg, dbeta, name, bm=128):
    rows = proj.shape[0]
    w3 = 3 * GDN_W

    def fn(i, x, p8, ba, w, sel_b, sel_a, a_row, dt_row, dq, dk, dv, dg, dbeta):
        conv, shifted = _conv_fwd(x, p8, w, i == 0)
        act, dact = _silu_and_grad(conv)
        qk, rs = _l2n(act[:, :2 * GDN_W])
        dqk = jnp.concatenate([dq, dk], axis=1)
        parts = []
        for h in range(2 * GDN_HEADS):
            sl = slice(h * LANES, (h + 1) * LANES)
            y, dy = qk[:, sl], dqk[:, sl]
            parts.append(rs[h] * (dy - y * jnp.sum(y * dy, axis=-1, keepdims=True)))
        dconv = jnp.concatenate(parts + [dv], axis=1) * dact
        dws = [_fold8(dconv * xs) for xs in reversed(shifted)] + [_fold8(dconv * x)]
        bl, al, beta, g, ea = _gate_math(ba, sel_b, sel_a, a_row, dt_row)
        dbl = dbeta * beta * (1.0 - beta)
        dal = dg * (-ea) * _sigmoid(al)
        dba = _dot(dbl, sel_b, NT, precision=HI) + _dot(dal, sel_a, NT, precision=HI)
        return (dconv, dba * (1.0 / LANES)) + tuple(dws) + (_fold8(dg * g), _fold8(dal))
    ins = [_row(proj, w3, 0), _prev8(proj, w3, 0), _row(proj, LANES, (GDN_IN_PAD - LANES) // LANES),
           _full(conv_w), _full(sel_b), _full(sel_a), _full(a_row), _full(dt_row),
           _row(dq), _row(dk), _row(dv), _row(dg), _row(dbeta)]
    return _rowcall(fn, name, rows, bm, ins, [(w3, F32), (LANES, BF16)],
                    [(SUBLANES, w3)] * CONV_K + [(SUBLANES, GDN_W)] * 2)


def _conv_bwd_input(dconv, conv_w, name, bm=256):
    rows, w3 = dconv.shape
    steps = rows // min(bm, rows)

    def fn(i, dc, n8, w):
        acc = dc * w[CONV_K - 1:CONV_K]
        for s in range(1, CONV_K):
            acc = acc + _shift_up(dc, n8, s, i == steps - 1) * w[CONV_K - 1 - s:CONV_K - s]
        return acc
    return _rowcall(fn, name, rows, bm, [_row(dconv), _next8(dconv, w3, 0), _full(conv_w)], [(w3, BF16)])[0]


def _gdn_post_fwd(o, proj, ng, name, bm=512):
    def fn(i, o, z, ng):
        outs = []
        for h in range(GDN_HEADS):
            sl = slice(h * LANES, (h + 1) * LANES)
            _, on = _rms_stats(o[:, sl])
            outs.append(on * ng * _silu_and_grad(z[:, sl])[0])
        return jnp.concatenate(outs, axis=1)
    return _rowcall(fn, name, o.shape[0], bm, [_row(o), _row(proj, GDN_W, 3), _full(ng)], [(GDN_W, BF16)])[0]


def _gdn_post_bwd(o, proj, ng, dcat, name, bm=256):
    def fn(i, o, z, ng, dm):
        dm = dm.astype(F32)
        dos, dzs = [], []
        dng = jnp.zeros((SUBLANES, LANES), F32)
        for h in range(GDN_HEADS):
            sl = slice(h * LANES, (h + 1) * LANES)
            s, ds = _silu_and_grad(z[:, sl])
            r, on = _rms_stats(o[:, sl])
            dzs.append(dm[:, sl] * on * ng * ds)
            dy = dm[:, sl] * s
            dxn = dy * ng
            dos.append(r * (dxn - on * jnp.mean(dxn * on, axis=-1, keepdims=True)))
            dng = dng + _fold8(dy * on)
        return jnp.concatenate(dos, axis=1), jnp.concatenate(dzs, axis=1), dng
    return _rowcall(fn, name, o.shape[0], bm, [_row(o), _row(proj, GDN_W, 3), _full(ng), _row(dcat, GDN_W, 0)],
                    [(GDN_W, F32), (GDN_W, BF16)], [(SUBLANES, LANES)])


def _gdn_chunk(q, k, v, g, beta):
    c = CHUNK
    row = lax.broadcasted_iota(jnp.int32, (c, c), 0)
    col = lax.broadcasted_iota(jnp.int32, (c, c), 1)
    tril, strict = row >= col, row > col
    trif = tril.astype(F32)
    lane0 = (lax.broadcasted_iota(jnp.int32, (c, LANES), 1) == 0).astype(F32)
    gc = _dot(trif, g, precision=HI)
    gc_row = _dot(lane0, gc, NT, precision=HI)
    dm = jnp.exp(jnp.where(tril, gc[:, :c] - gc_row, -1e30))
    eg = jnp.exp(gc)
    gcl = gc[c - 1:c, :]
    ekg = jnp.exp(gcl - gc)
    egl = jnp.exp(gcl)
    qs = q * (GDN_DK ** -0.5)
    kb = k * beta
    kk = _dot(kb, k, NT)
    a = jnp.where(strict, kk * dm, 0.0)
    eye = (row == col).astype(F32)
    y = -a
    t = eye + y
    for _ in range(5):
        y = _dot(y, y, precision=HI)
        t = t + _dot(t, y, precision=HI)
    vb = v * beta
    kbg = kb * eg
    u = _dot(t, vb, precision=HI)
    w = _dot(t, kbg, precision=HI)
    qk = _dot(qs, k, NT)
    p = jnp.where(tril, qk * dm, 0.0)
    return dict(tril=tril, strict=strict, trif=trif, dm=dm, eg=eg, ekg=ekg, egl=egl, qs=qs, kb=kb, kk=kk, a=a, t=t,
                vb=vb, kbg=kbg, u=u, w=w, qk=qk, p=p, qg=qs * eg, kg=k * ekg)


def _gdn_specs(n_chunks):
    blk = pl.BlockSpec((CHUNK, LANES), lambda h, n: (n, h))
    st = pl.BlockSpec((None, None, GDN_DK, LANES), lambda h, n: (h, n, 0, 0))
    return blk, st


def _gdn_fwd(q, k, v, g, beta, name):
    rows = q.shape[0]
    n_chunks = rows // CHUNK
    blk, st = _gdn_specs(n_chunks)

    def body(q_ref, k_ref, v_ref, g_ref, b_ref, o_ref, st_ref, s_ref):
        @pl.when(pl.program_id(1) == 0)
        def _():
            s_ref[...] = jnp.zeros(s_ref.shape, F32)
        c = _gdn_chunk(q_ref[...], k_ref[...], v_ref[...], g_ref[...], b_ref[...])
        s = s_ref[...]
        st_ref[...] = s
        vn = c["u"] - _dot(c["w"], s)
        o_ref[...] = _dot(c["qg"], s) + _dot(c["p"], vn)
        s_ref[...] = s * c["egl"] + _dot(c["kg"], vn, TN)

    return pl.pallas_call(
        body, name=name, grid=(GDN_HEADS, n_chunks), in_specs=[blk] * 5, out_specs=[blk, st],
        out_shape=[jax.ShapeDtypeStruct((rows, GDN_W), F32),
                   jax.ShapeDtypeStruct((GDN_HEADS, n_chunks, GDN_DK, LANES), F32)],
        scratch_shapes=[pltpu.VMEM((GDN_DK, LANES), F32)],
        compiler_params=_params(("parallel", "arbitrary")))(q, k, v, g, beta)


def _gdn_bwd(q, k, v, g, beta, states, do, name):
    rows = q.shape[0]
    n_chunks = rows // CHUNK
    blk = pl.BlockSpec((CHUNK, LANES), lambda h, n: (n_chunks - 1 - n, h))
    st = pl.BlockSpec((None, None, GDN_DK, LANES), lambda h, n: (h, n_chunks - 1 - n, 0, 0))

    def lanesum(x):
        return jnp.broadcast_to(jnp.sum(x, axis=-1, keepdims=True), x.shape)

    def body(q_ref, k_ref, v_ref, g_ref, b_ref, st_ref, do_ref, dq_ref, dk_ref, dv_ref, dg_ref, db_ref, ds_ref):
        @pl.when(pl.program_id(1) == 0)
        def _():
            ds_ref[...] = jnp.zeros(ds_ref.shape, F32)
        q, k, v, beta = q_ref[...], k_ref[...], v_ref[...], b_ref[...]
        c = _gdn_chunk(q, k, v, g_ref[...], beta)
        s, dsn, do = st_ref[...], ds_ref[...], do_ref[...]
        tril, strict, dm, t = c["tril"], c["strict"], c["dm"], c["t"]
        vn = c["u"] - _dot(c["w"], s)
        dvn = _dot(c["p"], do, TN) + _dot(c["kg"], dsn)
        dp = jnp.where(tril, _dot(do, vn, NT), 0.0)
        dqg = _dot(do, s, NT)
        dkg = _dot(vn, dsn, NT)
        dgl = jnp.sum(jnp.sum(s * dsn, axis=1, keepdims=True), axis=0, keepdims=True) * c["egl"]
        ds_ref[...] = dsn * c["egl"] + _dot(c["qg"], do, TN) - _dot(c["w"], dvn, TN)
        dw = -_dot(dvn, s, NT)
        dvb = _dot(t, dvn, TN, precision=HI)
        dkbg = _dot(t, dw, TN, precision=HI)
        dt = _dot(dvn, c["vb"], NT) + _dot(dw, c["kbg"], NT)
        da = -_dot(_dot(t, dt, TN, precision=HI), t, NT, precision=HI)
        da = jnp.where(strict, da, 0.0)
        dkk = da * dm
        dqk = dp * dm
        dkb = _dot(dkk, k) + dkbg * c["eg"]
        dk = _dot(dkk, c["kb"], TN) + _dot(dqk, c["qs"], TN) + dkg * c["ekg"] + dkb * beta
        dqs = _dot(dqk, k) + dqg * c["eg"]
        e = da * c["a"] + dp * c["p"]
        ones = jnp.ones((CHUNK, LANES), F32)
        kg_term = lanesum(dkg * c["kg"])
        dgc = (_dot(e, ones, precision=HI) - _dot(e, ones, TN, precision=HI)
               + lanesum(dqg * c["qg"]) - kg_term + lanesum(dkbg * c["kbg"]))
        dgcl = jnp.sum(kg_term, axis=0, keepdims=True) + dgl
        last = lax.broadcasted_iota(jnp.int32, (CHUNK, LANES), 0) == CHUNK - 1
        dgc = dgc + jnp.where(last, dgcl, 0.0)
        dq_ref[...] = dqs * (GDN_DK ** -0.5)
        dk_ref[...] = dk
        dv_ref[...] = dvb * beta
        dg_ref[...] = _dot(c["trif"], dgc, TN, precision=HI)
        db_ref[...] = lanesum(dvb * v) + lanesum(dkb * k)

    return pl.pallas_call(
        body, name=name, grid=(GDN_HEADS, n_chunks), in_specs=[blk] * 5 + [st, blk], out_specs=[blk] * 5,
        out_shape=[jax.ShapeDtypeStruct((rows, GDN_W), F32)] * 5,
        scratch_shapes=[pltpu.VMEM((GDN_DK, LANES), F32)],
        compiler_params=_params(("parallel", "arbitrary")))(q, k, v, g, beta, states, do)


def _swa_masks(first):
    r = lax.broadcasted_iota(jnp.int32, (SWA_BLOCK, 2 * SWA_BLOCK), 0)
    c = lax.broadcasted_iota(jnp.int32, (SWA_BLOCK, 2 * SWA_BLOCK), 1)
    band = (c > r) & (c <= r + SWA_BLOCK)
    return band & (jnp.logical_not(first) | (c >= SWA_BLOCK))


def _head_mask(g, dtype):
    lane = lax.broadcasted_iota(jnp.int32, (1, SWA_GW), 1)
    return ((lane >= g * SWA_DH) & (lane < (g + 1) * SWA_DH)).astype(dtype)


def _swa_probs(qj, kw, g, sink, mask):
    s = _dot(qj * _head_mask(g, qj.dtype), kw, NT) * (SWA_DH ** -0.5)
    s = jnp.where(mask, s, -1e30)
    m = jnp.maximum(jnp.max(s, axis=-1, keepdims=True), sink)
    p = jnp.where(mask, jnp.exp(s - m), 0.0)
    es = jnp.exp(sink - m)
    inv = 1.0 / (jnp.sum(p, axis=-1, keepdims=True) + es)
    return p * inv, es * inv


def _swa_fwd(q, kt, vt, sinks, name):
    rows = q.shape[0]
    nb = rows // SWA_BLOCK
    cur = pl.BlockSpec((SWA_BLOCK, SWA_HEADS * SWA_DH), lambda i: (i, 0))
    prev = pl.BlockSpec((SWA_BLOCK, SWA_HEADS * SWA_DH), lambda i: (jnp.maximum(i - 1, 0), 0))

    def body(sink_ref, q_ref, kc_ref, kp_ref, vc_ref, vp_ref, o_ref):
        mask = _swa_masks(pl.program_id(0) == 0)
        for j in range(SWA_KV_HEADS):
            sl = slice(j * SWA_GW, (j + 1) * SWA_GW)
            qj = q_ref[:, sl]
            kw = jnp.concatenate([kp_ref[:, sl], kc_ref[:, sl]], axis=0)
            vw = jnp.concatenate([vp_ref[:, sl], vc_ref[:, sl]], axis=0)
            acc = jnp.zeros((SWA_BLOCK, SWA_GW), F32)
            for g in range(SWA_GROUP):
                p, _ = _swa_probs(qj, kw, g, sink_ref[j * SWA_GROUP + g], mask)
                acc = acc + _dot(p.astype(BF16), vw) * _head_mask(g, F32)
            o_ref[:, sl] = acc.astype(o_ref.dtype)

    return pl.pallas_call(
        body, name=name, grid=(nb,),
        in_specs=[pl.BlockSpec(memory_space=pltpu.SMEM), cur, cur, prev, cur, prev], out_specs=cur,
        out_shape=jax.ShapeDtypeStruct((rows, SWA_HEADS * SWA_DH), BF16),
        compiler_params=_params(("arbitrary",)))(sinks, q, kt, kt, vt, vt)


def _swa_bwd(q, kt, vt, sinks, dcat, name):
    rows = q.shape[0]
    nb = rows // SWA_BLOCK
    w = SWA_HEADS * SWA_DH
    cur = pl.BlockSpec((SWA_BLOCK, w), lambda i: (jnp.minimum(i, nb - 1), 0))
    prev = pl.BlockSpec((SWA_BLOCK, w), lambda i: (jnp.clip(i - 1, 0, nb - 1), 0))
    late = pl.BlockSpec((SWA_BLOCK, w), lambda i: (jnp.maximum(i - 1, 0), 0))
    acc_spec = pl.BlockSpec((SUBLANES, LANES), lambda i: (0, 0))

    def body(sink_ref, q_ref, kc_ref, kp_ref, vc_ref, vp_ref, do_ref, dq_ref, dk_ref, dv_ref, dsk_ref,
             ck_ref, cv_ref):
        i = pl.program_id(0)

        @pl.when(i == 0)
        def _():
            ck_ref[...] = jnp.zeros(ck_ref.shape, F32)
            cv_ref[...] = jnp.zeros(cv_ref.shape, F32)
            dsk_ref[...] = jnp.zeros(dsk_ref.shape, F32)

        @pl.when(i == nb)
        def _():
            dk_ref[...] = ck_ref[...]
            dv_ref[...] = cv_ref[...]

        @pl.when(i < nb)
        def _():
            mask = _swa_masks(i == 0)
            lane = lax.broadcasted_iota(jnp.int32, (SUBLANES, LANES), 1)
            dsk = jnp.zeros((SUBLANES, LANES), F32)
            for j in range(SWA_KV_HEADS):
                sl = slice(j * SWA_GW, (j + 1) * SWA_GW)
                qj = q_ref[:, sl]
                doj = do_ref[:, sl]
                kw = jnp.concatenate([kp_ref[:, sl], kc_ref[:, sl]], axis=0)
                vw = jnp.concatenate([vp_ref[:, sl], vc_ref[:, sl]], axis=0)
                dq = jnp.zeros((SWA_BLOCK, SWA_GW), F32)
                dk = jnp.zeros((2 * SWA_BLOCK, SWA_GW), F32)
                dv = jnp.zeros((2 * SWA_BLOCK, SWA_GW), F32)
                for g in range(SWA_GROUP):
                    hm = _head_mask(g, BF16)
                    p, ps = _swa_probs(qj, kw, g, sink_ref[j * SWA_GROUP + g], mask)
                    dog = doj * hm
                    dpr = _dot(dog, vw, NT)
                    delta = jnp.sum(p * dpr, axis=-1, keepdims=True)
                    ds = (p * (dpr - delta)).astype(BF16)
                    dq = dq + _dot(ds, kw) * _head_mask(g, F32)
                    dk = dk + _dot(ds, qj * hm, TN)
                    dv = dv + _dot(p.astype(BF16), dog, TN)
                    dsg = jnp.sum(-ps * delta, axis=0, keepdims=True)
                    dsk = dsk + jnp.where(lane == j * SWA_GROUP + g, dsg, 0.0)
                dq_ref[:, sl] = dq * (SWA_DH ** -0.5)
                dk = dk * (SWA_DH ** -0.5)
                dk_ref[:, sl] = ck_ref[:, sl] + dk[:SWA_BLOCK]
                dv_ref[:, sl] = cv_ref[:, sl] + dv[:SWA_BLOCK]
                ck_ref[:, sl] = dk[SWA_BLOCK:]
                cv_ref[:, sl] = dv[SWA_BLOCK:]
            dsk_ref[...] += dsk

    f = jax.ShapeDtypeStruct((rows, w), F32)
    return pl.pallas_call(
        body, name=name, grid=(nb + 1,),
        in_specs=[pl.BlockSpec(memory_space=pltpu.SMEM), cur, cur, prev, cur, prev, pl.BlockSpec((SWA_BLOCK, w), lambda i: (jnp.minimum(i, nb - 1), 0))],
        out_specs=[cur, late, late, acc_spec],
        out_shape=[f, f, f, jax.ShapeDtypeStruct((SUBLANES, LANES), F32)],
        scratch_shapes=[pltpu.VMEM((SWA_BLOCK, w), F32), pltpu.VMEM((SWA_BLOCK, w), F32)],
        compiler_params=_params(("arbitrary",)))(sinks, q, kt, kt, vt, vt, dcat)


def _mem_probs(mq, kbd):
    s = _dot(mq.astype(BF16), kbd) * (MEM_DH ** -0.5)
    ps = []
    for h in range(MEM_HEADS):
        sh = s[:, h * MEM_LEN:(h + 1) * MEM_LEN]
        e = jnp.exp(sh - jnp.max(sh, axis=-1, keepdims=True))
        ps.append(e / jnp.sum(e, axis=-1, keepdims=True))
    return ps


def _mem_fwd(proj, cb, kbd, vbd, name, bm=512):
    def fn(i, mq, kbd, vbd):
        p = jnp.concatenate(_mem_probs(mq, kbd), axis=1)
        return _dot(p.astype(BF16), vbd)
    return _rowcall(fn, name, proj.shape[0], bm, [_row(proj, MEM_W, cb), _full(kbd), _full(vbd)], [(MEM_W, BF16)])[0]


def _mem_bwd(proj, cb, kbd, vbd, dcat, name, bm=512):
    def fn(i, mq, kbd, vbd, do):
        ps = _mem_probs(mq, kbd)
        dp = _dot(do, vbd, NT)
        dss = []
        for h in range(MEM_HEADS):
            dph = dp[:, h * MEM_LEN:(h + 1) * MEM_LEN]
            dss.append(ps[h] * (dph - jnp.sum(ps[h] * dph, axis=-1, keepdims=True)))
        ds = (jnp.concatenate(dss, axis=1) * (MEM_DH ** -0.5)).astype(BF16)
        p = jnp.concatenate(ps, axis=1).astype(BF16)
        return _dot(ds, kbd, NT), _dot(mq.astype(BF16), ds, TN), _dot(p, do, TN)
    return _rowcall(fn, name, proj.shape[0], bm, [_row(proj, MEM_W, cb), _full(kbd), _full(vbd), _row(dcat, MEM_W, 3)],
                    [(MEM_W, BF16)], [(MEM_W, MEM_HEADS * MEM_LEN), (MEM_HEADS * MEM_LEN, MEM_W)])


def _mem_expand(mkv):
    kbd = jnp.zeros((MEM_W, MEM_HEADS * MEM_LEN), F32)
    vbd = jnp.zeros((MEM_HEADS * MEM_LEN, MEM_W), F32)
    for h in range(MEM_HEADS):
        f, m = slice(h * MEM_DH, (h + 1) * MEM_DH), slice(h * MEM_LEN, (h + 1) * MEM_LEN)
        kbd = kbd.at[f, m].set(mkv[:, f].T)
        vbd = vbd.at[m, f].set(mkv[:, MEM_W + h * MEM_DH:MEM_W + (h + 1) * MEM_DH])
    return kbd.astype(BF16), vbd.astype(BF16)


def _mem_collapse(dkbd, dvbd):
    dk = [dkbd[h * MEM_DH:(h + 1) * MEM_DH, h * MEM_LEN:(h + 1) * MEM_LEN].T for h in range(MEM_HEADS)]
    dv = [dvbd[h * MEM_LEN:(h + 1) * MEM_LEN, h * MEM_DH:(h + 1) * MEM_DH] for h in range(MEM_HEADS)]
    return jnp.concatenate(dk + dv, axis=1)


def _adamw(w, g, m, v, name, bm=512):
    def fn(i, w, g, m, v):
        m = ADAM_B1 * m + (1.0 - ADAM_B1) * g
        v = ADAM_B2 * v + (1.0 - ADAM_B2) * (g * g)
        m_hat = m / (1.0 - ADAM_B1 ** ADAM_STEP)
        v_hat = v / (1.0 - ADAM_B2 ** ADAM_STEP)
        return -ADAM_LR * (m_hat / (jnp.sqrt(v_hat) + ADAM_EPS) + ADAM_WD * w), m, v
    d = w.shape[1]
    return _rowcall(fn, name, w.shape[0], bm, [_row(w), _row(g), _row(m), _row(v)], [(d, F32)] * 3)


def _sum_slots(buf, name, bm=256):
    n, rows, w = buf.shape
    bm = min(bm, rows)
    assert rows % bm == 0

    def body(b_ref, o_ref):
        acc = b_ref[0].astype(F32)
        for s in range(1, n):
            acc = acc + b_ref[s].astype(F32)
        o_ref[...] = acc

    return pl.pallas_call(
        body, name=name, grid=(rows // bm,), in_specs=[pl.BlockSpec((n, bm, w), lambda i: (0, i, 0))],
        out_specs=pl.BlockSpec((bm, w), lambda i: (i, 0)), out_shape=jax.ShapeDtypeStruct((rows, w), F32),
        compiler_params=_params(("parallel",)))(buf)


def _exchange(src, masks, name):
    same = src.ndim == 2
    rows, w = src.shape[-2:]
    slots = N_DEV if len(masks) == N_DEV - 1 else 2
    n = len(masks)

    def body(src_ref, out_ref, send_sems, recv_sems, local_sem):
        x, y, c = lax.axis_index("x"), lax.axis_index("y"), lax.axis_index("c")
        me = 4 * x + 2 * y + c

        def flip(v, bit):
            return 1 - v if bit else v

        def slot_of(dev):
            return dev if slots == N_DEV else dev % 2

        def piece(p):
            return src_ref if same else src_ref.at[p]

        mine = pltpu.make_async_copy(piece(me), out_ref.at[slot_of(me)], local_sem)
        mine.start()
        copies = []
        for idx, k in enumerate(masks):
            peer = (flip(x, k & 4), flip(y, k & 2), flip(c, k & 1))
            peer_id = 4 * peer[0] + 2 * peer[1] + peer[2]
            cp = pltpu.make_async_remote_copy(
                src_ref=piece(peer_id), dst_ref=out_ref.at[slot_of(me)],
                send_sem=send_sems.at[idx], recv_sem=recv_sems.at[idx], device_id=peer, device_id_type=MESH)
            cp.start()
            copies.append((cp, pltpu.make_async_remote_copy(
                src_ref=piece(peer_id), dst_ref=out_ref.at[slot_of(peer_id)],
                send_sem=send_sems.at[idx], recv_sem=recv_sems.at[idx], device_id=peer, device_id_type=MESH)))
        for cp, landing in copies:
            cp.wait_send()
            landing.wait_recv()
        mine.wait()

    any_spec = pl.BlockSpec(memory_space=pl.ANY)
    return pl.pallas_call(
        body, name=name, in_specs=[any_spec], out_specs=any_spec,
        out_shape=jax.ShapeDtypeStruct((slots, rows, w), src.dtype),
        scratch_shapes=[pltpu.SemaphoreType.DMA((n,)), pltpu.SemaphoreType.DMA((n,)), pltpu.SemaphoreType.DMA],
        )(src)


ALL_PEERS = tuple(range(1, N_DEV))
SIBLING = (1,)


def _pack(arrays, rows):
    flat = jnp.concatenate([a.reshape(-1) for a in arrays])
    return jnp.pad(flat, (0, rows * D_MODEL - flat.shape[0])).reshape(rows, D_MODEL)


def _unpack(buf, shapes):
    flat = buf.reshape(-1)
    out, off = [], 0
    for s in shapes:
        n = math.prod(s)
        out.append(flat[off:off + n].reshape(s))
        off += n
    return out


def _rows_for(shapes, mult):
    n = sum(math.prod(s) for s in shapes)
    rows = -(-n // D_MODEL)
    return -(-rows // mult) * mult


SHARD_AXIS = dict(w_mem_kv=1, w_out=1, w_gate_up=2, w_down=1, gdn_w_in=2, swa_w_q=1, w_kv=0, gdn_conv=2)


def _split4(a, axis):
    return jnp.split(a, 4, axis=axis)


def _rope_tables(positions):
    half = ROT_DIM // 2
    inv = ROPE_THETA ** (-jnp.arange(0, ROT_DIM, 2, dtype=F32) / ROT_DIM)
    ang = positions.astype(F32)[:, None] * inv
    cos, sin = jnp.cos(ang), jnp.sin(ang)
    rows = positions.shape[0]
    one = jnp.ones((rows, SWA_DH - ROT_DIM), F32)
    zero = jnp.zeros((rows, SWA_DH - ROT_DIM), F32)
    zh = jnp.zeros((rows, half), F32)
    c64 = jnp.concatenate([cos, cos, one], axis=1)
    a64 = jnp.concatenate([-sin, zh, zero], axis=1)
    b64 = jnp.concatenate([zh, sin, zero], axis=1)
    return tuple(jnp.concatenate([t, t], axis=1) for t in (c64, a64, b64))


def _repeat_heads(t):
    rows = t.shape[0]
    t = t.reshape(rows, SWA_KV_HEADS, 1, SWA_DH)
    return jnp.broadcast_to(t, (rows, SWA_KV_HEADS, SWA_GROUP, SWA_DH)).reshape(rows, SWA_HEADS * SWA_DH)


def _fold_heads(t):
    rows = t.shape[0]
    return t.reshape(rows, SWA_KV_HEADS, SWA_GROUP, SWA_DH).sum(axis=2).reshape(rows, SWA_KV_HEADS * SWA_DH)


def _gdn_in_pad(w):
    o2 = 4 * GDN_W
    pad = jnp.zeros((w.shape[0], GDN_IN_PAD - GDN_IN), w.dtype)
    return jnp.concatenate([w[:, :o2], w[:, o2 + 2 * GDN_HEADS:], w[:, o2:o2 + 2 * GDN_HEADS], pad], axis=1)


def _gdn_in_unpad(w):
    o2 = 4 * GDN_W
    return jnp.concatenate([w[:, :o2], w[:, o2 + MEM_W:o2 + MEM_W + 2 * GDN_HEADS], w[:, o2:o2 + MEM_W]], axis=1)


def _head_rows(v):
    return jnp.repeat(v.astype(F32), LANES)[None, :]


def _selectors():
    lane = jnp.arange(LANES)[:, None]
    head = (jnp.arange(GDN_W) // LANES)[None, :]
    return (lane == head).astype(F32), (lane == head + GDN_HEADS).astype(F32)


def _local_step(x, mem, positions, target, w):
    rows = x.shape[0]
    tabs = _rope_tables(positions)
    sel_b, sel_a = _selectors()
    row2 = lambda v: v.reshape(1, -1).astype(F32)

    mem_n = _rms_fwd(mem, row2(w["ln_mem"]), "mem_norm")
    saved = []
    kt = vt = None
    for l in range(DEPTH):
        s = dict(x0=x)
        h = _rms_fwd(x, row2(w["ln_mix"][l]), f"norm_mix{l}")
        mkv = _mm(mem_n, w["w_mem_kv"][l], "nn", f"mem_kv{l}")
        kbd, vbd = _mem_expand(mkv)
        if l < N_A:
            proj = _mm(h, w["gdn_w_in"][l], "nn", f"gdn_in{l}")
            a_row, dt_row = _head_rows(w["gdn_A_log"][l]), _head_rows(w["gdn_dt_bias"][l])
            q, k, v, g, beta = _gdn_pre_fwd(proj, w["gdn_conv"][l], sel_b, sel_a, a_row, dt_row, f"gdn_pre{l}")
            o, states = _gdn_fwd(q, k, v, g, beta, f"gdn_scan{l}")
            mix = _gdn_post_fwd(o, proj, row2(w["gdn_norm"][l]), f"gdn_post{l}")
            mq_cb = (3 * GDN_W + GDN_W) // MEM_W
            s.update(q=q, k=k, v=v, g=g, beta=beta, o=o, states=states, a_row=a_row, dt_row=dt_row)
        else:
            proj = _mm(h, w["swa_w_q"][l - N_A], "nn", f"swa_in{l}")
            qr = _rope(proj, SWA_HEADS * SWA_DH, 0, tabs, 1, f"rope_q{l}", BF16)
            mix = _swa_fwd(qr, kt, vt, w["swa_sinks"][l - N_A], f"swa{l}")
            mq_cb = (SWA_HEADS * SWA_DH) // MEM_W
            s.update(qr=qr)
        mem_o = _mem_fwd(proj, mq_cb, kbd, vbd, f"mem_attn{l}")
        cat = jnp.concatenate([mix, mem_o], axis=1)
        x1 = _mm(cat, w["w_out"][l], "nn", f"out_proj{l}", add=x)
        h2 = _rms_fwd(x1, row2(w["ln_ffn"][l]), f"norm_ffn{l}")
        gu = _mm(h2, w["w_gate_up"][l], "nn", f"gate_up{l}")
        act = _swiglu_fwd(gu, f"swiglu{l}")
        x = _mm(act, w["w_down"][l], "nn", f"down{l}", add=x1)
        s.update(h=h, proj=proj, kbd=kbd, vbd=vbd, mq_cb=mq_cb, cat=cat, x1=x1, h2=h2, gu=gu, act=act)
        saved.append(s)
        if l == N_A - 1:
            x_kv = x
            h_kv = _rms_fwd(x, row2(w["ln_kv"]), "norm_kv")
            kv = _mm(h_kv, w["w_kv"], "nn", "kv_proj")
            kr = _rope(kv, LANES, 0, tabs, 1, "rope_k", F32)
            kt = _repeat_heads(kr).astype(BF16)
            vt = _repeat_heads(kv[:, LANES:]).astype(BF16)

    gr = {}
    dx, dxb, loss_part, dlnf = _final_loss(x, row2(w["ln_final"]), target, "final_loss")
    gr["ln_final"] = dlnf.sum(axis=0)
    dln_mix, dln_ffn = [None] * DEPTH, [None] * DEPTH
    dw_mem_kv, dw_out, dw_gu, dw_dn = [None] * DEPTH, [None] * DEPTH, [None] * DEPTH, [None] * DEPTH
    dgdn_in, dgdn_conv, dgdn_a, dgdn_dt, dgdn_norm = [None] * N_A, [None] * N_A, [None] * N_A, [None] * N_A, [None] * N_A
    dswa_q, dswa_sinks = [None] * N_B, [None] * N_B
    dmem_n = None
    dkt = dvt = None
    for l in reversed(range(DEPTH)):
        s = saved[l]
        if l == N_A - 1:
            dkr = _fold_heads(dkt)
            dk = _rope(dkr, LANES, 0, tabs, -1, "rope_k_bwd", BF16)
            dkv = jnp.concatenate([dk, _fold_heads(dvt).astype(BF16)], axis=1)
            dh_kv = _mm(dkv, w["w_kv"], "nt", "kv_proj_dx")
            gr["w_kv"] = _mm(h_kv, dkv, "tn", "kv_proj_dw", BF16)
            dx, dxb, dg = _rms_bwd(x_kv, row2(w["ln_kv"]), dh_kv, dx, "norm_kv_bwd")
            gr["ln_kv"] = dg.sum(axis=0)
        dact = _mm(dxb, w["w_down"][l], "nt", f"down_dx{l}", BF16)
        dw_dn[l] = _mm(s["act"], dxb, "tn", f"down_dw{l}", BF16)
        dgu = _swiglu_bwd(s["gu"], dact, f"swiglu_bwd{l}")
        dh2 = _mm(dgu, w["w_gate_up"][l], "nt", f"gate_up_dx{l}")
        dw_gu[l] = _mm(s["h2"], dgu, "tn", f"gate_up_dw{l}", BF16)
        dx, dxb, dg = _rms_bwd(s["x1"], row2(w["ln_ffn"][l]), dh2, dx, f"norm_ffn_bwd{l}")
        dln_ffn[l] = dg.sum(axis=0)
        dcat = _mm(dxb, w["w_out"][l], "nt", f"out_proj_dx{l}", BF16)
        dw_out[l] = _mm(s["cat"], dxb, "tn", f"out_proj_dw{l}", BF16)
        dmq, dkbd, dvbd = _mem_bwd(s["proj"], s["mq_cb"], s["kbd"], s["vbd"], dcat, f"mem_attn_bwd{l}")
        dmkv = _mem_collapse(dkbd, dvbd).astype(BF16)
        dw_mem_kv[l] = _mm(mem_n, dmkv, "tn", f"mem_kv_dw{l}", BF16)
        dmem_n = _mm(dmkv, w["w_mem_kv"][l], "nt", f"mem_kv_dx{l}", add=dmem_n)
        if l < N_A:
            do, dz, dng = _gdn_post_bwd(s["o"], s["proj"], row2(w["gdn_norm"][l]), dcat, f"gdn_post_bwd{l}")
            dq, dk, dv, dg_, dbeta = _gdn_bwd(s["q"], s["k"], s["v"], s["g"], s["beta"], s["states"], do, f"gdn_scan_bwd{l}")
            res = _gdn_pre_bwd(s["proj"], w["gdn_conv"][l], sel_b, sel_a, s["a_row"], s["dt_row"], dq, dk, dv, dg_, dbeta,
                               f"gdn_pre_bwd{l}")
            dconv, dba = res[0], res[1]
            dgdn_conv[l] = jnp.stack([r.sum(axis=0) for r in res[2:2 + CONV_K]])
            dgdn_a[l] = res[2 + CONV_K].sum(axis=0)[::LANES]
            dgdn_dt[l] = res[3 + CONV_K].sum(axis=0)[::LANES]
            dgdn_norm[l] = dng.sum(axis=0)
            dqkv = _conv_bwd_input(dconv, w["gdn_conv"][l], f"gdn_conv_bwd{l}")
            dproj = jnp.concatenate([dqkv, dz, dmq, dba], axis=1)
            dh = _mm(dproj, w["gdn_w_in"][l], "nt", f"gdn_in_dx{l}")
            dgdn_in[l] = _mm(s["h"], dproj, "tn", f"gdn_in_dw{l}", BF16)
        else:
            b = l - N_A
            dqr, dkt_l, dvt_l, dsk = _swa_bwd(s["qr"], kt, vt, w["swa_sinks"][b], dcat, f"swa_bwd{l}")
            dkt = dkt_l if dkt is None else dkt + dkt_l
            dvt = dvt_l if dvt is None else dvt + dvt_l
            dswa_sinks[b] = dsk[0, :SWA_HEADS]
            dq = _rope(dqr, SWA_HEADS * SWA_DH, 0, tabs, -1, f"rope_q_bwd{l}", BF16)
            dproj = jnp.concatenate([dq, dmq], axis=1)
            dh = _mm(dproj, w["swa_w_q"][b], "nt", f"swa_in_dx{l}")
            dswa_q[b] = _mm(s["h"], dproj, "tn", f"swa_in_dw{l}", BF16)
        dx, dxb, dg = _rms_bwd(s["x0"], row2(w["ln_mix"][l]), dh, dx, f"norm_mix_bwd{l}")
        dln_mix[l] = dg.sum(axis=0)
    _, _, dg = _rms_bwd(mem, row2(w["ln_mem"]), dmem_n, None, "mem_norm_bwd")
    gr["ln_mem"] = dg.sum(axis=0)
    gr.update(ln_mix=jnp.stack(dln_mix), ln_ffn=jnp.stack(dln_ffn), w_mem_kv=jnp.stack(dw_mem_kv), w_out=jnp.stack(dw_out),
              w_gate_up=jnp.stack(dw_gu), w_down=jnp.stack(dw_dn), gdn_w_in=jnp.stack(dgdn_in), gdn_conv=jnp.stack(dgdn_conv),
              gdn_A_log=jnp.stack(dgdn_a), gdn_dt_bias=jnp.stack(dgdn_dt), gdn_norm=jnp.stack(dgdn_norm),
              swa_w_q=jnp.stack(dswa_q), swa_sinks=jnp.stack(dswa_sinks))
    return loss_part, dx, gr


def kernel(x, mem, positions, ln_mix, ln_ffn, ln_mem, w_mem_kv, w_out, w_gate_up, w_down, gdn_w_in, gdn_conv, gdn_A_log, gdn_dt_bias, gdn_norm, swa_w_q, swa_sinks, ln_kv, w_kv, ln_final, loss_target, m_ln_mix, m_ln_ffn, m_ln_mem, m_w_mem_kv, m_w_out, m_w_gate_up, m_w_down, m_gdn_w_in, m_gdn_conv, m_gdn_A_log, m_gdn_dt_bias, m_gdn_norm, m_swa_w_q, m_swa_sinks, m_ln_kv, m_w_kv, m_ln_final, v_ln_mix, v_ln_ffn, v_ln_mem, v_w_mem_kv, v_w_out, v_w_gate_up, v_w_down, v_gdn_w_in, v_gdn_conv, v_gdn_A_log, v_gdn_dt_bias, v_gdn_norm, v_swa_w_q, v_swa_sinks, v_ln_kv, v_w_kv, v_ln_final):
    given = dict(locals())
    wts = {n: given[n] for n in WEIGHTS}
    c = lax.axis_index("c")

    shard_shapes = [wts[n].shape for n in SHARDED]
    rows_w = _rows_for(shard_shapes, 512)
    half = rows_w // 2
    wpack = _pack([wts[n] for n in SHARDED], rows_w)
    my_half = lax.dynamic_slice_in_dim(wpack.astype(BF16), c * half, half, axis=0)
    gathered = _exchange(my_half, ALL_PEERS, "gather_weights").reshape(4, rows_w, D_MODEL)
    conv_shape = wts["gdn_conv"].shape
    cpack = _pack([wts["gdn_conv"]], 16)
    conv_half = lax.dynamic_slice_in_dim(cpack, c * SUBLANES, SUBLANES, axis=0)
    conv_all = _exchange(conv_half, ALL_PEERS, "gather_conv").reshape(4, 16, D_MODEL)
    full = {n: wts[n] for n in SMALL}
    per_chip = [_unpack(gathered[s], shard_shapes) for s in range(4)]
    for i, n in enumerate(SHARDED):
        full[n] = jnp.concatenate([per_chip[s][i] for s in range(4)], axis=SHARD_AXIS[n])
    full["gdn_conv"] = jnp.concatenate([_unpack(conv_all[s], [conv_shape])[0] for s in range(4)], axis=2)
    full["gdn_w_in"] = jnp.stack([_gdn_in_pad(full["gdn_w_in"][a]) for a in range(N_A)])

    loss_part, dx, gr = _local_step(x[0], mem[0], positions[0], loss_target[0], full)
    gr["gdn_w_in"] = jnp.stack([_gdn_in_unpad(gr["gdn_w_in"][a]) for a in range(N_A)])

    pieces = []
    for s in range(4):
        pieces.append(_pack([_split4(gr[n], SHARD_AXIS[n])[s].astype(BF16) for n in SHARDED], rows_w))
    gpack = jnp.stack(pieces).reshape(N_DEV, half, D_MODEL)
    parts = _exchange(gpack, ALL_PEERS, "scatter_grads")
    mine = _sum_slots(parts, "sum_grads")
    g_shard = _exchange(mine, SIBLING, "swap_grad_halves").reshape(rows_w, D_MODEL)

    small_shapes = [wts[n].shape for n in SMALL] + [conv_shape[:2] + (4 * conv_shape[2],), (SUBLANES, LANES)]
    rows_s = _rows_for(small_shapes, SUBLANES)
    spack = _pack([gr[n] for n in SMALL] + [gr["gdn_conv"], loss_part], rows_s)
    sparts = _exchange(spack, ALL_PEERS, "share_small")
    ssum = _unpack(_sum_slots(sparts, "sum_small"), small_shapes)
    g_small = dict(zip(SMALL, ssum[:len(SMALL)]))
    chip = 2 * lax.axis_index("x") + lax.axis_index("y")
    g_conv = lax.dynamic_slice_in_dim(ssum[len(SMALL)], chip * conv_shape[2], conv_shape[2], axis=2)
    loss = jnp.sum(ssum[-1])

    m_pack = _pack([given["m_" + n] for n in SHARDED], rows_w)
    v_pack = _pack([given["v_" + n] for n in SHARDED], rows_w)
    d_b, m_b, v_b = _adamw(wpack, g_shard, m_pack, v_pack, "adamw_shards")
    small_names = SMALL + ("gdn_conv",)
    small_w_shapes = [wts[n].shape for n in small_names]
    rows_a = _rows_for(small_w_shapes, SUBLANES)
    g_small["gdn_conv"] = g_conv
    d_s, m_s, v_s = _adamw(_pack([wts[n] for n in small_names], rows_a), _pack([g_small[n] for n in small_names], rows_a),
                           _pack([given["m_" + n] for n in small_names], rows_a),
                           _pack([given["v_" + n] for n in small_names], rows_a), "adamw_small")
    out = {}
    for kind, big, small in (("grad", g_shard, None), ("delta", d_b, d_s), ("new_m", m_b, m_s), ("new_v", v_b, v_s)):
        out[kind] = dict(zip(SHARDED, _unpack(big, shard_shapes)))
        if small is None:
            out[kind].update(g_small)
        else:
            out[kind].update(zip(small_names, _unpack(small, small_w_shapes)))
    return (loss, dx[None], *[out["grad"][n] for n in WEIGHTS], *[out["delta"][n] for n in WEIGHTS],
            *[out["new_m"][n] for n in WEIGHTS], *[out["new_v"][n] for n in WEIGHTS])
```

```python
import functools
import math

import jax
import jax.numpy as jnp
from jax import lax
from jax.experimental import pallas as pl
from jax.experimental.pallas import tpu as pltpu

F32 = jnp.float32
BF16 = jnp.bfloat16
HI = lax.Precision.HIGHEST
MESH = pl.DeviceIdType.MESH

D_MODEL = 1024
DEPTH = 4
N_A = 2
N_B = 2
EPS = 1e-6
GDN_HEADS = 6
GDN_DK = 128
GDN_W = 768
CONV_K = 4
CHUNK = 64
SWA_HEADS = 12
SWA_KV_HEADS = 2
SWA_DH = 64
SWA_GROUP = 6
SWA_GW = SWA_GROUP * SWA_DH
SWA_BLOCK = 128
ROPE_THETA = 500000.0
ROT_DIM = 16
MEM_LEN = 256
MEM_HEADS = 4
MEM_DH = 64
MEM_W = 256
D_FF = 2816
GDN_IN = 3340
GDN_IN_PAD = 3456
ADAM_LR = 0.001
ADAM_B1 = 0.9
ADAM_B2 = 0.999
ADAM_EPS = 1e-08
ADAM_WD = 0.01
ADAM_STEP = 10

N_DEV = 8
LANES = 128
SUBLANES = 8
V7X_VMEM_LIMIT = 56 * 2**20

SHARDED = ("w_mem_kv", "w_out", "w_gate_up", "w_down", "gdn_w_in", "swa_w_q", "w_kv")
SMALL = ("ln_mix", "ln_ffn", "ln_mem", "gdn_A_log", "gdn_dt_bias", "gdn_norm", "swa_sinks", "ln_kv", "ln_final")
WEIGHTS = ("ln_mix", "ln_ffn", "ln_mem", "w_mem_kv", "w_out", "w_gate_up", "w_down", "gdn_w_in", "gdn_conv",
           "gdn_A_log", "gdn_dt_bias", "gdn_norm", "swa_w_q", "swa_sinks", "ln_kv", "w_kv", "ln_final")


def _params(sem=None, **kw):
    return pltpu.CompilerParams(dimension_semantics=sem, vmem_limit_bytes=V7X_VMEM_LIMIT, **kw)


def _dot(a, b, dims=(((1,), (0,)), ((), ())), precision=None):
    return lax.dot_general(a, b, dims, precision=precision, preferred_element_type=F32)


NT = (((1,), (1,)), ((), ()))
TN = (((0,), (0,)), ((), ()))


def _fold8(v):
    r, w = v.shape
    return v.reshape(r // SUBLANES, SUBLANES, w).sum(axis=0)


def _row(a, w=None, cb=0):
    return ("row", a, a.shape[1] if w is None else w, cb)


def _full(a):
    return ("full", a, None, None)


def _prev8(a, w, cb=0):
    return ("prev8", a, w, cb)


def _next8(a, w, cb=0):
    return ("next8", a, w, cb)


def _rowcall(fn, name, rows, bm, ins, outs, accs=()):
    bm = min(bm, rows)
    assert rows % bm == 0 and bm % SUBLANES == 0
    steps = rows // bm
    r8 = bm // SUBLANES
    in_specs, arrays = [], []
    for kind, a, w, cb in ins:
        arrays.append(a)
        if kind == "row":
            in_specs.append(pl.BlockSpec((bm, w), lambda i, cb=cb: (i, cb)))
        elif kind == "full":
            in_specs.append(pl.BlockSpec(a.shape, lambda i, nd=a.ndim: (0,) * nd))
        elif kind == "prev8":
            in_specs.append(pl.BlockSpec((SUBLANES, w), lambda i, cb=cb: (jnp.maximum(i * r8 - 1, 0), cb)))
        else:
            last = rows // SUBLANES - 1
            in_specs.append(pl.BlockSpec((SUBLANES, w), lambda i, cb=cb: (jnp.minimum((i + 1) * r8, last), cb)))
    out_shape = [jax.ShapeDtypeStruct((rows, w), dt) for w, dt in outs]
    out_specs = [pl.BlockSpec((bm, w), lambda i: (i, 0)) for w, _ in outs]
    out_shape += [jax.ShapeDtypeStruct(s, F32) for s in accs]
    out_specs += [pl.BlockSpec(s, lambda i: (0, 0)) for s in accs]
    n_in, n_out = len(ins), len(outs)

    def body(*refs):
        i = pl.program_id(0)
        res = fn(i, *[r[...] for r in refs[:n_in]])
        if not isinstance(res, (tuple, list)):
            res = (res,)
        for r, v in zip(refs[n_in:n_in + n_out], res[:n_out]):
            r[...] = v.astype(r.dtype)
        if accs:
            @pl.when(i == 0)
            def _():
                for r in refs[n_in + n_out:]:
                    r[...] = jnp.zeros(r.shape, F32)
            for r, v in zip(refs[n_in + n_out:], res[n_out:]):
                r[...] += v

    res = pl.pallas_call(
        body, name=name, grid=(steps,), in_specs=in_specs, out_specs=out_specs, out_shape=out_shape,
        compiler_params=_params(("arbitrary",)))(*arrays)
    return res


def _tile(n, cap):
    for t in (1408, 1152, 1024, 896, 768, 640, 512, 384, 256, 128):
        if t <= cap and n % t == 0:
            return t
    return n


def _mm(a, b, mode, name, out_dtype=F32, add=None):
    if mode == "tn":
        s, m = a.shape
        n = b.shape[1]
        bm, bn, bk = _tile(m, 1408), _tile(n, 512), min(s, 2048)
        nk = s // bk

        def body(a_ref, b_ref, o_ref, acc_ref):
            k = pl.program_id(2)

            @pl.when(k == 0)
            def _():
                acc_ref[...] = jnp.zeros(acc_ref.shape, F32)
            acc_ref[...] += _dot(a_ref[...].astype(BF16), b_ref[...].astype(BF16), TN)

            @pl.when(k == nk - 1)
            def _():
                o_ref[...] = acc_ref[...].astype(o_ref.dtype)

        return pl.pallas_call(
            body, name=name, grid=(m // bm, n // bn, nk),
            in_specs=[pl.BlockSpec((bk, bm), lambda i, j, k: (k, i)), pl.BlockSpec((bk, bn), lambda i, j, k: (k, j))],
            out_specs=pl.BlockSpec((bm, bn), lambda i, j, k: (i, j)),
            out_shape=jax.ShapeDtypeStruct((m, n), out_dtype),
            scratch_shapes=[pltpu.VMEM((bm, bn), F32)],
            compiler_params=_params(("parallel", "parallel", "arbitrary")))(a, b)

    m, k = a.shape
    n = b.shape[1] if mode == "nn" else b.shape[0]
    big = k > 2048
    bm = min(m, 512 if big else 1024)
    bn = _tile(n, 512 if big else 1024)
    dims = NT if mode == "nt" else (((1,), (0,)), ((), ()))
    b_spec = (pl.BlockSpec((k, bn), lambda i, j: (0, j)) if mode == "nn" else pl.BlockSpec((bn, k), lambda i, j: (j, 0)))
    in_specs = [pl.BlockSpec((bm, k), lambda i, j: (i, 0)), b_spec]
    args = [a, b]
    if add is not None:
        in_specs.append(pl.BlockSpec((bm, bn), lambda i, j: (i, j)))
        args.append(add)

    def body(a_ref, b_ref, *rest):
        o_ref = rest[-1]
        acc = _dot(a_ref[...].astype(BF16), b_ref[...].astype(BF16), dims)
        if add is not None:
            acc = acc + rest[0][...]
        o_ref[...] = acc.astype(o_ref.dtype)

    return pl.pallas_call(
        body, name=name, grid=(m // bm, n // bn), in_specs=in_specs,
        out_specs=pl.BlockSpec((bm, bn), lambda i, j: (i, j)),
        out_shape=jax.ShapeDtypeStruct((m, n), out_dtype),
        compiler_params=_params(("parallel", "parallel")))(*args)


def _sigmoid(x):
    return 1.0 / (1.0 + jnp.exp(-x))


def _softplus(x):
    return jnp.maximum(x, 0.0) + jnp.log(1.0 + jnp.exp(-jnp.abs(x)))


def _silu_and_grad(x):
    s = _sigmoid(x)
    return x * s, s * (1.0 + x * (1.0 - s))


def _rms_stats(x):
    r = lax.rsqrt(jnp.mean(x * x, axis=-1, keepdims=True) + EPS)
    return r, x * r


def _rms_fwd(x, g, name, out_dtype=BF16, bm=512):
    def fn(i, x, g):
        _, xn = _rms_stats(x)
        return xn * g
    return _rowcall(fn, name, x.shape[0], bm, [_row(x), _full(g)], [(x.shape[1], out_dtype)])[0]


def _rms_bwd_math(x, g, dy):
    r, xn = _rms_stats(x)
    dxn = dy * g
    dx = r * (dxn - xn * jnp.mean(dxn * xn, axis=-1, keepdims=True))
    return dx, dy * xn


def _rms_bwd(x, g, dy, res, name, bm=256):
    d = x.shape[1]

    def fn(i, x, g, dy, *res_):
        dx, dg = _rms_bwd_math(x, g, dy.astype(F32))
        if res_:
            dx = dx + res_[0]
        return dx, dx, _fold8(dg)
    ins = [_row(x), _full(g), _row(dy)] + ([_row(res)] if res is not None else [])
    return _rowcall(fn, name, x.shape[0], bm, ins, [(d, F32), (d, BF16)], [(SUBLANES, d)])


def _final_loss(x, g, target, name, bm=256):
    d = x.shape[1]

    def fn(i, x, g, t):
        r, xn = _rms_stats(x)
        err = xn * g - t
        dy = err * (1.0 / d)
        dxn = dy * g
        dx = r * (dxn - xn * jnp.mean(dxn * xn, axis=-1, keepdims=True))
        e2 = _fold8(err * err)
        lp = e2[:, 0:LANES]
        for c in range(1, d // LANES):
            lp = lp + e2[:, c * LANES:(c + 1) * LANES]
        return dx, dx, lp * (0.5 / d), _fold8(dy * xn)
    return _rowcall(fn, name, x.shape[0], bm, [_row(x), _full(g), _row(target)], [(d, F32), (d, BF16)],
                    [(SUBLANES, LANES), (SUBLANES, d)])


FF_TILE = 256


def _ff_interleave(w):
    lead = w.shape[:-1]
    w = w.reshape(lead + (2, D_FF // FF_TILE, FF_TILE))
    return jnp.swapaxes(w, -3, -2).reshape(lead + (2 * D_FF,))


def _ff_deinterleave(w):
    lead = w.shape[:-1]
    w = w.reshape(lead + (D_FF // FF_TILE, 2, FF_TILE))
    return jnp.swapaxes(w, -3, -2).reshape(lead + (2 * D_FF,))


def _gate_up_fwd(h, w_gu, name, bm=1024):
    rows, k = h.shape
    bm = min(bm, rows)

    def body(a_ref, b_ref, gu_ref, act_ref):
        acc = _dot(a_ref[...], b_ref[...])
        gu_ref[...] = acc.astype(gu_ref.dtype)
        act_ref[...] = (_silu_and_grad(acc[:, :FF_TILE])[0] * acc[:, FF_TILE:]).astype(act_ref.dtype)

    return pl.pallas_call(
        body, name=name, grid=(rows // bm, D_FF // FF_TILE),
        in_specs=[pl.BlockSpec((bm, k), lambda i, j: (i, 0)), pl.BlockSpec((k, 2 * FF_TILE), lambda i, j: (0, j))],
        out_specs=[pl.BlockSpec((bm, 2 * FF_TILE), lambda i, j: (i, j)), pl.BlockSpec((bm, FF_TILE), lambda i, j: (i, j))],
        out_shape=[jax.ShapeDtypeStruct((rows, 2 * D_FF), BF16), jax.ShapeDtypeStruct((rows, D_FF), BF16)],
        compiler_params=_params(("parallel", "parallel")))(h, w_gu)


def _down_bwd(dx, w_down, gu, name, bm=1024):
    rows, k = dx.shape
    bm = min(bm, rows)

    def body(a_ref, b_ref, gu_ref, o_ref):
        da = _dot(a_ref[...], b_ref[...], NT)
        gu = gu_ref[...].astype(F32)
        s, ds = _silu_and_grad(gu[:, :FF_TILE])
        o_ref[:, :FF_TILE] = (da * gu[:, FF_TILE:] * ds).astype(o_ref.dtype)
        o_ref[:, FF_TILE:] = (da * s).astype(o_ref.dtype)

    return pl.pallas_call(
        body, name=name, grid=(rows // bm, D_FF // FF_TILE),
        in_specs=[pl.BlockSpec((bm, k), lambda i, j: (i, 0)), pl.BlockSpec((FF_TILE, k), lambda i, j: (j, 0)),
                  pl.BlockSpec((bm, 2 * FF_TILE), lambda i, j: (i, j))],
        out_specs=pl.BlockSpec((bm, 2 * FF_TILE), lambda i, j: (i, j)),
        out_shape=jax.ShapeDtypeStruct((rows, 2 * D_FF), BF16),
        compiler_params=_params(("parallel", "parallel")))(dx, w_down, gu)


def _rope_apply(x, tabs, sign):
    cos, ta, tb = tabs
    outs = []
    for c in range(x.shape[1] // LANES):
        xc = x[:, c * LANES:(c + 1) * LANES]
        if sign > 0:
            o = xc * cos + pltpu.roll(xc, LANES - 8, 1) * ta + pltpu.roll(xc, 8, 1) * tb
        else:
            o = xc * cos + pltpu.roll(xc * ta, 8, 1) + pltpu.roll(xc * tb, LANES - 8, 1)
        outs.append(o)
    return outs[0] if len(outs) == 1 else jnp.concatenate(outs, axis=1)


def _rope(x, w, cb, tabs, sign, name, out_dtype, bm=512):
    def fn(i, x, c, a, b):
        return _rope_apply(x.astype(F32), (c, a, b), sign)
    return _rowcall(fn, name, x.shape[0], bm, [_row(x, w, cb)] + [_row(t) for t in tabs], [(w, out_dtype)])[0]


def _shift_down(x, prev8, s, first):
    xs = pltpu.roll(x, s, 0)
    rp = pltpu.roll(prev8, s, 0) * jnp.where(first, 0.0, 1.0)
    rid = lax.broadcasted_iota(jnp.int32, rp.shape, 0)
    top = jnp.where(rid < s, rp, xs[0:SUBLANES])
    return jnp.concatenate([top, xs[SUBLANES:]], axis=0)


def _shift_up(x, next8, s, last):
    n = x.shape[0]
    xs = pltpu.roll(x, n - s, 0)
    rn = pltpu.roll(next8, SUBLANES - s, 0) * jnp.where(last, 0.0, 1.0)
    rid = lax.broadcasted_iota(jnp.int32, rn.shape, 0)
    bot = jnp.where(rid >= SUBLANES - s, rn, xs[n - SUBLANES:])
    return jnp.concatenate([xs[:n - SUBLANES], bot], axis=0)


def _conv_fwd(x, prev8, w, first):
    acc = x * w[CONV_K - 1:CONV_K]
    shifted = []
    for s in range(1, CONV_K):
        xs = _shift_down(x, prev8, s, first)
        shifted.append(xs)
        acc = acc + xs * w[CONV_K - 1 - s:CONV_K - s]
    return acc, shifted


def _l2n(x):
    outs, rs = [], []
    for h in range(x.shape[1] // LANES):
        xh = x[:, h * LANES:(h + 1) * LANES]
        r = lax.rsqrt(jnp.sum(xh * xh, axis=-1, keepdims=True) + EPS)
        outs.append(xh * r)
        rs.append(r)
    return jnp.concatenate(outs, axis=1), rs


def _gate_math(ba, sel_b, sel_a, a_row, dt_row):
    bl = _dot(ba, sel_b, precision=HI)
    al = _dot(ba, sel_a, precision=HI) + dt_row
    beta = _sigmoid(bl)
    ea = jnp.exp(a_row)
    g = -ea * _softplus(al)
    return bl, al, beta, g, ea


def _cumsum_chunks(x, reverse=False):
    n = x.shape[0]
    rid = lax.broadcasted_iota(jnp.int32, x.shape, 0) % CHUNK
    s = 1
    while s < CHUNK:
        if reverse:
            x = x + jnp.where(rid < CHUNK - s, pltpu.roll(x, n - s, 0), 0.0)
        else:
            x = x + jnp.where(rid >= s, pltpu.roll(x, s, 0), 0.0)
        s *= 2
    return x


def _gdn_pre_fwd(proj, conv_w, sel_b, sel_a, a_row, dt_row, name, bm=256):
    rows = proj.shape[0]
    w3 = 3 * GDN_W

    def fn(i, x, p8, ba, w, sel_b, sel_a, a_row, dt_row):
        conv, _ = _conv_fwd(x, p8, w, i == 0)
        act = _silu_and_grad(conv)[0]
        qk, _ = _l2n(act[:, :2 * GDN_W])
        _, _, beta, g, _ = _gate_math(ba, sel_b, sel_a, a_row, dt_row)
        return qk[:, :GDN_W], qk[:, GDN_W:], act[:, 2 * GDN_W:], _cumsum_chunks(g), beta
    ins = [_row(proj, w3, 0), _prev8(proj, w3, 0), _row(proj, LANES, (GDN_IN_PAD - LANES) // LANES),
           _full(conv_w), _full(sel_b), _full(sel_a), _full(a_row), _full(dt_row)]
    return _rowcall(fn, name, rows, bm, ins, [(GDN_W, F32)] * 5)


def _gdn_pre_bwd(proj, conv_w, sel_b, sel_a, a_row, dt_row, dq, dk, dv, dgc, dbeta, name, bm=128):
    rows = proj.shape[0]
    w3 = 3 * GDN_W

    def fn(i, x, p8, ba, w, sel_b, sel_a, a_row, dt_row, dq, dk, dv, dgc, dbeta):
        dg = _cumsum_chunks(dgc, reverse=True)
        conv, shifted = _conv_fwd(x, p8, w, i == 0)
        act, dact = _silu_and_grad(conv)
        qk, rs = _l2n(act[:, :2 * GDN_W])
        dqk = jnp.concatenate([dq, dk], axis=1)
        parts = []
        for h in range(2 * GDN_HEADS):
            sl = slice(h * LANES, (h + 1) * LANES)
            y, dy = qk[:, sl], dqk[:, sl]
            parts.append(rs[h] * (dy - y * jnp.sum(y * dy, axis=-1, keepdims=True)))
        dconv = jnp.concatenate(parts + [dv], axis=1) * dact
        dws = [_fold8(dconv * xs) for xs in reversed(shifted)] + [_fold8(dconv * x)]
        bl, al, beta, g, ea = _gate_math(ba, sel_b, sel_a, a_row, dt_row)
        dbl = dbeta * beta * (1.0 - beta)
        dal = dg * (-ea) * _sigmoid(al)
        dba = _dot(dbl, sel_b, NT, precision=HI) + _dot(dal, sel_a, NT, precision=HI)
        return (dconv, dba * (1.0 / LANES)) + tuple(dws) + (_fold8(dg * g), _fold8(dal))
    ins = [_row(proj, w3, 0), _prev8(proj, w3, 0), _row(proj, LANES, (GDN_IN_PAD - LANES) // LANES),
           _full(conv_w), _full(sel_b), _full(sel_a), _full(a_row), _full(dt_row),
           _row(dq), _row(dk), _row(dv), _row(dgc), _row(dbeta)]
    return _rowcall(fn, name, rows, bm, ins, [(w3, F32), (LANES, BF16)],
                    [(SUBLANES, w3)] * CONV_K + [(SUBLANES, GDN_W)] * 2)


def _conv_bwd_input(dconv, conv_w, name, bm=256):
    rows, w3 = dconv.shape
    steps = rows // min(bm, rows)

    def fn(i, dc, n8, w):
        acc = dc * w[CONV_K - 1:CONV_K]
        for s in range(1, CONV_K):
            acc = acc + _shift_up(dc, n8, s, i == steps - 1) * w[CONV_K - 1 - s:CONV_K - s]
        return acc
    return _rowcall(fn, name, rows, bm, [_row(dconv), _next8(dconv, w3, 0), _full(conv_w)], [(w3, BF16)])[0]


def _gdn_post_fwd(o, proj, ng, name, bm=512):
    def fn(i, o, z, ng):
        outs = []
        for h in range(GDN_HEADS):
            sl = slice(h * LANES, (h + 1) * LANES)
            _, on = _rms_stats(o[:, sl])
            outs.append(on * ng * _silu_and_grad(z[:, sl])[0])
        return jnp.concatenate(outs, axis=1)
    return _rowcall(fn, name, o.shape[0], bm, [_row(o), _row(proj, GDN_W, 3), _full(ng)], [(GDN_W, BF16)])[0]


def _gdn_post_bwd(o, proj, ng, dcat, name, bm=256):
    def fn(i, o, z, ng, dm):
        dm = dm.astype(F32)
        dos, dzs = [], []
        dng = jnp.zeros((SUBLANES, LANES), F32)
        for h in range(GDN_HEADS):
            sl = slice(h * LANES, (h + 1) * LANES)
            s, ds = _silu_and_grad(z[:, sl])
            r, on = _rms_stats(o[:, sl])
            dzs.append(dm[:, sl] * on * ng * ds)
            dy = dm[:, sl] * s
            dxn = dy * ng
            dos.append(r * (dxn - on * jnp.mean(dxn * on, axis=-1, keepdims=True)))
            dng = dng + _fold8(dy * on)
        return jnp.concatenate(dos, axis=1), jnp.concatenate(dzs, axis=1), dng
    return _rowcall(fn, name, o.shape[0], bm, [_row(o), _row(proj, GDN_W, 3), _full(ng), _row(dcat, GDN_W, 0)],
                    [(GDN_W, F32), (GDN_W, BF16)], [(SUBLANES, LANES)])


def _split3(x):
    hi = x.astype(BF16)
    r = x - hi.astype(F32)
    mid = r.astype(BF16)
    return hi, mid, (r - mid.astype(F32)).astype(BF16)


def _dot3(a, b, dims=(((1,), (0,)), ((), ()))):
    ah, bh = a.astype(BF16), b.astype(BF16)
    al, bl = (a - ah.astype(F32)).astype(BF16), (b - bh.astype(F32)).astype(BF16)
    return _dot(ah, bh, dims) + _dot(ah, bl, dims) + _dot(al, bh, dims)


def _gdn_chunk(q, k, v, gc, beta, t=None):
    c = CHUNK
    row = lax.broadcasted_iota(jnp.int32, (c, c), 0)
    col = lax.broadcasted_iota(jnp.int32, (c, c), 1)
    tril, strict = row >= col, row > col
    lane0 = (lax.broadcasted_iota(jnp.int32, (c, LANES), 1) == 0).astype(BF16)
    gc_row = sum(_dot(lane0, part, NT) for part in _split3(gc))
    dm = jnp.exp(jnp.where(tril, gc[:, :c] - gc_row, -1e30))
    eg = jnp.exp(gc)
    gcl = gc[c - 1:c, :]
    ekg = jnp.exp(gcl - gc)
    egl = jnp.exp(gcl)
    qs = q * (GDN_DK ** -0.5)
    kb = k * beta
    kk = _dot(kb, k, NT)
    a = jnp.where(strict, kk * dm, 0.0)
    vb = v * beta
    kbg = kb * eg
    qk = _dot(qs, k, NT)
    p = jnp.where(tril, qk * dm, 0.0)
    out = dict(tril=tril, strict=strict, dm=dm, eg=eg, ekg=ekg, egl=egl, qs=qs, kb=kb, kk=kk, a=a,
               vb=vb, kbg=kbg, qk=qk, p=p, qg=qs * eg, kg=k * ekg)
    if t is None:
        y = -a
        t = (row == col).astype(F32) + y
        for _ in range(5):
            y = _dot3(y, y)
            t = t + _dot3(t, y)
        out.update(u=_dot3(t, vb), w=_dot3(t, kbg))
    out["t"] = t
    return out


def _gdn_fwd(q, k, v, gc, beta, name):
    rows = q.shape[0]
    n_chunks = rows // CHUNK
    blk = pl.BlockSpec((CHUNK, GDN_W), lambda n: (n, 0))
    st = pl.BlockSpec((GDN_HEADS, None, GDN_DK, LANES), lambda n: (0, n, 0, 0))
    tinv = pl.BlockSpec((None, GDN_HEADS, CHUNK, CHUNK), lambda n: (n, 0, 0, 0))

    def body(q_ref, k_ref, v_ref, g_ref, b_ref, o_ref, st_ref, t_ref, w_ref, vn_ref, s_ref):
        @pl.when(pl.program_id(0) == 0)
        def _():
            s_ref[...] = jnp.zeros(s_ref.shape, F32)
        for h in range(GDN_HEADS):
            sl = slice(h * LANES, (h + 1) * LANES)
            c = _gdn_chunk(q_ref[:, sl], k_ref[:, sl], v_ref[:, sl], g_ref[:, sl], b_ref[:, sl])
            s = s_ref[h]
            st_ref[h] = s
            vn = c["u"] - _dot(c["w"], s)
            o_ref[:, sl] = _dot(c["qg"], s) + _dot(c["p"], vn)
            s_ref[h] = s * c["egl"] + _dot(c["kg"], vn, TN)
            t_ref[h] = c["t"]
            w_ref[:, sl] = c["w"]
            vn_ref[:, sl] = vn

    f = jax.ShapeDtypeStruct((rows, GDN_W), F32)
    return pl.pallas_call(
        body, name=name, grid=(n_chunks,), in_specs=[blk] * 5, out_specs=[blk, st, tinv, blk, blk],
        out_shape=[f, jax.ShapeDtypeStruct((GDN_HEADS, n_chunks, GDN_DK, LANES), F32),
                   jax.ShapeDtypeStruct((n_chunks, GDN_HEADS, CHUNK, CHUNK), F32), f, f],
        scratch_shapes=[pltpu.VMEM((GDN_HEADS, GDN_DK, LANES), F32)],
        compiler_params=_params(("arbitrary",)))(q, k, v, gc, beta)


def _gdn_bwd(q, k, v, gc, beta, states, tinv, w, vn, do, name):
    rows = q.shape[0]
    n_chunks = rows // CHUNK
    blk = pl.BlockSpec((CHUNK, GDN_W), lambda n: (n_chunks - 1 - n, 0))
    st = pl.BlockSpec((GDN_HEADS, None, GDN_DK, LANES), lambda n: (0, n_chunks - 1 - n, 0, 0))
    ti = pl.BlockSpec((None, GDN_HEADS, CHUNK, CHUNK), lambda n: (n_chunks - 1 - n, 0, 0, 0))

    def lanesum(x):
        return jnp.broadcast_to(jnp.sum(x, axis=-1, keepdims=True), x.shape)

    def body(q_ref, k_ref, v_ref, g_ref, b_ref, st_ref, t_ref, w_ref, vn_ref, do_ref,
             dq_ref, dk_ref, dv_ref, dg_ref, db_ref, ds_ref):
        @pl.when(pl.program_id(0) == 0)
        def _():
            ds_ref[...] = jnp.zeros(ds_ref.shape, F32)
        ones = jnp.ones((CHUNK, LANES), BF16)
        last = lax.broadcasted_iota(jnp.int32, (CHUNK, LANES), 0) == CHUNK - 1
        for h in range(GDN_HEADS):
            sl = slice(h * LANES, (h + 1) * LANES)
            q, k, v, beta = q_ref[:, sl], k_ref[:, sl], v_ref[:, sl], b_ref[:, sl]
            t, w, vn, do = t_ref[h], w_ref[:, sl], vn_ref[:, sl], do_ref[:, sl]
            c = _gdn_chunk(q, k, v, g_ref[:, sl], beta, t)
            s, dsn = st_ref[h], ds_ref[h]
            tril, strict, dm = c["tril"], c["strict"], c["dm"]
            dvn = _dot(c["p"], do, TN) + _dot(c["kg"], dsn)
            dp = jnp.where(tril, _dot(do, vn, NT), 0.0)
            dqg = _dot(do, s, NT)
            dkg = _dot(vn, dsn, NT)
            dgl = jnp.sum(jnp.sum(s * dsn, axis=1, keepdims=True), axis=0, keepdims=True) * c["egl"]
            ds_ref[h] = dsn * c["egl"] + _dot(c["qg"], do, TN) - _dot(w, dvn, TN)
            dw = -_dot(dvn, s, NT)
            dvb = _dot3(t, dvn, TN)
            dkbg = _dot3(t, dw, TN)
            dt = _dot(dvn, c["vb"], NT) + _dot(dw, c["kbg"], NT)
            da = jnp.where(strict, -_dot3(_dot3(t, dt, TN), t, NT), 0.0)
            dkk = da * dm
            dqk = dp * dm
            dkb = _dot(dkk, k) + dkbg * c["eg"]
            dk = _dot(dkk, c["kb"], TN) + _dot(dqk, c["qs"], TN) + dkg * c["ekg"] + dkb * beta
            dqs = _dot(dqk, k) + dqg * c["eg"]
            e = da * c["a"] + dp * c["p"]
            col_sums = sum(_dot(part, ones, TN) for part in _split3(e))
            kg_term = lanesum(dkg * c["kg"])
            dgc = (jnp.broadcast_to(jnp.sum(e, axis=-1, keepdims=True), (CHUNK, LANES)) - col_sums
                   + lanesum(dqg * c["qg"]) - kg_term + lanesum(dkbg * c["kbg"]))
            dgcl = jnp.sum(kg_term, axis=0, keepdims=True) + dgl
            dq_ref[:, sl] = dqs * (GDN_DK ** -0.5)
            dk_ref[:, sl] = dk
            dv_ref[:, sl] = dvb * beta
            dg_ref[:, sl] = dgc + jnp.where(last, dgcl, 0.0)
            db_ref[:, sl] = lanesum(dvb * v) + lanesum(dkb * k)

    return pl.pallas_call(
        body, name=name, grid=(n_chunks,), in_specs=[blk] * 5 + [st, ti, blk, blk, blk], out_specs=[blk] * 5,
        out_shape=[jax.ShapeDtypeStruct((rows, GDN_W), F32)] * 5,
        scratch_shapes=[pltpu.VMEM((GDN_HEADS, GDN_DK, LANES), F32)],
        compiler_params=_params(("arbitrary",)))(q, k, v, gc, beta, states, tinv, w, vn, do)


def _swa_masks(first):
    r = lax.broadcasted_iota(jnp.int32, (SWA_BLOCK, 2 * SWA_BLOCK), 0)
    c = lax.broadcasted_iota(jnp.int32, (SWA_BLOCK, 2 * SWA_BLOCK), 1)
    band = (c > r) & (c <= r + SWA_BLOCK)
    return band & (jnp.logical_not(first) | (c >= SWA_BLOCK))


def _swa_stack(ref, j):
    lane = lax.broadcasted_iota(jnp.int32, (1, LANES), 1)
    parts = []
    for g in range(SWA_GROUP):
        ch = j * (SWA_GROUP // 2) + g // 2
        keep = (lane < SWA_DH) if g % 2 == 0 else (lane >= SWA_DH)
        parts.append(ref[:, ch * LANES:(ch + 1) * LANES] * keep.astype(ref.dtype))
    return jnp.concatenate(parts, axis=0)


def _swa_unstack(x2, j, out_ref):
    low = lax.broadcasted_iota(jnp.int32, (SWA_BLOCK, LANES), 1) < SWA_DH
    for c3 in range(SWA_GROUP // 2):
        even = x2[(2 * c3) * SWA_BLOCK:(2 * c3 + 1) * SWA_BLOCK]
        odd = x2[(2 * c3 + 1) * SWA_BLOCK:(2 * c3 + 2) * SWA_BLOCK]
        ch = j * (SWA_GROUP // 2) + c3
        out_ref[:, ch * LANES:(ch + 1) * LANES] = jnp.where(low, even, odd).astype(out_ref.dtype)


def _swa_probs(s, sink, mask):
    s = jnp.where(mask, s, -1e30)
    m = jnp.maximum(jnp.max(s, axis=-1, keepdims=True), sink)
    p = jnp.where(mask, jnp.exp(s - m), 0.0)
    es = jnp.exp(sink - m)
    inv = 1.0 / (jnp.sum(p, axis=-1, keepdims=True) + es)
    return p * inv, es * inv


def _swa_scores(q_ref, kc_ref, kp_ref, sink_ref, j, mask):
    sl = slice(j * LANES, (j + 1) * LANES)
    qst = _swa_stack(q_ref, j)
    kw = jnp.concatenate([kp_ref[:, sl], kc_ref[:, sl]], axis=0)
    s = _dot(qst, kw, NT) * (SWA_DH ** -0.5)
    ps = [_swa_probs(s[g * SWA_BLOCK:(g + 1) * SWA_BLOCK], sink_ref[j * SWA_GROUP + g], mask)
          for g in range(SWA_GROUP)]
    return qst, kw, ps


def _swa_fwd(q, k2, v2, sinks, name):
    rows = q.shape[0]
    nb = rows // SWA_BLOCK
    w = SWA_HEADS * SWA_DH
    kvw = SWA_KV_HEADS * LANES
    cur = pl.BlockSpec((SWA_BLOCK, w), lambda i: (i, 0))
    kcur = pl.BlockSpec((SWA_BLOCK, kvw), lambda i: (i, 0))
    kprev = pl.BlockSpec((SWA_BLOCK, kvw), lambda i: (jnp.maximum(i - 1, 0), 0))

    def body(sink_ref, q_ref, kc_ref, kp_ref, vc_ref, vp_ref, o_ref):
        mask = _swa_masks(pl.program_id(0) == 0)
        for j in range(SWA_KV_HEADS):
            sl = slice(j * LANES, (j + 1) * LANES)
            _, _, ps = _swa_scores(q_ref, kc_ref, kp_ref, sink_ref, j, mask)
            vw = jnp.concatenate([vp_ref[:, sl], vc_ref[:, sl]], axis=0)
            pst = jnp.concatenate([p.astype(BF16) for p, _ in ps], axis=0)
            _swa_unstack(_dot(pst, vw), j, o_ref)

    return pl.pallas_call(
        body, name=name, grid=(nb,),
        in_specs=[pl.BlockSpec(memory_space=pltpu.SMEM), cur, kcur, kprev, kcur, kprev], out_specs=cur,
        out_shape=jax.ShapeDtypeStruct((rows, w), BF16),
        compiler_params=_params(("arbitrary",)))(sinks, q, k2, k2, v2, v2)


def _swa_bwd(q, k2, v2, sinks, dcat, name):
    rows = q.shape[0]
    nb = rows // SWA_BLOCK
    w = SWA_HEADS * SWA_DH
    kvw = SWA_KV_HEADS * LANES
    cur = pl.BlockSpec((SWA_BLOCK, w), lambda i: (jnp.minimum(i, nb - 1), 0))
    kcur = pl.BlockSpec((SWA_BLOCK, kvw), lambda i: (jnp.minimum(i, nb - 1), 0))
    kprev = pl.BlockSpec((SWA_BLOCK, kvw), lambda i: (jnp.clip(i - 1, 0, nb - 1), 0))
    late = pl.BlockSpec((SWA_BLOCK, kvw), lambda i: (jnp.maximum(i - 1, 0), 0))
    acc_spec = pl.BlockSpec((SUBLANES, LANES), lambda i: (0, 0))

    def body(sink_ref, q_ref, kc_ref, kp_ref, vc_ref, vp_ref, do_ref, dq_ref, dk_ref, dv_ref, dsk_ref,
             ck_ref, cv_ref):
        i = pl.program_id(0)

        @pl.when(i == 0)
        def _():
            ck_ref[...] = jnp.zeros(ck_ref.shape, F32)
            cv_ref[...] = jnp.zeros(cv_ref.shape, F32)
            dsk_ref[...] = jnp.zeros(dsk_ref.shape, F32)

        @pl.when(i == nb)
        def _():
            dk_ref[...] = ck_ref[...]
            dv_ref[...] = cv_ref[...]

        @pl.when(i < nb)
        def _():
            mask = _swa_masks(i == 0)
            lane = lax.broadcasted_iota(jnp.int32, (SUBLANES, LANES), 1)
            dsk = jnp.zeros((SUBLANES, LANES), F32)
            for j in range(SWA_KV_HEADS):
                sl = slice(j * LANES, (j + 1) * LANES)
                qst, kw, ps = _swa_scores(q_ref, kc_ref, kp_ref, sink_ref, j, mask)
                vw = jnp.concatenate([vp_ref[:, sl], vc_ref[:, sl]], axis=0)
                dost = _swa_stack(do_ref, j)
                dpr = _dot(dost, vw, NT)
                dss = []
                for g in range(SWA_GROUP):
                    p, sink_p = ps[g]
                    dpg = dpr[g * SWA_BLOCK:(g + 1) * SWA_BLOCK]
                    delta = jnp.sum(p * dpg, axis=-1, keepdims=True)
                    dss.append((p * (dpg - delta)).astype(BF16))
                    dsg = jnp.sum(-sink_p * delta, axis=0, keepdims=True)
                    dsk = dsk + jnp.where(lane == j * SWA_GROUP + g, dsg, 0.0)
                dsst = jnp.concatenate(dss, axis=0)
                pst = jnp.concatenate([p.astype(BF16) for p, _ in ps], axis=0)
                _swa_unstack(_dot(dsst, kw) * (SWA_DH ** -0.5), j, dq_ref)
                dk = _dot(dsst, qst, TN) * (SWA_DH ** -0.5)
                dv = _dot(pst, dost, TN)
                dk = dk + pltpu.roll(dk, SWA_DH, 1)
                dv = dv + pltpu.roll(dv, SWA_DH, 1)
                dk_ref[:, sl] = ck_ref[:, sl] + dk[:SWA_BLOCK]
                dv_ref[:, sl] = cv_ref[:, sl] + dv[:SWA_BLOCK]
                ck_ref[:, sl] = dk[SWA_BLOCK:]
                cv_ref[:, sl] = dv[SWA_BLOCK:]
            dsk_ref[...] += dsk

    f = jax.ShapeDtypeStruct((rows, kvw), F32)
    return pl.pallas_call(
        body, name=name, grid=(nb + 1,),
        in_specs=[pl.BlockSpec(memory_space=pltpu.SMEM), cur, kcur, kprev, kcur, kprev, cur],
        out_specs=[cur, late, late, acc_spec],
        out_shape=[jax.ShapeDtypeStruct((rows, w), F32), f, f, jax.ShapeDtypeStruct((SUBLANES, LANES), F32)],
        scratch_shapes=[pltpu.VMEM((SWA_BLOCK, kvw), F32), pltpu.VMEM((SWA_BLOCK, kvw), F32)],
        compiler_params=_params(("arbitrary",)))(sinks, q, k2, k2, v2, v2, dcat)


def _mem_probs(mq, kbd):
    s = _dot(mq.astype(BF16), kbd) * (MEM_DH ** -0.5)
    ps = []
    for h in range(MEM_HEADS):
        sh = s[:, h * MEM_LEN:(h + 1) * MEM_LEN]
        e = jnp.exp(sh - jnp.max(sh, axis=-1, keepdims=True))
        ps.append(e / jnp.sum(e, axis=-1, keepdims=True))
    return ps


def _mem_fwd(proj, cb, kbd, vbd, name, bm=512):
    def fn(i, mq, kbd, vbd):
        p = jnp.concatenate(_mem_probs(mq, kbd), axis=1)
        return _dot(p.astype(BF16), vbd)
    return _rowcall(fn, name, proj.shape[0], bm, [_row(proj, MEM_W, cb), _full(kbd), _full(vbd)], [(MEM_W, BF16)])[0]


def _mem_bwd(proj, cb, kbd, vbd, dcat, name, bm=512):
    def fn(i, mq, kbd, vbd, do):
        ps = _mem_probs(mq, kbd)
        dp = _dot(do, vbd, NT)
        dss = []
        for h in range(MEM_HEADS):
            dph = dp[:, h * MEM_LEN:(h + 1) * MEM_LEN]
            dss.append(ps[h] * (dph - jnp.sum(ps[h] * dph, axis=-1, keepdims=True)))
        ds = (jnp.concatenate(dss, axis=1) * (MEM_DH ** -0.5)).astype(BF16)
        p = jnp.concatenate(ps, axis=1).astype(BF16)
        return _dot(ds, kbd, NT), _dot(mq.astype(BF16), ds, TN), _dot(p, do, TN)
    return _rowcall(fn, name, proj.shape[0], bm, [_row(proj, MEM_W, cb), _full(kbd), _full(vbd), _row(dcat, MEM_W, 3)],
                    [(MEM_W, BF16)], [(MEM_W, MEM_HEADS * MEM_LEN), (MEM_HEADS * MEM_LEN, MEM_W)])


def _mem_expand(mkv):
    feat_head = jnp.arange(MEM_W) // MEM_DH
    slot_head = jnp.arange(MEM_HEADS * MEM_LEN) // MEM_LEN
    on = feat_head[:, None] == slot_head[None, :]
    kbd = jnp.where(on, jnp.tile(mkv[:, :MEM_W].T, (1, MEM_HEADS)), 0.0)
    vbd = jnp.where(on.T, jnp.tile(mkv[:, MEM_W:], (MEM_HEADS, 1)), 0.0)
    return kbd.astype(BF16), vbd.astype(BF16)


def _mem_collapse(dkbd, dvbd):
    dk = [dkbd[h * MEM_DH:(h + 1) * MEM_DH, h * MEM_LEN:(h + 1) * MEM_LEN].T for h in range(MEM_HEADS)]
    dv = [dvbd[h * MEM_LEN:(h + 1) * MEM_LEN, h * MEM_DH:(h + 1) * MEM_DH] for h in range(MEM_HEADS)]
    return jnp.concatenate(dk + dv, axis=1)


def _adamw(w, g, m, v, name, bm=512):
    def fn(i, w, g, m, v):
        m = ADAM_B1 * m + (1.0 - ADAM_B1) * g
        v = ADAM_B2 * v + (1.0 - ADAM_B2) * (g * g)
        m_hat = m / (1.0 - ADAM_B1 ** ADAM_STEP)
        v_hat = v / (1.0 - ADAM_B2 ** ADAM_STEP)
        return -ADAM_LR * (m_hat / (jnp.sqrt(v_hat) + ADAM_EPS) + ADAM_WD * w), m, v
    d = w.shape[1]
    return _rowcall(fn, name, w.shape[0], bm, [_row(w), _row(g), _row(m), _row(v)], [(d, F32)] * 3)


def _sum_slots(buf, name, bm=256):
    n, rows, w = buf.shape
    bm = min(bm, rows)
    assert rows % bm == 0

    def body(b_ref, o_ref):
        acc = b_ref[0].astype(F32)
        for s in range(1, n):
            acc = acc + b_ref[s].astype(F32)
        o_ref[...] = acc

    return pl.pallas_call(
        body, name=name, grid=(rows // bm,), in_specs=[pl.BlockSpec((n, bm, w), lambda i: (0, i, 0))],
        out_specs=pl.BlockSpec((bm, w), lambda i: (i, 0)), out_shape=jax.ShapeDtypeStruct((rows, w), F32),
        compiler_params=_params(("parallel",)))(buf)


def _exchange(src, masks, name):
    same = src.ndim == 2
    rows, w = src.shape[-2:]
    slots = N_DEV if len(masks) == N_DEV - 1 else 2
    n = len(masks)

    def body(src_ref, out_ref, send_sems, recv_sems, local_sem):
        x, y, c = lax.axis_index("x"), lax.axis_index("y"), lax.axis_index("c")
        me = 4 * x + 2 * y + c

        def flip(v, bit):
            return 1 - v if bit else v

        def slot_of(dev):
            return dev if slots == N_DEV else dev % 2

        def piece(p):
            return src_ref if same else src_ref.at[p]

        mine = pltpu.make_async_copy(piece(me), out_ref.at[slot_of(me)], local_sem)
        mine.start()
        copies = []
        for idx, k in enumerate(masks):
            peer = (flip(x, k & 4), flip(y, k & 2), flip(c, k & 1))
            peer_id = 4 * peer[0] + 2 * peer[1] + peer[2]
            cp = pltpu.make_async_remote_copy(
                src_ref=piece(peer_id), dst_ref=out_ref.at[slot_of(me)],
                send_sem=send_sems.at[idx], recv_sem=recv_sems.at[idx], device_id=peer, device_id_type=MESH)
            cp.start()
            copies.append((cp, pltpu.make_async_remote_copy(
                src_ref=piece(peer_id), dst_ref=out_ref.at[slot_of(peer_id)],
                send_sem=send_sems.at[idx], recv_sem=recv_sems.at[idx], device_id=peer, device_id_type=MESH)))
        for cp, landing in copies:
            cp.wait_send()
            landing.wait_recv()
        mine.wait()

    any_spec = pl.BlockSpec(memory_space=pl.ANY)
    return pl.pallas_call(
        body, name=name, in_specs=[any_spec], out_specs=any_spec,
        out_shape=jax.ShapeDtypeStruct((slots, rows, w), src.dtype),
        scratch_shapes=[pltpu.SemaphoreType.DMA((n,)), pltpu.SemaphoreType.DMA((n,)), pltpu.SemaphoreType.DMA],
        )(src)


ALL_PEERS = tuple(range(1, N_DEV))
SIBLING = (1,)


def _pack(arrays, rows):
    flat = jnp.concatenate([a.reshape(-1) for a in arrays])
    return jnp.pad(flat, (0, rows * D_MODEL - flat.shape[0])).reshape(rows, D_MODEL)


def _unpack(buf, shapes):
    flat = buf.reshape(-1)
    out, off = [], 0
    for s in shapes:
        n = math.prod(s)
        out.append(flat[off:off + n].reshape(s))
        off += n
    return out


def _rows_for(shapes, mult):
    n = sum(math.prod(s) for s in shapes)
    rows = -(-n // D_MODEL)
    return -(-rows // mult) * mult


SHARD_AXIS = dict(w_mem_kv=1, w_out=1, w_gate_up=2, w_down=1, gdn_w_in=2, swa_w_q=1, w_kv=0, gdn_conv=2)


def _split4(a, axis):
    return jnp.split(a, 4, axis=axis)


def _rope_tables(positions):
    half = ROT_DIM // 2
    inv = ROPE_THETA ** (-jnp.arange(0, ROT_DIM, 2, dtype=F32) / ROT_DIM)
    ang = positions.astype(F32)[:, None] * inv
    cos, sin = jnp.cos(ang), jnp.sin(ang)
    rows = positions.shape[0]
    one = jnp.ones((rows, SWA_DH - ROT_DIM), F32)
    zero = jnp.zeros((rows, SWA_DH - ROT_DIM), F32)
    zh = jnp.zeros((rows, half), F32)
    c64 = jnp.concatenate([cos, cos, one], axis=1)
    a64 = jnp.concatenate([-sin, zh, zero], axis=1)
    b64 = jnp.concatenate([zh, sin, zero], axis=1)
    return tuple(jnp.concatenate([t, t], axis=1) for t in (c64, a64, b64))


def _pair_heads(t):
    return jnp.concatenate([t[:, :SWA_DH], t[:, :SWA_DH], t[:, SWA_DH:], t[:, SWA_DH:]], axis=1)


def _unpair_heads(t):
    return jnp.concatenate([t[:, :SWA_DH], t[:, LANES:LANES + SWA_DH]], axis=1)


def _gdn_in_pad(w):
    o2 = 4 * GDN_W
    pad = jnp.zeros((w.shape[0], GDN_IN_PAD - GDN_IN), w.dtype)
    return jnp.concatenate([w[:, :o2], w[:, o2 + 2 * GDN_HEADS:], w[:, o2:o2 + 2 * GDN_HEADS], pad], axis=1)


def _gdn_in_unpad(w):
    o2 = 4 * GDN_W
    return jnp.concatenate([w[:, :o2], w[:, o2 + MEM_W:o2 + MEM_W + 2 * GDN_HEADS], w[:, o2:o2 + MEM_W]], axis=1)


def _head_rows(v):
    return jnp.repeat(v.astype(F32), LANES)[None, :]


def _selectors():
    lane = jnp.arange(LANES)[:, None]
    head = (jnp.arange(GDN_W) // LANES)[None, :]
    return (lane == head).astype(F32), (lane == head + GDN_HEADS).astype(F32)


def _local_step(x, mem, positions, target, w):
    rows = x.shape[0]
    tabs = _rope_tables(positions)
    sel_b, sel_a = _selectors()
    row2 = lambda v: v.reshape(1, -1).astype(F32)

    w_gu = _ff_interleave(w["w_gate_up"])
    mem_n = _rms_fwd(mem, row2(w["ln_mem"]), "mem_norm")
    saved = []
    kt = vt = None
    for l in range(DEPTH):
        s = dict(x0=x)
        h = _rms_fwd(x, row2(w["ln_mix"][l]), f"norm_mix{l}")
        mkv = _mm(mem_n, w["w_mem_kv"][l], "nn", f"mem_kv{l}")
        kbd, vbd = _mem_expand(mkv)
        if l < N_A:
            proj = _mm(h, w["gdn_w_in"][l], "nn", f"gdn_in{l}")
            a_row, dt_row = _head_rows(w["gdn_A_log"][l]), _head_rows(w["gdn_dt_bias"][l])
            q, k, v, gc, beta = _gdn_pre_fwd(proj, w["gdn_conv"][l], sel_b, sel_a, a_row, dt_row, f"gdn_pre{l}")
            o, states, tinv, gw, vn = _gdn_fwd(q, k, v, gc, beta, f"gdn_scan{l}")
            mix = _gdn_post_fwd(o, proj, row2(w["gdn_norm"][l]), f"gdn_post{l}")
            mq_cb = (3 * GDN_W + GDN_W) // MEM_W
            s.update(q=q, k=k, v=v, gc=gc, beta=beta, o=o, states=states, tinv=tinv, gw=gw, vn=vn,
                     a_row=a_row, dt_row=dt_row)
        else:
            proj = _mm(h, w["swa_w_q"][l - N_A], "nn", f"swa_in{l}")
            qr = _rope(proj, SWA_HEADS * SWA_DH, 0, tabs, 1, f"rope_q{l}", BF16)
            mix = _swa_fwd(qr, kt, vt, w["swa_sinks"][l - N_A], f"swa{l}")
            mq_cb = (SWA_HEADS * SWA_DH) // MEM_W
            s.update(qr=qr)
        mem_o = _mem_fwd(proj, mq_cb, kbd, vbd, f"mem_attn{l}")
        cat = jnp.concatenate([mix, mem_o], axis=1)
        x1 = _mm(cat, w["w_out"][l], "nn", f"out_proj{l}", add=x)
        h2 = _rms_fwd(x1, row2(w["ln_ffn"][l]), f"norm_ffn{l}")
        gu, act = _gate_up_fwd(h2, w_gu[l], f"gate_up{l}")
        x = _mm(act, w["w_down"][l], "nn", f"down{l}", add=x1)
        s.update(h=h, proj=proj, kbd=kbd, vbd=vbd, mq_cb=mq_cb, cat=cat, x1=x1, h2=h2, gu=gu, act=act)
        saved.append(s)
        if l == N_A - 1:
            x_kv = x
            h_kv = _rms_fwd(x, row2(w["ln_kv"]), "norm_kv")
            kv = _mm(h_kv, w["w_kv"], "nn", "kv_proj")
            kr = _rope(kv, LANES, 0, tabs, 1, "rope_k", F32)
            kt = _pair_heads(kr).astype(BF16)
            vt = _pair_heads(kv[:, LANES:]).astype(BF16)

    gr = {}
    dx, dxb, loss_part, dlnf = _final_loss(x, row2(w["ln_final"]), target, "final_loss")
    gr["ln_final"] = dlnf.sum(axis=0)
    dln_mix, dln_ffn = [None] * DEPTH, [None] * DEPTH
    dw_mem_kv, dw_out, dw_gu, dw_dn = [None] * DEPTH, [None] * DEPTH, [None] * DEPTH, [None] * DEPTH
    dgdn_in, dgdn_conv, dgdn_a, dgdn_dt, dgdn_norm = [None] * N_A, [None] * N_A, [None] * N_A, [None] * N_A, [None] * N_A
    dswa_q, dswa_sinks = [None] * N_B, [None] * N_B
    dmem_n = None
    dkt = dvt = None
    for l in reversed(range(DEPTH)):
        s = saved[l]
        if l == N_A - 1:
            dkr = _unpair_heads(dkt)
            dk = _rope(dkr, LANES, 0, tabs, -1, "rope_k_bwd", BF16)
            dkv = jnp.concatenate([dk, _unpair_heads(dvt).astype(BF16)], axis=1)
            dh_kv = _mm(dkv, w["w_kv"], "nt", "kv_proj_dx")
            gr["w_kv"] = _mm(h_kv, dkv, "tn", "kv_proj_dw", BF16)
            dx, dxb, dg = _rms_bwd(x_kv, row2(w["ln_kv"]), dh_kv, dx, "norm_kv_bwd")
            gr["ln_kv"] = dg.sum(axis=0)
        dgu = _down_bwd(dxb, w["w_down"][l], s["gu"], f"down_dx{l}")
        dw_dn[l] = _mm(s["act"], dxb, "tn", f"down_dw{l}", BF16)
        dh2 = _mm(dgu, w_gu[l], "nt", f"gate_up_dx{l}")
        dw_gu[l] = _mm(s["h2"], dgu, "tn", f"gate_up_dw{l}", BF16)
        dx, dxb, dg = _rms_bwd(s["x1"], row2(w["ln_ffn"][l]), dh2, dx, f"norm_ffn_bwd{l}")
        dln_ffn[l] = dg.sum(axis=0)
        dcat = _mm(dxb, w["w_out"][l], "nt", f"out_proj_dx{l}", BF16)
        dw_out[l] = _mm(s["cat"], dxb, "tn", f"out_proj_dw{l}", BF16)
        dmq, dkbd, dvbd = _mem_bwd(s["proj"], s["mq_cb"], s["kbd"], s["vbd"], dcat, f"mem_attn_bwd{l}")
        dmkv = _mem_collapse(dkbd, dvbd).astype(BF16)
        dw_mem_kv[l] = _mm(mem_n, dmkv, "tn", f"mem_kv_dw{l}", BF16)
        dmem_n = _mm(dmkv, w["w_mem_kv"][l], "nt", f"mem_kv_dx{l}", add=dmem_n)
        if l < N_A:
            do, dz, dng = _gdn_post_bwd(s["o"], s["proj"], row2(w["gdn_norm"][l]), dcat, f"gdn_post_bwd{l}")
            dq, dk, dv, dg_, dbeta = _gdn_bwd(s["q"], s["k"], s["v"], s["gc"], s["beta"], s["states"], s["tinv"], s["gw"],
                                              s["vn"], do, f"gdn_scan_bwd{l}")
            res = _gdn_pre_bwd(s["proj"], w["gdn_conv"][l], sel_b, sel_a, s["a_row"], s["dt_row"], dq, dk, dv, dg_, dbeta,
                               f"gdn_pre_bwd{l}")
            dconv, dba = res[0], res[1]
            dgdn_conv[l] = jnp.stack([r.sum(axis=0) for r in res[2:2 + CONV_K]])
            dgdn_a[l] = res[2 + CONV_K].sum(axis=0)[::LANES]
            dgdn_dt[l] = res[3 + CONV_K].sum(axis=0)[::LANES]
            dgdn_norm[l] = dng.sum(axis=0)
            dqkv = _conv_bwd_input(dconv, w["gdn_conv"][l], f"gdn_conv_bwd{l}")
            dproj = jnp.concatenate([dqkv, dz, dmq, dba], axis=1)
            dh = _mm(dproj, w["gdn_w_in"][l], "nt", f"gdn_in_dx{l}")
            dgdn_in[l] = _mm(s["h"], dproj, "tn", f"gdn_in_dw{l}", BF16)
        else:
            b = l - N_A
            dqr, dkt_l, dvt_l, dsk = _swa_bwd(s["qr"], kt, vt, w["swa_sinks"][b], dcat, f"swa_bwd{l}")
            dkt = dkt_l if dkt is None else dkt + dkt_l
            dvt = dvt_l if dvt is None else dvt + dvt_l
            dswa_sinks[b] = dsk[0, :SWA_HEADS]
            dq = _rope(dqr, SWA_HEADS * SWA_DH, 0, tabs, -1, f"rope_q_bwd{l}", BF16)
            dproj = jnp.concatenate([dq, dmq], axis=1)
            dh = _mm(dproj, w["swa_w_q"][b], "nt", f"swa_in_dx{l}")
            dswa_q[b] = _mm(s["h"], dproj, "tn", f"swa_in_dw{l}", BF16)
        dx, dxb, dg = _rms_bwd(s["x0"], row2(w["ln_mix"][l]), dh, dx, f"norm_mix_bwd{l}")
        dln_mix[l] = dg.sum(axis=0)
    _, _, dg = _rms_bwd(mem, row2(w["ln_mem"]), dmem_n, None, "mem_norm_bwd")
    gr["ln_mem"] = dg.sum(axis=0)
    gr.update(ln_mix=jnp.stack(dln_mix), ln_ffn=jnp.stack(dln_ffn), w_mem_kv=jnp.stack(dw_mem_kv), w_out=jnp.stack(dw_out),
              w_gate_up=_ff_deinterleave(jnp.stack(dw_gu)), w_down=jnp.stack(dw_dn), gdn_w_in=jnp.stack(dgdn_in), gdn_conv=jnp.stack(dgdn_conv),
              gdn_A_log=jnp.stack(dgdn_a), gdn_dt_bias=jnp.stack(dgdn_dt), gdn_norm=jnp.stack(dgdn_norm),
              swa_w_q=jnp.stack(dswa_q), swa_sinks=jnp.stack(dswa_sinks))
    return loss_part, dx, gr


def kernel(x, mem, positions, ln_mix, ln_ffn, ln_mem, w_mem_kv, w_out, w_gate_up, w_down, gdn_w_in, gdn_conv, gdn_A_log, gdn_dt_bias, gdn_norm, swa_w_q, swa_sinks, ln_kv, w_kv, ln_final, loss_target, m_ln_mix, m_ln_ffn, m_ln_mem, m_w_mem_kv, m_w_out, m_w_gate_up, m_w_down, m_gdn_w_in, m_gdn_conv, m_gdn_A_log, m_gdn_dt_bias, m_gdn_norm, m_swa_w_q, m_swa_sinks, m_ln_kv, m_w_kv, m_ln_final, v_ln_mix, v_ln_ffn, v_ln_mem, v_w_mem_kv, v_w_out, v_w_gate_up, v_w_down, v_gdn_w_in, v_gdn_conv, v_gdn_A_log, v_gdn_dt_bias, v_gdn_norm, v_swa_w_q, v_swa_sinks, v_ln_kv, v_w_kv, v_ln_final):
    given = dict(locals())
    wts = {n: given[n] for n in WEIGHTS}
    c = lax.axis_index("c")

    shard_shapes = [wts[n].shape for n in SHARDED]
    rows_w = _rows_for(shard_shapes, 512)
    half = rows_w // 2
    wpack = _pack([wts[n] for n in SHARDED], rows_w)
    my_half = lax.dynamic_slice_in_dim(wpack.astype(BF16), c * half, half, axis=0)
    gathered = _exchange(my_half, ALL_PEERS, "gather_weights").reshape(4, rows_w, D_MODEL)
    conv_shape = wts["gdn_conv"].shape
    cpack = _pack([wts["gdn_conv"]], 16)
    conv_half = lax.dynamic_slice_in_dim(cpack, c * SUBLANES, SUBLANES, axis=0)
    conv_all = _exchange(conv_half, ALL_PEERS, "gather_conv").reshape(4, 16, D_MODEL)
    full = {n: wts[n] for n in SMALL}
    per_chip = [_unpack(gathered[s], shard_shapes) for s in range(4)]
    for i, n in enumerate(SHARDED):
        full[n] = jnp.concatenate([per_chip[s][i] for s in range(4)], axis=SHARD_AXIS[n])
    full["gdn_conv"] = jnp.concatenate([_unpack(conv_all[s], [conv_shape])[0] for s in range(4)], axis=2)
    full["gdn_w_in"] = jnp.stack([_gdn_in_pad(full["gdn_w_in"][a]) for a in range(N_A)])

    loss_part, dx, gr = _local_step(x[0], mem[0], positions[0], loss_target[0], full)
    gr["gdn_w_in"] = jnp.stack([_gdn_in_unpad(gr["gdn_w_in"][a]) for a in range(N_A)])

    pieces = []
    for s in range(4):
        pieces.append(_pack([_split4(gr[n], SHARD_AXIS[n])[s].astype(BF16) for n in SHARDED], rows_w))
    gpack = jnp.stack(pieces).reshape(N_DEV, half, D_MODEL)
    parts = _exchange(gpack, ALL_PEERS, "scatter_grads")
    mine = _sum_slots(parts, "sum_grads")
    g_shard = _exchange(mine, SIBLING, "swap_grad_halves").reshape(rows_w, D_MODEL)

    small_shapes = [wts[n].shape for n in SMALL] + [conv_shape[:2] + (4 * conv_shape[2],), (SUBLANES, LANES)]
    rows_s = _rows_for(small_shapes, SUBLANES)
    spack = _pack([gr[n] for n in SMALL] + [gr["gdn_conv"], loss_part], rows_s)
    sparts = _exchange(spack, ALL_PEERS, "share_small")
    ssum = _unpack(_sum_slots(sparts, "sum_small"), small_shapes)
    g_small = dict(zip(SMALL, ssum[:len(SMALL)]))
    chip = 2 * lax.axis_index("x") + lax.axis_index("y")
    g_conv = lax.dynamic_slice_in_dim(ssum[len(SMALL)], chip * conv_shape[2], conv_shape[2], axis=2)
    loss = jnp.sum(ssum[-1])

    m_pack = _pack([given["m_" + n] for n in SHARDED], rows_w)
    v_pack = _pack([given["v_" + n] for n in SHARDED], rows_w)
    d_b, m_b, v_b = _adamw(wpack, g_shard, m_pack, v_pack, "adamw_shards")
    small_names = SMALL + ("gdn_conv",)
    small_w_shapes = [wts[n].shape for n in small_names]
    rows_a = _rows_for(small_w_shapes, SUBLANES)
    g_small["gdn_conv"] = g_conv
    d_s, m_s, v_s = _adamw(_pack([wts[n] for n in small_names], rows_a), _pack([g_small[n] for n in small_names], rows_a),
                           _pack([given["m_" + n] for n in small_names], rows_a),
                           _pack([given["v_" + n] for n in small_names], rows_a), "adamw_small")
    out = {}
    for kind, big, small in (("grad", g_shard, None), ("delta", d_b, d_s), ("new_m", m_b, m_s), ("new_v", v_b, v_s)):
        out[kind] = dict(zip(SHARDED, _unpack(big, shard_shapes)))
        if small is None:
            out[kind].update(g_small)
        else:
            out[kind].update(zip(small_names, _unpack(small, small_w_shapes)))
    return (loss, dx[None], *[out["grad"][n] for n in WEIGHTS], *[out["delta"][n] for n in WEIGHTS],
            *[out["new_m"][n] for n in WEIGHTS], *[out["new_v"][n] for n in WEIGHTS])
```

```python
import functools
import math

import jax
import jax.numpy as jnp
from jax import lax
from jax.experimental import pallas as pl
from jax.experimental.pallas import tpu as pltpu

F32 = jnp.float32
BF16 = jnp.bfloat16
HI = lax.Precision.HIGHEST
MESH = pl.DeviceIdType.MESH

D_MODEL = 1024
DEPTH = 4
N_A = 2
N_B = 2
EPS = 1e-6
GDN_HEADS = 6
GDN_DK = 128
GDN_W = 768
CONV_K = 4
CHUNK = 64
SWA_HEADS = 12
SWA_KV_HEADS = 2
SWA_DH = 64
SWA_GROUP = 6
SWA_GW = SWA_GROUP * SWA_DH
SWA_BLOCK = 128
ROPE_THETA = 500000.0
ROT_DIM = 16
MEM_LEN = 256
MEM_HEADS = 4
MEM_DH = 64
MEM_W = 256
D_FF = 2816
GDN_IN = 3340
GDN_IN_PAD = 3456
ADAM_LR = 0.001
ADAM_B1 = 0.9
ADAM_B2 = 0.999
ADAM_EPS = 1e-08
ADAM_WD = 0.01
ADAM_STEP = 10

N_DEV = 8
LANES = 128
SUBLANES = 8
V7X_VMEM_LIMIT = 56 * 2**20

SHARDED = ("w_mem_kv", "w_out", "w_gate_up", "w_down", "gdn_w_in", "swa_w_q", "w_kv")
SMALL = ("ln_mix", "ln_ffn", "ln_mem", "gdn_A_log", "gdn_dt_bias", "gdn_norm", "swa_sinks", "ln_kv", "ln_final")
WEIGHTS = ("ln_mix", "ln_ffn", "ln_mem", "w_mem_kv", "w_out", "w_gate_up", "w_down", "gdn_w_in", "gdn_conv",
           "gdn_A_log", "gdn_dt_bias", "gdn_norm", "swa_w_q", "swa_sinks", "ln_kv", "w_kv", "ln_final")


def _params(sem=None, **kw):
    return pltpu.CompilerParams(dimension_semantics=sem, vmem_limit_bytes=V7X_VMEM_LIMIT, **kw)


def _dot(a, b, dims=(((1,), (0,)), ((), ())), precision=None):
    return lax.dot_general(a, b, dims, precision=precision, preferred_element_type=F32)


NT = (((1,), (1,)), ((), ()))
TN = (((0,), (0,)), ((), ()))


def _fold8(v):
    r, w = v.shape
    return v.reshape(r // SUBLANES, SUBLANES, w).sum(axis=0)


def _row(a, w=None, cb=0):
    return ("row", a, a.shape[1] if w is None else w, cb)


def _full(a):
    return ("full", a, None, None)


def _prev8(a, w, cb=0):
    return ("prev8", a, w, cb)


def _next8(a, w, cb=0):
    return ("next8", a, w, cb)


def _rowcall(fn, name, rows, bm, ins, outs, accs=()):
    bm = min(bm, rows)
    assert rows % bm == 0 and bm % SUBLANES == 0
    steps = rows // bm
    r8 = bm // SUBLANES
    in_specs, arrays = [], []
    for kind, a, w, cb in ins:
        arrays.append(a)
        if kind == "row":
            in_specs.append(pl.BlockSpec((bm, w), lambda i, cb=cb: (i, cb)))
        elif kind == "full":
            in_specs.append(pl.BlockSpec(a.shape, lambda i, nd=a.ndim: (0,) * nd))
        elif kind == "prev8":
            in_specs.append(pl.BlockSpec((SUBLANES, w), lambda i, cb=cb: (jnp.maximum(i * r8 - 1, 0), cb)))
        else:
            last = rows // SUBLANES - 1
            in_specs.append(pl.BlockSpec((SUBLANES, w), lambda i, cb=cb: (jnp.minimum((i + 1) * r8, last), cb)))
    out_shape = [jax.ShapeDtypeStruct((rows, w), dt) for w, dt in outs]
    out_specs = [pl.BlockSpec((bm, w), lambda i: (i, 0)) for w, _ in outs]
    out_shape += [jax.ShapeDtypeStruct(s, F32) for s in accs]
    out_specs += [pl.BlockSpec(s, lambda i: (0, 0)) for s in accs]
    n_in, n_out = len(ins), len(outs)

    def body(*refs):
        i = pl.program_id(0)
        res = fn(i, *[r[...] for r in refs[:n_in]])
        if not isinstance(res, (tuple, list)):
            res = (res,)
        for r, v in zip(refs[n_in:n_in + n_out], res[:n_out]):
            r[...] = v.astype(r.dtype)
        if accs:
            @pl.when(i == 0)
            def _():
                for r in refs[n_in + n_out:]:
                    r[...] = jnp.zeros(r.shape, F32)
            for r, v in zip(refs[n_in + n_out:], res[n_out:]):
                r[...] += v

    res = pl.pallas_call(
        body, name=name, grid=(steps,), in_specs=in_specs, out_specs=out_specs, out_shape=out_shape,
        compiler_params=_params(("arbitrary",)))(*arrays)
    return res


def _tile(n, cap):
    for t in (1408, 1152, 1024, 896, 768, 640, 512, 384, 256, 128):
        if t <= cap and n % t == 0:
            return t
    return n


def _mm(a, b, mode, name, out_dtype=F32, add=None):
    if mode == "tn":
        s, m = a.shape
        n = b.shape[1]
        bm, bn, bk = _tile(m, 1408), _tile(n, 512), min(s, 2048)
        nk = s // bk

        def body(a_ref, b_ref, o_ref, acc_ref):
            k = pl.program_id(2)

            @pl.when(k == 0)
            def _():
                acc_ref[...] = jnp.zeros(acc_ref.shape, F32)
            acc_ref[...] += _dot(a_ref[...].astype(BF16), b_ref[...].astype(BF16), TN)

            @pl.when(k == nk - 1)
            def _():
                o_ref[...] = acc_ref[...].astype(o_ref.dtype)

        return pl.pallas_call(
            body, name=name, grid=(m // bm, n // bn, nk),
            in_specs=[pl.BlockSpec((bk, bm), lambda i, j, k: (k, i)), pl.BlockSpec((bk, bn), lambda i, j, k: (k, j))],
            out_specs=pl.BlockSpec((bm, bn), lambda i, j, k: (i, j)),
            out_shape=jax.ShapeDtypeStruct((m, n), out_dtype),
            scratch_shapes=[pltpu.VMEM((bm, bn), F32)],
            compiler_params=_params(("parallel", "parallel", "arbitrary")))(a, b)

    m, k = a.shape
    n = b.shape[1] if mode == "nn" else b.shape[0]
    big = k > 2048
    bm = min(m, 512 if big else 1024)
    bn = _tile(n, 512 if big else 1024)
    dims = NT if mode == "nt" else (((1,), (0,)), ((), ()))
    b_spec = (pl.BlockSpec((k, bn), lambda i, j: (0, j)) if mode == "nn" else pl.BlockSpec((bn, k), lambda i, j: (j, 0)))
    in_specs = [pl.BlockSpec((bm, k), lambda i, j: (i, 0)), b_spec]
    args = [a, b]
    if add is not None:
        in_specs.append(pl.BlockSpec((bm, bn), lambda i, j: (i, j)))
        args.append(add)

    def body(a_ref, b_ref, *rest):
        o_ref = rest[-1]
        acc = _dot(a_ref[...].astype(BF16), b_ref[...].astype(BF16), dims)
        if add is not None:
            acc = acc + rest[0][...]
        o_ref[...] = acc.astype(o_ref.dtype)

    return pl.pallas_call(
        body, name=name, grid=(m // bm, n // bn), in_specs=in_specs,
        out_specs=pl.BlockSpec((bm, bn), lambda i, j: (i, j)),
        out_shape=jax.ShapeDtypeStruct((m, n), out_dtype),
        compiler_params=_params(("parallel", "parallel")))(*args)


def _sigmoid(x):
    return 1.0 / (1.0 + jnp.exp(-x))


def _softplus(x):
    return jnp.maximum(x, 0.0) + jnp.log(1.0 + jnp.exp(-jnp.abs(x)))


def _silu_and_grad(x):
    s = _sigmoid(x)
    return x * s, s * (1.0 + x * (1.0 - s))


def _rms_stats(x):
    r = lax.rsqrt(jnp.mean(x * x, axis=-1, keepdims=True) + EPS)
    return r, x * r


def _rms_fwd(x, g, name, out_dtype=BF16, bm=512):
    def fn(i, x, g):
        _, xn = _rms_stats(x)
        return xn * g
    return _rowcall(fn, name, x.shape[0], bm, [_row(x), _full(g)], [(x.shape[1], out_dtype)])[0]


def _rms_bwd_math(x, g, dy):
    r, xn = _rms_stats(x)
    dxn = dy * g
    dx = r * (dxn - xn * jnp.mean(dxn * xn, axis=-1, keepdims=True))
    return dx, dy * xn


def _rms_bwd(x, g, dy, res, name, bm=256):
    d = x.shape[1]

    def fn(i, x, g, dy, *res_):
        dx, dg = _rms_bwd_math(x, g, dy.astype(F32))
        if res_:
            dx = dx + res_[0]
        return dx, dx, _fold8(dg)
    ins = [_row(x), _full(g), _row(dy)] + ([_row(res)] if res is not None else [])
    return _rowcall(fn, name, x.shape[0], bm, ins, [(d, F32), (d, BF16)], [(SUBLANES, d)])


def _final_loss(x, g, target, name, bm=256):
    d = x.shape[1]

    def fn(i, x, g, t):
        r, xn = _rms_stats(x)
        err = xn * g - t
        dy = err * (1.0 / d)
        dxn = dy * g
        dx = r * (dxn - xn * jnp.mean(dxn * xn, axis=-1, keepdims=True))
        e2 = _fold8(err * err)
        lp = e2[:, 0:LANES]
        for c in range(1, d // LANES):
            lp = lp + e2[:, c * LANES:(c + 1) * LANES]
        return dx, dx, lp * (0.5 / d), _fold8(dy * xn)
    return _rowcall(fn, name, x.shape[0], bm, [_row(x), _full(g), _row(target)], [(d, F32), (d, BF16)],
                    [(SUBLANES, LANES), (SUBLANES, d)])


FF_TILE = 256


def _ff_interleave(w):
    lead = w.shape[:-1]
    w = w.reshape(lead + (2, D_FF // FF_TILE, FF_TILE))
    return jnp.swapaxes(w, -3, -2).reshape(lead + (2 * D_FF,))


def _ff_deinterleave(w):
    lead = w.shape[:-1]
    w = w.reshape(lead + (D_FF // FF_TILE, 2, FF_TILE))
    return jnp.swapaxes(w, -3, -2).reshape(lead + (2 * D_FF,))


def _gate_up_fwd(h, w_gu, name, bm=1024):
    rows, k = h.shape
    bm = min(bm, rows)

    def body(a_ref, b_ref, gu_ref, act_ref):
        acc = _dot(a_ref[...], b_ref[...])
        gu_ref[...] = acc.astype(gu_ref.dtype)
        act_ref[...] = (_silu_and_grad(acc[:, :FF_TILE])[0] * acc[:, FF_TILE:]).astype(act_ref.dtype)

    return pl.pallas_call(
        body, name=name, grid=(rows // bm, D_FF // FF_TILE),
        in_specs=[pl.BlockSpec((bm, k), lambda i, j: (i, 0)), pl.BlockSpec((k, 2 * FF_TILE), lambda i, j: (0, j))],
        out_specs=[pl.BlockSpec((bm, 2 * FF_TILE), lambda i, j: (i, j)), pl.BlockSpec((bm, FF_TILE), lambda i, j: (i, j))],
        out_shape=[jax.ShapeDtypeStruct((rows, 2 * D_FF), BF16), jax.ShapeDtypeStruct((rows, D_FF), BF16)],
        compiler_params=_params(("parallel", "parallel")))(h, w_gu)


def _down_bwd(dx, w_down, gu, name, bm=1024):
    rows, k = dx.shape
    bm = min(bm, rows)

    def body(a_ref, b_ref, gu_ref, o_ref):
        da = _dot(a_ref[...], b_ref[...], NT)
        gu = gu_ref[...].astype(F32)
        s, ds = _silu_and_grad(gu[:, :FF_TILE])
        o_ref[:, :FF_TILE] = (da * gu[:, FF_TILE:] * ds).astype(o_ref.dtype)
        o_ref[:, FF_TILE:] = (da * s).astype(o_ref.dtype)

    return pl.pallas_call(
        body, name=name, grid=(rows // bm, D_FF // FF_TILE),
        in_specs=[pl.BlockSpec((bm, k), lambda i, j: (i, 0)), pl.BlockSpec((FF_TILE, k), lambda i, j: (j, 0)),
                  pl.BlockSpec((bm, 2 * FF_TILE), lambda i, j: (i, j))],
        out_specs=pl.BlockSpec((bm, 2 * FF_TILE), lambda i, j: (i, j)),
        out_shape=jax.ShapeDtypeStruct((rows, 2 * D_FF), BF16),
        compiler_params=_params(("parallel", "parallel")))(dx, w_down, gu)


def _rope_apply(x, tabs, sign):
    cos, ta, tb = tabs
    outs = []
    for c in range(x.shape[1] // LANES):
        xc = x[:, c * LANES:(c + 1) * LANES]
        if sign > 0:
            o = xc * cos + pltpu.roll(xc, LANES - 8, 1) * ta + pltpu.roll(xc, 8, 1) * tb
        else:
            o = xc * cos + pltpu.roll(xc * ta, 8, 1) + pltpu.roll(xc * tb, LANES - 8, 1)
        outs.append(o)
    return outs[0] if len(outs) == 1 else jnp.concatenate(outs, axis=1)


def _rope(x, w, cb, tabs, sign, name, out_dtype, bm=512):
    def fn(i, x, c, a, b):
        return _rope_apply(x.astype(F32), (c, a, b), sign)
    return _rowcall(fn, name, x.shape[0], bm, [_row(x, w, cb)] + [_row(t) for t in tabs], [(w, out_dtype)])[0]


def _shift_down(x, prev8, s, first):
    xs = pltpu.roll(x, s, 0)
    rp = pltpu.roll(prev8, s, 0) * jnp.where(first, 0.0, 1.0)
    rid = lax.broadcasted_iota(jnp.int32, rp.shape, 0)
    top = jnp.where(rid < s, rp, xs[0:SUBLANES])
    return jnp.concatenate([top, xs[SUBLANES:]], axis=0)


def _shift_up(x, next8, s, last):
    n = x.shape[0]
    xs = pltpu.roll(x, n - s, 0)
    rn = pltpu.roll(next8, SUBLANES - s, 0) * jnp.where(last, 0.0, 1.0)
    rid = lax.broadcasted_iota(jnp.int32, rn.shape, 0)
    bot = jnp.where(rid >= SUBLANES - s, rn, xs[n - SUBLANES:])
    return jnp.concatenate([xs[:n - SUBLANES], bot], axis=0)


def _conv_fwd(x, prev8, w, first):
    acc = x * w[CONV_K - 1:CONV_K]
    shifted = []
    for s in range(1, CONV_K):
        xs = _shift_down(x, prev8, s, first)
        shifted.append(xs)
        acc = acc + xs * w[CONV_K - 1 - s:CONV_K - s]
    return acc, shifted


def _l2n(x):
    outs, rs = [], []
    for h in range(x.shape[1] // LANES):
        xh = x[:, h * LANES:(h + 1) * LANES]
        r = lax.rsqrt(jnp.sum(xh * xh, axis=-1, keepdims=True) + EPS)
        outs.append(xh * r)
        rs.append(r)
    return jnp.concatenate(outs, axis=1), rs


def _gate_math(ba, sel_b, sel_a, a_row, dt_row):
    bl = _dot(ba, sel_b, precision=HI)
    al = _dot(ba, sel_a, precision=HI) + dt_row
    beta = _sigmoid(bl)
    ea = jnp.exp(a_row)
    g = -ea * _softplus(al)
    return bl, al, beta, g, ea


def _cumsum_chunks(x, reverse=False):
    n = x.shape[0]
    rid = lax.broadcasted_iota(jnp.int32, x.shape, 0) % CHUNK
    s = 1
    while s < CHUNK:
        if reverse:
            x = x + jnp.where(rid < CHUNK - s, pltpu.roll(x, n - s, 0), 0.0)
        else:
            x = x + jnp.where(rid >= s, pltpu.roll(x, s, 0), 0.0)
        s *= 2
    return x


def _gdn_pre_fwd(proj, conv_w, sel_b, sel_a, a_row, dt_row, name, bm=256):
    rows = proj.shape[0]
    w3 = 3 * GDN_W

    def fn(i, x, p8, ba, w, sel_b, sel_a, a_row, dt_row):
        conv, _ = _conv_fwd(x, p8, w, i == 0)
        act = _silu_and_grad(conv)[0]
        qk, _ = _l2n(act[:, :2 * GDN_W])
        _, _, beta, g, _ = _gate_math(ba, sel_b, sel_a, a_row, dt_row)
        return qk[:, :GDN_W], qk[:, GDN_W:], act[:, 2 * GDN_W:], _cumsum_chunks(g), beta
    ins = [_row(proj, w3, 0), _prev8(proj, w3, 0), _row(proj, LANES, (GDN_IN_PAD - LANES) // LANES),
           _full(conv_w), _full(sel_b), _full(sel_a), _full(a_row), _full(dt_row)]
    return _rowcall(fn, name, rows, bm, ins, [(GDN_W, F32)] * 5)


def _gdn_pre_bwd(proj, conv_w, sel_b, sel_a, a_row, dt_row, dq, dk, dv, dgc, dbeta, name, bm=128):
    rows = proj.shape[0]
    w3 = 3 * GDN_W

    def fn(i, x, p8, ba, w, sel_b, sel_a, a_row, dt_row, dq, dk, dv, dgc, dbeta):
        dg = _cumsum_chunks(dgc, reverse=True)
        conv, shifted = _conv_fwd(x, p8, w, i == 0)
        act, dact = _silu_and_grad(conv)
        qk, rs = _l2n(act[:, :2 * GDN_W])
        dqk = jnp.concatenate([dq, dk], axis=1)
        parts = []
        for h in range(2 * GDN_HEADS):
            sl = slice(h * LANES, (h + 1) * LANES)
            y, dy = qk[:, sl], dqk[:, sl]
            parts.append(rs[h] * (dy - y * jnp.sum(y * dy, axis=-1, keepdims=True)))
        dconv = jnp.concatenate(parts + [dv], axis=1) * dact
        dws = [_fold8(dconv * xs) for xs in reversed(shifted)] + [_fold8(dconv * x)]
        bl, al, beta, g, ea = _gate_math(ba, sel_b, sel_a, a_row, dt_row)
        dbl = dbeta * beta * (1.0 - beta)
        dal = dg * (-ea) * _sigmoid(al)
        dba = _dot(dbl, sel_b, NT, precision=HI) + _dot(dal, sel_a, NT, precision=HI)
        return (dconv, dba * (1.0 / LANES)) + tuple(dws) + (_fold8(dg * g), _fold8(dal))
    ins = [_row(proj, w3, 0), _prev8(proj, w3, 0), _row(proj, LANES, (GDN_IN_PAD - LANES) // LANES),
           _full(conv_w), _full(sel_b), _full(sel_a), _full(a_row), _full(dt_row),
           _row(dq), _row(dk), _row(dv), _row(dgc), _row(dbeta)]
    return _rowcall(fn, name, rows, bm, ins, [(w3, F32), (LANES, BF16)],
                    [(SUBLANES, w3)] * CONV_K + [(SUBLANES, GDN_W)] * 2)


def _conv_bwd_input(dconv, conv_w, name, bm=256):
    rows, w3 = dconv.shape
    steps = rows // min(bm, rows)

    def fn(i, dc, n8, w):
        acc = dc * w[CONV_K - 1:CONV_K]
        for s in range(1, CONV_K):
            acc = acc + _shift_up(dc, n8, s, i == steps - 1) * w[CONV_K - 1 - s:CONV_K - s]
        return acc
    return _rowcall(fn, name, rows, bm, [_row(dconv), _next8(dconv, w3, 0), _full(conv_w)], [(w3, BF16)])[0]


def _gdn_post_fwd(o, proj, ng, name, bm=512):
    def fn(i, o, z, ng):
        outs = []
        for h in range(GDN_HEADS):
            sl = slice(h * LANES, (h + 1) * LANES)
            _, on = _rms_stats(o[:, sl])
            outs.append(on * ng * _silu_and_grad(z[:, sl])[0])
        return jnp.concatenate(outs, axis=1)
    return _rowcall(fn, name, o.shape[0], bm, [_row(o), _row(proj, GDN_W, 3), _full(ng)], [(GDN_W, BF16)])[0]


def _gdn_post_bwd(o, proj, ng, dcat, name, bm=256):
    def fn(i, o, z, ng, dm):
        dm = dm.astype(F32)
        dos, dzs = [], []
        dng = jnp.zeros((SUBLANES, LANES), F32)
        for h in range(GDN_HEADS):
            sl = slice(h * LANES, (h + 1) * LANES)
            s, ds = _silu_and_grad(z[:, sl])
            r, on = _rms_stats(o[:, sl])
            dzs.append(dm[:, sl] * on * ng * ds)
            dy = dm[:, sl] * s
            dxn = dy * ng
            dos.append(r * (dxn - on * jnp.mean(dxn * on, axis=-1, keepdims=True)))
            dng = dng + _fold8(dy * on)
        return jnp.concatenate(dos, axis=1), jnp.concatenate(dzs, axis=1), dng
    return _rowcall(fn, name, o.shape[0], bm, [_row(o), _row(proj, GDN_W, 3), _full(ng), _row(dcat, GDN_W, 0)],
                    [(GDN_W, F32), (GDN_W, BF16)], [(SUBLANES, LANES)])


def _split3(x):
    hi = x.astype(BF16)
    r = x - hi.astype(F32)
    mid = r.astype(BF16)
    return hi, mid, (r - mid.astype(F32)).astype(BF16)


def _bdot(a, b, mode="nn"):
    lc, rc = {"nn": (2, 1), "nt": (2, 2), "tn": (1, 1)}[mode]
    return lax.dot_general(a, b, (((lc,), (rc,)), ((0,), (0,))), preferred_element_type=F32)


def _bdot3(a, b, mode="nn"):
    ah, bh = a.astype(BF16), b.astype(BF16)
    al, bl = (a - ah.astype(F32)).astype(BF16), (b - bh.astype(F32)).astype(BF16)
    return _bdot(ah, bh, mode) + _bdot(ah, bl, mode) + _bdot(al, bh, mode)


GDN_CB = 2


def _gdn_chunk(q, k, v, gc, beta, t=None):
    c = CHUNK
    nb = q.shape[0]
    row = lax.broadcasted_iota(jnp.int32, (c, c), 0)
    col = lax.broadcasted_iota(jnp.int32, (c, c), 1)
    tril, strict = row >= col, row > col
    lane0 = (lax.broadcasted_iota(jnp.int32, (nb, c, LANES), 2) == 0).astype(BF16)
    gc_row = sum(_bdot(lane0, part, "nt") for part in _split3(gc))
    dm = jnp.exp(jnp.where(tril, gc[:, :, :c] - gc_row, -1e30))
    eg = jnp.exp(gc)
    gcl = gc[:, c - 1:c, :]
    ekg = jnp.exp(gcl - gc)
    egl = jnp.exp(gcl)
    qs = q * (GDN_DK ** -0.5)
    kb = k * beta
    kk = _bdot(kb, k, "nt")
    a = jnp.where(strict, kk * dm, 0.0)
    vb = v * beta
    kbg = kb * eg
    qk = _bdot(qs, k, "nt")
    p = jnp.where(tril, qk * dm, 0.0)
    out = dict(tril=tril, strict=strict, dm=dm, eg=eg, ekg=ekg, egl=egl, qs=qs, kb=kb, kk=kk, a=a,
               vb=vb, kbg=kbg, qk=qk, p=p, qg=qs * eg, kg=k * ekg)
    if t is None:
        y = -a
        t = (row == col).astype(F32) + y
        for _ in range(5):
            y = _bdot3(y, y)
            t = t + _bdot3(t, y)
        out.update(u=_bdot3(t, vb), w=_bdot3(t, kbg))
    out["t"] = t
    return out


def _gdn_stack(ref):
    return jnp.stack([ref[c * CHUNK:(c + 1) * CHUNK, h * LANES:(h + 1) * LANES]
                      for c in range(GDN_CB) for h in range(GDN_HEADS)])


def _gdn_unstack(x, ref):
    for c in range(GDN_CB):
        for h in range(GDN_HEADS):
            ref[c * CHUNK:(c + 1) * CHUNK, h * LANES:(h + 1) * LANES] = x[c * GDN_HEADS + h]


def _gdn_fwd(q, k, v, gc, beta, name):
    rows = q.shape[0]
    n_chunks = rows // CHUNK
    steps = n_chunks // GDN_CB
    blk = pl.BlockSpec((GDN_CB * CHUNK, GDN_W), lambda n: (n, 0))
    st = pl.BlockSpec((GDN_HEADS, GDN_CB, GDN_DK, LANES), lambda n: (0, n, 0, 0))
    tinv = pl.BlockSpec((GDN_CB, GDN_HEADS, CHUNK, CHUNK), lambda n: (n, 0, 0, 0))

    def body(q_ref, k_ref, v_ref, g_ref, b_ref, o_ref, st_ref, t_ref, w_ref, vn_ref, s_ref):
        @pl.when(pl.program_id(0) == 0)
        def _():
            s_ref[...] = jnp.zeros(s_ref.shape, F32)
        c = _gdn_chunk(*[_gdn_stack(r) for r in (q_ref, k_ref, v_ref, g_ref, b_ref)])
        _gdn_unstack(c["w"], w_ref)
        s = s_ref[...]
        for i in range(GDN_CB):
            hs = slice(i * GDN_HEADS, (i + 1) * GDN_HEADS)
            rs = slice(i * CHUNK, (i + 1) * CHUNK)
            st_ref[:, i] = s
            vn = c["u"][hs] - _bdot(c["w"][hs], s)
            o = _bdot(c["qg"][hs], s) + _bdot(c["p"][hs], vn)
            s = s * c["egl"][hs] + _bdot(c["kg"][hs], vn, "tn")
            t_ref[i] = c["t"][hs]
            for h in range(GDN_HEADS):
                o_ref[rs, h * LANES:(h + 1) * LANES] = o[h]
                vn_ref[rs, h * LANES:(h + 1) * LANES] = vn[h]
        s_ref[...] = s

    f = jax.ShapeDtypeStruct((rows, GDN_W), F32)
    return pl.pallas_call(
        body, name=name, grid=(steps,), in_specs=[blk] * 5, out_specs=[blk, st, tinv, blk, blk],
        out_shape=[f, jax.ShapeDtypeStruct((GDN_HEADS, n_chunks, GDN_DK, LANES), F32),
                   jax.ShapeDtypeStruct((n_chunks, GDN_HEADS, CHUNK, CHUNK), F32), f, f],
        scratch_shapes=[pltpu.VMEM((GDN_HEADS, GDN_DK, LANES), F32)],
        compiler_params=_params(("arbitrary",)))(q, k, v, gc, beta)


def _gdn_bwd(q, k, v, gc, beta, states, tinv, w, vn, do, name):
    rows = q.shape[0]
    n_chunks = rows // CHUNK
    steps = n_chunks // GDN_CB
    blk = pl.BlockSpec((GDN_CB * CHUNK, GDN_W), lambda n: (steps - 1 - n, 0))
    st = pl.BlockSpec((GDN_HEADS, GDN_CB, GDN_DK, LANES), lambda n: (0, steps - 1 - n, 0, 0))
    ti = pl.BlockSpec((GDN_CB, GDN_HEADS, CHUNK, CHUNK), lambda n: (steps - 1 - n, 0, 0, 0))
    nbatch = GDN_CB * GDN_HEADS

    def lanesum(x):
        return jnp.broadcast_to(jnp.sum(x, axis=-1, keepdims=True), x.shape)

    def body(q_ref, k_ref, v_ref, g_ref, b_ref, st_ref, t_ref, w_ref, vn_ref, do_ref,
             dq_ref, dk_ref, dv_ref, dg_ref, db_ref, ds_ref):
        @pl.when(pl.program_id(0) == 0)
        def _():
            ds_ref[...] = jnp.zeros(ds_ref.shape, F32)
        q, k, v, gc, beta, w, vn, do = [_gdn_stack(r) for r in (q_ref, k_ref, v_ref, g_ref, b_ref, w_ref, vn_ref, do_ref)]
        t = t_ref[...].reshape(nbatch, CHUNK, CHUNK)
        s = jnp.stack([st_ref[h, i] for i in range(GDN_CB) for h in range(GDN_HEADS)])
        c = _gdn_chunk(q, k, v, gc, beta, t)
        tril, strict, dm = c["tril"], c["strict"], c["dm"]
        dsn = ds_ref[...]
        dvn_c, dkg_c, dgl_c = [None] * GDN_CB, [None] * GDN_CB, [None] * GDN_CB
        for i in reversed(range(GDN_CB)):
            hs = slice(i * GDN_HEADS, (i + 1) * GDN_HEADS)
            dvn_c[i] = _bdot(c["p"][hs], do[hs], "tn") + _bdot(c["kg"][hs], dsn)
            dkg_c[i] = _bdot(vn[hs], dsn, "nt")
            dgl_c[i] = jnp.sum(jnp.sum(s[hs] * dsn, axis=2, keepdims=True), axis=1, keepdims=True) * c["egl"][hs]
            dsn = dsn * c["egl"][hs] + _bdot(c["qg"][hs], do[hs], "tn") - _bdot(w[hs], dvn_c[i], "tn")
        ds_ref[...] = dsn
        dvn, dkg, dgl = jnp.concatenate(dvn_c), jnp.concatenate(dkg_c), jnp.concatenate(dgl_c)
        dp = jnp.where(tril, _bdot(do, vn, "nt"), 0.0)
        dqg = _bdot(do, s, "nt")
        dw = -_bdot(dvn, s, "nt")
        dvb = _bdot3(t, dvn, "tn")
        dkbg = _bdot3(t, dw, "tn")
        dt = _bdot(dvn, c["vb"], "nt") + _bdot(dw, c["kbg"], "nt")
        da = jnp.where(strict, -_bdot3(_bdot3(t, dt, "tn"), t, "nt"), 0.0)
        dkk = da * dm
        dqk = dp * dm
        dkb = _bdot(dkk, k) + dkbg * c["eg"]
        dk = _bdot(dkk, c["kb"], "tn") + _bdot(dqk, c["qs"], "tn") + dkg * c["ekg"] + dkb * beta
        dqs = _bdot(dqk, k) + dqg * c["eg"]
        e = da * c["a"] + dp * c["p"]
        ones = jnp.ones((nbatch, CHUNK, LANES), BF16)
        col_sums = sum(_bdot(part, ones, "tn") for part in _split3(e))
        kg_term = lanesum(dkg * c["kg"])
        dgc = (jnp.broadcast_to(jnp.sum(e, axis=-1, keepdims=True), (nbatch, CHUNK, LANES)) - col_sums
               + lanesum(dqg * c["qg"]) - kg_term + lanesum(dkbg * c["kbg"]))
        dgcl = jnp.sum(kg_term, axis=1, keepdims=True) + dgl
        last = lax.broadcasted_iota(jnp.int32, (CHUNK, LANES), 0) == CHUNK - 1
        _gdn_unstack(dqs * (GDN_DK ** -0.5), dq_ref)
        _gdn_unstack(dk, dk_ref)
        _gdn_unstack(dvb * beta, dv_ref)
        _gdn_unstack(dgc + jnp.where(last, dgcl, 0.0), dg_ref)
        _gdn_unstack(lanesum(dvb * v) + lanesum(dkb * k), db_ref)

    return pl.pallas_call(
        body, name=name, grid=(steps,), in_specs=[blk] * 5 + [st, ti, blk, blk, blk], out_specs=[blk] * 5,
        out_shape=[jax.ShapeDtypeStruct((rows, GDN_W), F32)] * 5,
        scratch_shapes=[pltpu.VMEM((GDN_HEADS, GDN_DK, LANES), F32)],
        compiler_params=_params(("arbitrary",)))(q, k, v, gc, beta, states, tinv, w, vn, do)


def _swa_masks(first):
    r = lax.broadcasted_iota(jnp.int32, (SWA_BLOCK, 2 * SWA_BLOCK), 0)
    c = lax.broadcasted_iota(jnp.int32, (SWA_BLOCK, 2 * SWA_BLOCK), 1)
    band = (c > r) & (c <= r + SWA_BLOCK)
    return band & (jnp.logical_not(first) | (c >= SWA_BLOCK))


def _swa_stack(ref, j):
    lane = lax.broadcasted_iota(jnp.int32, (1, LANES), 1)
    parts = []
    for g in range(SWA_GROUP):
        ch = j * (SWA_GROUP // 2) + g // 2
        keep = (lane < SWA_DH) if g % 2 == 0 else (lane >= SWA_DH)
        parts.append(ref[:, ch * LANES:(ch + 1) * LANES] * keep.astype(ref.dtype))
    return jnp.concatenate(parts, axis=0)


def _swa_unstack(x2, j, out_ref):
    low = lax.broadcasted_iota(jnp.int32, (SWA_BLOCK, LANES), 1) < SWA_DH
    for c3 in range(SWA_GROUP // 2):
        even = x2[(2 * c3) * SWA_BLOCK:(2 * c3 + 1) * SWA_BLOCK]
        odd = x2[(2 * c3 + 1) * SWA_BLOCK:(2 * c3 + 2) * SWA_BLOCK]
        ch = j * (SWA_GROUP // 2) + c3
        out_ref[:, ch * LANES:(ch + 1) * LANES] = jnp.where(low, even, odd).astype(out_ref.dtype)


def _swa_probs(s, sink, mask):
    s = jnp.where(mask, s, -1e30)
    m = jnp.maximum(jnp.max(s, axis=-1, keepdims=True), sink)
    p = jnp.where(mask, jnp.exp(s - m), 0.0)
    es = jnp.exp(sink - m)
    inv = 1.0 / (jnp.sum(p, axis=-1, keepdims=True) + es)
    return p * inv, es * inv


def _swa_scores(q_ref, kc_ref, kp_ref, sink_ref, j, mask):
    sl = slice(j * LANES, (j + 1) * LANES)
    qst = _swa_stack(q_ref, j)
    kw = jnp.concatenate([kp_ref[:, sl], kc_ref[:, sl]], axis=0)
    s = _dot(qst, kw, NT) * (SWA_DH ** -0.5)
    ps = [_swa_probs(s[g * SWA_BLOCK:(g + 1) * SWA_BLOCK], sink_ref[j * SWA_GROUP + g], mask)
          for g in range(SWA_GROUP)]
    return qst, kw, ps


def _swa_fwd(q, k2, v2, sinks, name):
    rows = q.shape[0]
    nb = rows // SWA_BLOCK
    w = SWA_HEADS * SWA_DH
    kvw = SWA_KV_HEADS * LANES
    cur = pl.BlockSpec((SWA_BLOCK, w), lambda i: (i, 0))
    kcur = pl.BlockSpec((SWA_BLOCK, kvw), lambda i: (i, 0))
    kprev = pl.BlockSpec((SWA_BLOCK, kvw), lambda i: (jnp.maximum(i - 1, 0), 0))

    def body(sink_ref, q_ref, kc_ref, kp_ref, vc_ref, vp_ref, o_ref):
        mask = _swa_masks(pl.program_id(0) == 0)
        for j in range(SWA_KV_HEADS):
            sl = slice(j * LANES, (j + 1) * LANES)
            _, _, ps = _swa_scores(q_ref, kc_ref, kp_ref, sink_ref, j, mask)
            vw = jnp.concatenate([vp_ref[:, sl], vc_ref[:, sl]], axis=0)
            pst = jnp.concatenate([p.astype(BF16) for p, _ in ps], axis=0)
            _swa_unstack(_dot(pst, vw), j, o_ref)

    return pl.pallas_call(
        body, name=name, grid=(nb,),
        in_specs=[pl.BlockSpec(memory_space=pltpu.SMEM), cur, kcur, kprev, kcur, kprev], out_specs=cur,
        out_shape=jax.ShapeDtypeStruct((rows, w), BF16),
        compiler_params=_params(("arbitrary",)))(sinks, q, k2, k2, v2, v2)


def _swa_bwd(q, k2, v2, sinks, dcat, name):
    rows = q.shape[0]
    nb = rows // SWA_BLOCK
    w = SWA_HEADS * SWA_DH
    kvw = SWA_KV_HEADS * LANES
    cur = pl.BlockSpec((SWA_BLOCK, w), lambda i: (jnp.minimum(i, nb - 1), 0))
    kcur = pl.BlockSpec((SWA_BLOCK, kvw), lambda i: (jnp.minimum(i, nb - 1), 0))
    kprev = pl.BlockSpec((SWA_BLOCK, kvw), lambda i: (jnp.clip(i - 1, 0, nb - 1), 0))
    late = pl.BlockSpec((SWA_BLOCK, kvw), lambda i: (jnp.maximum(i - 1, 0), 0))
    acc_spec = pl.BlockSpec((SUBLANES, LANES), lambda i: (0, 0))

    def body(sink_ref, q_ref, kc_ref, kp_ref, vc_ref, vp_ref, do_ref, dq_ref, dk_ref, dv_ref, dsk_ref,
             ck_ref, cv_ref):
        i = pl.program_id(0)

        @pl.when(i == 0)
        def _():
            ck_ref[...] = jnp.zeros(ck_ref.shape, F32)
            cv_ref[...] = jnp.zeros(cv_ref.shape, F32)
            dsk_ref[...] = jnp.zeros(dsk_ref.shape, F32)

        @pl.when(i == nb)
        def _():
            dk_ref[...] = ck_ref[...]
            dv_ref[...] = cv_ref[...]

        @pl.when(i < nb)
        def _():
            mask = _swa_masks(i == 0)
            lane = lax.broadcasted_iota(jnp.int32, (SUBLANES, LANES), 1)
            dsk = jnp.zeros((SUBLANES, LANES), F32)
            for j in range(SWA_KV_HEADS):
                sl = slice(j * LANES, (j + 1) * LANES)
                qst, kw, ps = _swa_scores(q_ref, kc_ref, kp_ref, sink_ref, j, mask)
                vw = jnp.concatenate([vp_ref[:, sl], vc_ref[:, sl]], axis=0)
                dost = _swa_stack(do_ref, j)
                dpr = _dot(dost, vw, NT)
                dss = []
                for g in range(SWA_GROUP):
                    p, sink_p = ps[g]
                    dpg = dpr[g * SWA_BLOCK:(g + 1) * SWA_BLOCK]
                    delta = jnp.sum(p * dpg, axis=-1, keepdims=True)
                    dss.append((p * (dpg - delta)).astype(BF16))
                    dsg = jnp.sum(-sink_p * delta, axis=0, keepdims=True)
                    dsk = dsk + jnp.where(lane == j * SWA_GROUP + g, dsg, 0.0)
                dsst = jnp.concatenate(dss, axis=0)
                pst = jnp.concatenate([p.astype(BF16) for p, _ in ps], axis=0)
                _swa_unstack(_dot(dsst, kw) * (SWA_DH ** -0.5), j, dq_ref)
                dk = _dot(dsst, qst, TN) * (SWA_DH ** -0.5)
                dv = _dot(pst, dost, TN)
                dk = dk + pltpu.roll(dk, SWA_DH, 1)
                dv = dv + pltpu.roll(dv, SWA_DH, 1)
                dk_ref[:, sl] = ck_ref[:, sl] + dk[:SWA_BLOCK]
                dv_ref[:, sl] = cv_ref[:, sl] + dv[:SWA_BLOCK]
                ck_ref[:, sl] = dk[SWA_BLOCK:]
                cv_ref[:, sl] = dv[SWA_BLOCK:]
            dsk_ref[...] += dsk

    f = jax.ShapeDtypeStruct((rows, kvw), F32)
    return pl.pallas_call(
        body, name=name, grid=(nb + 1,),
        in_specs=[pl.BlockSpec(memory_space=pltpu.SMEM), cur, kcur, kprev, kcur, kprev, cur],
        out_specs=[cur, late, late, acc_spec],
        out_shape=[jax.ShapeDtypeStruct((rows, w), F32), f, f, jax.ShapeDtypeStruct((SUBLANES, LANES), F32)],
        scratch_shapes=[pltpu.VMEM((SWA_BLOCK, kvw), F32), pltpu.VMEM((SWA_BLOCK, kvw), F32)],
        compiler_params=_params(("arbitrary",)))(sinks, q, k2, k2, v2, v2, dcat)


def _mem_probs(mq, kbd):
    s = _dot(mq.astype(BF16), kbd) * (MEM_DH ** -0.5)
    ps = []
    for h in range(MEM_HEADS):
        sh = s[:, h * MEM_LEN:(h + 1) * MEM_LEN]
        e = jnp.exp(sh - jnp.max(sh, axis=-1, keepdims=True))
        ps.append(e / jnp.sum(e, axis=-1, keepdims=True))
    return ps


def _mem_fwd(proj, cb, kbd, vbd, name, bm=512):
    def fn(i, mq, kbd, vbd):
        p = jnp.concatenate(_mem_probs(mq, kbd), axis=1)
        return _dot(p.astype(BF16), vbd)
    return _rowcall(fn, name, proj.shape[0], bm, [_row(proj, MEM_W, cb), _full(kbd), _full(vbd)], [(MEM_W, BF16)])[0]


def _mem_bwd(proj, cb, kbd, vbd, dcat, name, bm=512):
    def fn(i, mq, kbd, vbd, do):
        ps = _mem_probs(mq, kbd)
        dp = _dot(do, vbd, NT)
        dss = []
        for h in range(MEM_HEADS):
            dph = dp[:, h * MEM_LEN:(h + 1) * MEM_LEN]
            dss.append(ps[h] * (dph - jnp.sum(ps[h] * dph, axis=-1, keepdims=True)))
        ds = (jnp.concatenate(dss, axis=1) * (MEM_DH ** -0.5)).astype(BF16)
        p = jnp.concatenate(ps, axis=1).astype(BF16)
        return _dot(ds, kbd, NT), _dot(mq.astype(BF16), ds, TN), _dot(p, do, TN)
    return _rowcall(fn, name, proj.shape[0], bm, [_row(proj, MEM_W, cb), _full(kbd), _full(vbd), _row(dcat, MEM_W, 3)],
                    [(MEM_W, BF16)], [(MEM_W, MEM_HEADS * MEM_LEN), (MEM_HEADS * MEM_LEN, MEM_W)])


def _mem_expand(mkv):
    feat_head = jnp.arange(MEM_W) // MEM_DH
    slot_head = jnp.arange(MEM_HEADS * MEM_LEN) // MEM_LEN
    on = feat_head[:, None] == slot_head[None, :]
    kbd = jnp.where(on, jnp.tile(mkv[:, :MEM_W].T, (1, MEM_HEADS)), 0.0)
    vbd = jnp.where(on.T, jnp.tile(mkv[:, MEM_W:], (MEM_HEADS, 1)), 0.0)
    return kbd.astype(BF16), vbd.astype(BF16)


def _mem_collapse(dkbd, dvbd):
    dk = [dkbd[h * MEM_DH:(h + 1) * MEM_DH, h * MEM_LEN:(h + 1) * MEM_LEN].T for h in range(MEM_HEADS)]
    dv = [dvbd[h * MEM_LEN:(h + 1) * MEM_LEN, h * MEM_DH:(h + 1) * MEM_DH] for h in range(MEM_HEADS)]
    return jnp.concatenate(dk + dv, axis=1)


def _adamw(w, g, m, v, name, bm=512):
    def fn(i, w, g, m, v):
        m = ADAM_B1 * m + (1.0 - ADAM_B1) * g
        v = ADAM_B2 * v + (1.0 - ADAM_B2) * (g * g)
        m_hat = m / (1.0 - ADAM_B1 ** ADAM_STEP)
        v_hat = v / (1.0 - ADAM_B2 ** ADAM_STEP)
        return -ADAM_LR * (m_hat / (jnp.sqrt(v_hat) + ADAM_EPS) + ADAM_WD * w), m, v
    d = w.shape[1]
    return _rowcall(fn, name, w.shape[0], bm, [_row(w), _row(g), _row(m), _row(v)], [(d, F32)] * 3)


def _sum_slots(buf, name, bm=256):
    n, rows, w = buf.shape
    bm = min(bm, rows)
    assert rows % bm == 0

    def body(b_ref, o_ref):
        acc = b_ref[0].astype(F32)
        for s in range(1, n):
            acc = acc + b_ref[s].astype(F32)
        o_ref[...] = acc

    return pl.pallas_call(
        body, name=name, grid=(rows // bm,), in_specs=[pl.BlockSpec((n, bm, w), lambda i: (0, i, 0))],
        out_specs=pl.BlockSpec((bm, w), lambda i: (i, 0)), out_shape=jax.ShapeDtypeStruct((rows, w), F32),
        compiler_params=_params(("parallel",)))(buf)


def _exchange(src, masks, name, chunks=1):
    same = src.ndim == 2
    rows, w = src.shape[-2:]
    slots = N_DEV if len(masks) == N_DEV - 1 else 2
    n = len(masks) * chunks
    cr = rows // chunks
    assert rows % chunks == 0 and cr % SUBLANES == 0

    def body(src_ref, out_ref, send_sems, recv_sems, local_sem):
        x, y, c = lax.axis_index("x"), lax.axis_index("y"), lax.axis_index("c")
        me = 4 * x + 2 * y + c

        def flip(v, bit):
            return 1 - v if bit else v

        def slot_of(dev):
            return dev if slots == N_DEV else dev % 2

        def piece(p):
            return src_ref if same else src_ref.at[p]

        mine = pltpu.make_async_copy(piece(me), out_ref.at[slot_of(me)], local_sem)
        mine.start()
        copies = []
        for idx, k in enumerate(masks):
            peer = (flip(x, k & 4), flip(y, k & 2), flip(c, k & 1))
            peer_id = 4 * peer[0] + 2 * peer[1] + peer[2]
            for j in range(chunks):
                part = pl.ds(j * cr, cr)
                sem = idx * chunks + j
                cp = pltpu.make_async_remote_copy(
                    src_ref=piece(peer_id).at[part], dst_ref=out_ref.at[slot_of(me), part],
                    send_sem=send_sems.at[sem], recv_sem=recv_sems.at[sem], device_id=peer, device_id_type=MESH)
                cp.start()
                copies.append((cp, pltpu.make_async_remote_copy(
                    src_ref=piece(peer_id).at[part], dst_ref=out_ref.at[slot_of(peer_id), part],
                    send_sem=send_sems.at[sem], recv_sem=recv_sems.at[sem], device_id=peer, device_id_type=MESH)))
        for cp, landing in copies:
            cp.wait_send()
            landing.wait_recv()
        mine.wait()

    any_spec = pl.BlockSpec(memory_space=pl.ANY)
    return pl.pallas_call(
        body, name=name, in_specs=[any_spec], out_specs=any_spec,
        out_shape=jax.ShapeDtypeStruct((slots, rows, w), src.dtype),
        scratch_shapes=[pltpu.SemaphoreType.DMA((n,)), pltpu.SemaphoreType.DMA((n,)), pltpu.SemaphoreType.DMA],
        )(src)


ALL_PEERS = tuple(range(1, N_DEV))
SIBLING = (1,)


def _pack(arrays, rows):
    flat = jnp.concatenate([a.reshape(-1) for a in arrays])
    return jnp.pad(flat, (0, rows * D_MODEL - flat.shape[0])).reshape(rows, D_MODEL)


def _unpack(buf, shapes):
    flat = buf.reshape(-1)
    out, off = [], 0
    for s in shapes:
        n = math.prod(s)
        out.append(flat[off:off + n].reshape(s))
        off += n
    return out


def _rows_for(shapes, mult):
    n = sum(math.prod(s) for s in shapes)
    rows = -(-n // D_MODEL)
    return -(-rows // mult) * mult


SHARD_AXIS = dict(w_mem_kv=1, w_out=1, w_gate_up=2, w_down=1, gdn_w_in=2, swa_w_q=1, w_kv=0, gdn_conv=2)


def _split4(a, axis):
    return jnp.split(a, 4, axis=axis)


def _rope_tables(positions):
    half = ROT_DIM // 2
    inv = ROPE_THETA ** (-jnp.arange(0, ROT_DIM, 2, dtype=F32) / ROT_DIM)
    ang = positions.astype(F32)[:, None] * inv
    cos, sin = jnp.cos(ang), jnp.sin(ang)
    rows = positions.shape[0]
    one = jnp.ones((rows, SWA_DH - ROT_DIM), F32)
    zero = jnp.zeros((rows, SWA_DH - ROT_DIM), F32)
    zh = jnp.zeros((rows, half), F32)
    c64 = jnp.concatenate([cos, cos, one], axis=1)
    a64 = jnp.concatenate([-sin, zh, zero], axis=1)
    b64 = jnp.concatenate([zh, sin, zero], axis=1)
    return tuple(jnp.concatenate([t, t], axis=1) for t in (c64, a64, b64))


def _pair_heads(t):
    return jnp.concatenate([t[:, :SWA_DH], t[:, :SWA_DH], t[:, SWA_DH:], t[:, SWA_DH:]], axis=1)


def _unpair_heads(t):
    return jnp.concatenate([t[:, :SWA_DH], t[:, LANES:LANES + SWA_DH]], axis=1)


def _gdn_in_pad(w):
    o2 = 4 * GDN_W
    pad = jnp.zeros((w.shape[0], GDN_IN_PAD - GDN_IN), w.dtype)
    return jnp.concatenate([w[:, :o2], w[:, o2 + 2 * GDN_HEADS:], w[:, o2:o2 + 2 * GDN_HEADS], pad], axis=1)


def _gdn_in_unpad(w):
    o2 = 4 * GDN_W
    return jnp.concatenate([w[:, :o2], w[:, o2 + MEM_W:o2 + MEM_W + 2 * GDN_HEADS], w[:, o2:o2 + MEM_W]], axis=1)


def _head_rows(v):
    return jnp.repeat(v.astype(F32), LANES)[None, :]


def _selectors():
    lane = jnp.arange(LANES)[:, None]
    head = (jnp.arange(GDN_W) // LANES)[None, :]
    return (lane == head).astype(F32), (lane == head + GDN_HEADS).astype(F32)


def _local_step(x, mem, positions, target, w):
    rows = x.shape[0]
    tabs = _rope_tables(positions)
    sel_b, sel_a = _selectors()
    row2 = lambda v: v.reshape(1, -1).astype(F32)

    w_gu = _ff_interleave(w["w_gate_up"])
    mem_n = _rms_fwd(mem, row2(w["ln_mem"]), "mem_norm")
    saved = []
    kt = vt = None
    for l in range(DEPTH):
        s = dict(x0=x)
        h = _rms_fwd(x, row2(w["ln_mix"][l]), f"norm_mix{l}")
        mkv = _mm(mem_n, w["w_mem_kv"][l], "nn", f"mem_kv{l}")
        kbd, vbd = _mem_expand(mkv)
        if l < N_A:
            proj = _mm(h, w["gdn_w_in"][l], "nn", f"gdn_in{l}")
            a_row, dt_row = _head_rows(w["gdn_A_log"][l]), _head_rows(w["gdn_dt_bias"][l])
            q, k, v, gc, beta = _gdn_pre_fwd(proj, w["gdn_conv"][l], sel_b, sel_a, a_row, dt_row, f"gdn_pre{l}")
            o, states, tinv, gw, vn = _gdn_fwd(q, k, v, gc, beta, f"gdn_scan{l}")
            mix = _gdn_post_fwd(o, proj, row2(w["gdn_norm"][l]), f"gdn_post{l}")
            mq_cb = (3 * GDN_W + GDN_W) // MEM_W
            s.update(q=q, k=k, v=v, gc=gc, beta=beta, o=o, states=states, tinv=tinv, gw=gw, vn=vn,
                     a_row=a_row, dt_row=dt_row)
        else:
            proj = _mm(h, w["swa_w_q"][l - N_A], "nn", f"swa_in{l}")
            qr = _rope(proj, SWA_HEADS * SWA_DH, 0, tabs, 1, f"rope_q{l}", BF16)
            mix = _swa_fwd(qr, kt, vt, w["swa_sinks"][l - N_A], f"swa{l}")
            mq_cb = (SWA_HEADS * SWA_DH) // MEM_W
            s.update(qr=qr)
        mem_o = _mem_fwd(proj, mq_cb, kbd, vbd, f"mem_attn{l}")
        cat = jnp.concatenate([mix, mem_o], axis=1)
        x1 = _mm(cat, w["w_out"][l], "nn", f"out_proj{l}", add=x)
        h2 = _rms_fwd(x1, row2(w["ln_ffn"][l]), f"norm_ffn{l}")
        gu, act = _gate_up_fwd(h2, w_gu[l], f"gate_up{l}")
        x = _mm(act, w["w_down"][l], "nn", f"down{l}", add=x1)
        s.update(h=h, proj=proj, kbd=kbd, vbd=vbd, mq_cb=mq_cb, cat=cat, x1=x1, h2=h2, gu=gu, act=act)
        saved.append(s)
        if l == N_A - 1:
            x_kv = x
            h_kv = _rms_fwd(x, row2(w["ln_kv"]), "norm_kv")
            kv = _mm(h_kv, w["w_kv"], "nn", "kv_proj")
            kr = _rope(kv, LANES, 0, tabs, 1, "rope_k", F32)
            kt = _pair_heads(kr).astype(BF16)
            vt = _pair_heads(kv[:, LANES:]).astype(BF16)

    gr = {}
    dx, dxb, loss_part, dlnf = _final_loss(x, row2(w["ln_final"]), target, "final_loss")
    gr["ln_final"] = dlnf.sum(axis=0)
    dln_mix, dln_ffn = [None] * DEPTH, [None] * DEPTH
    dw_mem_kv, dw_out, dw_gu, dw_dn = [None] * DEPTH, [None] * DEPTH, [None] * DEPTH, [None] * DEPTH
    dgdn_in, dgdn_conv, dgdn_a, dgdn_dt, dgdn_norm = [None] * N_A, [None] * N_A, [None] * N_A, [None] * N_A, [None] * N_A
    dswa_q, dswa_sinks = [None] * N_B, [None] * N_B
    dmem_n = None
    dkt = dvt = None
    for l in reversed(range(DEPTH)):
        s = saved[l]
        if l == N_A - 1:
            dkr = _unpair_heads(dkt)
            dk = _rope(dkr, LANES, 0, tabs, -1, "rope_k_bwd", BF16)
            dkv = jnp.concatenate([dk, _unpair_heads(dvt).astype(BF16)], axis=1)
            dh_kv = _mm(dkv, w["w_kv"], "nt", "kv_proj_dx")
            gr["w_kv"] = _mm(h_kv, dkv, "tn", "kv_proj_dw", BF16)
            dx, dxb, dg = _rms_bwd(x_kv, row2(w["ln_kv"]), dh_kv, dx, "norm_kv_bwd")
            gr["ln_kv"] = dg.sum(axis=0)
        dgu = _down_bwd(dxb, w["w_down"][l], s["gu"], f"down_dx{l}")
        dw_dn[l] = _mm(s["act"], dxb, "tn", f"down_dw{l}", BF16)
        dh2 = _mm(dgu, w_gu[l], "nt", f"gate_up_dx{l}")
        dw_gu[l] = _mm(s["h2"], dgu, "tn", f"gate_up_dw{l}", BF16)
        dx, dxb, dg = _rms_bwd(s["x1"], row2(w["ln_ffn"][l]), dh2, dx, f"norm_ffn_bwd{l}")
        dln_ffn[l] = dg.sum(axis=0)
        dcat = _mm(dxb, w["w_out"][l], "nt", f"out_proj_dx{l}", BF16)
        dw_out[l] = _mm(s["cat"], dxb, "tn", f"out_proj_dw{l}", BF16)
        dmq, dkbd, dvbd = _mem_bwd(s["proj"], s["mq_cb"], s["kbd"], s["vbd"], dcat, f"mem_attn_bwd{l}")
        dmkv = _mem_collapse(dkbd, dvbd).astype(BF16)
        dw_mem_kv[l] = _mm(mem_n, dmkv, "tn", f"mem_kv_dw{l}", BF16)
        dmem_n = _mm(dmkv, w["w_mem_kv"][l], "nt", f"mem_kv_dx{l}", add=dmem_n)
        if l < N_A:
            do, dz, dng = _gdn_post_bwd(s["o"], s["proj"], row2(w["gdn_norm"][l]), dcat, f"gdn_post_bwd{l}")
            dq, dk, dv, dg_, dbeta = _gdn_bwd(s["q"], s["k"], s["v"], s["gc"], s["beta"], s["states"], s["tinv"], s["gw"],
                                              s["vn"], do, f"gdn_scan_bwd{l}")
            res = _gdn_pre_bwd(s["proj"], w["gdn_conv"][l], sel_b, sel_a, s["a_row"], s["dt_row"], dq, dk, dv, dg_, dbeta,
                               f"gdn_pre_bwd{l}")
            dconv, dba = res[0], res[1]
            dgdn_conv[l] = jnp.stack([r.sum(axis=0) for r in res[2:2 + CONV_K]])
            dgdn_a[l] = res[2 + CONV_K].sum(axis=0)[::LANES]
            dgdn_dt[l] = res[3 + CONV_K].sum(axis=0)[::LANES]
            dgdn_norm[l] = dng.sum(axis=0)
            dqkv = _conv_bwd_input(dconv, w["gdn_conv"][l], f"gdn_conv_bwd{l}")
            dproj = jnp.concatenate([dqkv, dz, dmq, dba], axis=1)
            dh = _mm(dproj, w["gdn_w_in"][l], "nt", f"gdn_in_dx{l}")
            dgdn_in[l] = _mm(s["h"], dproj, "tn", f"gdn_in_dw{l}", BF16)
        else:
            b = l - N_A
            dqr, dkt_l, dvt_l, dsk = _swa_bwd(s["qr"], kt, vt, w["swa_sinks"][b], dcat, f"swa_bwd{l}")
            dkt = dkt_l if dkt is None else dkt + dkt_l
            dvt = dvt_l if dvt is None else dvt + dvt_l
            dswa_sinks[b] = dsk[0, :SWA_HEADS]
            dq = _rope(dqr, SWA_HEADS * SWA_DH, 0, tabs, -1, f"rope_q_bwd{l}", BF16)
            dproj = jnp.concatenate([dq, dmq], axis=1)
            dh = _mm(dproj, w["swa_w_q"][b], "nt", f"swa_in_dx{l}")
            dswa_q[b] = _mm(s["h"], dproj, "tn", f"swa_in_dw{l}", BF16)
        dx, dxb, dg = _rms_bwd(s["x0"], row2(w["ln_mix"][l]), dh, dx, f"norm_mix_bwd{l}")
        dln_mix[l] = dg.sum(axis=0)
    _, _, dg = _rms_bwd(mem, row2(w["ln_mem"]), dmem_n, None, "mem_norm_bwd")
    gr["ln_mem"] = dg.sum(axis=0)
    gr.update(ln_mix=jnp.stack(dln_mix), ln_ffn=jnp.stack(dln_ffn), w_mem_kv=jnp.stack(dw_mem_kv), w_out=jnp.stack(dw_out),
              w_gate_up=_ff_deinterleave(jnp.stack(dw_gu)), w_down=jnp.stack(dw_dn), gdn_w_in=jnp.stack(dgdn_in), gdn_conv=jnp.stack(dgdn_conv),
              gdn_A_log=jnp.stack(dgdn_a), gdn_dt_bias=jnp.stack(dgdn_dt), gdn_norm=jnp.stack(dgdn_norm),
              swa_w_q=jnp.stack(dswa_q), swa_sinks=jnp.stack(dswa_sinks))
    return loss_part, dx, gr


def kernel(x, mem, positions, ln_mix, ln_ffn, ln_mem, w_mem_kv, w_out, w_gate_up, w_down, gdn_w_in, gdn_conv, gdn_A_log, gdn_dt_bias, gdn_norm, swa_w_q, swa_sinks, ln_kv, w_kv, ln_final, loss_target, m_ln_mix, m_ln_ffn, m_ln_mem, m_w_mem_kv, m_w_out, m_w_gate_up, m_w_down, m_gdn_w_in, m_gdn_conv, m_gdn_A_log, m_gdn_dt_bias, m_gdn_norm, m_swa_w_q, m_swa_sinks, m_ln_kv, m_w_kv, m_ln_final, v_ln_mix, v_ln_ffn, v_ln_mem, v_w_mem_kv, v_w_out, v_w_gate_up, v_w_down, v_gdn_w_in, v_gdn_conv, v_gdn_A_log, v_gdn_dt_bias, v_gdn_norm, v_swa_w_q, v_swa_sinks, v_ln_kv, v_w_kv, v_ln_final):
    given = dict(locals())
    wts = {n: given[n] for n in WEIGHTS}
    c = lax.axis_index("c")

    shard_shapes = [wts[n].shape for n in SHARDED]
    rows_w = _rows_for(shard_shapes, 512)
    half = rows_w // 2
    wpack = _pack([wts[n] for n in SHARDED], rows_w)
    my_half = lax.dynamic_slice_in_dim(wpack.astype(BF16), c * half, half, axis=0)
    gathered = _exchange(my_half, ALL_PEERS, "gather_weights").reshape(4, rows_w, D_MODEL)
    conv_shape = wts["gdn_conv"].shape
    cpack = _pack([wts["gdn_conv"]], 16)
    conv_half = lax.dynamic_slice_in_dim(cpack, c * SUBLANES, SUBLANES, axis=0)
    conv_all = _exchange(conv_half, ALL_PEERS, "gather_conv").reshape(4, 16, D_MODEL)
    full = {n: wts[n] for n in SMALL}
    per_chip = [_unpack(gathered[s], shard_shapes) for s in range(4)]
    for i, n in enumerate(SHARDED):
        full[n] = jnp.concatenate([per_chip[s][i] for s in range(4)], axis=SHARD_AXIS[n])
    full["gdn_conv"] = jnp.concatenate([_unpack(conv_all[s], [conv_shape])[0] for s in range(4)], axis=2)
    full["gdn_w_in"] = jnp.stack([_gdn_in_pad(full["gdn_w_in"][a]) for a in range(N_A)])

    loss_part, dx, gr = _local_step(x[0], mem[0], positions[0], loss_target[0], full)
    gr["gdn_w_in"] = jnp.stack([_gdn_in_unpad(gr["gdn_w_in"][a]) for a in range(N_A)])

    pieces = []
    for s in range(4):
        pieces.append(_pack([_split4(gr[n], SHARD_AXIS[n])[s].astype(BF16) for n in SHARDED], rows_w))
    gpack = jnp.stack(pieces).reshape(N_DEV, half, D_MODEL)
    parts = _exchange(gpack, ALL_PEERS, "scatter_grads")
    mine = _sum_slots(parts, "sum_grads")
    g_shard = _exchange(mine, SIBLING, "swap_grad_halves", chunks=8).reshape(rows_w, D_MODEL)

    small_shapes = [wts[n].shape for n in SMALL] + [conv_shape[:2] + (4 * conv_shape[2],), (SUBLANES, LANES)]
    rows_s = _rows_for(small_shapes, SUBLANES)
    spack = _pack([gr[n] for n in SMALL] + [gr["gdn_conv"], loss_part], rows_s)
    sparts = _exchange(spack, ALL_PEERS, "share_small")
    ssum = _unpack(_sum_slots(sparts, "sum_small"), small_shapes)
    g_small = dict(zip(SMALL, ssum[:len(SMALL)]))
    chip = 2 * lax.axis_index("x") + lax.axis_index("y")
    g_conv = lax.dynamic_slice_in_dim(ssum[len(SMALL)], chip * conv_shape[2], conv_shape[2], axis=2)
    loss = jnp.sum(ssum[-1])

    m_pack = _pack([given["m_" + n] for n in SHARDED], rows_w)
    v_pack = _pack([given["v_" + n] for n in SHARDED], rows_w)
    d_b, m_b, v_b = _adamw(wpack, g_shard, m_pack, v_pack, "adamw_shards")
    small_names = SMALL + ("gdn_conv",)
    small_w_shapes = [wts[n].shape for n in small_names]
    rows_a = _rows_for(small_w_shapes, SUBLANES)
    g_small["gdn_conv"] = g_conv
    d_s, m_s, v_s = _adamw(_pack([wts[n] for n in small_names], rows_a), _pack([g_small[n] for n in small_names], rows_a),
                           _pack([given["m_" + n] for n in small_names], rows_a),
                           _pack([given["v_" + n] for n in small_names], rows_a), "adamw_small")
    out = {}
    for kind, big, small in (("grad", g_shard, None), ("delta", d_b, d_s), ("new_m", m_b, m_s), ("new_v", v_b, v_s)):
        out[kind] = dict(zip(SHARDED, _unpack(big, shard_shapes)))
        if small is None:
            out[kind].update(g_small)
        else:
            out[kind].update(zip(small_names, _unpack(small, small_w_shapes)))
    return (loss, dx[None], *[out["grad"][n] for n in WEIGHTS], *[out["delta"][n] for n in WEIGHTS],
            *[out["new_m"][n] for n in WEIGHTS], *[out["new_v"][n] for n in WEIGHTS])
```

```python
import functools
import math

import jax
import jax.numpy as jnp
from jax import lax
from jax.experimental import pallas as pl
from jax.experimental.pallas import tpu as pltpu

F32 = jnp.float32
BF16 = jnp.bfloat16
HI = lax.Precision.HIGHEST
MESH = pl.DeviceIdType.MESH

D_MODEL = 1024
DEPTH = 4
N_A = 2
N_B = 2
EPS = 1e-6
GDN_HEADS = 6
GDN_DK = 128
GDN_W = 768
CONV_K = 4
CHUNK = 64
SWA_HEADS = 12
SWA_KV_HEADS = 2
SWA_DH = 64
SWA_GROUP = 6
SWA_GW = SWA_GROUP * SWA_DH
SWA_BLOCK = 128
ROPE_THETA = 500000.0
ROT_DIM = 16
MEM_LEN = 256
MEM_HEADS = 4
MEM_DH = 64
MEM_W = 256
D_FF = 2816
GDN_IN = 3340
GDN_IN_PAD = 3456
ADAM_LR = 0.001
ADAM_B1 = 0.9
ADAM_B2 = 0.999
ADAM_EPS = 1e-08
ADAM_WD = 0.01
ADAM_STEP = 10

N_DEV = 8
LANES = 128
SUBLANES = 8
V7X_VMEM_LIMIT = 56 * 2**20

SHARDED = ("w_mem_kv", "w_out", "w_gate_up", "w_down", "gdn_w_in", "swa_w_q", "w_kv")
SMALL = ("ln_mix", "ln_ffn", "ln_mem", "gdn_A_log", "gdn_dt_bias", "gdn_norm", "swa_sinks", "ln_kv", "ln_final")
WEIGHTS = ("ln_mix", "ln_ffn", "ln_mem", "w_mem_kv", "w_out", "w_gate_up", "w_down", "gdn_w_in", "gdn_conv",
           "gdn_A_log", "gdn_dt_bias", "gdn_norm", "swa_w_q", "swa_sinks", "ln_kv", "w_kv", "ln_final")


def _params(sem=None, **kw):
    return pltpu.CompilerParams(dimension_semantics=sem, vmem_limit_bytes=V7X_VMEM_LIMIT, **kw)


def _dot(a, b, dims=(((1,), (0,)), ((), ())), precision=None):
    return lax.dot_general(a, b, dims, precision=precision, preferred_element_type=F32)


NT = (((1,), (1,)), ((), ()))
TN = (((0,), (0,)), ((), ()))


def _fold8(v):
    r, w = v.shape
    return v.reshape(r // SUBLANES, SUBLANES, w).sum(axis=0)


def _row(a, w=None, cb=0):
    return ("row", a, a.shape[1] if w is None else w, cb)


def _full(a):
    return ("full", a, None, None)


def _prev8(a, w, cb=0):
    return ("prev8", a, w, cb)


def _next8(a, w, cb=0):
    return ("next8", a, w, cb)


def _rowcall(fn, name, rows, bm, ins, outs, accs=()):
    bm = min(bm, rows)
    assert rows % bm == 0 and bm % SUBLANES == 0
    steps = rows // bm
    r8 = bm // SUBLANES
    in_specs, arrays = [], []
    for kind, a, w, cb in ins:
        arrays.append(a)
        if kind == "row":
            in_specs.append(pl.BlockSpec((bm, w), lambda i, cb=cb: (i, cb)))
        elif kind == "full":
            in_specs.append(pl.BlockSpec(a.shape, lambda i, nd=a.ndim: (0,) * nd))
        elif kind == "prev8":
            in_specs.append(pl.BlockSpec((SUBLANES, w), lambda i, cb=cb: (jnp.maximum(i * r8 - 1, 0), cb)))
        else:
            last = rows // SUBLANES - 1
            in_specs.append(pl.BlockSpec((SUBLANES, w), lambda i, cb=cb: (jnp.minimum((i + 1) * r8, last), cb)))
    out_shape = [jax.ShapeDtypeStruct((rows, w), dt) for w, dt in outs]
    out_specs = [pl.BlockSpec((bm, w), lambda i: (i, 0)) for w, _ in outs]
    out_shape += [jax.ShapeDtypeStruct(s, F32) for s in accs]
    out_specs += [pl.BlockSpec(s, lambda i: (0, 0)) for s in accs]
    n_in, n_out = len(ins), len(outs)

    def body(*refs):
        i = pl.program_id(0)
        res = fn(i, *[r[...] for r in refs[:n_in]])
        if not isinstance(res, (tuple, list)):
            res = (res,)
        for r, v in zip(refs[n_in:n_in + n_out], res[:n_out]):
            r[...] = v.astype(r.dtype)
        if accs:
            @pl.when(i == 0)
            def _():
                for r in refs[n_in + n_out:]:
                    r[...] = jnp.zeros(r.shape, F32)
            for r, v in zip(refs[n_in + n_out:], res[n_out:]):
                r[...] += v

    res = pl.pallas_call(
        body, name=name, grid=(steps,), in_specs=in_specs, out_specs=out_specs, out_shape=out_shape,
        compiler_params=_params(("arbitrary",)))(*arrays)
    return res


def _tile(n, cap):
    for t in (1408, 1152, 1024, 896, 768, 640, 512, 384, 256, 128):
        if t <= cap and n % t == 0:
            return t
    return n


def _mm(a, b, mode, name, out_dtype=F32, add=None):
    if mode == "tn":
        s, m = a.shape
        n = b.shape[1]
        bm, bn, bk = _tile(m, 1408), _tile(n, 512), min(s, 2048)
        nk = s // bk

        def body(a_ref, b_ref, o_ref, acc_ref):
            k = pl.program_id(2)

            @pl.when(k == 0)
            def _():
                acc_ref[...] = jnp.zeros(acc_ref.shape, F32)
            acc_ref[...] += _dot(a_ref[...].astype(BF16), b_ref[...].astype(BF16), TN)

            @pl.when(k == nk - 1)
            def _():
                o_ref[...] = acc_ref[...].astype(o_ref.dtype)

        return pl.pallas_call(
            body, name=name, grid=(m // bm, n // bn, nk),
            in_specs=[pl.BlockSpec((bk, bm), lambda i, j, k: (k, i)), pl.BlockSpec((bk, bn), lambda i, j, k: (k, j))],
            out_specs=pl.BlockSpec((bm, bn), lambda i, j, k: (i, j)),
            out_shape=jax.ShapeDtypeStruct((m, n), out_dtype),
            scratch_shapes=[pltpu.VMEM((bm, bn), F32)],
            compiler_params=_params(("parallel", "parallel", "arbitrary")))(a, b)

    m, k = a.shape
    n = b.shape[1] if mode == "nn" else b.shape[0]
    big = k > 2048
    bm = min(m, 512 if big else 1024)
    bn = _tile(n, 512 if big else 1024)
    dims = NT if mode == "nt" else (((1,), (0,)), ((), ()))
    b_spec = (pl.BlockSpec((k, bn), lambda i, j: (0, j)) if mode == "nn" else pl.BlockSpec((bn, k), lambda i, j: (j, 0)))
    in_specs = [pl.BlockSpec((bm, k), lambda i, j: (i, 0)), b_spec]
    args = [a, b]
    if add is not None:
        in_specs.append(pl.BlockSpec((bm, bn), lambda i, j: (i, j)))
        args.append(add)

    def body(a_ref, b_ref, *rest):
        o_ref = rest[-1]
        acc = _dot(a_ref[...].astype(BF16), b_ref[...].astype(BF16), dims)
        if add is not None:
            acc = acc + rest[0][...]
        o_ref[...] = acc.astype(o_ref.dtype)

    return pl.pallas_call(
        body, name=name, grid=(m // bm, n // bn), in_specs=in_specs,
        out_specs=pl.BlockSpec((bm, bn), lambda i, j: (i, j)),
        out_shape=jax.ShapeDtypeStruct((m, n), out_dtype),
        compiler_params=_params(("parallel", "parallel")))(*args)


def _sigmoid(x):
    return 0.5 * jnp.tanh(0.5 * x) + 0.5


def _softplus(x):
    return jnp.maximum(x, 0.0) + jnp.log(1.0 + jnp.exp(-jnp.abs(x)))


def _silu_and_grad(x):
    s = _sigmoid(x)
    return x * s, s * (1.0 + x * (1.0 - s))


def _rms_stats(x):
    r = lax.rsqrt(jnp.mean(x * x, axis=-1, keepdims=True) + EPS)
    return r, x * r


def _rms_fwd(x, g, name, out_dtype=BF16, bm=512):
    def fn(i, x, g):
        _, xn = _rms_stats(x)
        return xn * g
    return _rowcall(fn, name, x.shape[0], bm, [_row(x), _full(g)], [(x.shape[1], out_dtype)])[0]


def _rms_bwd_math(x, g, dy):
    r, xn = _rms_stats(x)
    dxn = dy * g
    dx = r * (dxn - xn * jnp.mean(dxn * xn, axis=-1, keepdims=True))
    return dx, dy * xn


def _rms_bwd(x, g, dy, res, name, bm=256):
    d = x.shape[1]

    def fn(i, x, g, dy, *res_):
        dx, dg = _rms_bwd_math(x, g, dy.astype(F32))
        if res_:
            dx = dx + res_[0]
        return dx, dx, _fold8(dg)
    ins = [_row(x), _full(g), _row(dy)] + ([_row(res)] if res is not None else [])
    return _rowcall(fn, name, x.shape[0], bm, ins, [(d, F32), (d, BF16)], [(SUBLANES, d)])


def _final_loss(x, g, target, name, bm=256):
    d = x.shape[1]

    def fn(i, x, g, t):
        r, xn = _rms_stats(x)
        err = xn * g - t
        dy = err * (1.0 / d)
        dxn = dy * g
        dx = r * (dxn - xn * jnp.mean(dxn * xn, axis=-1, keepdims=True))
        e2 = _fold8(err * err)
        lp = e2[:, 0:LANES]
        for c in range(1, d // LANES):
            lp = lp + e2[:, c * LANES:(c + 1) * LANES]
        return dx, dx, lp * (0.5 / d), _fold8(dy * xn)
    return _rowcall(fn, name, x.shape[0], bm, [_row(x), _full(g), _row(target)], [(d, F32), (d, BF16)],
                    [(SUBLANES, LANES), (SUBLANES, d)])


FF_TILE = 256


def _ff_interleave(w):
    lead = w.shape[:-1]
    w = w.reshape(lead + (2, D_FF // FF_TILE, FF_TILE))
    return jnp.swapaxes(w, -3, -2).reshape(lead + (2 * D_FF,))


def _ff_deinterleave(w):
    lead = w.shape[:-1]
    w = w.reshape(lead + (D_FF // FF_TILE, 2, FF_TILE))
    return jnp.swapaxes(w, -3, -2).reshape(lead + (2 * D_FF,))


def _gate_up_fwd(h, w_gu, name, bm=1024):
    rows, k = h.shape
    bm = min(bm, rows)

    def body(a_ref, b_ref, gu_ref, act_ref):
        acc = _dot(a_ref[...], b_ref[...])
        gu_ref[...] = acc.astype(gu_ref.dtype)
        act_ref[...] = (_silu_and_grad(acc[:, :FF_TILE])[0] * acc[:, FF_TILE:]).astype(act_ref.dtype)

    return pl.pallas_call(
        body, name=name, grid=(rows // bm, D_FF // FF_TILE),
        in_specs=[pl.BlockSpec((bm, k), lambda i, j: (i, 0)), pl.BlockSpec((k, 2 * FF_TILE), lambda i, j: (0, j))],
        out_specs=[pl.BlockSpec((bm, 2 * FF_TILE), lambda i, j: (i, j)), pl.BlockSpec((bm, FF_TILE), lambda i, j: (i, j))],
        out_shape=[jax.ShapeDtypeStruct((rows, 2 * D_FF), BF16), jax.ShapeDtypeStruct((rows, D_FF), BF16)],
        compiler_params=_params(("parallel", "parallel")))(h, w_gu)


def _down_bwd(dx, w_down, gu, name, bm=1024):
    rows, k = dx.shape
    bm = min(bm, rows)

    def body(a_ref, b_ref, gu_ref, o_ref):
        da = _dot(a_ref[...], b_ref[...], NT)
        gu = gu_ref[...].astype(F32)
        s, ds = _silu_and_grad(gu[:, :FF_TILE])
        o_ref[:, :FF_TILE] = (da * gu[:, FF_TILE:] * ds).astype(o_ref.dtype)
        o_ref[:, FF_TILE:] = (da * s).astype(o_ref.dtype)

    return pl.pallas_call(
        body, name=name, grid=(rows // bm, D_FF // FF_TILE),
        in_specs=[pl.BlockSpec((bm, k), lambda i, j: (i, 0)), pl.BlockSpec((FF_TILE, k), lambda i, j: (j, 0)),
                  pl.BlockSpec((bm, 2 * FF_TILE), lambda i, j: (i, j))],
        out_specs=pl.BlockSpec((bm, 2 * FF_TILE), lambda i, j: (i, j)),
        out_shape=jax.ShapeDtypeStruct((rows, 2 * D_FF), BF16),
        compiler_params=_params(("parallel", "parallel")))(dx, w_down, gu)


def _rope_apply(x, tabs, sign):
    cos, ta, tb = tabs
    outs = []
    for c in range(x.shape[1] // LANES):
        xc = x[:, c * LANES:(c + 1) * LANES]
        if sign > 0:
            o = xc * cos + pltpu.roll(xc, LANES - 8, 1) * ta + pltpu.roll(xc, 8, 1) * tb
        else:
            o = xc * cos + pltpu.roll(xc * ta, 8, 1) + pltpu.roll(xc * tb, LANES - 8, 1)
        outs.append(o)
    return outs[0] if len(outs) == 1 else jnp.concatenate(outs, axis=1)


def _rope(x, w, cb, tabs, sign, name, out_dtype, bm=512):
    def fn(i, x, c, a, b):
        return _rope_apply(x.astype(F32), (c, a, b), sign)
    return _rowcall(fn, name, x.shape[0], bm, [_row(x, w, cb)] + [_row(t) for t in tabs], [(w, out_dtype)])[0]


def _shift_down(x, prev8, s, first):
    xs = pltpu.roll(x, s, 0)
    rp = pltpu.roll(prev8, s, 0) * jnp.where(first, 0.0, 1.0)
    rid = lax.broadcasted_iota(jnp.int32, rp.shape, 0)
    top = jnp.where(rid < s, rp, xs[0:SUBLANES])
    return jnp.concatenate([top, xs[SUBLANES:]], axis=0)


def _shift_up(x, next8, s, last):
    n = x.shape[0]
    xs = pltpu.roll(x, n - s, 0)
    rn = pltpu.roll(next8, SUBLANES - s, 0) * jnp.where(last, 0.0, 1.0)
    rid = lax.broadcasted_iota(jnp.int32, rn.shape, 0)
    bot = jnp.where(rid >= SUBLANES - s, rn, xs[n - SUBLANES:])
    return jnp.concatenate([xs[:n - SUBLANES], bot], axis=0)


def _conv_fwd(x, prev8, w, first):
    acc = x * w[CONV_K - 1:CONV_K]
    shifted = []
    for s in range(1, CONV_K):
        xs = _shift_down(x, prev8, s, first)
        shifted.append(xs)
        acc = acc + xs * w[CONV_K - 1 - s:CONV_K - s]
    return acc, shifted


def _l2n(x):
    outs, rs = [], []
    for h in range(x.shape[1] // LANES):
        xh = x[:, h * LANES:(h + 1) * LANES]
        r = lax.rsqrt(jnp.sum(xh * xh, axis=-1, keepdims=True) + EPS)
        outs.append(xh * r)
        rs.append(r)
    return jnp.concatenate(outs, axis=1), rs


def _gate_math(ba, sel_b, sel_a, a_row, dt_row):
    bl = _dot(ba, sel_b, precision=HI)
    al = _dot(ba, sel_a, precision=HI) + dt_row
    beta = _sigmoid(bl)
    ea = jnp.exp(a_row)
    g = -ea * _softplus(al)
    return bl, al, beta, g, ea


def _cumsum_chunks(x, reverse=False):
    n = x.shape[0]
    rid = lax.broadcasted_iota(jnp.int32, x.shape, 0) % CHUNK
    s = 1
    while s < CHUNK:
        if reverse:
            x = x + jnp.where(rid < CHUNK - s, pltpu.roll(x, n - s, 0), 0.0)
        else:
            x = x + jnp.where(rid >= s, pltpu.roll(x, s, 0), 0.0)
        s *= 2
    return x


def _gdn_pre_fwd(proj, conv_w, sel_b, sel_a, a_row, dt_row, name, bm=256):
    rows = proj.shape[0]
    w3 = 3 * GDN_W

    def fn(i, x, p8, ba, w, sel_b, sel_a, a_row, dt_row):
        conv, _ = _conv_fwd(x, p8, w, i == 0)
        act = _silu_and_grad(conv)[0]
        qk, _ = _l2n(act[:, :2 * GDN_W])
        _, _, beta, g, _ = _gate_math(ba, sel_b, sel_a, a_row, dt_row)
        return qk[:, :GDN_W], qk[:, GDN_W:], act[:, 2 * GDN_W:], _cumsum_chunks(g), beta
    ins = [_row(proj, w3, 0), _prev8(proj, w3, 0), _row(proj, LANES, (GDN_IN_PAD - LANES) // LANES),
           _full(conv_w), _full(sel_b), _full(sel_a), _full(a_row), _full(dt_row)]
    return _rowcall(fn, name, rows, bm, ins, [(GDN_W, F32)] * 5)


def _gdn_pre_bwd(proj, conv_w, sel_b, sel_a, a_row, dt_row, dq, dk, dv, dgc, dbeta, name, bm=128):
    rows = proj.shape[0]
    w3 = 3 * GDN_W

    def fn(i, x, p8, ba, w, sel_b, sel_a, a_row, dt_row, dq, dk, dv, dgc, dbeta):
        dg = _cumsum_chunks(dgc, reverse=True)
        conv, shifted = _conv_fwd(x, p8, w, i == 0)
        act, dact = _silu_and_grad(conv)
        qk, rs = _l2n(act[:, :2 * GDN_W])
        dqk = jnp.concatenate([dq, dk], axis=1)
        parts = []
        for h in range(2 * GDN_HEADS):
            sl = slice(h * LANES, (h + 1) * LANES)
            y, dy = qk[:, sl], dqk[:, sl]
            parts.append(rs[h] * (dy - y * jnp.sum(y * dy, axis=-1, keepdims=True)))
        dconv = jnp.concatenate(parts + [dv], axis=1) * dact
        dws = [_fold8(dconv * xs) for xs in reversed(shifted)] + [_fold8(dconv * x)]
        bl, al, beta, g, ea = _gate_math(ba, sel_b, sel_a, a_row, dt_row)
        dbl = dbeta * beta * (1.0 - beta)
        dal = dg * (-ea) * _sigmoid(al)
        dba = _dot(dbl, sel_b, NT, precision=HI) + _dot(dal, sel_a, NT, precision=HI)
        return (dconv, dba * (1.0 / LANES)) + tuple(dws) + (_fold8(dg * g), _fold8(dal))
    ins = [_row(proj, w3, 0), _prev8(proj, w3, 0), _row(proj, LANES, (GDN_IN_PAD - LANES) // LANES),
           _full(conv_w), _full(sel_b), _full(sel_a), _full(a_row), _full(dt_row),
           _row(dq), _row(dk), _row(dv), _row(dgc), _row(dbeta)]
    return _rowcall(fn, name, rows, bm, ins, [(w3, F32), (LANES, BF16)],
                    [(SUBLANES, w3)] * CONV_K + [(SUBLANES, GDN_W)] * 2)


def _conv_bwd_input(dconv, conv_w, name, bm=256):
    rows, w3 = dconv.shape
    steps = rows // min(bm, rows)

    def fn(i, dc, n8, w):
        acc = dc * w[CONV_K - 1:CONV_K]
        for s in range(1, CONV_K):
            acc = acc + _shift_up(dc, n8, s, i == steps - 1) * w[CONV_K - 1 - s:CONV_K - s]
        return acc
    return _rowcall(fn, name, rows, bm, [_row(dconv), _next8(dconv, w3, 0), _full(conv_w)], [(w3, BF16)])[0]


def _gdn_post_fwd(o, proj, ng, name, bm=512):
    def fn(i, o, z, ng):
        outs = []
        for h in range(GDN_HEADS):
            sl = slice(h * LANES, (h + 1) * LANES)
            _, on = _rms_stats(o[:, sl])
            outs.append(on * ng * _silu_and_grad(z[:, sl])[0])
        return jnp.concatenate(outs, axis=1)
    return _rowcall(fn, name, o.shape[0], bm, [_row(o), _row(proj, GDN_W, 3), _full(ng)], [(GDN_W, BF16)])[0]


def _gdn_post_bwd(o, proj, ng, dcat, name, bm=256):
    def fn(i, o, z, ng, dm):
        dm = dm.astype(F32)
        dos, dzs = [], []
        dng = jnp.zeros((SUBLANES, LANES), F32)
        for h in range(GDN_HEADS):
            sl = slice(h * LANES, (h + 1) * LANES)
            s, ds = _silu_and_grad(z[:, sl])
            r, on = _rms_stats(o[:, sl])
            dzs.append(dm[:, sl] * on * ng * ds)
            dy = dm[:, sl] * s
            dxn = dy * ng
            dos.append(r * (dxn - on * jnp.mean(dxn * on, axis=-1, keepdims=True)))
            dng = dng + _fold8(dy * on)
        return jnp.concatenate(dos, axis=1), jnp.concatenate(dzs, axis=1), dng
    return _rowcall(fn, name, o.shape[0], bm, [_row(o), _row(proj, GDN_W, 3), _full(ng), _row(dcat, GDN_W, 0)],
                    [(GDN_W, F32), (GDN_W, BF16)], [(SUBLANES, LANES)])


def _split3(x):
    hi = x.astype(BF16)
    r = x - hi.astype(F32)
    mid = r.astype(BF16)
    return hi, mid, (r - mid.astype(F32)).astype(BF16)


def _bdot(a, b, mode="nn"):
    lc, rc = {"nn": (2, 1), "nt": (2, 2), "tn": (1, 1)}[mode]
    return lax.dot_general(a, b, (((lc,), (rc,)), ((0,), (0,))), preferred_element_type=F32)


def _bdot3(a, b, mode="nn"):
    ah, bh = a.astype(BF16), b.astype(BF16)
    al, bl = (a - ah.astype(F32)).astype(BF16), (b - bh.astype(F32)).astype(BF16)
    return _bdot(ah, bh, mode) + _bdot(ah, bl, mode) + _bdot(al, bh, mode)


GDN_CB = 2


def _gdn_chunk(q, k, v, gc, beta, t=None):
    c = CHUNK
    nb = q.shape[0]
    row = lax.broadcasted_iota(jnp.int32, (c, c), 0)
    col = lax.broadcasted_iota(jnp.int32, (c, c), 1)
    tril, strict = row >= col, row > col
    lane0 = (lax.broadcasted_iota(jnp.int32, (nb, c, LANES), 2) == 0).astype(BF16)
    gc_row = sum(_bdot(lane0, part, "nt") for part in _split3(gc))
    dm = jnp.exp(jnp.where(tril, gc[:, :, :c] - gc_row, -1e30))
    eg = jnp.exp(gc)
    gcl = gc[:, c - 1:c, :]
    ekg = jnp.exp(gcl - gc)
    egl = jnp.exp(gcl)
    qs = q * (GDN_DK ** -0.5)
    kb = k * beta
    kk = _bdot(kb, k, "nt")
    a = jnp.where(strict, kk * dm, 0.0)
    vb = v * beta
    kbg = kb * eg
    qk = _bdot(qs, k, "nt")
    p = jnp.where(tril, qk * dm, 0.0)
    out = dict(tril=tril, strict=strict, dm=dm, eg=eg, ekg=ekg, egl=egl, qs=qs, kb=kb, kk=kk, a=a,
               vb=vb, kbg=kbg, qk=qk, p=p, qg=qs * eg, kg=k * ekg)
    if t is None:
        y = -a
        t = (row == col).astype(F32) + y
        for _ in range(5):
            y = _bdot3(y, y)
            t = t + _bdot3(t, y)
        out.update(u=_bdot3(t, vb), w=_bdot3(t, kbg))
    out["t"] = t
    return out


def _gdn_stack(ref):
    return jnp.stack([ref[c * CHUNK:(c + 1) * CHUNK, h * LANES:(h + 1) * LANES]
                      for c in range(GDN_CB) for h in range(GDN_HEADS)])


def _gdn_unstack(x, ref):
    for c in range(GDN_CB):
        for h in range(GDN_HEADS):
            ref[c * CHUNK:(c + 1) * CHUNK, h * LANES:(h + 1) * LANES] = x[c * GDN_HEADS + h]


def _gdn_fwd(q, k, v, gc, beta, name):
    rows = q.shape[0]
    n_chunks = rows // CHUNK
    steps = n_chunks // GDN_CB
    blk = pl.BlockSpec((GDN_CB * CHUNK, GDN_W), lambda n: (n, 0))
    st = pl.BlockSpec((GDN_HEADS, GDN_CB, GDN_DK, LANES), lambda n: (0, n, 0, 0))
    tinv = pl.BlockSpec((GDN_CB, GDN_HEADS, CHUNK, CHUNK), lambda n: (n, 0, 0, 0))

    def body(q_ref, k_ref, v_ref, g_ref, b_ref, o_ref, st_ref, t_ref, w_ref, vn_ref, s_ref):
        @pl.when(pl.program_id(0) == 0)
        def _():
            s_ref[...] = jnp.zeros(s_ref.shape, F32)
        c = _gdn_chunk(*[_gdn_stack(r) for r in (q_ref, k_ref, v_ref, g_ref, b_ref)])
        _gdn_unstack(c["w"], w_ref)
        s = s_ref[...]
        for i in range(GDN_CB):
            hs = slice(i * GDN_HEADS, (i + 1) * GDN_HEADS)
            rs = slice(i * CHUNK, (i + 1) * CHUNK)
            st_ref[:, i] = s
            vn = c["u"][hs] - _bdot(c["w"][hs], s)
            o = _bdot(c["qg"][hs], s) + _bdot(c["p"][hs], vn)
            s = s * c["egl"][hs] + _bdot(c["kg"][hs], vn, "tn")
            t_ref[i] = c["t"][hs]
            for h in range(GDN_HEADS):
                o_ref[rs, h * LANES:(h + 1) * LANES] = o[h]
                vn_ref[rs, h * LANES:(h + 1) * LANES] = vn[h]
        s_ref[...] = s

    f = jax.ShapeDtypeStruct((rows, GDN_W), F32)
    return pl.pallas_call(
        body, name=name, grid=(steps,), in_specs=[blk] * 5, out_specs=[blk, st, tinv, blk, blk],
        out_shape=[f, jax.ShapeDtypeStruct((GDN_HEADS, n_chunks, GDN_DK, LANES), F32),
                   jax.ShapeDtypeStruct((n_chunks, GDN_HEADS, CHUNK, CHUNK), F32), f, f],
        scratch_shapes=[pltpu.VMEM((GDN_HEADS, GDN_DK, LANES), F32)],
        compiler_params=_params(("arbitrary",)))(q, k, v, gc, beta)


def _gdn_bwd(q, k, v, gc, beta, states, tinv, w, vn, do, name):
    rows = q.shape[0]
    n_chunks = rows // CHUNK
    steps = n_chunks // GDN_CB
    blk = pl.BlockSpec((GDN_CB * CHUNK, GDN_W), lambda n: (steps - 1 - n, 0))
    st = pl.BlockSpec((GDN_HEADS, GDN_CB, GDN_DK, LANES), lambda n: (0, steps - 1 - n, 0, 0))
    ti = pl.BlockSpec((GDN_CB, GDN_HEADS, CHUNK, CHUNK), lambda n: (steps - 1 - n, 0, 0, 0))
    nbatch = GDN_CB * GDN_HEADS

    def lanesum(x):
        return jnp.broadcast_to(jnp.sum(x, axis=-1, keepdims=True), x.shape)

    def body(q_ref, k_ref, v_ref, g_ref, b_ref, st_ref, t_ref, w_ref, vn_ref, do_ref,
             dq_ref, dk_ref, dv_ref, dg_ref, db_ref, ds_ref):
        @pl.when(pl.program_id(0) == 0)
        def _():
            ds_ref[...] = jnp.zeros(ds_ref.shape, F32)
        q, k, v, gc, beta, w, vn, do = [_gdn_stack(r) for r in (q_ref, k_ref, v_ref, g_ref, b_ref, w_ref, vn_ref, do_ref)]
        t = t_ref[...].reshape(nbatch, CHUNK, CHUNK)
        s = jnp.stack([st_ref[h, i] for i in range(GDN_CB) for h in range(GDN_HEADS)])
        c = _gdn_chunk(q, k, v, gc, beta, t)
        tril, strict, dm = c["tril"], c["strict"], c["dm"]
        dsn = ds_ref[...]
        dvn_c, dkg_c, dgl_c = [None] * GDN_CB, [None] * GDN_CB, [None] * GDN_CB
        for i in reversed(range(GDN_CB)):
            hs = slice(i * GDN_HEADS, (i + 1) * GDN_HEADS)
            dvn_c[i] = _bdot(c["p"][hs], do[hs], "tn") + _bdot(c["kg"][hs], dsn)
            dkg_c[i] = _bdot(vn[hs], dsn, "nt")
            dgl_c[i] = jnp.sum(jnp.sum(s[hs] * dsn, axis=2, keepdims=True), axis=1, keepdims=True) * c["egl"][hs]
            dsn = dsn * c["egl"][hs] + _bdot(c["qg"][hs], do[hs], "tn") - _bdot(w[hs], dvn_c[i], "tn")
        ds_ref[...] = dsn
        dvn, dkg, dgl = jnp.concatenate(dvn_c), jnp.concatenate(dkg_c), jnp.concatenate(dgl_c)
        dp = jnp.where(tril, _bdot(do, vn, "nt"), 0.0)
        dqg = _bdot(do, s, "nt")
        dw = -_bdot(dvn, s, "nt")
        dvb = _bdot3(t, dvn, "tn")
        dkbg = _bdot3(t, dw, "tn")
        dt = _bdot(dvn, c["vb"], "nt") + _bdot(dw, c["kbg"], "nt")
        da = jnp.where(strict, -_bdot3(_bdot3(t, dt, "tn"), t, "nt"), 0.0)
        dkk = da * dm
        dqk = dp * dm
        dkb = _bdot(dkk, k) + dkbg * c["eg"]
        dk = _bdot(dkk, c["kb"], "tn") + _bdot(dqk, c["qs"], "tn") + dkg * c["ekg"] + dkb * beta
        dqs = _bdot(dqk, k) + dqg * c["eg"]
        e = da * c["a"] + dp * c["p"]
        ones = jnp.ones((nbatch, CHUNK, LANES), BF16)
        col_sums = sum(_bdot(part, ones, "tn") for part in _split3(e))
        kg_term = lanesum(dkg * c["kg"])
        dgc = (jnp.broadcast_to(jnp.sum(e, axis=-1, keepdims=True), (nbatch, CHUNK, LANES)) - col_sums
               + lanesum(dqg * c["qg"]) - kg_term + lanesum(dkbg * c["kbg"]))
        dgcl = jnp.sum(kg_term, axis=1, keepdims=True) + dgl
        last = lax.broadcasted_iota(jnp.int32, (CHUNK, LANES), 0) == CHUNK - 1
        _gdn_unstack(dqs * (GDN_DK ** -0.5), dq_ref)
        _gdn_unstack(dk, dk_ref)
        _gdn_unstack(dvb * beta, dv_ref)
        _gdn_unstack(dgc + jnp.where(last, dgcl, 0.0), dg_ref)
        _gdn_unstack(lanesum(dvb * v) + lanesum(dkb * k), db_ref)

    return pl.pallas_call(
        body, name=name, grid=(steps,), in_specs=[blk] * 5 + [st, ti, blk, blk, blk], out_specs=[blk] * 5,
        out_shape=[jax.ShapeDtypeStruct((rows, GDN_W), F32)] * 5,
        scratch_shapes=[pltpu.VMEM((GDN_HEADS, GDN_DK, LANES), F32)],
        compiler_params=_params(("arbitrary",)))(q, k, v, gc, beta, states, tinv, w, vn, do)


def _swa_masks(first):
    r = lax.broadcasted_iota(jnp.int32, (SWA_BLOCK, 2 * SWA_BLOCK), 0)
    c = lax.broadcasted_iota(jnp.int32, (SWA_BLOCK, 2 * SWA_BLOCK), 1)
    band = (c > r) & (c <= r + SWA_BLOCK)
    return band & (jnp.logical_not(first) | (c >= SWA_BLOCK))


def _swa_stack(ref, j):
    lane = lax.broadcasted_iota(jnp.int32, (1, LANES), 1)
    parts = []
    for g in range(SWA_GROUP):
        ch = j * (SWA_GROUP // 2) + g // 2
        keep = (lane < SWA_DH) if g % 2 == 0 else (lane >= SWA_DH)
        parts.append(ref[:, ch * LANES:(ch + 1) * LANES] * keep.astype(ref.dtype))
    return jnp.concatenate(parts, axis=0)


def _swa_unstack(x2, j, out_ref):
    low = lax.broadcasted_iota(jnp.int32, (SWA_BLOCK, LANES), 1) < SWA_DH
    for c3 in range(SWA_GROUP // 2):
        even = x2[(2 * c3) * SWA_BLOCK:(2 * c3 + 1) * SWA_BLOCK]
        odd = x2[(2 * c3 + 1) * SWA_BLOCK:(2 * c3 + 2) * SWA_BLOCK]
        ch = j * (SWA_GROUP // 2) + c3
        out_ref[:, ch * LANES:(ch + 1) * LANES] = jnp.where(low, even, odd).astype(out_ref.dtype)


def _swa_probs(s, sink, mask):
    s = jnp.where(mask, s, -1e30)
    m = jnp.maximum(jnp.max(s, axis=-1, keepdims=True), sink)
    p = jnp.where(mask, jnp.exp(s - m), 0.0)
    es = jnp.exp(sink - m)
    inv = 1.0 / (jnp.sum(p, axis=-1, keepdims=True) + es)
    return p * inv, es * inv


def _swa_scores(q_ref, kc_ref, kp_ref, sink_ref, j, mask):
    sl = slice(j * LANES, (j + 1) * LANES)
    qst = _swa_stack(q_ref, j)
    kw = jnp.concatenate([kp_ref[:, sl], kc_ref[:, sl]], axis=0)
    s = _dot(qst, kw, NT) * (SWA_DH ** -0.5)
    ps = [_swa_probs(s[g * SWA_BLOCK:(g + 1) * SWA_BLOCK], sink_ref[j * SWA_GROUP + g], mask)
          for g in range(SWA_GROUP)]
    return qst, kw, ps


def _swa_fwd(q, k2, v2, sinks, name):
    rows = q.shape[0]
    nb = rows // SWA_BLOCK
    w = SWA_HEADS * SWA_DH
    kvw = SWA_KV_HEADS * LANES
    cur = pl.BlockSpec((SWA_BLOCK, w), lambda i: (i, 0))
    kcur = pl.BlockSpec((SWA_BLOCK, kvw), lambda i: (i, 0))
    kprev = pl.BlockSpec((SWA_BLOCK, kvw), lambda i: (jnp.maximum(i - 1, 0), 0))

    def body(sink_ref, q_ref, kc_ref, kp_ref, vc_ref, vp_ref, o_ref):
        mask = _swa_masks(pl.program_id(0) == 0)
        for j in range(SWA_KV_HEADS):
            sl = slice(j * LANES, (j + 1) * LANES)
            _, _, ps = _swa_scores(q_ref, kc_ref, kp_ref, sink_ref, j, mask)
            vw = jnp.concatenate([vp_ref[:, sl], vc_ref[:, sl]], axis=0)
            pst = jnp.concatenate([p.astype(BF16) for p, _ in ps], axis=0)
            _swa_unstack(_dot(pst, vw), j, o_ref)

    return pl.pallas_call(
        body, name=name, grid=(nb,),
        in_specs=[pl.BlockSpec(memory_space=pltpu.SMEM), cur, kcur, kprev, kcur, kprev], out_specs=cur,
        out_shape=jax.ShapeDtypeStruct((rows, w), BF16),
        compiler_params=_params(("arbitrary",)))(sinks, q, k2, k2, v2, v2)


def _swa_bwd(q, k2, v2, sinks, dcat, name):
    rows = q.shape[0]
    nb = rows // SWA_BLOCK
    w = SWA_HEADS * SWA_DH
    kvw = SWA_KV_HEADS * LANES
    cur = pl.BlockSpec((SWA_BLOCK, w), lambda i: (jnp.minimum(i, nb - 1), 0))
    kcur = pl.BlockSpec((SWA_BLOCK, kvw), lambda i: (jnp.minimum(i, nb - 1), 0))
    kprev = pl.BlockSpec((SWA_BLOCK, kvw), lambda i: (jnp.clip(i - 1, 0, nb - 1), 0))
    late = pl.BlockSpec((SWA_BLOCK, kvw), lambda i: (jnp.maximum(i - 1, 0), 0))
    acc_spec = pl.BlockSpec((SUBLANES, LANES), lambda i: (0, 0))

    def body(sink_ref, q_ref, kc_ref, kp_ref, vc_ref, vp_ref, do_ref, dq_ref, dk_ref, dv_ref, dsk_ref,
             ck_ref, cv_ref):
        i = pl.program_id(0)

        @pl.when(i == 0)
        def _():
            ck_ref[...] = jnp.zeros(ck_ref.shape, F32)
            cv_ref[...] = jnp.zeros(cv_ref.shape, F32)
            dsk_ref[...] = jnp.zeros(dsk_ref.shape, F32)

        @pl.when(i == nb)
        def _():
            dk_ref[...] = ck_ref[...]
            dv_ref[...] = cv_ref[...]

        @pl.when(i < nb)
        def _():
            mask = _swa_masks(i == 0)
            lane = lax.broadcasted_iota(jnp.int32, (SUBLANES, LANES), 1)
            dsk = jnp.zeros((SUBLANES, LANES), F32)
            for j in range(SWA_KV_HEADS):
                sl = slice(j * LANES, (j + 1) * LANES)
                qst, kw, ps = _swa_scores(q_ref, kc_ref, kp_ref, sink_ref, j, mask)
                vw = jnp.concatenate([vp_ref[:, sl], vc_ref[:, sl]], axis=0)
                dost = _swa_stack(do_ref, j)
                dpr = _dot(dost, vw, NT)
                dss = []
                for g in range(SWA_GROUP):
                    p, sink_p = ps[g]
                    dpg = dpr[g * SWA_BLOCK:(g + 1) * SWA_BLOCK]
                    delta = jnp.sum(p * dpg, axis=-1, keepdims=True)
                    dss.append((p * (dpg - delta)).astype(BF16))
                    dsg = jnp.sum(-sink_p * delta, axis=0, keepdims=True)
                    dsk = dsk + jnp.where(lane == j * SWA_GROUP + g, dsg, 0.0)
                dsst = jnp.concatenate(dss, axis=0)
                pst = jnp.concatenate([p.astype(BF16) for p, _ in ps], axis=0)
                _swa_unstack(_dot(dsst, kw) * (SWA_DH ** -0.5), j, dq_ref)
                dk = _dot(dsst, qst, TN) * (SWA_DH ** -0.5)
                dv = _dot(pst, dost, TN)
                dk = dk + pltpu.roll(dk, SWA_DH, 1)
                dv = dv + pltpu.roll(dv, SWA_DH, 1)
                dk_ref[:, sl] = ck_ref[:, sl] + dk[:SWA_BLOCK]
                dv_ref[:, sl] = cv_ref[:, sl] + dv[:SWA_BLOCK]
                ck_ref[:, sl] = dk[SWA_BLOCK:]
                cv_ref[:, sl] = dv[SWA_BLOCK:]
            dsk_ref[...] += dsk

    f = jax.ShapeDtypeStruct((rows, kvw), F32)
    return pl.pallas_call(
        body, name=name, grid=(nb + 1,),
        in_specs=[pl.BlockSpec(memory_space=pltpu.SMEM), cur, kcur, kprev, kcur, kprev, cur],
        out_specs=[cur, late, late, acc_spec],
        out_shape=[jax.ShapeDtypeStruct((rows, w), F32), f, f, jax.ShapeDtypeStruct((SUBLANES, LANES), F32)],
        scratch_shapes=[pltpu.VMEM((SWA_BLOCK, kvw), F32), pltpu.VMEM((SWA_BLOCK, kvw), F32)],
        compiler_params=_params(("arbitrary",)))(sinks, q, k2, k2, v2, v2, dcat)


def _mem_probs(mq, kbd):
    s = _dot(mq.astype(BF16), kbd) * (MEM_DH ** -0.5)
    ps = []
    for h in range(MEM_HEADS):
        sh = s[:, h * MEM_LEN:(h + 1) * MEM_LEN]
        e = jnp.exp(sh - jnp.max(sh, axis=-1, keepdims=True))
        ps.append(e / jnp.sum(e, axis=-1, keepdims=True))
    return ps


def _mem_fwd(proj, cb, kbd, vbd, name, bm=512):
    def fn(i, mq, kbd, vbd):
        p = jnp.concatenate(_mem_probs(mq, kbd), axis=1)
        return _dot(p.astype(BF16), vbd)
    return _rowcall(fn, name, proj.shape[0], bm, [_row(proj, MEM_W, cb), _full(kbd), _full(vbd)], [(MEM_W, BF16)])[0]


def _mem_bwd(proj, cb, kbd, vbd, dcat, name, bm=512):
    def fn(i, mq, kbd, vbd, do):
        ps = _mem_probs(mq, kbd)
        dp = _dot(do, vbd, NT)
        dss = []
        for h in range(MEM_HEADS):
            dph = dp[:, h * MEM_LEN:(h + 1) * MEM_LEN]
            dss.append(ps[h] * (dph - jnp.sum(ps[h] * dph, axis=-1, keepdims=True)))
        ds = (jnp.concatenate(dss, axis=1) * (MEM_DH ** -0.5)).astype(BF16)
        p = jnp.concatenate(ps, axis=1).astype(BF16)
        return _dot(ds, kbd, NT), _dot(mq.astype(BF16), ds, TN), _dot(p, do, TN)
    return _rowcall(fn, name, proj.shape[0], bm, [_row(proj, MEM_W, cb), _full(kbd), _full(vbd), _row(dcat, MEM_W, 3)],
                    [(MEM_W, BF16)], [(MEM_W, MEM_HEADS * MEM_LEN), (MEM_HEADS * MEM_LEN, MEM_W)])


def _mem_expand(mkv):
    feat_head = jnp.arange(MEM_W) // MEM_DH
    slot_head = jnp.arange(MEM_HEADS * MEM_LEN) // MEM_LEN
    on = feat_head[:, None] == slot_head[None, :]
    kbd = jnp.where(on, jnp.tile(mkv[:, :MEM_W].T, (1, MEM_HEADS)), 0.0)
    vbd = jnp.where(on.T, jnp.tile(mkv[:, MEM_W:], (MEM_HEADS, 1)), 0.0)
    return kbd.astype(BF16), vbd.astype(BF16)


def _mem_collapse(dkbd, dvbd):
    dk = [dkbd[h * MEM_DH:(h + 1) * MEM_DH, h * MEM_LEN:(h + 1) * MEM_LEN].T for h in range(MEM_HEADS)]
    dv = [dvbd[h * MEM_LEN:(h + 1) * MEM_LEN, h * MEM_DH:(h + 1) * MEM_DH] for h in range(MEM_HEADS)]
    return jnp.concatenate(dk + dv, axis=1)


def _adamw(w, g, m, v, name, bm=512):
    def fn(i, w, g, m, v):
        m = ADAM_B1 * m + (1.0 - ADAM_B1) * g
        v = ADAM_B2 * v + (1.0 - ADAM_B2) * (g * g)
        m_hat = m / (1.0 - ADAM_B1 ** ADAM_STEP)
        v_hat = v / (1.0 - ADAM_B2 ** ADAM_STEP)
        return -ADAM_LR * (m_hat / (jnp.sqrt(v_hat) + ADAM_EPS) + ADAM_WD * w), m, v
    d = w.shape[1]
    return _rowcall(fn, name, w.shape[0], bm, [_row(w), _row(g), _row(m), _row(v)], [(d, F32)] * 3)


def _sum_slots(buf, name, bm=128):
    n, rows, w = buf.shape
    bm = min(bm, rows)
    assert rows % bm == 0

    def body(b_ref, o_ref):
        acc = b_ref[0].astype(F32)
        for s in range(1, n):
            acc = acc + b_ref[s].astype(F32)
        o_ref[...] = acc

    return pl.pallas_call(
        body, name=name, grid=(rows // bm,), in_specs=[pl.BlockSpec((n, bm, w), lambda i: (0, i, 0))],
        out_specs=pl.BlockSpec((bm, w), lambda i: (i, 0)), out_shape=jax.ShapeDtypeStruct((rows, w), F32),
        compiler_params=_params(("parallel",)))(buf)


def _exchange(srcs, same, masks, name):
    slots = N_DEV if len(masks) == N_DEV - 1 else 2
    n_arr, n_peer = len(srcs), len(masks)
    shapes = [s.shape if sm else s.shape[1:] for s, sm in zip(srcs, same)]

    def body(*refs):
        src_refs, out_refs = refs[:n_arr], refs[n_arr:2 * n_arr]
        send_sems, recv_sems, local_sems = refs[2 * n_arr:]
        x, y, c = lax.axis_index("x"), lax.axis_index("y"), lax.axis_index("c")
        me = 4 * x + 2 * y + c

        def flip(v, bit):
            return 1 - v if bit else v

        def slot_of(dev):
            return dev if slots == N_DEV else dev % 2

        def piece(a, p):
            return src_refs[a] if same[a] else src_refs[a].at[p]

        local = [pltpu.make_async_copy(piece(a, me), out_refs[a].at[slot_of(me)], local_sems.at[a]) for a in range(n_arr)]
        for cp in local:
            cp.start()
        copies = []
        for idx, k in enumerate(masks):
            peer = (flip(x, k & 4), flip(y, k & 2), flip(c, k & 1))
            peer_id = 4 * peer[0] + 2 * peer[1] + peer[2]
            for a in range(n_arr):
                sem = idx * n_arr + a
                cp = pltpu.make_async_remote_copy(
                    src_ref=piece(a, peer_id), dst_ref=out_refs[a].at[slot_of(me)],
                    send_sem=send_sems.at[sem], recv_sem=recv_sems.at[sem], device_id=peer, device_id_type=MESH)
                cp.start()
                copies.append((cp, pltpu.make_async_remote_copy(
                    src_ref=piece(a, peer_id), dst_ref=out_refs[a].at[slot_of(peer_id)],
                    send_sem=send_sems.at[sem], recv_sem=recv_sems.at[sem], device_id=peer, device_id_type=MESH)))
        for cp, landing in copies:
            cp.wait_send()
            landing.wait_recv()
        for cp in local:
            cp.wait()

    any_spec = pl.BlockSpec(memory_space=pl.ANY)
    n_sem = n_arr * n_peer
    return pl.pallas_call(
        body, name=name, in_specs=[any_spec] * n_arr, out_specs=[any_spec] * n_arr,
        out_shape=[jax.ShapeDtypeStruct((slots,) + tuple(sh), s.dtype) for sh, s in zip(shapes, srcs)],
        scratch_shapes=[pltpu.SemaphoreType.DMA((n_sem,)), pltpu.SemaphoreType.DMA((n_sem,)),
                        pltpu.SemaphoreType.DMA((n_arr,))],
        )(*srcs)


ALL_PEERS = tuple(range(1, N_DEV))
SIBLING = (1,)


def _pack(arrays, rows):
    flat = jnp.concatenate([a.reshape(-1) for a in arrays])
    return jnp.pad(flat, (0, rows * D_MODEL - flat.shape[0])).reshape(rows, D_MODEL)


def _unpack(buf, shapes):
    flat = buf.reshape(-1)
    out, off = [], 0
    for s in shapes:
        n = math.prod(s)
        out.append(flat[off:off + n].reshape(s))
        off += n
    return out


def _rows_for(shapes, mult):
    n = sum(math.prod(s) for s in shapes)
    rows = -(-n // D_MODEL)
    return -(-rows // mult) * mult


SHARD_AXIS = dict(w_mem_kv=1, w_out=1, w_gate_up=2, w_down=1, gdn_w_in=2, swa_w_q=1, w_kv=0, gdn_conv=2)
HALF_AXIS = dict(w_mem_kv=1, w_out=1, w_gate_up=1, w_down=1, gdn_w_in=1, swa_w_q=1, w_kv=0)


def _my_half(shard, name, c):
    ax = HALF_AXIS[name]
    h = shard.shape[ax] // 2
    return lax.dynamic_slice_in_dim(shard, c * h, h, axis=ax)


def _piece_layout(name, half_shape):
    dims, pos = [], {}
    for i, d in enumerate(half_shape):
        if i == SHARD_AXIS[name]:
            pos["chip"] = len(dims)
            dims.append(4)
        if i == HALF_AXIS[name]:
            pos["core"] = len(dims)
            dims.append(2)
        pos[i] = len(dims)
        dims.append(d)
    return dims, [pos["chip"], pos["core"]] + [pos[i] for i in range(len(half_shape))]


def _full_shape(name, half_shape):
    return tuple(d * (4 if i == SHARD_AXIS[name] else 1) * (2 if i == HALF_AXIS[name] else 1)
                 for i, d in enumerate(half_shape))


def _assemble(pieces, name):
    half_shape = pieces.shape[1:]
    dims, perm = _piece_layout(name, half_shape)
    inverse = [perm.index(i) for i in range(len(perm))]
    return pieces.reshape((4, 2) + half_shape).transpose(inverse).reshape(_full_shape(name, half_shape))


def _to_pieces(full, name, half_shape):
    dims, perm = _piece_layout(name, half_shape)
    return full.reshape(dims).transpose(perm).reshape((N_DEV,) + tuple(half_shape))


def _from_halves(halves, name):
    ax = HALF_AXIS[name]
    s = jnp.moveaxis(halves, 0, ax)
    return s.reshape(s.shape[:ax] + (2 * s.shape[ax + 1],) + s.shape[ax + 2:])


def _rope_tables(positions):
    half = ROT_DIM // 2
    inv = ROPE_THETA ** (-jnp.arange(0, ROT_DIM, 2, dtype=F32) / ROT_DIM)
    ang = positions.astype(F32)[:, None] * inv
    cos, sin = jnp.cos(ang), jnp.sin(ang)
    rows = positions.shape[0]
    one = jnp.ones((rows, SWA_DH - ROT_DIM), F32)
    zero = jnp.zeros((rows, SWA_DH - ROT_DIM), F32)
    zh = jnp.zeros((rows, half), F32)
    c64 = jnp.concatenate([cos, cos, one], axis=1)
    a64 = jnp.concatenate([-sin, zh, zero], axis=1)
    b64 = jnp.concatenate([zh, sin, zero], axis=1)
    return tuple(jnp.concatenate([t, t], axis=1) for t in (c64, a64, b64))


def _pair_heads(t):
    return jnp.concatenate([t[:, :SWA_DH], t[:, :SWA_DH], t[:, SWA_DH:], t[:, SWA_DH:]], axis=1)


def _unpair_heads(t):
    return jnp.concatenate([t[:, :SWA_DH], t[:, LANES:LANES + SWA_DH]], axis=1)


def _gdn_in_pad(w):
    o2 = 4 * GDN_W
    pad = jnp.zeros(w.shape[:-1] + (GDN_IN_PAD - GDN_IN,), w.dtype)
    return jnp.concatenate([w[..., :o2], w[..., o2 + 2 * GDN_HEADS:], w[..., o2:o2 + 2 * GDN_HEADS], pad], axis=-1)


def _gdn_in_unpad(w):
    o2 = 4 * GDN_W
    return jnp.concatenate([w[..., :o2], w[..., o2 + MEM_W:o2 + MEM_W + 2 * GDN_HEADS], w[..., o2:o2 + MEM_W]], axis=-1)


def _head_rows(v):
    return jnp.repeat(v.astype(F32), LANES)[None, :]


def _selectors():
    lane = jnp.arange(LANES)[:, None]
    head = (jnp.arange(GDN_W) // LANES)[None, :]
    return (lane == head).astype(F32), (lane == head + GDN_HEADS).astype(F32)


def _local_step(x, mem, positions, target, w):
    rows = x.shape[0]
    tabs = _rope_tables(positions)
    sel_b, sel_a = _selectors()
    row2 = lambda v: v.reshape(1, -1).astype(F32)

    w_gu = _ff_interleave(w["w_gate_up"])
    mem_n = _rms_fwd(mem, row2(w["ln_mem"]), "mem_norm")
    saved = []
    kt = vt = None
    for l in range(DEPTH):
        s = dict(x0=x)
        h = _rms_fwd(x, row2(w["ln_mix"][l]), f"norm_mix{l}")
        mkv = _mm(mem_n, w["w_mem_kv"][l], "nn", f"mem_kv{l}")
        kbd, vbd = _mem_expand(mkv)
        if l < N_A:
            proj = _mm(h, w["gdn_w_in"][l], "nn", f"gdn_in{l}")
            a_row, dt_row = _head_rows(w["gdn_A_log"][l]), _head_rows(w["gdn_dt_bias"][l])
            q, k, v, gc, beta = _gdn_pre_fwd(proj, w["gdn_conv"][l], sel_b, sel_a, a_row, dt_row, f"gdn_pre{l}")
            o, states, tinv, gw, vn = _gdn_fwd(q, k, v, gc, beta, f"gdn_scan{l}")
            mix = _gdn_post_fwd(o, proj, row2(w["gdn_norm"][l]), f"gdn_post{l}")
            mq_cb = (3 * GDN_W + GDN_W) // MEM_W
            s.update(q=q, k=k, v=v, gc=gc, beta=beta, o=o, states=states, tinv=tinv, gw=gw, vn=vn,
                     a_row=a_row, dt_row=dt_row)
        else:
            proj = _mm(h, w["swa_w_q"][l - N_A], "nn", f"swa_in{l}")
            qr = _rope(proj, SWA_HEADS * SWA_DH, 0, tabs, 1, f"rope_q{l}", BF16)
            mix = _swa_fwd(qr, kt, vt, w["swa_sinks"][l - N_A], f"swa{l}")
            mq_cb = (SWA_HEADS * SWA_DH) // MEM_W
            s.update(qr=qr)
        mem_o = _mem_fwd(proj, mq_cb, kbd, vbd, f"mem_attn{l}")
        cat = jnp.concatenate([mix, mem_o], axis=1)
        x1 = _mm(cat, w["w_out"][l], "nn", f"out_proj{l}", add=x)
        h2 = _rms_fwd(x1, row2(w["ln_ffn"][l]), f"norm_ffn{l}")
        gu, act = _gate_up_fwd(h2, w_gu[l], f"gate_up{l}")
        x = _mm(act, w["w_down"][l], "nn", f"down{l}", add=x1)
        s.update(h=h, proj=proj, kbd=kbd, vbd=vbd, mq_cb=mq_cb, cat=cat, x1=x1, h2=h2, gu=gu, act=act)
        saved.append(s)
        if l == N_A - 1:
            x_kv = x
            h_kv = _rms_fwd(x, row2(w["ln_kv"]), "norm_kv")
            kv = _mm(h_kv, w["w_kv"], "nn", "kv_proj")
            kr = _rope(kv, LANES, 0, tabs, 1, "rope_k", F32)
            kt = _pair_heads(kr).astype(BF16)
            vt = _pair_heads(kv[:, LANES:]).astype(BF16)

    gr = {}
    dx, dxb, loss_part, dlnf = _final_loss(x, row2(w["ln_final"]), target, "final_loss")
    gr["ln_final"] = dlnf.sum(axis=0)
    dln_mix, dln_ffn = [None] * DEPTH, [None] * DEPTH
    dw_mem_kv, dw_out, dw_gu, dw_dn = [None] * DEPTH, [None] * DEPTH, [None] * DEPTH, [None] * DEPTH
    dgdn_in, dgdn_conv, dgdn_a, dgdn_dt, dgdn_norm = [None] * N_A, [None] * N_A, [None] * N_A, [None] * N_A, [None] * N_A
    dswa_q, dswa_sinks = [None] * N_B, [None] * N_B
    dmem_n = None
    dkt = dvt = None
    for l in reversed(range(DEPTH)):
        s = saved[l]
        if l == N_A - 1:
            dkr = _unpair_heads(dkt)
            dk = _rope(dkr, LANES, 0, tabs, -1, "rope_k_bwd", BF16)
            dkv = jnp.concatenate([dk, _unpair_heads(dvt).astype(BF16)], axis=1)
            dh_kv = _mm(dkv, w["w_kv"], "nt", "kv_proj_dx")
            gr["w_kv"] = _mm(h_kv, dkv, "tn", "kv_proj_dw", BF16)
            dx, dxb, dg = _rms_bwd(x_kv, row2(w["ln_kv"]), dh_kv, dx, "norm_kv_bwd")
            gr["ln_kv"] = dg.sum(axis=0)
        dgu = _down_bwd(dxb, w["w_down"][l], s["gu"], f"down_dx{l}")
        dw_dn[l] = _mm(s["act"], dxb, "tn", f"down_dw{l}", BF16)
        dh2 = _mm(dgu, w_gu[l], "nt", f"gate_up_dx{l}")
        dw_gu[l] = _mm(s["h2"], dgu, "tn", f"gate_up_dw{l}", BF16)
        dx, dxb, dg = _rms_bwd(s["x1"], row2(w["ln_ffn"][l]), dh2, dx, f"norm_ffn_bwd{l}")
        dln_ffn[l] = dg.sum(axis=0)
        dcat = _mm(dxb, w["w_out"][l], "nt", f"out_proj_dx{l}", BF16)
        dw_out[l] = _mm(s["cat"], dxb, "tn", f"out_proj_dw{l}", BF16)
        dmq, dkbd, dvbd = _mem_bwd(s["proj"], s["mq_cb"], s["kbd"], s["vbd"], dcat, f"mem_attn_bwd{l}")
        dmkv = _mem_collapse(dkbd, dvbd).astype(BF16)
        dw_mem_kv[l] = _mm(mem_n, dmkv, "tn", f"mem_kv_dw{l}", BF16)
        dmem_n = _mm(dmkv, w["w_mem_kv"][l], "nt", f"mem_kv_dx{l}", add=dmem_n)
        if l < N_A:
            do, dz, dng = _gdn_post_bwd(s["o"], s["proj"], row2(w["gdn_norm"][l]), dcat, f"gdn_post_bwd{l}")
            dq, dk, dv, dg_, dbeta = _gdn_bwd(s["q"], s["k"], s["v"], s["gc"], s["beta"], s["states"], s["tinv"], s["gw"],
                                              s["vn"], do, f"gdn_scan_bwd{l}")
            res = _gdn_pre_bwd(s["proj"], w["gdn_conv"][l], sel_b, sel_a, s["a_row"], s["dt_row"], dq, dk, dv, dg_, dbeta,
                               f"gdn_pre_bwd{l}")
            dconv, dba = res[0], res[1]
            dgdn_conv[l] = jnp.stack([r.sum(axis=0) for r in res[2:2 + CONV_K]])
            dgdn_a[l] = res[2 + CONV_K].sum(axis=0)[::LANES]
            dgdn_dt[l] = res[3 + CONV_K].sum(axis=0)[::LANES]
            dgdn_norm[l] = dng.sum(axis=0)
            dqkv = _conv_bwd_input(dconv, w["gdn_conv"][l], f"gdn_conv_bwd{l}")
            dproj = jnp.concatenate([dqkv, dz, dmq, dba], axis=1)
            dh = _mm(dproj, w["gdn_w_in"][l], "nt", f"gdn_in_dx{l}")
            dgdn_in[l] = _mm(s["h"], dproj, "tn", f"gdn_in_dw{l}", BF16)
        else:
            b = l - N_A
            dqr, dkt_l, dvt_l, dsk = _swa_bwd(s["qr"], kt, vt, w["swa_sinks"][b], dcat, f"swa_bwd{l}")
            dkt = dkt_l if dkt is None else dkt + dkt_l
            dvt = dvt_l if dvt is None else dvt + dvt_l
            dswa_sinks[b] = dsk[0, :SWA_HEADS]
            dq = _rope(dqr, SWA_HEADS * SWA_DH, 0, tabs, -1, f"rope_q_bwd{l}", BF16)
            dproj = jnp.concatenate([dq, dmq], axis=1)
            dh = _mm(dproj, w["swa_w_q"][b], "nt", f"swa_in_dx{l}")
            dswa_q[b] = _mm(s["h"], dproj, "tn", f"swa_in_dw{l}", BF16)
        dx, dxb, dg = _rms_bwd(s["x0"], row2(w["ln_mix"][l]), dh, dx, f"norm_mix_bwd{l}")
        dln_mix[l] = dg.sum(axis=0)
    _, _, dg = _rms_bwd(mem, row2(w["ln_mem"]), dmem_n, None, "mem_norm_bwd")
    gr["ln_mem"] = dg.sum(axis=0)
    gr.update(ln_mix=jnp.stack(dln_mix), ln_ffn=jnp.stack(dln_ffn), w_mem_kv=jnp.stack(dw_mem_kv), w_out=jnp.stack(dw_out),
              w_gate_up=_ff_deinterleave(jnp.stack(dw_gu)), w_down=jnp.stack(dw_dn), gdn_w_in=jnp.stack(dgdn_in), gdn_conv=jnp.stack(dgdn_conv),
              gdn_A_log=jnp.stack(dgdn_a), gdn_dt_bias=jnp.stack(dgdn_dt), gdn_norm=jnp.stack(dgdn_norm),
              swa_w_q=jnp.stack(dswa_q), swa_sinks=jnp.stack(dswa_sinks))
    return loss_part, dx, gr


def kernel(x, mem, positions, ln_mix, ln_ffn, ln_mem, w_mem_kv, w_out, w_gate_up, w_down, gdn_w_in, gdn_conv, gdn_A_log, gdn_dt_bias, gdn_norm, swa_w_q, swa_sinks, ln_kv, w_kv, ln_final, loss_target, m_ln_mix, m_ln_ffn, m_ln_mem, m_w_mem_kv, m_w_out, m_w_gate_up, m_w_down, m_gdn_w_in, m_gdn_conv, m_gdn_A_log, m_gdn_dt_bias, m_gdn_norm, m_swa_w_q, m_swa_sinks, m_ln_kv, m_w_kv, m_ln_final, v_ln_mix, v_ln_ffn, v_ln_mem, v_w_mem_kv, v_w_out, v_w_gate_up, v_w_down, v_gdn_w_in, v_gdn_conv, v_gdn_A_log, v_gdn_dt_bias, v_gdn_norm, v_swa_w_q, v_swa_sinks, v_ln_kv, v_w_kv, v_ln_final):
    given = dict(locals())
    wts = {n: given[n] for n in WEIGHTS}
    c = lax.axis_index("c")

    halves = [_my_half(wts[n].astype(BF16), n, c) for n in SHARDED]
    half_shapes = [h.shape for h in halves]
    conv_shape = wts["gdn_conv"].shape
    cpack = _pack([wts["gdn_conv"]], 16)
    conv_half = lax.dynamic_slice_in_dim(cpack, c * SUBLANES, SUBLANES, axis=0)
    got = _exchange(halves + [conv_half], [True] * (len(halves) + 1), ALL_PEERS, "gather_weights")
    full = {n: wts[n] for n in SMALL}
    for n, g in zip(SHARDED, got):
        full[n] = _assemble(g, n)
    conv_all = got[-1].reshape(4, 16, D_MODEL)
    full["gdn_conv"] = jnp.concatenate([_unpack(conv_all[s], [conv_shape])[0] for s in range(4)], axis=2)
    full["gdn_w_in"] = _gdn_in_pad(full["gdn_w_in"])

    loss_part, dx, gr = _local_step(x[0], mem[0], positions[0], loss_target[0], full)
    gr["gdn_w_in"] = _gdn_in_unpad(gr["gdn_w_in"])

    pieces = [_to_pieces(gr[n].astype(BF16), n, hs) for n, hs in zip(SHARDED, half_shapes)]
    small_shapes = [wts[n].shape for n in SMALL] + [conv_shape[:2] + (4 * conv_shape[2],), (SUBLANES, LANES)]
    rows_s = _rows_for(small_shapes, SUBLANES)
    spack = _pack([gr[n] for n in SMALL] + [gr["gdn_conv"], loss_part], rows_s)
    parts = _exchange(pieces + [spack], [False] * len(pieces) + [True], ALL_PEERS, "scatter_grads")
    mine = [_sum_slots(p.reshape(N_DEV, -1, p.shape[-1]), f"sum_{n}").reshape(hs)
            for n, p, hs in zip(SHARDED, parts, half_shapes)]
    ssum = _unpack(_sum_slots(parts[-1], "sum_small"), small_shapes)
    both = _exchange(mine, [True] * len(mine), SIBLING, "swap_grad_halves")
    g_all = {n: _from_halves(b, n) for n, b in zip(SHARDED, both)}
    g_all.update(zip(SMALL, ssum[:len(SMALL)]))
    chip = 2 * lax.axis_index("x") + lax.axis_index("y")
    g_all["gdn_conv"] = lax.dynamic_slice_in_dim(ssum[len(SMALL)], chip * conv_shape[2], conv_shape[2], axis=2)
    loss = jnp.sum(ssum[-1])

    out = dict(grad=g_all, delta={}, new_m={}, new_v={})
    for n in SHARDED:
        as2d = lambda a: a.reshape(-1, a.shape[-1])
        res = _adamw(as2d(wts[n]), as2d(g_all[n]), as2d(given["m_" + n]), as2d(given["v_" + n]), f"adamw_{n}", bm=256)
        for kind, r in zip(("delta", "new_m", "new_v"), res):
            out[kind][n] = r.reshape(wts[n].shape)
    small_names = SMALL + ("gdn_conv",)
    small_w_shapes = [wts[n].shape for n in small_names]
    rows_a = _rows_for(small_w_shapes, SUBLANES)
    res = _adamw(_pack([wts[n] for n in small_names], rows_a), _pack([g_all[n] for n in small_names], rows_a),
                 _pack([given["m_" + n] for n in small_names], rows_a),
                 _pack([given["v_" + n] for n in small_names], rows_a), "adamw_small")
    for kind, r in zip(("delta", "new_m", "new_v"), res):
        out[kind].update(zip(small_names, _unpack(r, small_w_shapes)))
    return (loss, dx[None], *[out["grad"][n] for n in WEIGHTS], *[out["delta"][n] for n in WEIGHTS],
            *[out["new_m"][n] for n in WEIGHTS], *[out["new_v"][n] for n in WEIGHTS])
```

```python
import functools
import math

import jax
import jax.numpy as jnp
from jax import lax
from jax.experimental import pallas as pl
from jax.experimental.pallas import tpu as pltpu

F32 = jnp.float32
BF16 = jnp.bfloat16
HI = lax.Precision.HIGHEST
MESH = pl.DeviceIdType.MESH

D_MODEL = 1024
DEPTH = 4
N_A = 2
N_B = 2
EPS = 1e-6
GDN_HEADS = 6
GDN_DK = 128
GDN_W = 768
CONV_K = 4
CHUNK = 64
SWA_HEADS = 12
SWA_KV_HEADS = 2
SWA_DH = 64
SWA_GROUP = 6
SWA_GW = SWA_GROUP * SWA_DH
SWA_BLOCK = 128
ROPE_THETA = 500000.0
ROT_DIM = 16
MEM_LEN = 256
MEM_HEADS = 4
MEM_DH = 64
MEM_W = 256
D_FF = 2816
GDN_IN = 3340
GDN_IN_PAD = 3456
ADAM_LR = 0.001
ADAM_B1 = 0.9
ADAM_B2 = 0.999
ADAM_EPS = 1e-08
ADAM_WD = 0.01
ADAM_STEP = 10

N_DEV = 8
LANES = 128
SUBLANES = 8
V7X_VMEM_LIMIT = 56 * 2**20

SHARDED = ("w_mem_kv", "w_out", "w_gate_up", "w_down", "gdn_w_in", "swa_w_q", "w_kv")
SMALL = ("ln_mix", "ln_ffn", "ln_mem", "gdn_A_log", "gdn_dt_bias", "gdn_norm", "swa_sinks", "ln_kv", "ln_final")
WEIGHTS = ("ln_mix", "ln_ffn", "ln_mem", "w_mem_kv", "w_out", "w_gate_up", "w_down", "gdn_w_in", "gdn_conv",
           "gdn_A_log", "gdn_dt_bias", "gdn_norm", "swa_w_q", "swa_sinks", "ln_kv", "w_kv", "ln_final")


def _params(sem=None, **kw):
    return pltpu.CompilerParams(dimension_semantics=sem, vmem_limit_bytes=V7X_VMEM_LIMIT, **kw)


def _dot(a, b, dims=(((1,), (0,)), ((), ())), precision=None):
    return lax.dot_general(a, b, dims, precision=precision, preferred_element_type=F32)


NT = (((1,), (1,)), ((), ()))
TN = (((0,), (0,)), ((), ()))


def _fold8(v):
    r, w = v.shape
    return v.reshape(r // SUBLANES, SUBLANES, w).sum(axis=0)


def _row(a, w=None, cb=0):
    return ("row", a, a.shape[1] if w is None else w, cb)


def _full(a):
    return ("full", a, None, None)


def _prev8(a, w, cb=0):
    return ("prev8", a, w, cb)


def _next8(a, w, cb=0):
    return ("next8", a, w, cb)


def _rowcall(fn, name, rows, bm, ins, outs, accs=()):
    bm = min(bm, rows)
    assert rows % bm == 0 and bm % SUBLANES == 0
    steps = rows // bm
    r8 = bm // SUBLANES
    in_specs, arrays = [], []
    for kind, a, w, cb in ins:
        arrays.append(a)
        if kind == "row":
            in_specs.append(pl.BlockSpec((bm, w), lambda i, cb=cb: (i, cb)))
        elif kind == "full":
            in_specs.append(pl.BlockSpec(a.shape, lambda i, nd=a.ndim: (0,) * nd))
        elif kind == "prev8":
            in_specs.append(pl.BlockSpec((SUBLANES, w), lambda i, cb=cb: (jnp.maximum(i * r8 - 1, 0), cb)))
        else:
            last = rows // SUBLANES - 1
            in_specs.append(pl.BlockSpec((SUBLANES, w), lambda i, cb=cb: (jnp.minimum((i + 1) * r8, last), cb)))
    out_shape = [jax.ShapeDtypeStruct((rows, w), dt) for w, dt in outs]
    out_specs = [pl.BlockSpec((bm, w), lambda i: (i, 0)) for w, _ in outs]
    out_shape += [jax.ShapeDtypeStruct(s, F32) for s in accs]
    out_specs += [pl.BlockSpec(s, lambda i: (0, 0)) for s in accs]
    n_in, n_out = len(ins), len(outs)

    def body(*refs):
        i = pl.program_id(0)
        res = fn(i, *[r[...] for r in refs[:n_in]])
        if not isinstance(res, (tuple, list)):
            res = (res,)
        for r, v in zip(refs[n_in:n_in + n_out], res[:n_out]):
            r[...] = v.astype(r.dtype)
        if accs:
            @pl.when(i == 0)
            def _():
                for r in refs[n_in + n_out:]:
                    r[...] = jnp.zeros(r.shape, F32)
            for r, v in zip(refs[n_in + n_out:], res[n_out:]):
                r[...] += v

    res = pl.pallas_call(
        body, name=name, grid=(steps,), in_specs=in_specs, out_specs=out_specs, out_shape=out_shape,
        compiler_params=_params(("arbitrary",)))(*arrays)
    return res


def _tile(n, cap):
    for t in (1408, 1152, 1024, 896, 768, 640, 512, 384, 256, 128):
        if t <= cap and n % t == 0:
            return t
    return n


def _mm(a, b, mode, name, out_dtype=F32, add=None):
    if mode == "tn":
        s, m = a.shape
        n = b.shape[1]
        bm, bn, bk = _tile(m, 1408), _tile(n, 512), min(s, 2048)
        nk = s // bk

        def body(a_ref, b_ref, o_ref, acc_ref):
            k = pl.program_id(2)

            @pl.when(k == 0)
            def _():
                acc_ref[...] = jnp.zeros(acc_ref.shape, F32)
            acc_ref[...] += _dot(a_ref[...].astype(BF16), b_ref[...].astype(BF16), TN)

            @pl.when(k == nk - 1)
            def _():
                o_ref[...] = acc_ref[...].astype(o_ref.dtype)

        return pl.pallas_call(
            body, name=name, grid=(m // bm, n // bn, nk),
            in_specs=[pl.BlockSpec((bk, bm), lambda i, j, k: (k, i)), pl.BlockSpec((bk, bn), lambda i, j, k: (k, j))],
            out_specs=pl.BlockSpec((bm, bn), lambda i, j, k: (i, j)),
            out_shape=jax.ShapeDtypeStruct((m, n), out_dtype),
            scratch_shapes=[pltpu.VMEM((bm, bn), F32)],
            compiler_params=_params(("parallel", "parallel", "arbitrary")))(a, b)

    m, k = a.shape
    n = b.shape[1] if mode == "nn" else b.shape[0]
    big = k > 2048
    bm = min(m, 512 if big else 1024)
    bn = _tile(n, 512 if big else 1024)
    dims = NT if mode == "nt" else (((1,), (0,)), ((), ()))
    b_spec = (pl.BlockSpec((k, bn), lambda i, j: (0, j)) if mode == "nn" else pl.BlockSpec((bn, k), lambda i, j: (j, 0)))
    in_specs = [pl.BlockSpec((bm, k), lambda i, j: (i, 0)), b_spec]
    args = [a, b]
    if add is not None:
        in_specs.append(pl.BlockSpec((bm, bn), lambda i, j: (i, j)))
        args.append(add)

    def body(a_ref, b_ref, *rest):
        o_ref = rest[-1]
        acc = _dot(a_ref[...].astype(BF16), b_ref[...].astype(BF16), dims)
        if add is not None:
            acc = acc + rest[0][...]
        o_ref[...] = acc.astype(o_ref.dtype)

    return pl.pallas_call(
        body, name=name, grid=(m // bm, n // bn), in_specs=in_specs,
        out_specs=pl.BlockSpec((bm, bn), lambda i, j: (i, j)),
        out_shape=jax.ShapeDtypeStruct((m, n), out_dtype),
        compiler_params=_params(("parallel", "parallel")))(*args)


def _sigmoid(x):
    return 0.5 * jnp.tanh(0.5 * x) + 0.5


def _softplus(x):
    return jnp.maximum(x, 0.0) + jnp.log(1.0 + jnp.exp(-jnp.abs(x)))


def _silu_and_grad(x):
    s = _sigmoid(x)
    return x * s, s * (1.0 + x * (1.0 - s))


def _rms_stats(x):
    r = lax.rsqrt(jnp.mean(x * x, axis=-1, keepdims=True) + EPS)
    return r, x * r


def _rms_fwd(x, g, name, out_dtype=BF16, bm=512):
    def fn(i, x, g):
        _, xn = _rms_stats(x)
        return xn * g
    return _rowcall(fn, name, x.shape[0], bm, [_row(x), _full(g)], [(x.shape[1], out_dtype)])[0]


def _rms_bwd_math(x, g, dy):
    r, xn = _rms_stats(x)
    dxn = dy * g
    dx = r * (dxn - xn * jnp.mean(dxn * xn, axis=-1, keepdims=True))
    return dx, dy * xn


def _rms_bwd(x, g, dy, res, name, bm=256):
    d = x.shape[1]

    def fn(i, x, g, dy, *res_):
        dx, dg = _rms_bwd_math(x, g, dy.astype(F32))
        if res_:
            dx = dx + res_[0]
        return dx, dx, _fold8(dg)
    ins = [_row(x), _full(g), _row(dy)] + ([_row(res)] if res is not None else [])
    return _rowcall(fn, name, x.shape[0], bm, ins, [(d, F32), (d, BF16)], [(SUBLANES, d)])


def _final_loss(x, g, target, name, bm=256):
    d = x.shape[1]

    def fn(i, x, g, t):
        r, xn = _rms_stats(x)
        err = xn * g - t
        dy = err * (1.0 / d)
        dxn = dy * g
        dx = r * (dxn - xn * jnp.mean(dxn * xn, axis=-1, keepdims=True))
        e2 = _fold8(err * err)
        lp = e2[:, 0:LANES]
        for c in range(1, d // LANES):
            lp = lp + e2[:, c * LANES:(c + 1) * LANES]
        return dx, dx, lp * (0.5 / d), _fold8(dy * xn)
    return _rowcall(fn, name, x.shape[0], bm, [_row(x), _full(g), _row(target)], [(d, F32), (d, BF16)],
                    [(SUBLANES, LANES), (SUBLANES, d)])


FF_TILE = 256


def _ff_interleave(w):
    lead = w.shape[:-1]
    w = w.reshape(lead + (2, D_FF // FF_TILE, FF_TILE))
    return jnp.swapaxes(w, -3, -2).reshape(lead + (2 * D_FF,))


def _ff_deinterleave(w):
    lead = w.shape[:-1]
    w = w.reshape(lead + (D_FF // FF_TILE, 2, FF_TILE))
    return jnp.swapaxes(w, -3, -2).reshape(lead + (2 * D_FF,))


def _gate_up_fwd(h, w_gu, name, bm=1024):
    rows, k = h.shape
    bm = min(bm, rows)

    def body(a_ref, b_ref, gu_ref, act_ref):
        acc = _dot(a_ref[...], b_ref[...])
        gu_ref[...] = acc.astype(gu_ref.dtype)
        act_ref[...] = (_silu_and_grad(acc[:, :FF_TILE])[0] * acc[:, FF_TILE:]).astype(act_ref.dtype)

    return pl.pallas_call(
        body, name=name, grid=(rows // bm, D_FF // FF_TILE),
        in_specs=[pl.BlockSpec((bm, k), lambda i, j: (i, 0)), pl.BlockSpec((k, 2 * FF_TILE), lambda i, j: (0, j))],
        out_specs=[pl.BlockSpec((bm, 2 * FF_TILE), lambda i, j: (i, j)), pl.BlockSpec((bm, FF_TILE), lambda i, j: (i, j))],
        out_shape=[jax.ShapeDtypeStruct((rows, 2 * D_FF), BF16), jax.ShapeDtypeStruct((rows, D_FF), BF16)],
        compiler_params=_params(("parallel", "parallel")))(h, w_gu)


def _down_bwd(dx, w_down, gu, name, bm=1024):
    rows, k = dx.shape
    bm = min(bm, rows)

    def body(a_ref, b_ref, gu_ref, o_ref):
        da = _dot(a_ref[...], b_ref[...], NT)
        gu = gu_ref[...].astype(F32)
        s, ds = _silu_and_grad(gu[:, :FF_TILE])
        o_ref[:, :FF_TILE] = (da * gu[:, FF_TILE:] * ds).astype(o_ref.dtype)
        o_ref[:, FF_TILE:] = (da * s).astype(o_ref.dtype)

    return pl.pallas_call(
        body, name=name, grid=(rows // bm, D_FF // FF_TILE),
        in_specs=[pl.BlockSpec((bm, k), lambda i, j: (i, 0)), pl.BlockSpec((FF_TILE, k), lambda i, j: (j, 0)),
                  pl.BlockSpec((bm, 2 * FF_TILE), lambda i, j: (i, j))],
        out_specs=pl.BlockSpec((bm, 2 * FF_TILE), lambda i, j: (i, j)),
        out_shape=jax.ShapeDtypeStruct((rows, 2 * D_FF), BF16),
        compiler_params=_params(("parallel", "parallel")))(dx, w_down, gu)


def _rope_apply(x, tabs, sign):
    cos, ta, tb = tabs
    outs = []
    for c in range(x.shape[1] // LANES):
        xc = x[:, c * LANES:(c + 1) * LANES]
        if sign > 0:
            o = xc * cos + pltpu.roll(xc, LANES - 8, 1) * ta + pltpu.roll(xc, 8, 1) * tb
        else:
            o = xc * cos + pltpu.roll(xc * ta, 8, 1) + pltpu.roll(xc * tb, LANES - 8, 1)
        outs.append(o)
    return outs[0] if len(outs) == 1 else jnp.concatenate(outs, axis=1)


def _rope(x, w, cb, tabs, sign, name, out_dtype, bm=512):
    def fn(i, x, c, a, b):
        return _rope_apply(x.astype(F32), (c, a, b), sign)
    return _rowcall(fn, name, x.shape[0], bm, [_row(x, w, cb)] + [_row(t) for t in tabs], [(w, out_dtype)])[0]


def _shift_down(x, prev8, s, first):
    xs = pltpu.roll(x, s, 0)
    rp = pltpu.roll(prev8, s, 0) * jnp.where(first, 0.0, 1.0)
    rid = lax.broadcasted_iota(jnp.int32, rp.shape, 0)
    top = jnp.where(rid < s, rp, xs[0:SUBLANES])
    return jnp.concatenate([top, xs[SUBLANES:]], axis=0)


def _shift_up(x, next8, s, last):
    n = x.shape[0]
    xs = pltpu.roll(x, n - s, 0)
    rn = pltpu.roll(next8, SUBLANES - s, 0) * jnp.where(last, 0.0, 1.0)
    rid = lax.broadcasted_iota(jnp.int32, rn.shape, 0)
    bot = jnp.where(rid >= SUBLANES - s, rn, xs[n - SUBLANES:])
    return jnp.concatenate([xs[:n - SUBLANES], bot], axis=0)


def _conv_fwd(x, prev8, w, first):
    acc = x * w[CONV_K - 1:CONV_K]
    shifted = []
    for s in range(1, CONV_K):
        xs = _shift_down(x, prev8, s, first)
        shifted.append(xs)
        acc = acc + xs * w[CONV_K - 1 - s:CONV_K - s]
    return acc, shifted


def _l2n(x):
    outs, rs = [], []
    for h in range(x.shape[1] // LANES):
        xh = x[:, h * LANES:(h + 1) * LANES]
        r = lax.rsqrt(jnp.sum(xh * xh, axis=-1, keepdims=True) + EPS)
        outs.append(xh * r)
        rs.append(r)
    return jnp.concatenate(outs, axis=1), rs


def _gate_math(ba, sel_b, sel_a, a_row, dt_row):
    bl = _dot(ba, sel_b, precision=HI)
    al = _dot(ba, sel_a, precision=HI) + dt_row
    beta = _sigmoid(bl)
    ea = jnp.exp(a_row)
    g = -ea * _softplus(al)
    return bl, al, beta, g, ea


def _cumsum_chunks(x, reverse=False):
    n = x.shape[0]
    rid = lax.broadcasted_iota(jnp.int32, x.shape, 0) % CHUNK
    s = 1
    while s < CHUNK:
        if reverse:
            x = x + jnp.where(rid < CHUNK - s, pltpu.roll(x, n - s, 0), 0.0)
        else:
            x = x + jnp.where(rid >= s, pltpu.roll(x, s, 0), 0.0)
        s *= 2
    return x


def _gdn_pre_fwd(proj, conv_w, sel_b, sel_a, a_row, dt_row, name, bm=256):
    rows = proj.shape[0]
    w3 = 3 * GDN_W

    def fn(i, x, p8, ba, w, sel_b, sel_a, a_row, dt_row):
        conv, _ = _conv_fwd(x, p8, w, i == 0)
        act = _silu_and_grad(conv)[0]
        qk, _ = _l2n(act[:, :2 * GDN_W])
        _, _, beta, g, _ = _gate_math(ba, sel_b, sel_a, a_row, dt_row)
        return qk[:, :GDN_W], qk[:, GDN_W:], act[:, 2 * GDN_W:], _cumsum_chunks(g), beta
    ins = [_row(proj, w3, 0), _prev8(proj, w3, 0), _row(proj, LANES, (GDN_IN_PAD - LANES) // LANES),
           _full(conv_w), _full(sel_b), _full(sel_a), _full(a_row), _full(dt_row)]
    return _rowcall(fn, name, rows, bm, ins, [(GDN_W, F32)] * 5)


def _gdn_pre_bwd(proj, conv_w, sel_b, sel_a, a_row, dt_row, dq, dk, dv, dgc, dbeta, name, bm=128):
    rows = proj.shape[0]
    w3 = 3 * GDN_W

    def fn(i, x, p8, ba, w, sel_b, sel_a, a_row, dt_row, dq, dk, dv, dgc, dbeta):
        dg = _cumsum_chunks(dgc, reverse=True)
        conv, shifted = _conv_fwd(x, p8, w, i == 0)
        act, dact = _silu_and_grad(conv)
        qk, rs = _l2n(act[:, :2 * GDN_W])
        dqk = jnp.concatenate([dq, dk], axis=1)
        parts = []
        for h in range(2 * GDN_HEADS):
            sl = slice(h * LANES, (h + 1) * LANES)
            y, dy = qk[:, sl], dqk[:, sl]
            parts.append(rs[h] * (dy - y * jnp.sum(y * dy, axis=-1, keepdims=True)))
        dconv = jnp.concatenate(parts + [dv], axis=1) * dact
        dws = [_fold8(dconv * xs) for xs in reversed(shifted)] + [_fold8(dconv * x)]
        bl, al, beta, g, ea = _gate_math(ba, sel_b, sel_a, a_row, dt_row)
        dbl = dbeta * beta * (1.0 - beta)
        dal = dg * (-ea) * _sigmoid(al)
        dba = _dot(dbl, sel_b, NT, precision=HI) + _dot(dal, sel_a, NT, precision=HI)
        return (dconv, dba * (1.0 / LANES)) + tuple(dws) + (_fold8(dg * g), _fold8(dal))
    ins = [_row(proj, w3, 0), _prev8(proj, w3, 0), _row(proj, LANES, (GDN_IN_PAD - LANES) // LANES),
           _full(conv_w), _full(sel_b), _full(sel_a), _full(a_row), _full(dt_row),
           _row(dq), _row(dk), _row(dv), _row(dgc), _row(dbeta)]
    return _rowcall(fn, name, rows, bm, ins, [(w3, F32), (LANES, BF16)],
                    [(SUBLANES, w3)] * CONV_K + [(SUBLANES, GDN_W)] * 2)


def _conv_bwd_input(dconv, conv_w, name, bm=256):
    rows, w3 = dconv.shape
    steps = rows // min(bm, rows)

    def fn(i, dc, n8, w):
        acc = dc * w[CONV_K - 1:CONV_K]
        for s in range(1, CONV_K):
            acc = acc + _shift_up(dc, n8, s, i == steps - 1) * w[CONV_K - 1 - s:CONV_K - s]
        return acc
    return _rowcall(fn, name, rows, bm, [_row(dconv), _next8(dconv, w3, 0), _full(conv_w)], [(w3, BF16)])[0]


def _gdn_post_fwd(o, proj, ng, name, bm=512):
    def fn(i, o, z, ng):
        outs = []
        for h in range(GDN_HEADS):
            sl = slice(h * LANES, (h + 1) * LANES)
            _, on = _rms_stats(o[:, sl])
            outs.append(on * ng * _silu_and_grad(z[:, sl])[0])
        return jnp.concatenate(outs, axis=1)
    return _rowcall(fn, name, o.shape[0], bm, [_row(o), _row(proj, GDN_W, 3), _full(ng)], [(GDN_W, BF16)])[0]


def _gdn_post_bwd(o, proj, ng, dcat, name, bm=256):
    def fn(i, o, z, ng, dm):
        dm = dm.astype(F32)
        dos, dzs = [], []
        dng = jnp.zeros((SUBLANES, LANES), F32)
        for h in range(GDN_HEADS):
            sl = slice(h * LANES, (h + 1) * LANES)
            s, ds = _silu_and_grad(z[:, sl])
            r, on = _rms_stats(o[:, sl])
            dzs.append(dm[:, sl] * on * ng * ds)
            dy = dm[:, sl] * s
            dxn = dy * ng
            dos.append(r * (dxn - on * jnp.mean(dxn * on, axis=-1, keepdims=True)))
            dng = dng + _fold8(dy * on)
        return jnp.concatenate(dos, axis=1), jnp.concatenate(dzs, axis=1), dng
    return _rowcall(fn, name, o.shape[0], bm, [_row(o), _row(proj, GDN_W, 3), _full(ng), _row(dcat, GDN_W, 0)],
                    [(GDN_W, F32), (GDN_W, BF16)], [(SUBLANES, LANES)])


def _split3(x):
    hi = x.astype(BF16)
    r = x - hi.astype(F32)
    mid = r.astype(BF16)
    return hi, mid, (r - mid.astype(F32)).astype(BF16)


def _bdot(a, b, mode="nn"):
    lc, rc = {"nn": (2, 1), "nt": (2, 2), "tn": (1, 1)}[mode]
    return lax.dot_general(a, b, (((lc,), (rc,)), ((0,), (0,))), preferred_element_type=F32)


def _bdot3(a, b, mode="nn"):
    ah, bh = a.astype(BF16), b.astype(BF16)
    al, bl = (a - ah.astype(F32)).astype(BF16), (b - bh.astype(F32)).astype(BF16)
    return _bdot(ah, bh, mode) + _bdot(ah, bl, mode) + _bdot(al, bh, mode)


GDN_CB = 2


def _gdn_chunk(q, k, v, gc, beta, t=None):
    c = CHUNK
    nb = q.shape[0]
    row = lax.broadcasted_iota(jnp.int32, (c, c), 0)
    col = lax.broadcasted_iota(jnp.int32, (c, c), 1)
    tril, strict = row >= col, row > col
    lane0 = (lax.broadcasted_iota(jnp.int32, (nb, c, LANES), 2) == 0).astype(BF16)
    gc_row = sum(_bdot(lane0, part, "nt") for part in _split3(gc))
    dm = jnp.exp(jnp.where(tril, gc[:, :, :c] - gc_row, -1e30))
    eg = jnp.exp(gc)
    gcl = gc[:, c - 1:c, :]
    ekg = jnp.exp(gcl - gc)
    egl = jnp.exp(gcl)
    qs = q * (GDN_DK ** -0.5)
    kb = k * beta
    kk = _bdot(kb, k, "nt")
    a = jnp.where(strict, kk * dm, 0.0)
    vb = v * beta
    kbg = kb * eg
    qk = _bdot(qs, k, "nt")
    p = jnp.where(tril, qk * dm, 0.0)
    out = dict(tril=tril, strict=strict, dm=dm, eg=eg, ekg=ekg, egl=egl, qs=qs, kb=kb, kk=kk, a=a,
               vb=vb, kbg=kbg, qk=qk, p=p, qg=qs * eg, kg=k * ekg)
    if t is None:
        y = -a
        t = (row == col).astype(F32) + y
        for _ in range(5):
            y = _bdot3(y, y)
            t = t + _bdot3(t, y)
        out.update(u=_bdot3(t, vb), w=_bdot3(t, kbg))
    out["t"] = t
    return out


def _gdn_stack(ref):
    return jnp.stack([ref[c * CHUNK:(c + 1) * CHUNK, h * LANES:(h + 1) * LANES]
                      for c in range(GDN_CB) for h in range(GDN_HEADS)])


def _gdn_unstack(x, ref):
    for c in range(GDN_CB):
        for h in range(GDN_HEADS):
            ref[c * CHUNK:(c + 1) * CHUNK, h * LANES:(h + 1) * LANES] = x[c * GDN_HEADS + h]


def _gdn_fwd(q, k, v, gc, beta, name):
    rows = q.shape[0]
    n_chunks = rows // CHUNK
    steps = n_chunks // GDN_CB
    blk = pl.BlockSpec((GDN_CB * CHUNK, GDN_W), lambda n: (n, 0))
    st = pl.BlockSpec((GDN_HEADS, GDN_CB, GDN_DK, LANES), lambda n: (0, n, 0, 0))
    tinv = pl.BlockSpec((GDN_CB, GDN_HEADS, CHUNK, CHUNK), lambda n: (n, 0, 0, 0))

    def body(q_ref, k_ref, v_ref, g_ref, b_ref, o_ref, st_ref, t_ref, w_ref, vn_ref, s_ref):
        @pl.when(pl.program_id(0) == 0)
        def _():
            s_ref[...] = jnp.zeros(s_ref.shape, F32)
        c = _gdn_chunk(*[_gdn_stack(r) for r in (q_ref, k_ref, v_ref, g_ref, b_ref)])
        _gdn_unstack(c["w"], w_ref)
        s = s_ref[...]
        for i in range(GDN_CB):
            hs = slice(i * GDN_HEADS, (i + 1) * GDN_HEADS)
            rs = slice(i * CHUNK, (i + 1) * CHUNK)
            st_ref[:, i] = s
            vn = c["u"][hs] - _bdot(c["w"][hs], s)
            o = _bdot(c["qg"][hs], s) + _bdot(c["p"][hs], vn)
            s = s * c["egl"][hs] + _bdot(c["kg"][hs], vn, "tn")
            t_ref[i] = c["t"][hs]
            for h in range(GDN_HEADS):
                o_ref[rs, h * LANES:(h + 1) * LANES] = o[h]
                vn_ref[rs, h * LANES:(h + 1) * LANES] = vn[h]
        s_ref[...] = s

    f = jax.ShapeDtypeStruct((rows, GDN_W), F32)
    return pl.pallas_call(
        body, name=name, grid=(steps,), in_specs=[blk] * 5, out_specs=[blk, st, tinv, blk, blk],
        out_shape=[f, jax.ShapeDtypeStruct((GDN_HEADS, n_chunks, GDN_DK, LANES), F32),
                   jax.ShapeDtypeStruct((n_chunks, GDN_HEADS, CHUNK, CHUNK), F32), f, f],
        scratch_shapes=[pltpu.VMEM((GDN_HEADS, GDN_DK, LANES), F32)],
        compiler_params=_params(("arbitrary",)))(q, k, v, gc, beta)


def _gdn_bwd(q, k, v, gc, beta, states, tinv, w, vn, do, name):
    rows = q.shape[0]
    n_chunks = rows // CHUNK
    steps = n_chunks // GDN_CB
    blk = pl.BlockSpec((GDN_CB * CHUNK, GDN_W), lambda n: (steps - 1 - n, 0))
    st = pl.BlockSpec((GDN_HEADS, GDN_CB, GDN_DK, LANES), lambda n: (0, steps - 1 - n, 0, 0))
    ti = pl.BlockSpec((GDN_CB, GDN_HEADS, CHUNK, CHUNK), lambda n: (steps - 1 - n, 0, 0, 0))
    nbatch = GDN_CB * GDN_HEADS

    def lanesum(x):
        return jnp.broadcast_to(jnp.sum(x, axis=-1, keepdims=True), x.shape)

    def body(q_ref, k_ref, v_ref, g_ref, b_ref, st_ref, t_ref, w_ref, vn_ref, do_ref,
             dq_ref, dk_ref, dv_ref, dg_ref, db_ref, ds_ref):
        @pl.when(pl.program_id(0) == 0)
        def _():
            ds_ref[...] = jnp.zeros(ds_ref.shape, F32)
        q, k, v, gc, beta, w, vn, do = [_gdn_stack(r) for r in (q_ref, k_ref, v_ref, g_ref, b_ref, w_ref, vn_ref, do_ref)]
        t = t_ref[...].reshape(nbatch, CHUNK, CHUNK)
        s = jnp.stack([st_ref[h, i] for i in range(GDN_CB) for h in range(GDN_HEADS)])
        c = _gdn_chunk(q, k, v, gc, beta, t)
        tril, strict, dm = c["tril"], c["strict"], c["dm"]
        dsn = ds_ref[...]
        dvn_c, dkg_c, dgl_c = [None] * GDN_CB, [None] * GDN_CB, [None] * GDN_CB
        for i in reversed(range(GDN_CB)):
            hs = slice(i * GDN_HEADS, (i + 1) * GDN_HEADS)
            dvn_c[i] = _bdot(c["p"][hs], do[hs], "tn") + _bdot(c["kg"][hs], dsn)
            dkg_c[i] = _bdot(vn[hs], dsn, "nt")
            dgl_c[i] = jnp.sum(jnp.sum(s[hs] * dsn, axis=2, keepdims=True), axis=1, keepdims=True) * c["egl"][hs]
            dsn = dsn * c["egl"][hs] + _bdot(c["qg"][hs], do[hs], "tn") - _bdot(w[hs], dvn_c[i], "tn")
        ds_ref[...] = dsn
        dvn, dkg, dgl = jnp.concatenate(dvn_c), jnp.concatenate(dkg_c), jnp.concatenate(dgl_c)
        dp = jnp.where(tril, _bdot(do, vn, "nt"), 0.0)
        dqg = _bdot(do, s, "nt")
        dw = -_bdot(dvn, s, "nt")
        dvb = _bdot3(t, dvn, "tn")
        dkbg = _bdot3(t, dw, "tn")
        dt = _bdot(dvn, c["vb"], "nt") + _bdot(dw, c["kbg"], "nt")
        da = jnp.where(strict, -_bdot3(_bdot3(t, dt, "tn"), t, "nt"), 0.0)
        dkk = da * dm
        dqk = dp * dm
        dkb = _bdot(dkk, k) + dkbg * c["eg"]
        dk = _bdot(dkk, c["kb"], "tn") + _bdot(dqk, c["qs"], "tn") + dkg * c["ekg"] + dkb * beta
        dqs = _bdot(dqk, k) + dqg * c["eg"]
        e = da * c["a"] + dp * c["p"]
        ones = jnp.ones((nbatch, CHUNK, LANES), BF16)
        col_sums = sum(_bdot(part, ones, "tn") for part in _split3(e))
        kg_term = lanesum(dkg * c["kg"])
        dgc = (jnp.broadcast_to(jnp.sum(e, axis=-1, keepdims=True), (nbatch, CHUNK, LANES)) - col_sums
               + lanesum(dqg * c["qg"]) - kg_term + lanesum(dkbg * c["kbg"]))
        dgcl = jnp.sum(kg_term, axis=1, keepdims=True) + dgl
        last = lax.broadcasted_iota(jnp.int32, (CHUNK, LANES), 0) == CHUNK - 1
        _gdn_unstack(dqs * (GDN_DK ** -0.5), dq_ref)
        _gdn_unstack(dk, dk_ref)
        _gdn_unstack(dvb * beta, dv_ref)
        _gdn_unstack(dgc + jnp.where(last, dgcl, 0.0), dg_ref)
        _gdn_unstack(lanesum(dvb * v) + lanesum(dkb * k), db_ref)

    return pl.pallas_call(
        body, name=name, grid=(steps,), in_specs=[blk] * 5 + [st, ti, blk, blk, blk], out_specs=[blk] * 5,
        out_shape=[jax.ShapeDtypeStruct((rows, GDN_W), F32)] * 5,
        scratch_shapes=[pltpu.VMEM((GDN_HEADS, GDN_DK, LANES), F32)],
        compiler_params=_params(("arbitrary",)))(q, k, v, gc, beta, states, tinv, w, vn, do)


def _swa_masks(first):
    r = lax.broadcasted_iota(jnp.int32, (SWA_BLOCK, 2 * SWA_BLOCK), 0)
    c = lax.broadcasted_iota(jnp.int32, (SWA_BLOCK, 2 * SWA_BLOCK), 1)
    band = (c > r) & (c <= r + SWA_BLOCK)
    return band & (jnp.logical_not(first) | (c >= SWA_BLOCK))


def _swa_stack(ref, j):
    lane = lax.broadcasted_iota(jnp.int32, (1, LANES), 1)
    parts = []
    for g in range(SWA_GROUP):
        ch = j * (SWA_GROUP // 2) + g // 2
        keep = (lane < SWA_DH) if g % 2 == 0 else (lane >= SWA_DH)
        parts.append(ref[:, ch * LANES:(ch + 1) * LANES] * keep.astype(ref.dtype))
    return jnp.concatenate(parts, axis=0)


def _swa_unstack(x2, j, out_ref):
    low = lax.broadcasted_iota(jnp.int32, (SWA_BLOCK, LANES), 1) < SWA_DH
    for c3 in range(SWA_GROUP // 2):
        even = x2[(2 * c3) * SWA_BLOCK:(2 * c3 + 1) * SWA_BLOCK]
        odd = x2[(2 * c3 + 1) * SWA_BLOCK:(2 * c3 + 2) * SWA_BLOCK]
        ch = j * (SWA_GROUP // 2) + c3
        out_ref[:, ch * LANES:(ch + 1) * LANES] = jnp.where(low, even, odd).astype(out_ref.dtype)


def _swa_probs(s, sink, mask):
    s = jnp.where(mask, s, -1e30)
    m = jnp.maximum(jnp.max(s, axis=-1, keepdims=True), sink)
    p = jnp.where(mask, jnp.exp(s - m), 0.0)
    es = jnp.exp(sink - m)
    inv = 1.0 / (jnp.sum(p, axis=-1, keepdims=True) + es)
    return p * inv, es * inv


def _swa_scores(q_ref, kc_ref, kp_ref, sink_ref, j, mask):
    sl = slice(j * LANES, (j + 1) * LANES)
    qst = _swa_stack(q_ref, j)
    kw = jnp.concatenate([kp_ref[:, sl], kc_ref[:, sl]], axis=0)
    s = _dot(qst, kw, NT) * (SWA_DH ** -0.5)
    ps = [_swa_probs(s[g * SWA_BLOCK:(g + 1) * SWA_BLOCK], sink_ref[j * SWA_GROUP + g], mask)
          for g in range(SWA_GROUP)]
    return qst, kw, ps


def _swa_fwd(q, k2, v2, sinks, name):
    rows = q.shape[0]
    nb = rows // SWA_BLOCK
    w = SWA_HEADS * SWA_DH
    kvw = SWA_KV_HEADS * LANES
    cur = pl.BlockSpec((SWA_BLOCK, w), lambda i: (i, 0))
    kcur = pl.BlockSpec((SWA_BLOCK, kvw), lambda i: (i, 0))
    kprev = pl.BlockSpec((SWA_BLOCK, kvw), lambda i: (jnp.maximum(i - 1, 0), 0))

    def body(sink_ref, q_ref, kc_ref, kp_ref, vc_ref, vp_ref, o_ref):
        mask = _swa_masks(pl.program_id(0) == 0)
        for j in range(SWA_KV_HEADS):
            sl = slice(j * LANES, (j + 1) * LANES)
            _, _, ps = _swa_scores(q_ref, kc_ref, kp_ref, sink_ref, j, mask)
            vw = jnp.concatenate([vp_ref[:, sl], vc_ref[:, sl]], axis=0)
            pst = jnp.concatenate([p.astype(BF16) for p, _ in ps], axis=0)
            _swa_unstack(_dot(pst, vw), j, o_ref)

    return pl.pallas_call(
        body, name=name, grid=(nb,),
        in_specs=[pl.BlockSpec(memory_space=pltpu.SMEM), cur, kcur, kprev, kcur, kprev], out_specs=cur,
        out_shape=jax.ShapeDtypeStruct((rows, w), BF16),
        compiler_params=_params(("arbitrary",)))(sinks, q, k2, k2, v2, v2)


def _swa_bwd(q, k2, v2, sinks, dcat, name):
    rows = q.shape[0]
    nb = rows // SWA_BLOCK
    w = SWA_HEADS * SWA_DH
    kvw = SWA_KV_HEADS * LANES
    cur = pl.BlockSpec((SWA_BLOCK, w), lambda i: (jnp.minimum(i, nb - 1), 0))
    kcur = pl.BlockSpec((SWA_BLOCK, kvw), lambda i: (jnp.minimum(i, nb - 1), 0))
    kprev = pl.BlockSpec((SWA_BLOCK, kvw), lambda i: (jnp.clip(i - 1, 0, nb - 1), 0))
    late = pl.BlockSpec((SWA_BLOCK, kvw), lambda i: (jnp.maximum(i - 1, 0), 0))
    acc_spec = pl.BlockSpec((SUBLANES, LANES), lambda i: (0, 0))

    def body(sink_ref, q_ref, kc_ref, kp_ref, vc_ref, vp_ref, do_ref, dq_ref, dk_ref, dv_ref, dsk_ref,
             ck_ref, cv_ref):
        i = pl.program_id(0)

        @pl.when(i == 0)
        def _():
            ck_ref[...] = jnp.zeros(ck_ref.shape, F32)
            cv_ref[...] = jnp.zeros(cv_ref.shape, F32)
            dsk_ref[...] = jnp.zeros(dsk_ref.shape, F32)

        @pl.when(i == nb)
        def _():
            dk_ref[...] = ck_ref[...]
            dv_ref[...] = cv_ref[...]

        @pl.when(i < nb)
        def _():
            mask = _swa_masks(i == 0)
            lane = lax.broadcasted_iota(jnp.int32, (SUBLANES, LANES), 1)
            dsk = jnp.zeros((SUBLANES, LANES), F32)
            for j in range(SWA_KV_HEADS):
                sl = slice(j * LANES, (j + 1) * LANES)
                qst, kw, ps = _swa_scores(q_ref, kc_ref, kp_ref, sink_ref, j, mask)
                vw = jnp.concatenate([vp_ref[:, sl], vc_ref[:, sl]], axis=0)
                dost = _swa_stack(do_ref, j)
                dpr = _dot(dost, vw, NT)
                dss = []
                for g in range(SWA_GROUP):
                    p, sink_p = ps[g]
                    dpg = dpr[g * SWA_BLOCK:(g + 1) * SWA_BLOCK]
                    delta = jnp.sum(p * dpg, axis=-1, keepdims=True)
                    dss.append((p * (dpg - delta)).astype(BF16))
                    dsg = jnp.sum(-sink_p * delta, axis=0, keepdims=True)
                    dsk = dsk + jnp.where(lane == j * SWA_GROUP + g, dsg, 0.0)
                dsst = jnp.concatenate(dss, axis=0)
                pst = jnp.concatenate([p.astype(BF16) for p, _ in ps], axis=0)
                _swa_unstack(_dot(dsst, kw) * (SWA_DH ** -0.5), j, dq_ref)
                dk = _dot(dsst, qst, TN) * (SWA_DH ** -0.5)
                dv = _dot(pst, dost, TN)
                dk = dk + pltpu.roll(dk, SWA_DH, 1)
                dv = dv + pltpu.roll(dv, SWA_DH, 1)
                dk_ref[:, sl] = ck_ref[:, sl] + dk[:SWA_BLOCK]
                dv_ref[:, sl] = cv_ref[:, sl] + dv[:SWA_BLOCK]
                ck_ref[:, sl] = dk[SWA_BLOCK:]
                cv_ref[:, sl] = dv[SWA_BLOCK:]
            dsk_ref[...] += dsk

    f = jax.ShapeDtypeStruct((rows, kvw), F32)
    return pl.pallas_call(
        body, name=name, grid=(nb + 1,),
        in_specs=[pl.BlockSpec(memory_space=pltpu.SMEM), cur, kcur, kprev, kcur, kprev, cur],
        out_specs=[cur, late, late, acc_spec],
        out_shape=[jax.ShapeDtypeStruct((rows, w), F32), f, f, jax.ShapeDtypeStruct((SUBLANES, LANES), F32)],
        scratch_shapes=[pltpu.VMEM((SWA_BLOCK, kvw), F32), pltpu.VMEM((SWA_BLOCK, kvw), F32)],
        compiler_params=_params(("arbitrary",)))(sinks, q, k2, k2, v2, v2, dcat)


def _mem_probs(mq, kbd):
    s = _dot(mq.astype(BF16), kbd) * (MEM_DH ** -0.5)
    ps = []
    for h in range(MEM_HEADS):
        sh = s[:, h * MEM_LEN:(h + 1) * MEM_LEN]
        e = jnp.exp(sh - jnp.max(sh, axis=-1, keepdims=True))
        ps.append(e / jnp.sum(e, axis=-1, keepdims=True))
    return ps


def _mem_fwd(proj, cb, kbd, vbd, name, bm=512):
    def fn(i, mq, kbd, vbd):
        p = jnp.concatenate(_mem_probs(mq, kbd), axis=1)
        return _dot(p.astype(BF16), vbd)
    return _rowcall(fn, name, proj.shape[0], bm, [_row(proj, MEM_W, cb), _full(kbd), _full(vbd)], [(MEM_W, BF16)])[0]


def _mem_bwd(proj, cb, kbd, vbd, dcat, name, bm=512):
    def fn(i, mq, kbd, vbd, do):
        ps = _mem_probs(mq, kbd)
        dp = _dot(do, vbd, NT)
        dss = []
        for h in range(MEM_HEADS):
            dph = dp[:, h * MEM_LEN:(h + 1) * MEM_LEN]
            dss.append(ps[h] * (dph - jnp.sum(ps[h] * dph, axis=-1, keepdims=True)))
        ds = (jnp.concatenate(dss, axis=1) * (MEM_DH ** -0.5)).astype(BF16)
        p = jnp.concatenate(ps, axis=1).astype(BF16)
        return _dot(ds, kbd, NT), _dot(mq.astype(BF16), ds, TN), _dot(p, do, TN)
    return _rowcall(fn, name, proj.shape[0], bm, [_row(proj, MEM_W, cb), _full(kbd), _full(vbd), _row(dcat, MEM_W, 3)],
                    [(MEM_W, BF16)], [(MEM_W, MEM_HEADS * MEM_LEN), (MEM_HEADS * MEM_LEN, MEM_W)])


def _mem_expand(mkv):
    feat_head = jnp.arange(MEM_W) // MEM_DH
    slot_head = jnp.arange(MEM_HEADS * MEM_LEN) // MEM_LEN
    on = feat_head[:, None] == slot_head[None, :]
    kbd = jnp.where(on, jnp.tile(mkv[:, :MEM_W].T, (1, MEM_HEADS)), 0.0)
    vbd = jnp.where(on.T, jnp.tile(mkv[:, MEM_W:], (MEM_HEADS, 1)), 0.0)
    return kbd.astype(BF16), vbd.astype(BF16)


def _mem_collapse(dkbd, dvbd):
    dk = [dkbd[h * MEM_DH:(h + 1) * MEM_DH, h * MEM_LEN:(h + 1) * MEM_LEN].T for h in range(MEM_HEADS)]
    dv = [dvbd[h * MEM_LEN:(h + 1) * MEM_LEN, h * MEM_DH:(h + 1) * MEM_DH] for h in range(MEM_HEADS)]
    return jnp.concatenate(dk + dv, axis=1)


def _adamw_math(w, g, m, v):
    m = ADAM_B1 * m + (1.0 - ADAM_B1) * g
    v = ADAM_B2 * v + (1.0 - ADAM_B2) * (g * g)
    m_hat = m / (1.0 - ADAM_B1 ** ADAM_STEP)
    v_hat = v / (1.0 - ADAM_B2 ** ADAM_STEP)
    return -ADAM_LR * (m_hat / (jnp.sqrt(v_hat) + ADAM_EPS) + ADAM_WD * w), m, v


def _adamw(w, g, m, v, name, bm=512):
    d = w.shape[1]
    return _rowcall(lambda i, *a: _adamw_math(*a), name, w.shape[0], bm, [_row(w), _row(g), _row(m), _row(v)], [(d, F32)] * 3)


def _adamw_halves(w, g_halves, m, v, name):
    layers, rows, n = w.shape
    r = rows // 2
    bm = LANES if r % LANES == 0 else r
    per_half = r // bm
    nat = pl.BlockSpec((None, bm, n), lambda l, c, i: (l, c * per_half + i, 0))
    half = pl.BlockSpec((None, None, bm, n), lambda l, c, i: (c, l, i, 0))

    def body(w_ref, g_ref, m_ref, v_ref, go_ref, d_ref, mo_ref, vo_ref):
        g = g_ref[...]
        go_ref[...] = g
        d_ref[...], mo_ref[...], vo_ref[...] = _adamw_math(w_ref[...], g, m_ref[...], v_ref[...])

    return pl.pallas_call(
        body, name=name, grid=(layers, 2, per_half), in_specs=[nat, half, nat, nat], out_specs=[nat] * 4,
        out_shape=[jax.ShapeDtypeStruct(w.shape, F32)] * 4,
        compiler_params=_params(("parallel", "parallel", "parallel")))(w, g_halves, m, v)


def _sum_slots(buf, name, bm=128):
    n, rows, w = buf.shape
    bm = min(bm, rows)
    assert rows % bm == 0

    def body(b_ref, o_ref):
        acc = b_ref[0].astype(F32)
        for s in range(1, n):
            acc = acc + b_ref[s].astype(F32)
        o_ref[...] = acc

    return pl.pallas_call(
        body, name=name, grid=(rows // bm,), in_specs=[pl.BlockSpec((n, bm, w), lambda i: (0, i, 0))],
        out_specs=pl.BlockSpec((bm, w), lambda i: (i, 0)), out_shape=jax.ShapeDtypeStruct((rows, w), F32),
        compiler_params=_params(("parallel",)))(buf)


def _exchange(srcs, same, masks, name):
    slots = N_DEV if len(masks) == N_DEV - 1 else 2
    n_arr, n_peer = len(srcs), len(masks)
    shapes = [s.shape if sm else s.shape[1:] for s, sm in zip(srcs, same)]

    def body(*refs):
        src_refs, out_refs = refs[:n_arr], refs[n_arr:2 * n_arr]
        send_sems, recv_sems, local_sems = refs[2 * n_arr:]
        x, y, c = lax.axis_index("x"), lax.axis_index("y"), lax.axis_index("c")
        me = 4 * x + 2 * y + c

        def flip(v, bit):
            return 1 - v if bit else v

        def slot_of(dev):
            return dev if slots == N_DEV else dev % 2

        def piece(a, p):
            return src_refs[a] if same[a] else src_refs[a].at[p]

        local = [pltpu.make_async_copy(piece(a, me), out_refs[a].at[slot_of(me)], local_sems.at[a]) for a in range(n_arr)]
        for cp in local:
            cp.start()
        copies = []
        for idx, k in enumerate(masks):
            peer = (flip(x, k & 4), flip(y, k & 2), flip(c, k & 1))
            peer_id = 4 * peer[0] + 2 * peer[1] + peer[2]
            for a in range(n_arr):
                sem = idx * n_arr + a
                cp = pltpu.make_async_remote_copy(
                    src_ref=piece(a, peer_id), dst_ref=out_refs[a].at[slot_of(me)],
                    send_sem=send_sems.at[sem], recv_sem=recv_sems.at[sem], device_id=peer, device_id_type=MESH)
                cp.start()
                copies.append((cp, pltpu.make_async_remote_copy(
                    src_ref=piece(a, peer_id), dst_ref=out_refs[a].at[slot_of(peer_id)],
                    send_sem=send_sems.at[sem], recv_sem=recv_sems.at[sem], device_id=peer, device_id_type=MESH)))
        for cp, landing in copies:
            cp.wait_send()
            landing.wait_recv()
        for cp in local:
            cp.wait()

    any_spec = pl.BlockSpec(memory_space=pl.ANY)
    n_sem = n_arr * n_peer
    return pl.pallas_call(
        body, name=name, in_specs=[any_spec] * n_arr, out_specs=[any_spec] * n_arr,
        out_shape=[jax.ShapeDtypeStruct((slots,) + tuple(sh), s.dtype) for sh, s in zip(shapes, srcs)],
        scratch_shapes=[pltpu.SemaphoreType.DMA((n_sem,)), pltpu.SemaphoreType.DMA((n_sem,)),
                        pltpu.SemaphoreType.DMA((n_arr,))],
        )(*srcs)


ALL_PEERS = tuple(range(1, N_DEV))
SIBLING = (1,)


def _pack(arrays, rows):
    flat = jnp.concatenate([a.reshape(-1) for a in arrays])
    return jnp.pad(flat, (0, rows * D_MODEL - flat.shape[0])).reshape(rows, D_MODEL)


def _unpack(buf, shapes):
    flat = buf.reshape(-1)
    out, off = [], 0
    for s in shapes:
        n = math.prod(s)
        out.append(flat[off:off + n].reshape(s))
        off += n
    return out


def _rows_for(shapes, mult):
    n = sum(math.prod(s) for s in shapes)
    rows = -(-n // D_MODEL)
    return -(-rows // mult) * mult


SHARD_AXIS = dict(w_mem_kv=1, w_out=1, w_gate_up=2, w_down=1, gdn_w_in=2, swa_w_q=1, w_kv=0, gdn_conv=2)
HALF_AXIS = dict(w_mem_kv=1, w_out=1, w_gate_up=1, w_down=1, gdn_w_in=1, swa_w_q=1, w_kv=0)


def _my_half(shard, name, c):
    ax = HALF_AXIS[name]
    h = shard.shape[ax] // 2
    return lax.dynamic_slice_in_dim(shard, c * h, h, axis=ax)


def _piece_layout(name, half_shape):
    dims, pos = [], {}
    for i, d in enumerate(half_shape):
        if i == SHARD_AXIS[name]:
            pos["chip"] = len(dims)
            dims.append(4)
        if i == HALF_AXIS[name]:
            pos["core"] = len(dims)
            dims.append(2)
        pos[i] = len(dims)
        dims.append(d)
    return dims, [pos["chip"], pos["core"]] + [pos[i] for i in range(len(half_shape))]


def _full_shape(name, half_shape):
    return tuple(d * (4 if i == SHARD_AXIS[name] else 1) * (2 if i == HALF_AXIS[name] else 1)
                 for i, d in enumerate(half_shape))


def _assemble(pieces, name):
    half_shape = pieces.shape[1:]
    if half_shape[-1] % LANES:
        chips = [jnp.concatenate([pieces[2 * s], pieces[2 * s + 1]], axis=HALF_AXIS[name]) for s in range(4)]
        return jnp.concatenate(chips, axis=SHARD_AXIS[name])
    dims, perm = _piece_layout(name, half_shape)
    inverse = [perm.index(i) for i in range(len(perm))]
    return pieces.reshape((4, 2) + half_shape).transpose(inverse).reshape(_full_shape(name, half_shape))


def _to_pieces(full, name, half_shape):
    if half_shape[-1] % LANES:
        ns, nh = half_shape[SHARD_AXIS[name]], half_shape[HALF_AXIS[name]]
        return jnp.stack([lax.slice_in_dim(lax.slice_in_dim(full, s * ns, (s + 1) * ns, axis=SHARD_AXIS[name]),
                                           c * nh, (c + 1) * nh, axis=HALF_AXIS[name])
                          for s in range(4) for c in range(2)])
    dims, perm = _piece_layout(name, half_shape)
    return full.reshape(dims).transpose(perm).reshape((N_DEV,) + tuple(half_shape))


def _from_halves(halves, name):
    ax = HALF_AXIS[name]
    s = jnp.moveaxis(halves, 0, ax)
    return s.reshape(s.shape[:ax] + (2 * s.shape[ax + 1],) + s.shape[ax + 2:])


def _rope_tables(positions):
    half = ROT_DIM // 2
    inv = ROPE_THETA ** (-jnp.arange(0, ROT_DIM, 2, dtype=F32) / ROT_DIM)
    ang = positions.astype(F32)[:, None] * inv
    cos, sin = jnp.cos(ang), jnp.sin(ang)
    rows = positions.shape[0]
    one = jnp.ones((rows, SWA_DH - ROT_DIM), F32)
    zero = jnp.zeros((rows, SWA_DH - ROT_DIM), F32)
    zh = jnp.zeros((rows, half), F32)
    c64 = jnp.concatenate([cos, cos, one], axis=1)
    a64 = jnp.concatenate([-sin, zh, zero], axis=1)
    b64 = jnp.concatenate([zh, sin, zero], axis=1)
    return tuple(jnp.concatenate([t, t], axis=1) for t in (c64, a64, b64))


def _pair_heads(t):
    return jnp.concatenate([t[:, :SWA_DH], t[:, :SWA_DH], t[:, SWA_DH:], t[:, SWA_DH:]], axis=1)


def _unpair_heads(t):
    return jnp.concatenate([t[:, :SWA_DH], t[:, LANES:LANES + SWA_DH]], axis=1)


def _gdn_in_pad(w):
    o2 = 4 * GDN_W
    pad = jnp.zeros(w.shape[:-1] + (GDN_IN_PAD - GDN_IN,), w.dtype)
    return jnp.concatenate([w[..., :o2], w[..., o2 + 2 * GDN_HEADS:], w[..., o2:o2 + 2 * GDN_HEADS], pad], axis=-1)


def _gdn_in_unpad(w):
    o2 = 4 * GDN_W
    return jnp.concatenate([w[..., :o2], w[..., o2 + MEM_W:o2 + MEM_W + 2 * GDN_HEADS], w[..., o2:o2 + MEM_W]], axis=-1)


def _head_rows(v):
    return jnp.repeat(v.astype(F32), LANES)[None, :]


def _selectors():
    lane = jnp.arange(LANES)[:, None]
    head = (jnp.arange(GDN_W) // LANES)[None, :]
    return (lane == head).astype(F32), (lane == head + GDN_HEADS).astype(F32)


def _local_step(x, mem, positions, target, w):
    rows = x.shape[0]
    tabs = _rope_tables(positions)
    sel_b, sel_a = _selectors()
    row2 = lambda v: v.reshape(1, -1).astype(F32)

    w_gu = _ff_interleave(w["w_gate_up"])
    mem_n = _rms_fwd(mem, row2(w["ln_mem"]), "mem_norm")
    saved = []
    kt = vt = None
    for l in range(DEPTH):
        s = dict(x0=x)
        h = _rms_fwd(x, row2(w["ln_mix"][l]), f"norm_mix{l}")
        mkv = _mm(mem_n, w["w_mem_kv"][l], "nn", f"mem_kv{l}")
        kbd, vbd = _mem_expand(mkv)
        if l < N_A:
            proj = _mm(h, w["gdn_w_in"][l], "nn", f"gdn_in{l}")
            a_row, dt_row = _head_rows(w["gdn_A_log"][l]), _head_rows(w["gdn_dt_bias"][l])
            q, k, v, gc, beta = _gdn_pre_fwd(proj, w["gdn_conv"][l], sel_b, sel_a, a_row, dt_row, f"gdn_pre{l}")
            o, states, tinv, gw, vn = _gdn_fwd(q, k, v, gc, beta, f"gdn_scan{l}")
            mix = _gdn_post_fwd(o, proj, row2(w["gdn_norm"][l]), f"gdn_post{l}")
            mq_cb = (3 * GDN_W + GDN_W) // MEM_W
            s.update(q=q, k=k, v=v, gc=gc, beta=beta, o=o, states=states, tinv=tinv, gw=gw, vn=vn,
                     a_row=a_row, dt_row=dt_row)
        else:
            proj = _mm(h, w["swa_w_q"][l - N_A], "nn", f"swa_in{l}")
            qr = _rope(proj, SWA_HEADS * SWA_DH, 0, tabs, 1, f"rope_q{l}", BF16)
            mix = _swa_fwd(qr, kt, vt, w["swa_sinks"][l - N_A], f"swa{l}")
            mq_cb = (SWA_HEADS * SWA_DH) // MEM_W
            s.update(qr=qr)
        mem_o = _mem_fwd(proj, mq_cb, kbd, vbd, f"mem_attn{l}")
        cat = jnp.concatenate([mix, mem_o], axis=1)
        x1 = _mm(cat, w["w_out"][l], "nn", f"out_proj{l}", add=x)
        h2 = _rms_fwd(x1, row2(w["ln_ffn"][l]), f"norm_ffn{l}")
        gu, act = _gate_up_fwd(h2, w_gu[l], f"gate_up{l}")
        x = _mm(act, w["w_down"][l], "nn", f"down{l}", add=x1)
        s.update(h=h, proj=proj, kbd=kbd, vbd=vbd, mq_cb=mq_cb, cat=cat, x1=x1, h2=h2, gu=gu, act=act)
        saved.append(s)
        if l == N_A - 1:
            x_kv = x
            h_kv = _rms_fwd(x, row2(w["ln_kv"]), "norm_kv")
            kv = _mm(h_kv, w["w_kv"], "nn", "kv_proj")
            kr = _rope(kv, LANES, 0, tabs, 1, "rope_k", F32)
            kt = _pair_heads(kr).astype(BF16)
            vt = _pair_heads(kv[:, LANES:]).astype(BF16)

    gr = {}
    dx, dxb, loss_part, dlnf = _final_loss(x, row2(w["ln_final"]), target, "final_loss")
    gr["ln_final"] = dlnf.sum(axis=0)
    dln_mix, dln_ffn = [None] * DEPTH, [None] * DEPTH
    dw_mem_kv, dw_out, dw_gu, dw_dn = [None] * DEPTH, [None] * DEPTH, [None] * DEPTH, [None] * DEPTH
    dgdn_in, dgdn_conv, dgdn_a, dgdn_dt, dgdn_norm = [None] * N_A, [None] * N_A, [None] * N_A, [None] * N_A, [None] * N_A
    dswa_q, dswa_sinks = [None] * N_B, [None] * N_B
    dmem_n = None
    dkt = dvt = None
    for l in reversed(range(DEPTH)):
        s = saved[l]
        if l == N_A - 1:
            dkr = _unpair_heads(dkt)
            dk = _rope(dkr, LANES, 0, tabs, -1, "rope_k_bwd", BF16)
            dkv = jnp.concatenate([dk, _unpair_heads(dvt).astype(BF16)], axis=1)
            dh_kv = _mm(dkv, w["w_kv"], "nt", "kv_proj_dx")
            gr["w_kv"] = _mm(h_kv, dkv, "tn", "kv_proj_dw", BF16)
            dx, dxb, dg = _rms_bwd(x_kv, row2(w["ln_kv"]), dh_kv, dx, "norm_kv_bwd")
            gr["ln_kv"] = dg.sum(axis=0)
        dgu = _down_bwd(dxb, w["w_down"][l], s["gu"], f"down_dx{l}")
        dw_dn[l] = _mm(s["act"], dxb, "tn", f"down_dw{l}", BF16)
        dh2 = _mm(dgu, w_gu[l], "nt", f"gate_up_dx{l}")
        dw_gu[l] = _mm(s["h2"], dgu, "tn", f"gate_up_dw{l}", BF16)
        dx, dxb, dg = _rms_bwd(s["x1"], row2(w["ln_ffn"][l]), dh2, dx, f"norm_ffn_bwd{l}")
        dln_ffn[l] = dg.sum(axis=0)
        dcat = _mm(dxb, w["w_out"][l], "nt", f"out_proj_dx{l}", BF16)
        dw_out[l] = _mm(s["cat"], dxb, "tn", f"out_proj_dw{l}", BF16)
        dmq, dkbd, dvbd = _mem_bwd(s["proj"], s["mq_cb"], s["kbd"], s["vbd"], dcat, f"mem_attn_bwd{l}")
        dmkv = _mem_collapse(dkbd, dvbd).astype(BF16)
        dw_mem_kv[l] = _mm(mem_n, dmkv, "tn", f"mem_kv_dw{l}", BF16)
        dmem_n = _mm(dmkv, w["w_mem_kv"][l], "nt", f"mem_kv_dx{l}", add=dmem_n)
        if l < N_A:
            do, dz, dng = _gdn_post_bwd(s["o"], s["proj"], row2(w["gdn_norm"][l]), dcat, f"gdn_post_bwd{l}")
            dq, dk, dv, dg_, dbeta = _gdn_bwd(s["q"], s["k"], s["v"], s["gc"], s["beta"], s["states"], s["tinv"], s["gw"],
                                              s["vn"], do, f"gdn_scan_bwd{l}")
            res = _gdn_pre_bwd(s["proj"], w["gdn_conv"][l], sel_b, sel_a, s["a_row"], s["dt_row"], dq, dk, dv, dg_, dbeta,
                               f"gdn_pre_bwd{l}")
            dconv, dba = res[0], res[1]
            dgdn_conv[l] = jnp.stack([r.sum(axis=0) for r in res[2:2 + CONV_K]])
            dgdn_a[l] = res[2 + CONV_K].sum(axis=0)[::LANES]
            dgdn_dt[l] = res[3 + CONV_K].sum(axis=0)[::LANES]
            dgdn_norm[l] = dng.sum(axis=0)
            dqkv = _conv_bwd_input(dconv, w["gdn_conv"][l], f"gdn_conv_bwd{l}")
            dproj = jnp.concatenate([dqkv, dz, dmq, dba], axis=1)
            dh = _mm(dproj, w["gdn_w_in"][l], "nt", f"gdn_in_dx{l}")
            dgdn_in[l] = _mm(s["h"], dproj, "tn", f"gdn_in_dw{l}", BF16)
        else:
            b = l - N_A
            dqr, dkt_l, dvt_l, dsk = _swa_bwd(s["qr"], kt, vt, w["swa_sinks"][b], dcat, f"swa_bwd{l}")
            dkt = dkt_l if dkt is None else dkt + dkt_l
            dvt = dvt_l if dvt is None else dvt + dvt_l
            dswa_sinks[b] = dsk[0, :SWA_HEADS]
            dq = _rope(dqr, SWA_HEADS * SWA_DH, 0, tabs, -1, f"rope_q_bwd{l}", BF16)
            dproj = jnp.concatenate([dq, dmq], axis=1)
            dh = _mm(dproj, w["swa_w_q"][b], "nt", f"swa_in_dx{l}")
            dswa_q[b] = _mm(s["h"], dproj, "tn", f"swa_in_dw{l}", BF16)
        dx, dxb, dg = _rms_bwd(s["x0"], row2(w["ln_mix"][l]), dh, dx, f"norm_mix_bwd{l}")
        dln_mix[l] = dg.sum(axis=0)
    _, _, dg = _rms_bwd(mem, row2(w["ln_mem"]), dmem_n, None, "mem_norm_bwd")
    gr["ln_mem"] = dg.sum(axis=0)
    gr.update(ln_mix=jnp.stack(dln_mix), ln_ffn=jnp.stack(dln_ffn), w_mem_kv=jnp.stack(dw_mem_kv), w_out=jnp.stack(dw_out),
              w_gate_up=_ff_deinterleave(jnp.stack(dw_gu)), w_down=jnp.stack(dw_dn), gdn_w_in=jnp.stack(dgdn_in), gdn_conv=jnp.stack(dgdn_conv),
              gdn_A_log=jnp.stack(dgdn_a), gdn_dt_bias=jnp.stack(dgdn_dt), gdn_norm=jnp.stack(dgdn_norm),
              swa_w_q=jnp.stack(dswa_q), swa_sinks=jnp.stack(dswa_sinks))
    return loss_part, dx, gr


def kernel(x, mem, positions, ln_mix, ln_ffn, ln_mem, w_mem_kv, w_out, w_gate_up, w_down, gdn_w_in, gdn_conv, gdn_A_log, gdn_dt_bias, gdn_norm, swa_w_q, swa_sinks, ln_kv, w_kv, ln_final, loss_target, m_ln_mix, m_ln_ffn, m_ln_mem, m_w_mem_kv, m_w_out, m_w_gate_up, m_w_down, m_gdn_w_in, m_gdn_conv, m_gdn_A_log, m_gdn_dt_bias, m_gdn_norm, m_swa_w_q, m_swa_sinks, m_ln_kv, m_w_kv, m_ln_final, v_ln_mix, v_ln_ffn, v_ln_mem, v_w_mem_kv, v_w_out, v_w_gate_up, v_w_down, v_gdn_w_in, v_gdn_conv, v_gdn_A_log, v_gdn_dt_bias, v_gdn_norm, v_swa_w_q, v_swa_sinks, v_ln_kv, v_w_kv, v_ln_final):
    given = dict(locals())
    wts = {n: given[n] for n in WEIGHTS}
    c = lax.axis_index("c")

    halves = [_my_half(wts[n].astype(BF16), n, c) for n in SHARDED]
    half_shapes = [h.shape for h in halves]
    conv_shape = wts["gdn_conv"].shape
    cpack = _pack([wts["gdn_conv"]], 16)
    conv_half = lax.dynamic_slice_in_dim(cpack, c * SUBLANES, SUBLANES, axis=0)
    got = _exchange(halves + [conv_half], [True] * (len(halves) + 1), ALL_PEERS, "gather_weights")
    full = {n: wts[n] for n in SMALL}
    for n, g in zip(SHARDED, got):
        full[n] = _assemble(g, n)
    conv_all = got[-1].reshape(4, 16, D_MODEL)
    full["gdn_conv"] = jnp.concatenate([_unpack(conv_all[s], [conv_shape])[0] for s in range(4)], axis=2)
    full["gdn_w_in"] = _gdn_in_pad(full["gdn_w_in"])

    loss_part, dx, gr = _local_step(x[0], mem[0], positions[0], loss_target[0], full)
    gr["gdn_w_in"] = _gdn_in_unpad(gr["gdn_w_in"])

    pieces = [_to_pieces(gr[n].astype(BF16), n, hs) for n, hs in zip(SHARDED, half_shapes)]
    small_shapes = [wts[n].shape for n in SMALL] + [conv_shape[:2] + (4 * conv_shape[2],), (SUBLANES, LANES)]
    rows_s = _rows_for(small_shapes, SUBLANES)
    spack = _pack([gr[n] for n in SMALL] + [gr["gdn_conv"], loss_part], rows_s)
    parts = _exchange(pieces + [spack], [False] * len(pieces) + [True], ALL_PEERS, "scatter_grads")
    mine = [_sum_slots(p.reshape(N_DEV, -1, p.shape[-1]), f"sum_{n}").reshape(hs)
            for n, p, hs in zip(SHARDED, parts, half_shapes)]
    ssum = _unpack(_sum_slots(parts[-1], "sum_small"), small_shapes)
    both = _exchange(mine, [True] * len(mine), SIBLING, "swap_grad_halves")
    g_all = dict(zip(SMALL, ssum[:len(SMALL)]))
    chip = 2 * lax.axis_index("x") + lax.axis_index("y")
    g_all["gdn_conv"] = lax.dynamic_slice_in_dim(ssum[len(SMALL)], chip * conv_shape[2], conv_shape[2], axis=2)
    loss = jnp.sum(ssum[-1])

    out = dict(grad=g_all, delta={}, new_m={}, new_v={})
    for n, b in zip(SHARDED, both):
        as3d = lambda a: a.reshape((-1,) + a.shape[-2:])
        res = _adamw_halves(as3d(wts[n]), b.reshape((2, -1) + b.shape[-2:]), as3d(given["m_" + n]), as3d(given["v_" + n]),
                            f"adamw_{n}")
        for kind, r in zip(("grad", "delta", "new_m", "new_v"), res):
            out[kind][n] = r.reshape(wts[n].shape)
    small_names = SMALL + ("gdn_conv",)
    small_w_shapes = [wts[n].shape for n in small_names]
    rows_a = _rows_for(small_w_shapes, SUBLANES)
    res = _adamw(_pack([wts[n] for n in small_names], rows_a), _pack([g_all[n] for n in small_names], rows_a),
                 _pack([given["m_" + n] for n in small_names], rows_a),
                 _pack([given["v_" + n] for n in small_names], rows_a), "adamw_small")
    for kind, r in zip(("delta", "new_m", "new_v"), res):
        out[kind].update(zip(small_names, _unpack(r, small_w_shapes)))
    return (loss, dx[None], *[out["grad"][n] for n in WEIGHTS], *[out["delta"][n] for n in WEIGHTS],
            *[out["new_m"][n] for n in WEIGHTS], *[out["new_v"][n] for n in WEIGHTS])
```

```python
import functools
import math

import jax
import jax.numpy as jnp
from jax import lax
from jax.experimental import pallas as pl
from jax.experimental.pallas import tpu as pltpu

F32 = jnp.float32
BF16 = jnp.bfloat16
HI = lax.Precision.HIGHEST
MESH = pl.DeviceIdType.MESH

D_MODEL = 1024
DEPTH = 4
N_A = 2
N_B = 2
EPS = 1e-6
GDN_HEADS = 6
GDN_DK = 128
GDN_W = 768
CONV_K = 4
CHUNK = 64
SWA_HEADS = 12
SWA_KV_HEADS = 2
SWA_DH = 64
SWA_GROUP = 6
SWA_GW = SWA_GROUP * SWA_DH
SWA_BLOCK = 128
ROPE_THETA = 500000.0
ROT_DIM = 16
MEM_LEN = 256
MEM_HEADS = 4
MEM_DH = 64
MEM_W = 256
D_FF = 2816
GDN_IN = 3340
GDN_IN_PAD = 3456
ADAM_LR = 0.001
ADAM_B1 = 0.9
ADAM_B2 = 0.999
ADAM_EPS = 1e-08
ADAM_WD = 0.01
ADAM_STEP = 10

N_DEV = 8
LANES = 128
SUBLANES = 8
V7X_VMEM_LIMIT = 56 * 2**20
MM_VMEM_BUDGET = 44 * 2**20

SHARDED = ("w_mem_kv", "w_out", "w_gate_up", "w_down", "gdn_w_in", "swa_w_q", "w_kv")
SMALL = ("ln_mix", "ln_ffn", "ln_mem", "gdn_A_log", "gdn_dt_bias", "gdn_norm", "swa_sinks", "ln_kv", "ln_final")
WEIGHTS = ("ln_mix", "ln_ffn", "ln_mem", "w_mem_kv", "w_out", "w_gate_up", "w_down", "gdn_w_in", "gdn_conv",
           "gdn_A_log", "gdn_dt_bias", "gdn_norm", "swa_w_q", "swa_sinks", "ln_kv", "w_kv", "ln_final")


def _params(sem=None, **kw):
    return pltpu.CompilerParams(dimension_semantics=sem, vmem_limit_bytes=V7X_VMEM_LIMIT, **kw)


def _dot(a, b, dims=(((1,), (0,)), ((), ())), precision=None):
    return lax.dot_general(a, b, dims, precision=precision, preferred_element_type=F32)


NT = (((1,), (1,)), ((), ()))
TN = (((0,), (0,)), ((), ()))


def _fold8(v):
    r, w = v.shape
    return v.reshape(r // SUBLANES, SUBLANES, w).sum(axis=0)


def _row(a, w=None, cb=0):
    return ("row", a, a.shape[1] if w is None else w, cb)


def _full(a):
    return ("full", a, None, None)


def _prev8(a, w, cb=0):
    return ("prev8", a, w, cb)


def _next8(a, w, cb=0):
    return ("next8", a, w, cb)


def _rowcall(fn, name, rows, bm, ins, outs, accs=()):
    bm = min(bm, rows)
    assert rows % bm == 0 and bm % SUBLANES == 0
    steps = rows // bm
    r8 = bm // SUBLANES
    in_specs, arrays = [], []
    for kind, a, w, cb in ins:
        arrays.append(a)
        if kind == "row":
            in_specs.append(pl.BlockSpec((bm, w), lambda i, cb=cb: (i, cb)))
        elif kind == "full":
            in_specs.append(pl.BlockSpec(a.shape, lambda i, nd=a.ndim: (0,) * nd))
        elif kind == "prev8":
            in_specs.append(pl.BlockSpec((SUBLANES, w), lambda i, cb=cb: (jnp.maximum(i * r8 - 1, 0), cb)))
        else:
            last = rows // SUBLANES - 1
            in_specs.append(pl.BlockSpec((SUBLANES, w), lambda i, cb=cb: (jnp.minimum((i + 1) * r8, last), cb)))
    out_shape = [jax.ShapeDtypeStruct((rows, w), dt) for w, dt in outs]
    out_specs = [pl.BlockSpec((bm, w), lambda i: (i, 0)) for w, _ in outs]
    out_shape += [jax.ShapeDtypeStruct(s, F32) for s in accs]
    out_specs += [pl.BlockSpec(s, lambda i: (0, 0)) for s in accs]
    n_in, n_out = len(ins), len(outs)

    def body(*refs):
        i = pl.program_id(0)
        res = fn(i, *[r[...] for r in refs[:n_in]])
        if not isinstance(res, (tuple, list)):
            res = (res,)
        for r, v in zip(refs[n_in:n_in + n_out], res[:n_out]):
            r[...] = v.astype(r.dtype)
        if accs:
            @pl.when(i == 0)
            def _():
                for r in refs[n_in + n_out:]:
                    r[...] = jnp.zeros(r.shape, F32)
            for r, v in zip(refs[n_in + n_out:], res[n_out:]):
                r[...] += v

    res = pl.pallas_call(
        body, name=name, grid=(steps,), in_specs=in_specs, out_specs=out_specs, out_shape=out_shape,
        compiler_params=_params(("arbitrary",)))(*arrays)
    return res


def _tile(n, cap):
    for t in (1408, 1152, 1024, 896, 768, 640, 512, 384, 256, 128):
        if t <= cap and n % t == 0:
            return t
    return n


def _mm(a, b, mode, name, out_dtype=F32, add=None):
    if mode == "tn":
        s, m = a.shape
        n = b.shape[1]
        bm, bn, bk = _tile(m, 1408), _tile(n, 1408), min(s, 1024)
        nk = s // bk

        def body(a_ref, b_ref, o_ref, acc_ref):
            k = pl.program_id(2)

            @pl.when(k == 0)
            def _():
                acc_ref[...] = jnp.zeros(acc_ref.shape, F32)
            acc_ref[...] += _dot(a_ref[...].astype(BF16), b_ref[...].astype(BF16), TN)

            @pl.when(k == nk - 1)
            def _():
                o_ref[...] = acc_ref[...].astype(o_ref.dtype)

        return pl.pallas_call(
            body, name=name, grid=(m // bm, n // bn, nk),
            in_specs=[pl.BlockSpec((bk, bm), lambda i, j, k: (k, i)), pl.BlockSpec((bk, bn), lambda i, j, k: (k, j))],
            out_specs=pl.BlockSpec((bm, bn), lambda i, j, k: (i, j)),
            out_shape=jax.ShapeDtypeStruct((m, n), out_dtype),
            scratch_shapes=[pltpu.VMEM((bm, bn), F32)],
            compiler_params=_params(("parallel", "parallel", "arbitrary")))(a, b)

    m, k = a.shape
    n = b.shape[1] if mode == "nn" else b.shape[0]
    out_bytes = jnp.dtype(out_dtype).itemsize

    def vmem_need(bm, bn):
        need = 2 * bm * k * a.dtype.itemsize + 2 * bn * k * b.dtype.itemsize + bm * bn * (2 * out_bytes + 4)
        return need + (2 * bm * bn * 4 if add is not None else 0)

    bm, bn = min(m, 512), _tile(n, 512)
    for cand in ((2048, 1408), (2048, 1024), (2048, 512), (1024, 1408), (1024, 1024), (1024, 512)):
        tm, tn = min(m, cand[0]), _tile(n, cand[1])
        if m % tm == 0 and vmem_need(tm, tn) <= MM_VMEM_BUDGET:
            bm, bn = tm, tn
            break
    dims = NT if mode == "nt" else (((1,), (0,)), ((), ()))
    b_spec = (pl.BlockSpec((k, bn), lambda i, j: (0, j)) if mode == "nn" else pl.BlockSpec((bn, k), lambda i, j: (j, 0)))
    in_specs = [pl.BlockSpec((bm, k), lambda i, j: (i, 0)), b_spec]
    args = [a, b]
    if add is not None:
        in_specs.append(pl.BlockSpec((bm, bn), lambda i, j: (i, j)))
        args.append(add)

    def body(a_ref, b_ref, *rest):
        o_ref = rest[-1]
        acc = _dot(a_ref[...].astype(BF16), b_ref[...].astype(BF16), dims)
        if add is not None:
            acc = acc + rest[0][...]
        o_ref[...] = acc.astype(o_ref.dtype)

    return pl.pallas_call(
        body, name=name, grid=(m // bm, n // bn), in_specs=in_specs,
        out_specs=pl.BlockSpec((bm, bn), lambda i, j: (i, j)),
        out_shape=jax.ShapeDtypeStruct((m, n), out_dtype),
        compiler_params=_params(("parallel", "parallel")))(*args)


def _sigmoid(x):
    return 0.5 * jnp.tanh(0.5 * x) + 0.5


def _softplus(x):
    return jnp.maximum(x, 0.0) + jnp.log(1.0 + jnp.exp(-jnp.abs(x)))


def _silu_and_grad(x):
    s = _sigmoid(x)
    return x * s, s * (1.0 + x * (1.0 - s))


def _rms_stats(x):
    r = lax.rsqrt(jnp.mean(x * x, axis=-1, keepdims=True) + EPS)
    return r, x * r


def _rms_fwd(x, g, name, out_dtype=BF16, bm=512):
    def fn(i, x, g):
        _, xn = _rms_stats(x)
        return xn * g
    return _rowcall(fn, name, x.shape[0], bm, [_row(x), _full(g)], [(x.shape[1], out_dtype)])[0]


def _rms_bwd_math(x, g, dy):
    r, xn = _rms_stats(x)
    dxn = dy * g
    dx = r * (dxn - xn * jnp.mean(dxn * xn, axis=-1, keepdims=True))
    return dx, dy * xn


def _rms_bwd(x, g, dy, res, name, bm=256):
    d = x.shape[1]

    def fn(i, x, g, dy, *res_):
        dx, dg = _rms_bwd_math(x, g, dy.astype(F32))
        if res_:
            dx = dx + res_[0]
        return dx, dx, _fold8(dg)
    ins = [_row(x), _full(g), _row(dy)] + ([_row(res)] if res is not None else [])
    return _rowcall(fn, name, x.shape[0], bm, ins, [(d, F32), (d, BF16)], [(SUBLANES, d)])


def _final_loss(x, g, target, name, bm=256):
    d = x.shape[1]

    def fn(i, x, g, t):
        r, xn = _rms_stats(x)
        err = xn * g - t
        dy = err * (1.0 / d)
        dxn = dy * g
        dx = r * (dxn - xn * jnp.mean(dxn * xn, axis=-1, keepdims=True))
        e2 = _fold8(err * err)
        lp = e2[:, 0:LANES]
        for c in range(1, d // LANES):
            lp = lp + e2[:, c * LANES:(c + 1) * LANES]
        return dx, dx, lp * (0.5 / d), _fold8(dy * xn)
    return _rowcall(fn, name, x.shape[0], bm, [_row(x), _full(g), _row(target)], [(d, F32), (d, BF16)],
                    [(SUBLANES, LANES), (SUBLANES, d)])


FF_TILE = 256


def _ff_interleave(w):
    lead = w.shape[:-1]
    w = w.reshape(lead + (2, D_FF // FF_TILE, FF_TILE))
    return jnp.swapaxes(w, -3, -2).reshape(lead + (2 * D_FF,))


def _ff_deinterleave(w):
    lead = w.shape[:-1]
    w = w.reshape(lead + (D_FF // FF_TILE, 2, FF_TILE))
    return jnp.swapaxes(w, -3, -2).reshape(lead + (2 * D_FF,))


def _gate_up_fwd(h, w_gu, name, bm=2048):
    rows, k = h.shape
    bm = min(bm, rows)

    def body(a_ref, b_ref, gu_ref, act_ref):
        acc = _dot(a_ref[...], b_ref[...])
        gu_ref[...] = acc.astype(gu_ref.dtype)
        act_ref[...] = (_silu_and_grad(acc[:, :FF_TILE])[0] * acc[:, FF_TILE:]).astype(act_ref.dtype)

    return pl.pallas_call(
        body, name=name, grid=(rows // bm, D_FF // FF_TILE),
        in_specs=[pl.BlockSpec((bm, k), lambda i, j: (i, 0)), pl.BlockSpec((k, 2 * FF_TILE), lambda i, j: (0, j))],
        out_specs=[pl.BlockSpec((bm, 2 * FF_TILE), lambda i, j: (i, j)), pl.BlockSpec((bm, FF_TILE), lambda i, j: (i, j))],
        out_shape=[jax.ShapeDtypeStruct((rows, 2 * D_FF), BF16), jax.ShapeDtypeStruct((rows, D_FF), BF16)],
        compiler_params=_params(("parallel", "parallel")))(h, w_gu)


def _down_bwd(dx, w_down, gu, name, bm=2048):
    rows, k = dx.shape
    bm = min(bm, rows)

    def body(a_ref, b_ref, gu_ref, o_ref):
        da = _dot(a_ref[...], b_ref[...], NT)
        gu = gu_ref[...].astype(F32)
        s, ds = _silu_and_grad(gu[:, :FF_TILE])
        o_ref[:, :FF_TILE] = (da * gu[:, FF_TILE:] * ds).astype(o_ref.dtype)
        o_ref[:, FF_TILE:] = (da * s).astype(o_ref.dtype)

    return pl.pallas_call(
        body, name=name, grid=(rows // bm, D_FF // FF_TILE),
        in_specs=[pl.BlockSpec((bm, k), lambda i, j: (i, 0)), pl.BlockSpec((FF_TILE, k), lambda i, j: (j, 0)),
                  pl.BlockSpec((bm, 2 * FF_TILE), lambda i, j: (i, j))],
        out_specs=pl.BlockSpec((bm, 2 * FF_TILE), lambda i, j: (i, j)),
        out_shape=jax.ShapeDtypeStruct((rows, 2 * D_FF), BF16),
        compiler_params=_params(("parallel", "parallel")))(dx, w_down, gu)


def _rope_apply(x, tabs, sign):
    cos, ta, tb = tabs
    outs = []
    for c in range(x.shape[1] // LANES):
        xc = x[:, c * LANES:(c + 1) * LANES]
        if sign > 0:
            o = xc * cos + pltpu.roll(xc, LANES - 8, 1) * ta + pltpu.roll(xc, 8, 1) * tb
        else:
            o = xc * cos + pltpu.roll(xc * ta, 8, 1) + pltpu.roll(xc * tb, LANES - 8, 1)
        outs.append(o)
    return outs[0] if len(outs) == 1 else jnp.concatenate(outs, axis=1)


def _rope(x, w, cb, tabs, sign, name, out_dtype, bm=512):
    def fn(i, x, c, a, b):
        return _rope_apply(x.astype(F32), (c, a, b), sign)
    return _rowcall(fn, name, x.shape[0], bm, [_row(x, w, cb)] + [_row(t) for t in tabs], [(w, out_dtype)])[0]


def _shift_down(x, prev8, s, first):
    xs = pltpu.roll(x, s, 0)
    rp = pltpu.roll(prev8, s, 0) * jnp.where(first, 0.0, 1.0)
    rid = lax.broadcasted_iota(jnp.int32, rp.shape, 0)
    top = jnp.where(rid < s, rp, xs[0:SUBLANES])
    return jnp.concatenate([top, xs[SUBLANES:]], axis=0)


def _shift_up(x, next8, s, last):
    n = x.shape[0]
    xs = pltpu.roll(x, n - s, 0)
    rn = pltpu.roll(next8, SUBLANES - s, 0) * jnp.where(last, 0.0, 1.0)
    rid = lax.broadcasted_iota(jnp.int32, rn.shape, 0)
    bot = jnp.where(rid >= SUBLANES - s, rn, xs[n - SUBLANES:])
    return jnp.concatenate([xs[:n - SUBLANES], bot], axis=0)


def _conv_fwd(x, prev8, w, first):
    acc = x * w[CONV_K - 1:CONV_K]
    shifted = []
    for s in range(1, CONV_K):
        xs = _shift_down(x, prev8, s, first)
        shifted.append(xs)
        acc = acc + xs * w[CONV_K - 1 - s:CONV_K - s]
    return acc, shifted


def _l2n(x):
    outs, rs = [], []
    for h in range(x.shape[1] // LANES):
        xh = x[:, h * LANES:(h + 1) * LANES]
        r = lax.rsqrt(jnp.sum(xh * xh, axis=-1, keepdims=True) + EPS)
        outs.append(xh * r)
        rs.append(r)
    return jnp.concatenate(outs, axis=1), rs


def _gate_math(ba, sel_b, sel_a, a_row, dt_row):
    bl = _dot(ba, sel_b, precision=HI)
    al = _dot(ba, sel_a, precision=HI) + dt_row
    beta = _sigmoid(bl)
    ea = jnp.exp(a_row)
    g = -ea * _softplus(al)
    return bl, al, beta, g, ea


def _cumsum_chunks(x, reverse=False):
    n = x.shape[0]
    rid = lax.broadcasted_iota(jnp.int32, x.shape, 0) % CHUNK
    s = 1
    while s < CHUNK:
        if reverse:
            x = x + jnp.where(rid < CHUNK - s, pltpu.roll(x, n - s, 0), 0.0)
        else:
            x = x + jnp.where(rid >= s, pltpu.roll(x, s, 0), 0.0)
        s *= 2
    return x


def _gdn_pre_fwd(proj, conv_w, sel_b, sel_a, a_row, dt_row, name, bm=256):
    rows = proj.shape[0]
    w3 = 3 * GDN_W

    def fn(i, x, p8, ba, w, sel_b, sel_a, a_row, dt_row):
        conv, _ = _conv_fwd(x, p8, w, i == 0)
        act = _silu_and_grad(conv)[0]
        qk, _ = _l2n(act[:, :2 * GDN_W])
        _, _, beta, g, _ = _gate_math(ba, sel_b, sel_a, a_row, dt_row)
        return qk[:, :GDN_W], qk[:, GDN_W:], act[:, 2 * GDN_W:], _cumsum_chunks(g), beta
    ins = [_row(proj, w3, 0), _prev8(proj, w3, 0), _row(proj, LANES, (GDN_IN_PAD - LANES) // LANES),
           _full(conv_w), _full(sel_b), _full(sel_a), _full(a_row), _full(dt_row)]
    return _rowcall(fn, name, rows, bm, ins, [(GDN_W, F32)] * 5)


def _gdn_pre_bwd(proj, conv_w, sel_b, sel_a, a_row, dt_row, dq, dk, dv, dgc, dbeta, name, bm=128):
    rows = proj.shape[0]
    w3 = 3 * GDN_W

    def fn(i, x, p8, ba, w, sel_b, sel_a, a_row, dt_row, dq, dk, dv, dgc, dbeta):
        dg = _cumsum_chunks(dgc, reverse=True)
        conv, shifted = _conv_fwd(x, p8, w, i == 0)
        act, dact = _silu_and_grad(conv)
        qk, rs = _l2n(act[:, :2 * GDN_W])
        dqk = jnp.concatenate([dq, dk], axis=1)
        parts = []
        for h in range(2 * GDN_HEADS):
            sl = slice(h * LANES, (h + 1) * LANES)
            y, dy = qk[:, sl], dqk[:, sl]
            parts.append(rs[h] * (dy - y * jnp.sum(y * dy, axis=-1, keepdims=True)))
        dconv = jnp.concatenate(parts + [dv], axis=1) * dact
        dws = [_fold8(dconv * xs) for xs in reversed(shifted)] + [_fold8(dconv * x)]
        bl, al, beta, g, ea = _gate_math(ba, sel_b, sel_a, a_row, dt_row)
        dbl = dbeta * beta * (1.0 - beta)
        dal = dg * (-ea) * _sigmoid(al)
        dba = _dot(dbl, sel_b, NT, precision=HI) + _dot(dal, sel_a, NT, precision=HI)
        return (dconv, dba * (1.0 / LANES)) + tuple(dws) + (_fold8(dg * g), _fold8(dal))
    ins = [_row(proj, w3, 0), _prev8(proj, w3, 0), _row(proj, LANES, (GDN_IN_PAD - LANES) // LANES),
           _full(conv_w), _full(sel_b), _full(sel_a), _full(a_row), _full(dt_row),
           _row(dq), _row(dk), _row(dv), _row(dgc), _row(dbeta)]
    return _rowcall(fn, name, rows, bm, ins, [(w3, F32), (LANES, BF16)],
                    [(SUBLANES, w3)] * CONV_K + [(SUBLANES, GDN_W)] * 2)


def _conv_bwd_input(dconv, conv_w, name, bm=256):
    rows, w3 = dconv.shape
    steps = rows // min(bm, rows)

    def fn(i, dc, n8, w):
        acc = dc * w[CONV_K - 1:CONV_K]
        for s in range(1, CONV_K):
            acc = acc + _shift_up(dc, n8, s, i == steps - 1) * w[CONV_K - 1 - s:CONV_K - s]
        return acc
    return _rowcall(fn, name, rows, bm, [_row(dconv), _next8(dconv, w3, 0), _full(conv_w)], [(w3, BF16)])[0]


def _gdn_post_fwd(o, proj, ng, name, bm=512):
    def fn(i, o, z, ng):
        outs = []
        for h in range(GDN_HEADS):
            sl = slice(h * LANES, (h + 1) * LANES)
            _, on = _rms_stats(o[:, sl])
            outs.append(on * ng * _silu_and_grad(z[:, sl])[0])
        return jnp.concatenate(outs, axis=1)
    return _rowcall(fn, name, o.shape[0], bm, [_row(o), _row(proj, GDN_W, 3), _full(ng)], [(GDN_W, BF16)])[0]


def _gdn_post_bwd(o, proj, ng, dcat, name, bm=256):
    def fn(i, o, z, ng, dm):
        dm = dm.astype(F32)
        dos, dzs = [], []
        dng = jnp.zeros((SUBLANES, LANES), F32)
        for h in range(GDN_HEADS):
            sl = slice(h * LANES, (h + 1) * LANES)
            s, ds = _silu_and_grad(z[:, sl])
            r, on = _rms_stats(o[:, sl])
            dzs.append(dm[:, sl] * on * ng * ds)
            dy = dm[:, sl] * s
            dxn = dy * ng
            dos.append(r * (dxn - on * jnp.mean(dxn * on, axis=-1, keepdims=True)))
            dng = dng + _fold8(dy * on)
        return jnp.concatenate(dos, axis=1), jnp.concatenate(dzs, axis=1), dng
    return _rowcall(fn, name, o.shape[0], bm, [_row(o), _row(proj, GDN_W, 3), _full(ng), _row(dcat, GDN_W, 0)],
                    [(GDN_W, F32), (GDN_W, BF16)], [(SUBLANES, LANES)])


def _split3(x):
    hi = x.astype(BF16)
    r = x - hi.astype(F32)
    mid = r.astype(BF16)
    return hi, mid, (r - mid.astype(F32)).astype(BF16)


def _bdot(a, b, mode="nn"):
    lc, rc = {"nn": (2, 1), "nt": (2, 2), "tn": (1, 1)}[mode]
    return lax.dot_general(a, b, (((lc,), (rc,)), ((0,), (0,))), preferred_element_type=F32)


def _bdot3(a, b, mode="nn"):
    ah, bh = a.astype(BF16), b.astype(BF16)
    al, bl = (a - ah.astype(F32)).astype(BF16), (b - bh.astype(F32)).astype(BF16)
    return _bdot(ah, bh, mode) + _bdot(ah, bl, mode) + _bdot(al, bh, mode)


GDN_CB = 4


def _gdn_chunk(q, k, v, gc, beta, t=None):
    c = CHUNK
    nb = q.shape[0]
    row = lax.broadcasted_iota(jnp.int32, (c, c), 0)
    col = lax.broadcasted_iota(jnp.int32, (c, c), 1)
    tril, strict = row >= col, row > col
    lane0 = (lax.broadcasted_iota(jnp.int32, (nb, c, LANES), 2) == 0).astype(BF16)
    gc_row = sum(_bdot(lane0, part, "nt") for part in _split3(gc))
    dm = jnp.exp(jnp.where(tril, gc[:, :, :c] - gc_row, -1e30))
    eg = jnp.exp(gc)
    gcl = gc[:, c - 1:c, :]
    ekg = jnp.exp(gcl - gc)
    egl = jnp.exp(gcl)
    qs = q * (GDN_DK ** -0.5)
    kb = k * beta
    kk = _bdot(kb, k, "nt")
    a = jnp.where(strict, kk * dm, 0.0)
    vb = v * beta
    kbg = kb * eg
    qk = _bdot(qs, k, "nt")
    p = jnp.where(tril, qk * dm, 0.0)
    out = dict(tril=tril, strict=strict, dm=dm, eg=eg, ekg=ekg, egl=egl, qs=qs, kb=kb, kk=kk, a=a,
               vb=vb, kbg=kbg, qk=qk, p=p, qg=qs * eg, kg=k * ekg)
    if t is None:
        y = -a
        t = (row == col).astype(F32) + y
        for _ in range(5):
            y = _bdot3(y, y)
            t = t + _bdot3(t, y)
        out.update(u=_bdot3(t, vb), w=_bdot3(t, kbg))
    out["t"] = t
    return out


def _gdn_stack(ref):
    return jnp.stack([ref[c * CHUNK:(c + 1) * CHUNK, h * LANES:(h + 1) * LANES]
                      for c in range(GDN_CB) for h in range(GDN_HEADS)])


def _gdn_unstack(x, ref):
    for c in range(GDN_CB):
        for h in range(GDN_HEADS):
            ref[c * CHUNK:(c + 1) * CHUNK, h * LANES:(h + 1) * LANES] = x[c * GDN_HEADS + h]


def _gdn_fwd(q, k, v, gc, beta, name):
    rows = q.shape[0]
    n_chunks = rows // CHUNK
    steps = n_chunks // GDN_CB
    blk = pl.BlockSpec((GDN_CB * CHUNK, GDN_W), lambda n: (n, 0))
    st = pl.BlockSpec((GDN_HEADS, GDN_CB, GDN_DK, LANES), lambda n: (0, n, 0, 0))
    tinv = pl.BlockSpec((GDN_CB, GDN_HEADS, CHUNK, CHUNK), lambda n: (n, 0, 0, 0))

    def body(q_ref, k_ref, v_ref, g_ref, b_ref, o_ref, st_ref, t_ref, w_ref, vn_ref, s_ref):
        @pl.when(pl.program_id(0) == 0)
        def _():
            s_ref[...] = jnp.zeros(s_ref.shape, F32)
        c = _gdn_chunk(*[_gdn_stack(r) for r in (q_ref, k_ref, v_ref, g_ref, b_ref)])
        _gdn_unstack(c["w"], w_ref)
        s = s_ref[...]
        for i in range(GDN_CB):
            hs = slice(i * GDN_HEADS, (i + 1) * GDN_HEADS)
            rs = slice(i * CHUNK, (i + 1) * CHUNK)
            st_ref[:, i] = s
            vn = c["u"][hs] - _bdot(c["w"][hs], s)
            o = _bdot(c["qg"][hs], s) + _bdot(c["p"][hs], vn)
            s = s * c["egl"][hs] + _bdot(c["kg"][hs], vn, "tn")
            t_ref[i] = c["t"][hs]
            for h in range(GDN_HEADS):
                o_ref[rs, h * LANES:(h + 1) * LANES] = o[h]
                vn_ref[rs, h * LANES:(h + 1) * LANES] = vn[h]
        s_ref[...] = s

    f = jax.ShapeDtypeStruct((rows, GDN_W), F32)
    return pl.pallas_call(
        body, name=name, grid=(steps,), in_specs=[blk] * 5, out_specs=[blk, st, tinv, blk, blk],
        out_shape=[f, jax.ShapeDtypeStruct((GDN_HEADS, n_chunks, GDN_DK, LANES), F32),
                   jax.ShapeDtypeStruct((n_chunks, GDN_HEADS, CHUNK, CHUNK), F32), f, f],
        scratch_shapes=[pltpu.VMEM((GDN_HEADS, GDN_DK, LANES), F32)],
        compiler_params=_params(("arbitrary",)))(q, k, v, gc, beta)


def _gdn_bwd(q, k, v, gc, beta, states, tinv, w, vn, do, name):
    rows = q.shape[0]
    n_chunks = rows // CHUNK
    steps = n_chunks // GDN_CB
    blk = pl.BlockSpec((GDN_CB * CHUNK, GDN_W), lambda n: (steps - 1 - n, 0))
    st = pl.BlockSpec((GDN_HEADS, GDN_CB, GDN_DK, LANES), lambda n: (0, steps - 1 - n, 0, 0))
    ti = pl.BlockSpec((GDN_CB, GDN_HEADS, CHUNK, CHUNK), lambda n: (steps - 1 - n, 0, 0, 0))
    nbatch = GDN_CB * GDN_HEADS

    def lanesum(x):
        return jnp.broadcast_to(jnp.sum(x, axis=-1, keepdims=True), x.shape)

    def body(q_ref, k_ref, v_ref, g_ref, b_ref, st_ref, t_ref, w_ref, vn_ref, do_ref,
             dq_ref, dk_ref, dv_ref, dg_ref, db_ref, ds_ref):
        @pl.when(pl.program_id(0) == 0)
        def _():
            ds_ref[...] = jnp.zeros(ds_ref.shape, F32)
        q, k, v, gc, beta, w, vn, do = [_gdn_stack(r) for r in (q_ref, k_ref, v_ref, g_ref, b_ref, w_ref, vn_ref, do_ref)]
        t = t_ref[...].reshape(nbatch, CHUNK, CHUNK)
        s = jnp.stack([st_ref[h, i] for i in range(GDN_CB) for h in range(GDN_HEADS)])
        c = _gdn_chunk(q, k, v, gc, beta, t)
        tril, strict, dm = c["tril"], c["strict"], c["dm"]
        dsn = ds_ref[...]
        dvn_c, dkg_c, dgl_c = [None] * GDN_CB, [None] * GDN_CB, [None] * GDN_CB
        for i in reversed(range(GDN_CB)):
            hs = slice(i * GDN_HEADS, (i + 1) * GDN_HEADS)
            dvn_c[i] = _bdot(c["p"][hs], do[hs], "tn") + _bdot(c["kg"][hs], dsn)
            dkg_c[i] = _bdot(vn[hs], dsn, "nt")
            dgl_c[i] = jnp.sum(jnp.sum(s[hs] * dsn, axis=2, keepdims=True), axis=1, keepdims=True) * c["egl"][hs]
            dsn = dsn * c["egl"][hs] + _bdot(c["qg"][hs], do[hs], "tn") - _bdot(w[hs], dvn_c[i], "tn")
        ds_ref[...] = dsn
        dvn, dkg, dgl = jnp.concatenate(dvn_c), jnp.concatenate(dkg_c), jnp.concatenate(dgl_c)
        dp = jnp.where(tril, _bdot(do, vn, "nt"), 0.0)
        dqg = _bdot(do, s, "nt")
        dw = -_bdot(dvn, s, "nt")
        dvb = _bdot3(t, dvn, "tn")
        dkbg = _bdot3(t, dw, "tn")
        dt = _bdot(dvn, c["vb"], "nt") + _bdot(dw, c["kbg"], "nt")
        da = jnp.where(strict, -_bdot3(_bdot3(t, dt, "tn"), t, "nt"), 0.0)
        dkk = da * dm
        dqk = dp * dm
        dkb = _bdot(dkk, k) + dkbg * c["eg"]
        dk = _bdot(dkk, c["kb"], "tn") + _bdot(dqk, c["qs"], "tn") + dkg * c["ekg"] + dkb * beta
        dqs = _bdot(dqk, k) + dqg * c["eg"]
        e = da * c["a"] + dp * c["p"]
        ones = jnp.ones((nbatch, CHUNK, LANES), BF16)
        col_sums = sum(_bdot(part, ones, "tn") for part in _split3(e))
        kg_term = lanesum(dkg * c["kg"])
        dgc = (jnp.broadcast_to(jnp.sum(e, axis=-1, keepdims=True), (nbatch, CHUNK, LANES)) - col_sums
               + lanesum(dqg * c["qg"]) - kg_term + lanesum(dkbg * c["kbg"]))
        dgcl = jnp.sum(kg_term, axis=1, keepdims=True) + dgl
        last = lax.broadcasted_iota(jnp.int32, (CHUNK, LANES), 0) == CHUNK - 1
        _gdn_unstack(dqs * (GDN_DK ** -0.5), dq_ref)
        _gdn_unstack(dk, dk_ref)
        _gdn_unstack(dvb * beta, dv_ref)
        _gdn_unstack(dgc + jnp.where(last, dgcl, 0.0), dg_ref)
        _gdn_unstack(lanesum(dvb * v) + lanesum(dkb * k), db_ref)

    return pl.pallas_call(
        body, name=name, grid=(steps,), in_specs=[blk] * 5 + [st, ti, blk, blk, blk], out_specs=[blk] * 5,
        out_shape=[jax.ShapeDtypeStruct((rows, GDN_W), F32)] * 5,
        scratch_shapes=[pltpu.VMEM((GDN_HEADS, GDN_DK, LANES), F32)],
        compiler_params=_params(("arbitrary",)))(q, k, v, gc, beta, states, tinv, w, vn, do)


def _swa_masks(first):
    r = lax.broadcasted_iota(jnp.int32, (SWA_BLOCK, 2 * SWA_BLOCK), 0)
    c = lax.broadcasted_iota(jnp.int32, (SWA_BLOCK, 2 * SWA_BLOCK), 1)
    band = (c > r) & (c <= r + SWA_BLOCK)
    return band & (jnp.logical_not(first) | (c >= SWA_BLOCK))


def _swa_stack(ref, j):
    lane = lax.broadcasted_iota(jnp.int32, (1, LANES), 1)
    parts = []
    for g in range(SWA_GROUP):
        ch = j * (SWA_GROUP // 2) + g // 2
        keep = (lane < SWA_DH) if g % 2 == 0 else (lane >= SWA_DH)
        parts.append(ref[:, ch * LANES:(ch + 1) * LANES] * keep.astype(ref.dtype))
    return jnp.concatenate(parts, axis=0)


def _swa_unstack(x2, j, out_ref):
    low = lax.broadcasted_iota(jnp.int32, (SWA_BLOCK, LANES), 1) < SWA_DH
    for c3 in range(SWA_GROUP // 2):
        even = x2[(2 * c3) * SWA_BLOCK:(2 * c3 + 1) * SWA_BLOCK]
        odd = x2[(2 * c3 + 1) * SWA_BLOCK:(2 * c3 + 2) * SWA_BLOCK]
        ch = j * (SWA_GROUP // 2) + c3
        out_ref[:, ch * LANES:(ch + 1) * LANES] = jnp.where(low, even, odd).astype(out_ref.dtype)


def _swa_probs(s, sink, mask):
    s = jnp.where(mask, s, -1e30)
    m = jnp.maximum(jnp.max(s, axis=-1, keepdims=True), sink)
    p = jnp.where(mask, jnp.exp(s - m), 0.0)
    es = jnp.exp(sink - m)
    inv = 1.0 / (jnp.sum(p, axis=-1, keepdims=True) + es)
    return p * inv, es * inv


def _swa_scores(q_ref, kc_ref, kp_ref, sink_ref, j, mask):
    sl = slice(j * LANES, (j + 1) * LANES)
    qst = _swa_stack(q_ref, j)
    kw = jnp.concatenate([kp_ref[:, sl], kc_ref[:, sl]], axis=0)
    s = _dot(qst, kw, NT) * (SWA_DH ** -0.5)
    ps = [_swa_probs(s[g * SWA_BLOCK:(g + 1) * SWA_BLOCK], sink_ref[j * SWA_GROUP + g], mask)
          for g in range(SWA_GROUP)]
    return qst, kw, ps


def _swa_fwd(q, k2, v2, sinks, name):
    rows = q.shape[0]
    nb = rows // SWA_BLOCK
    w = SWA_HEADS * SWA_DH
    kvw = SWA_KV_HEADS * LANES
    cur = pl.BlockSpec((SWA_BLOCK, w), lambda i: (i, 0))
    kcur = pl.BlockSpec((SWA_BLOCK, kvw), lambda i: (i, 0))
    kprev = pl.BlockSpec((SWA_BLOCK, kvw), lambda i: (jnp.maximum(i - 1, 0), 0))

    def body(sink_ref, q_ref, kc_ref, kp_ref, vc_ref, vp_ref, o_ref):
        mask = _swa_masks(pl.program_id(0) == 0)
        for j in range(SWA_KV_HEADS):
            sl = slice(j * LANES, (j + 1) * LANES)
            _, _, ps = _swa_scores(q_ref, kc_ref, kp_ref, sink_ref, j, mask)
            vw = jnp.concatenate([vp_ref[:, sl], vc_ref[:, sl]], axis=0)
            pst = jnp.concatenate([p.astype(BF16) for p, _ in ps], axis=0)
            _swa_unstack(_dot(pst, vw), j, o_ref)

    return pl.pallas_call(
        body, name=name, grid=(nb,),
        in_specs=[pl.BlockSpec(memory_space=pltpu.SMEM), cur, kcur, kprev, kcur, kprev], out_specs=cur,
        out_shape=jax.ShapeDtypeStruct((rows, w), BF16),
        compiler_params=_params(("arbitrary",)))(sinks, q, k2, k2, v2, v2)


def _swa_bwd(q, k2, v2, sinks, dcat, name):
    rows = q.shape[0]
    nb = rows // SWA_BLOCK
    w = SWA_HEADS * SWA_DH
    kvw = SWA_KV_HEADS * LANES
    cur = pl.BlockSpec((SWA_BLOCK, w), lambda i: (jnp.minimum(i, nb - 1), 0))
    kcur = pl.BlockSpec((SWA_BLOCK, kvw), lambda i: (jnp.minimum(i, nb - 1), 0))
    kprev = pl.BlockSpec((SWA_BLOCK, kvw), lambda i: (jnp.clip(i - 1, 0, nb - 1), 0))
    late = pl.BlockSpec((SWA_BLOCK, kvw), lambda i: (jnp.maximum(i - 1, 0), 0))
    acc_spec = pl.BlockSpec((SUBLANES, LANES), lambda i: (0, 0))

    def body(sink_ref, q_ref, kc_ref, kp_ref, vc_ref, vp_ref, do_ref, dq_ref, dk_ref, dv_ref, dsk_ref,
             ck_ref, cv_ref):
        i = pl.program_id(0)

        @pl.when(i == 0)
        def _():
            ck_ref[...] = jnp.zeros(ck_ref.shape, F32)
            cv_ref[...] = jnp.zeros(cv_ref.shape, F32)
            dsk_ref[...] = jnp.zeros(dsk_ref.shape, F32)

        @pl.when(i == nb)
        def _():
            dk_ref[...] = ck_ref[...]
            dv_ref[...] = cv_ref[...]

        @pl.when(i < nb)
        def _():
            mask = _swa_masks(i == 0)
            lane = lax.broadcasted_iota(jnp.int32, (SUBLANES, LANES), 1)
            dsk = jnp.zeros((SUBLANES, LANES), F32)
            for j in range(SWA_KV_HEADS):
                sl = slice(j * LANES, (j + 1) * LANES)
                qst, kw, ps = _swa_scores(q_ref, kc_ref, kp_ref, sink_ref, j, mask)
                vw = jnp.concatenate([vp_ref[:, sl], vc_ref[:, sl]], axis=0)
                dost = _swa_stack(do_ref, j)
                dpr = _dot(dost, vw, NT)
                dss = []
                for g in range(SWA_GROUP):
                    p, sink_p = ps[g]
                    dpg = dpr[g * SWA_BLOCK:(g + 1) * SWA_BLOCK]
                    delta = jnp.sum(p * dpg, axis=-1, keepdims=True)
                    dss.append((p * (dpg - delta)).astype(BF16))
                    dsg = jnp.sum(-sink_p * delta, axis=0, keepdims=True)
                    dsk = dsk + jnp.where(lane == j * SWA_GROUP + g, dsg, 0.0)
                dsst = jnp.concatenate(dss, axis=0)
                pst = jnp.concatenate([p.astype(BF16) for p, _ in ps], axis=0)
                _swa_unstack(_dot(dsst, kw) * (SWA_DH ** -0.5), j, dq_ref)
                dk = _dot(dsst, qst, TN) * (SWA_DH ** -0.5)
                dv = _dot(pst, dost, TN)
                dk = dk + pltpu.roll(dk, SWA_DH, 1)
                dv = dv + pltpu.roll(dv, SWA_DH, 1)
                dk_ref[:, sl] = ck_ref[:, sl] + dk[:SWA_BLOCK]
                dv_ref[:, sl] = cv_ref[:, sl] + dv[:SWA_BLOCK]
                ck_ref[:, sl] = dk[SWA_BLOCK:]
                cv_ref[:, sl] = dv[SWA_BLOCK:]
            dsk_ref[...] += dsk

    f = jax.ShapeDtypeStruct((rows, kvw), F32)
    return pl.pallas_call(
        body, name=name, grid=(nb + 1,),
        in_specs=[pl.BlockSpec(memory_space=pltpu.SMEM), cur, kcur, kprev, kcur, kprev, cur],
        out_specs=[cur, late, late, acc_spec],
        out_shape=[jax.ShapeDtypeStruct((rows, w), F32), f, f, jax.ShapeDtypeStruct((SUBLANES, LANES), F32)],
        scratch_shapes=[pltpu.VMEM((SWA_BLOCK, kvw), F32), pltpu.VMEM((SWA_BLOCK, kvw), F32)],
        compiler_params=_params(("arbitrary",)))(sinks, q, k2, k2, v2, v2, dcat)


def _mem_probs(mq, kbd):
    s = _dot(mq.astype(BF16), kbd) * (MEM_DH ** -0.5)
    ps = []
    for h in range(MEM_HEADS):
        sh = s[:, h * MEM_LEN:(h + 1) * MEM_LEN]
        e = jnp.exp(sh - jnp.max(sh, axis=-1, keepdims=True))
        ps.append(e / jnp.sum(e, axis=-1, keepdims=True))
    return ps


def _mem_fwd(proj, cb, kbd, vbd, name, bm=512):
    def fn(i, mq, kbd, vbd):
        p = jnp.concatenate(_mem_probs(mq, kbd), axis=1)
        return _dot(p.astype(BF16), vbd)
    return _rowcall(fn, name, proj.shape[0], bm, [_row(proj, MEM_W, cb), _full(kbd), _full(vbd)], [(MEM_W, BF16)])[0]


def _mem_bwd(proj, cb, kbd, vbd, dcat, name, bm=512):
    def fn(i, mq, kbd, vbd, do):
        ps = _mem_probs(mq, kbd)
        dp = _dot(do, vbd, NT)
        dss = []
        for h in range(MEM_HEADS):
            dph = dp[:, h * MEM_LEN:(h + 1) * MEM_LEN]
            dss.append(ps[h] * (dph - jnp.sum(ps[h] * dph, axis=-1, keepdims=True)))
        ds = (jnp.concatenate(dss, axis=1) * (MEM_DH ** -0.5)).astype(BF16)
        p = jnp.concatenate(ps, axis=1).astype(BF16)
        return _dot(ds, kbd, NT), _dot(mq.astype(BF16), ds, TN), _dot(p, do, TN)
    return _rowcall(fn, name, proj.shape[0], bm, [_row(proj, MEM_W, cb), _full(kbd), _full(vbd), _row(dcat, MEM_W, 3)],
                    [(MEM_W, BF16)], [(MEM_W, MEM_HEADS * MEM_LEN), (MEM_HEADS * MEM_LEN, MEM_W)])


def _mem_expand(mkv):
    feat_head = jnp.arange(MEM_W) // MEM_DH
    slot_head = jnp.arange(MEM_HEADS * MEM_LEN) // MEM_LEN
    on = feat_head[:, None] == slot_head[None, :]
    kbd = jnp.where(on, jnp.tile(mkv[:, :MEM_W].T, (1, MEM_HEADS)), 0.0)
    vbd = jnp.where(on.T, jnp.tile(mkv[:, MEM_W:], (MEM_HEADS, 1)), 0.0)
    return kbd.astype(BF16), vbd.astype(BF16)


def _mem_collapse(dkbd, dvbd):
    dk = [dkbd[h * MEM_DH:(h + 1) * MEM_DH, h * MEM_LEN:(h + 1) * MEM_LEN].T for h in range(MEM_HEADS)]
    dv = [dvbd[h * MEM_LEN:(h + 1) * MEM_LEN, h * MEM_DH:(h + 1) * MEM_DH] for h in range(MEM_HEADS)]
    return jnp.concatenate(dk + dv, axis=1)


def _adamw_math(w, g, m, v):
    m = ADAM_B1 * m + (1.0 - ADAM_B1) * g
    v = ADAM_B2 * v + (1.0 - ADAM_B2) * (g * g)
    m_hat = m / (1.0 - ADAM_B1 ** ADAM_STEP)
    v_hat = v / (1.0 - ADAM_B2 ** ADAM_STEP)
    return -ADAM_LR * (m_hat / (jnp.sqrt(v_hat) + ADAM_EPS) + ADAM_WD * w), m, v


def _adamw(w, g, m, v, name, bm=512):
    d = w.shape[1]
    return _rowcall(lambda i, *a: _adamw_math(*a), name, w.shape[0], bm, [_row(w), _row(g), _row(m), _row(v)], [(d, F32)] * 3)


def _adamw_halves(w, g_own, g_other, my_core, m, v, name):
    layers, rows, n = w.shape
    r = rows // 2
    bm = LANES if r % LANES == 0 else r
    per_half = r // bm
    nat = pl.BlockSpec((None, bm, n), lambda l, c, i: (l, c * per_half + i, 0))
    own = pl.BlockSpec((None, bm, n), lambda l, c, i: (l, i, 0))
    other = pl.BlockSpec((None, None, bm, n), lambda l, c, i: (0, l, i, 0))

    def body(core_ref, w_ref, own_ref, other_ref, m_ref, v_ref, go_ref, d_ref, mo_ref, vo_ref):
        g = jnp.where(core_ref[0] == pl.program_id(1), own_ref[...], other_ref[...])
        go_ref[...] = g
        d_ref[...], mo_ref[...], vo_ref[...] = _adamw_math(w_ref[...], g, m_ref[...], v_ref[...])

    return pl.pallas_call(
        body, name=name, grid=(layers, 2, per_half),
        in_specs=[pl.BlockSpec(memory_space=pltpu.SMEM), nat, own, other, nat, nat], out_specs=[nat] * 4,
        out_shape=[jax.ShapeDtypeStruct(w.shape, F32)] * 4,
        compiler_params=_params(("parallel", "parallel", "parallel")))(my_core, w, g_own, g_other, m, v)


def _sum_slots(buf, name, bm=128):
    n, rows, w = buf.shape
    bm = min(bm, rows)
    assert rows % bm == 0

    def body(b_ref, o_ref):
        acc = b_ref[0].astype(F32)
        for s in range(1, n):
            acc = acc + b_ref[s].astype(F32)
        o_ref[...] = acc

    return pl.pallas_call(
        body, name=name, grid=(rows // bm,), in_specs=[pl.BlockSpec((n, bm, w), lambda i: (0, i, 0))],
        out_specs=pl.BlockSpec((bm, w), lambda i: (i, 0)), out_shape=jax.ShapeDtypeStruct((rows, w), F32),
        compiler_params=_params(("parallel",)))(buf)


def _exchange(srcs, same, masks, name, keep_own=True):
    slots = N_DEV if len(masks) == N_DEV - 1 else (2 if keep_own else 1)
    n_arr, n_peer = len(srcs), len(masks)
    shapes = [s.shape if sm else s.shape[1:] for s, sm in zip(srcs, same)]

    def body(*refs):
        src_refs, out_refs = refs[:n_arr], refs[n_arr:2 * n_arr]
        send_sems, recv_sems, local_sems = refs[2 * n_arr:]
        x, y, c = lax.axis_index("x"), lax.axis_index("y"), lax.axis_index("c")
        me = 4 * x + 2 * y + c

        def flip(v, bit):
            return 1 - v if bit else v

        def slot_of(dev):
            return dev if slots == N_DEV else (dev % 2 if slots == 2 else 0)

        def piece(a, p):
            return src_refs[a] if same[a] else src_refs[a].at[p]

        local = []
        if keep_own:
            local = [pltpu.make_async_copy(piece(a, me), out_refs[a].at[slot_of(me)], local_sems.at[a])
                     for a in range(n_arr)]
        for cp in local:
            cp.start()
        copies = []
        for idx, k in enumerate(masks):
            peer = (flip(x, k & 4), flip(y, k & 2), flip(c, k & 1))
            peer_id = 4 * peer[0] + 2 * peer[1] + peer[2]
            for a in range(n_arr):
                sem = idx * n_arr + a
                cp = pltpu.make_async_remote_copy(
                    src_ref=piece(a, peer_id), dst_ref=out_refs[a].at[slot_of(me)],
                    send_sem=send_sems.at[sem], recv_sem=recv_sems.at[sem], device_id=peer, device_id_type=MESH)
                cp.start()
                copies.append((cp, pltpu.make_async_remote_copy(
                    src_ref=piece(a, peer_id), dst_ref=out_refs[a].at[slot_of(peer_id)],
                    send_sem=send_sems.at[sem], recv_sem=recv_sems.at[sem], device_id=peer, device_id_type=MESH)))
        for cp, landing in copies:
            cp.wait_send()
            landing.wait_recv()
        for cp in local:
            cp.wait()

    any_spec = pl.BlockSpec(memory_space=pl.ANY)
    n_sem = n_arr * n_peer
    return pl.pallas_call(
        body, name=name, in_specs=[any_spec] * n_arr, out_specs=[any_spec] * n_arr,
        out_shape=[jax.ShapeDtypeStruct((slots,) + tuple(sh), s.dtype) for sh, s in zip(shapes, srcs)],
        scratch_shapes=[pltpu.SemaphoreType.DMA((n_sem,)), pltpu.SemaphoreType.DMA((n_sem,)),
                        pltpu.SemaphoreType.DMA((n_arr,))],
        )(*srcs)


ALL_PEERS = tuple(range(1, N_DEV))
SIBLING = (1,)


def _pack(arrays, rows):
    flat = jnp.concatenate([a.reshape(-1) for a in arrays])
    return jnp.pad(flat, (0, rows * D_MODEL - flat.shape[0])).reshape(rows, D_MODEL)


def _unpack(buf, shapes):
    flat = buf.reshape(-1)
    out, off = [], 0
    for s in shapes:
        n = math.prod(s)
        out.append(flat[off:off + n].reshape(s))
        off += n
    return out


def _rows_for(shapes, mult):
    n = sum(math.prod(s) for s in shapes)
    rows = -(-n // D_MODEL)
    return -(-rows // mult) * mult


SHARD_AXIS = dict(w_mem_kv=1, w_out=1, w_gate_up=2, w_down=1, gdn_w_in=2, swa_w_q=1, w_kv=0, gdn_conv=2)
HALF_AXIS = dict(w_mem_kv=1, w_out=1, w_gate_up=1, w_down=1, gdn_w_in=1, swa_w_q=1, w_kv=0)


def _my_half(shard, name, c):
    ax = HALF_AXIS[name]
    h = shard.shape[ax] // 2
    return lax.dynamic_slice_in_dim(shard, c * h, h, axis=ax)


def _piece_layout(name, half_shape):
    dims, pos = [], {}
    for i, d in enumerate(half_shape):
        if i == SHARD_AXIS[name]:
            pos["chip"] = len(dims)
            dims.append(4)
        if i == HALF_AXIS[name]:
            pos["core"] = len(dims)
            dims.append(2)
        pos[i] = len(dims)
        dims.append(d)
    return dims, [pos["chip"], pos["core"]] + [pos[i] for i in range(len(half_shape))]


def _full_shape(name, half_shape):
    return tuple(d * (4 if i == SHARD_AXIS[name] else 1) * (2 if i == HALF_AXIS[name] else 1)
                 for i, d in enumerate(half_shape))


def _assemble(pieces, name):
    half_shape = pieces.shape[1:]
    if half_shape[-1] % LANES:
        chips = [jnp.concatenate([pieces[2 * s], pieces[2 * s + 1]], axis=HALF_AXIS[name]) for s in range(4)]
        return jnp.concatenate(chips, axis=SHARD_AXIS[name])
    dims, perm = _piece_layout(name, half_shape)
    inverse = [perm.index(i) for i in range(len(perm))]
    return pieces.reshape((4, 2) + half_shape).transpose(inverse).reshape(_full_shape(name, half_shape))


def _to_pieces(full, name, half_shape):
    if half_shape[-1] % LANES:
        ns, nh = half_shape[SHARD_AXIS[name]], half_shape[HALF_AXIS[name]]
        return jnp.stack([lax.slice_in_dim(lax.slice_in_dim(full, s * ns, (s + 1) * ns, axis=SHARD_AXIS[name]),
                                           c * nh, (c + 1) * nh, axis=HALF_AXIS[name])
                          for s in range(4) for c in range(2)])
    dims, perm = _piece_layout(name, half_shape)
    return full.reshape(dims).transpose(perm).reshape((N_DEV,) + tuple(half_shape))


def _from_halves(halves, name):
    ax = HALF_AXIS[name]
    s = jnp.moveaxis(halves, 0, ax)
    return s.reshape(s.shape[:ax] + (2 * s.shape[ax + 1],) + s.shape[ax + 2:])


def _rope_tables(positions):
    half = ROT_DIM // 2
    inv = ROPE_THETA ** (-jnp.arange(0, ROT_DIM, 2, dtype=F32) / ROT_DIM)
    ang = positions.astype(F32)[:, None] * inv
    cos, sin = jnp.cos(ang), jnp.sin(ang)
    rows = positions.shape[0]
    one = jnp.ones((rows, SWA_DH - ROT_DIM), F32)
    zero = jnp.zeros((rows, SWA_DH - ROT_DIM), F32)
    zh = jnp.zeros((rows, half), F32)
    c64 = jnp.concatenate([cos, cos, one], axis=1)
    a64 = jnp.concatenate([-sin, zh, zero], axis=1)
    b64 = jnp.concatenate([zh, sin, zero], axis=1)
    return tuple(jnp.concatenate([t, t], axis=1) for t in (c64, a64, b64))


def _pair_heads(t):
    return jnp.concatenate([t[:, :SWA_DH], t[:, :SWA_DH], t[:, SWA_DH:], t[:, SWA_DH:]], axis=1)


def _unpair_heads(t):
    return jnp.concatenate([t[:, :SWA_DH], t[:, LANES:LANES + SWA_DH]], axis=1)


def _gdn_in_pad(w):
    o2 = 4 * GDN_W
    pad = jnp.zeros(w.shape[:-1] + (GDN_IN_PAD - GDN_IN,), w.dtype)
    return jnp.concatenate([w[..., :o2], w[..., o2 + 2 * GDN_HEADS:], w[..., o2:o2 + 2 * GDN_HEADS], pad], axis=-1)


def _gdn_in_unpad(w):
    o2 = 4 * GDN_W
    return jnp.concatenate([w[..., :o2], w[..., o2 + MEM_W:o2 + MEM_W + 2 * GDN_HEADS], w[..., o2:o2 + MEM_W]], axis=-1)


def _head_rows(v):
    return jnp.repeat(v.astype(F32), LANES)[None, :]


def _selectors():
    lane = jnp.arange(LANES)[:, None]
    head = (jnp.arange(GDN_W) // LANES)[None, :]
    return (lane == head).astype(F32), (lane == head + GDN_HEADS).astype(F32)


def _local_step(x, mem, positions, target, w):
    rows = x.shape[0]
    tabs = _rope_tables(positions)
    sel_b, sel_a = _selectors()
    row2 = lambda v: v.reshape(1, -1).astype(F32)

    w_gu = _ff_interleave(w["w_gate_up"])
    mem_n = _rms_fwd(mem, row2(w["ln_mem"]), "mem_norm")
    saved = []
    kt = vt = None
    for l in range(DEPTH):
        s = dict(x0=x)
        h = _rms_fwd(x, row2(w["ln_mix"][l]), f"norm_mix{l}")
        mkv = _mm(mem_n, w["w_mem_kv"][l], "nn", f"mem_kv{l}")
        kbd, vbd = _mem_expand(mkv)
        if l < N_A:
            proj = _mm(h, w["gdn_w_in"][l], "nn", f"gdn_in{l}")
            a_row, dt_row = _head_rows(w["gdn_A_log"][l]), _head_rows(w["gdn_dt_bias"][l])
            q, k, v, gc, beta = _gdn_pre_fwd(proj, w["gdn_conv"][l], sel_b, sel_a, a_row, dt_row, f"gdn_pre{l}")
            o, states, tinv, gw, vn = _gdn_fwd(q, k, v, gc, beta, f"gdn_scan{l}")
            mix = _gdn_post_fwd(o, proj, row2(w["gdn_norm"][l]), f"gdn_post{l}")
            mq_cb = (3 * GDN_W + GDN_W) // MEM_W
            s.update(q=q, k=k, v=v, gc=gc, beta=beta, o=o, states=states, tinv=tinv, gw=gw, vn=vn,
                     a_row=a_row, dt_row=dt_row)
        else:
            proj = _mm(h, w["swa_w_q"][l - N_A], "nn", f"swa_in{l}")
            qr = _rope(proj, SWA_HEADS * SWA_DH, 0, tabs, 1, f"rope_q{l}", BF16)
            mix = _swa_fwd(qr, kt, vt, w["swa_sinks"][l - N_A], f"swa{l}")
            mq_cb = (SWA_HEADS * SWA_DH) // MEM_W
            s.update(qr=qr)
        mem_o = _mem_fwd(proj, mq_cb, kbd, vbd, f"mem_attn{l}")
        cat = jnp.concatenate([mix, mem_o], axis=1)
        x1 = _mm(cat, w["w_out"][l], "nn", f"out_proj{l}", add=x)
        h2 = _rms_fwd(x1, row2(w["ln_ffn"][l]), f"norm_ffn{l}")
        gu, act = _gate_up_fwd(h2, w_gu[l], f"gate_up{l}")
        x = _mm(act, w["w_down"][l], "nn", f"down{l}", add=x1)
        s.update(h=h, proj=proj, kbd=kbd, vbd=vbd, mq_cb=mq_cb, cat=cat, x1=x1, h2=h2, gu=gu, act=act)
        saved.append(s)
        if l == N_A - 1:
            x_kv = x
            h_kv = _rms_fwd(x, row2(w["ln_kv"]), "norm_kv")
            kv = _mm(h_kv, w["w_kv"], "nn", "kv_proj")
            kr = _rope(kv, LANES, 0, tabs, 1, "rope_k", F32)
            kt = _pair_heads(kr).astype(BF16)
            vt = _pair_heads(kv[:, LANES:]).astype(BF16)

    gr = {}
    dx, dxb, loss_part, dlnf = _final_loss(x, row2(w["ln_final"]), target, "final_loss")
    gr["ln_final"] = dlnf.sum(axis=0)
    dln_mix, dln_ffn = [None] * DEPTH, [None] * DEPTH
    dw_mem_kv, dw_out, dw_gu, dw_dn = [None] * DEPTH, [None] * DEPTH, [None] * DEPTH, [None] * DEPTH
    dgdn_in, dgdn_conv, dgdn_a, dgdn_dt, dgdn_norm = [None] * N_A, [None] * N_A, [None] * N_A, [None] * N_A, [None] * N_A
    dswa_q, dswa_sinks = [None] * N_B, [None] * N_B
    dmem_n = None
    dkt = dvt = None
    for l in reversed(range(DEPTH)):
        s = saved[l]
        if l == N_A - 1:
            dkr = _unpair_heads(dkt)
            dk = _rope(dkr, LANES, 0, tabs, -1, "rope_k_bwd", BF16)
            dkv = jnp.concatenate([dk, _unpair_heads(dvt).astype(BF16)], axis=1)
            dh_kv = _mm(dkv, w["w_kv"], "nt", "kv_proj_dx")
            gr["w_kv"] = _mm(h_kv, dkv, "tn", "kv_proj_dw", BF16)
            dx, dxb, dg = _rms_bwd(x_kv, row2(w["ln_kv"]), dh_kv, dx, "norm_kv_bwd")
            gr["ln_kv"] = dg.sum(axis=0)
        dgu = _down_bwd(dxb, w["w_down"][l], s["gu"], f"down_dx{l}")
        dw_dn[l] = _mm(s["act"], dxb, "tn", f"down_dw{l}", BF16)
        dh2 = _mm(dgu, w_gu[l], "nt", f"gate_up_dx{l}")
        dw_gu[l] = _mm(s["h2"], dgu, "tn", f"gate_up_dw{l}", BF16)
        dx, dxb, dg = _rms_bwd(s["x1"], row2(w["ln_ffn"][l]), dh2, dx, f"norm_ffn_bwd{l}")
        dln_ffn[l] = dg.sum(axis=0)
        dcat = _mm(dxb, w["w_out"][l], "nt", f"out_proj_dx{l}", BF16)
        dw_out[l] = _mm(s["cat"], dxb, "tn", f"out_proj_dw{l}", BF16)
        dmq, dkbd, dvbd = _mem_bwd(s["proj"], s["mq_cb"], s["kbd"], s["vbd"], dcat, f"mem_attn_bwd{l}")
        dmkv = _mem_collapse(dkbd, dvbd).astype(BF16)
        dw_mem_kv[l] = _mm(mem_n, dmkv, "tn", f"mem_kv_dw{l}", BF16)
        dmem_n = _mm(dmkv, w["w_mem_kv"][l], "nt", f"mem_kv_dx{l}", add=dmem_n)
        if l < N_A:
            do, dz, dng = _gdn_post_bwd(s["o"], s["proj"], row2(w["gdn_norm"][l]), dcat, f"gdn_post_bwd{l}")
            dq, dk, dv, dg_, dbeta = _gdn_bwd(s["q"], s["k"], s["v"], s["gc"], s["beta"], s["states"], s["tinv"], s["gw"],
                                              s["vn"], do, f"gdn_scan_bwd{l}")
            res = _gdn_pre_bwd(s["proj"], w["gdn_conv"][l], sel_b, sel_a, s["a_row"], s["dt_row"], dq, dk, dv, dg_, dbeta,
                               f"gdn_pre_bwd{l}")
            dconv, dba = res[0], res[1]
            dgdn_conv[l] = jnp.stack([r.sum(axis=0) for r in res[2:2 + CONV_K]])
            dgdn_a[l] = res[2 + CONV_K].sum(axis=0)[::LANES]
            dgdn_dt[l] = res[3 + CONV_K].sum(axis=0)[::LANES]
            dgdn_norm[l] = dng.sum(axis=0)
            dqkv = _conv_bwd_input(dconv, w["gdn_conv"][l], f"gdn_conv_bwd{l}")
            dproj = jnp.concatenate([dqkv, dz, dmq, dba], axis=1)
            dh = _mm(dproj, w["gdn_w_in"][l], "nt", f"gdn_in_dx{l}")
            dgdn_in[l] = _mm(s["h"], dproj, "tn", f"gdn_in_dw{l}", BF16)
        else:
            b = l - N_A
            dqr, dkt_l, dvt_l, dsk = _swa_bwd(s["qr"], kt, vt, w["swa_sinks"][b], dcat, f"swa_bwd{l}")
            dkt = dkt_l if dkt is None else dkt + dkt_l
            dvt = dvt_l if dvt is None else dvt + dvt_l
            dswa_sinks[b] = dsk[0, :SWA_HEADS]
            dq = _rope(dqr, SWA_HEADS * SWA_DH, 0, tabs, -1, f"rope_q_bwd{l}", BF16)
            dproj = jnp.concatenate([dq, dmq], axis=1)
            dh = _mm(dproj, w["swa_w_q"][b], "nt", f"swa_in_dx{l}")
            dswa_q[b] = _mm(s["h"], dproj, "tn", f"swa_in_dw{l}", BF16)
        dx, dxb, dg = _rms_bwd(s["x0"], row2(w["ln_mix"][l]), dh, dx, f"norm_mix_bwd{l}")
        dln_mix[l] = dg.sum(axis=0)
    _, _, dg = _rms_bwd(mem, row2(w["ln_mem"]), dmem_n, None, "mem_norm_bwd")
    gr["ln_mem"] = dg.sum(axis=0)
    gr.update(ln_mix=jnp.stack(dln_mix), ln_ffn=jnp.stack(dln_ffn), w_mem_kv=jnp.stack(dw_mem_kv), w_out=jnp.stack(dw_out),
              w_gate_up=_ff_deinterleave(jnp.stack(dw_gu)), w_down=jnp.stack(dw_dn), gdn_w_in=jnp.stack(dgdn_in), gdn_conv=jnp.stack(dgdn_conv),
              gdn_A_log=jnp.stack(dgdn_a), gdn_dt_bias=jnp.stack(dgdn_dt), gdn_norm=jnp.stack(dgdn_norm),
              swa_w_q=jnp.stack(dswa_q), swa_sinks=jnp.stack(dswa_sinks))
    return loss_part, dx, gr


def kernel(x, mem, positions, ln_mix, ln_ffn, ln_mem, w_mem_kv, w_out, w_gate_up, w_down, gdn_w_in, gdn_conv, gdn_A_log, gdn_dt_bias, gdn_norm, swa_w_q, swa_sinks, ln_kv, w_kv, ln_final, loss_target, m_ln_mix, m_ln_ffn, m_ln_mem, m_w_mem_kv, m_w_out, m_w_gate_up, m_w_down, m_gdn_w_in, m_gdn_conv, m_gdn_A_log, m_gdn_dt_bias, m_gdn_norm, m_swa_w_q, m_swa_sinks, m_ln_kv, m_w_kv, m_ln_final, v_ln_mix, v_ln_ffn, v_ln_mem, v_w_mem_kv, v_w_out, v_w_gate_up, v_w_down, v_gdn_w_in, v_gdn_conv, v_gdn_A_log, v_gdn_dt_bias, v_gdn_norm, v_swa_w_q, v_swa_sinks, v_ln_kv, v_w_kv, v_ln_final):
    given = dict(locals())
    wts = {n: given[n] for n in WEIGHTS}
    c = lax.axis_index("c")

    halves = [_my_half(wts[n].astype(BF16), n, c) for n in SHARDED]
    half_shapes = [h.shape for h in halves]
    conv_shape = wts["gdn_conv"].shape
    cpack = _pack([wts["gdn_conv"]], 16)
    conv_half = lax.dynamic_slice_in_dim(cpack, c * SUBLANES, SUBLANES, axis=0)
    got = _exchange(halves + [conv_half], [True] * (len(halves) + 1), ALL_PEERS, "gather_weights")
    full = {n: wts[n] for n in SMALL}
    for n, g in zip(SHARDED, got):
        full[n] = _assemble(g, n)
    conv_all = got[-1].reshape(4, 16, D_MODEL)
    full["gdn_conv"] = jnp.concatenate([_unpack(conv_all[s], [conv_shape])[0] for s in range(4)], axis=2)
    full["gdn_w_in"] = _gdn_in_pad(full["gdn_w_in"])

    loss_part, dx, gr = _local_step(x[0], mem[0], positions[0], loss_target[0], full)
    gr["gdn_w_in"] = _gdn_in_unpad(gr["gdn_w_in"])

    pieces = [_to_pieces(gr[n].astype(BF16), n, hs) for n, hs in zip(SHARDED, half_shapes)]
    small_shapes = [wts[n].shape for n in SMALL] + [conv_shape[:2] + (4 * conv_shape[2],), (SUBLANES, LANES)]
    rows_s = _rows_for(small_shapes, SUBLANES)
    spack = _pack([gr[n] for n in SMALL] + [gr["gdn_conv"], loss_part], rows_s)
    parts = _exchange(pieces + [spack], [False] * len(pieces) + [True], ALL_PEERS, "scatter_grads")
    mine = [_sum_slots(p.reshape(N_DEV, -1, p.shape[-1]), f"sum_{n}").reshape(hs)
            for n, p, hs in zip(SHARDED, parts, half_shapes)]
    ssum = _unpack(_sum_slots(parts[-1], "sum_small"), small_shapes)
    theirs = _exchange(mine, [True] * len(mine), SIBLING, "swap_grad_halves", keep_own=False)
    g_all = dict(zip(SMALL, ssum[:len(SMALL)]))
    chip = 2 * lax.axis_index("x") + lax.axis_index("y")
    g_all["gdn_conv"] = lax.dynamic_slice_in_dim(ssum[len(SMALL)], chip * conv_shape[2], conv_shape[2], axis=2)
    loss = jnp.sum(ssum[-1])

    out = dict(grad=g_all, delta={}, new_m={}, new_v={})
    my_core = c.astype(jnp.int32).reshape(1)
    for n, own, other in zip(SHARDED, mine, theirs):
        as3d = lambda a: a.reshape((-1,) + a.shape[-2:])
        res = _adamw_halves(as3d(wts[n]), as3d(own), other.reshape((1, -1) + other.shape[-2:]), my_core,
                            as3d(given["m_" + n]), as3d(given["v_" + n]), f"adamw_{n}")
        for kind, r in zip(("grad", "delta", "new_m", "new_v"), res):
            out[kind][n] = r.reshape(wts[n].shape)
    small_names = SMALL + ("gdn_conv",)
    small_w_shapes = [wts[n].shape for n in small_names]
    rows_a = _rows_for(small_w_shapes, SUBLANES)
    res = _adamw(_pack([wts[n] for n in small_names], rows_a), _pack([g_all[n] for n in small_names], rows_a),
                 _pack([given["m_" + n] for n in small_names], rows_a),
                 _pack([given["v_" + n] for n in small_names], rows_a), "adamw_small")
    for kind, r in zip(("delta", "new_m", "new_v"), res):
        out[kind].update(zip(small_names, _unpack(r, small_w_shapes)))
    return (loss, dx[None], *[out["grad"][n] for n in WEIGHTS], *[out["delta"][n] for n in WEIGHTS],
            *[out["new_m"][n] for n in WEIGHTS], *[out["new_v"][n] for n in WEIGHTS])
```

```python
import functools
import math

import jax
import jax.numpy as jnp
from jax import lax
from jax.experimental import pallas as pl
from jax.experimental.pallas import tpu as pltpu

F32 = jnp.float32
BF16 = jnp.bfloat16
HI = lax.Precision.HIGHEST
MESH = pl.DeviceIdType.MESH

D_MODEL = 1024
DEPTH = 4
N_A = 2
N_B = 2
EPS = 1e-6
GDN_HEADS = 6
GDN_DK = 128
GDN_W = 768
CONV_K = 4
CHUNK = 64
SWA_HEADS = 12
SWA_KV_HEADS = 2
SWA_DH = 64
SWA_GROUP = 6
SWA_GW = SWA_GROUP * SWA_DH
SWA_BLOCK = 128
ROPE_THETA = 500000.0
ROT_DIM = 16
MEM_LEN = 256
MEM_HEADS = 4
MEM_DH = 64
MEM_W = 256
D_FF = 2816
GDN_IN = 3340
GDN_IN_PAD = 3456
ADAM_LR = 0.001
ADAM_B1 = 0.9
ADAM_B2 = 0.999
ADAM_EPS = 1e-08
ADAM_WD = 0.01
ADAM_STEP = 10

N_DEV = 8
LANES = 128
SUBLANES = 8
V7X_VMEM_LIMIT = 56 * 2**20
MM_VMEM_BUDGET = 44 * 2**20

SHARDED = ("w_mem_kv", "w_out", "w_gate_up", "w_down", "gdn_w_in", "swa_w_q", "w_kv")
SMALL = ("ln_mix", "ln_ffn", "ln_mem", "gdn_A_log", "gdn_dt_bias", "gdn_norm", "swa_sinks", "ln_kv", "ln_final")
WEIGHTS = ("ln_mix", "ln_ffn", "ln_mem", "w_mem_kv", "w_out", "w_gate_up", "w_down", "gdn_w_in", "gdn_conv",
           "gdn_A_log", "gdn_dt_bias", "gdn_norm", "swa_w_q", "swa_sinks", "ln_kv", "w_kv", "ln_final")


def _params(sem=None, **kw):
    return pltpu.CompilerParams(dimension_semantics=sem, vmem_limit_bytes=V7X_VMEM_LIMIT, **kw)


def _dot(a, b, dims=(((1,), (0,)), ((), ())), precision=None):
    return lax.dot_general(a, b, dims, precision=precision, preferred_element_type=F32)


NT = (((1,), (1,)), ((), ()))
TN = (((0,), (0,)), ((), ()))


def _fold8(v):
    r, w = v.shape
    return v.reshape(r // SUBLANES, SUBLANES, w).sum(axis=0)


def _row(a, w=None, cb=0):
    return ("row", a, a.shape[1] if w is None else w, cb)


def _full(a):
    return ("full", a, None, None)


def _prev8(a, w, cb=0):
    return ("prev8", a, w, cb)


def _next8(a, w, cb=0):
    return ("next8", a, w, cb)


def _rowcall(fn, name, rows, bm, ins, outs, accs=()):
    bm = min(bm, rows)
    assert rows % bm == 0 and bm % SUBLANES == 0
    steps = rows // bm
    r8 = bm // SUBLANES
    in_specs, arrays = [], []
    for kind, a, w, cb in ins:
        arrays.append(a)
        if kind == "row":
            in_specs.append(pl.BlockSpec((bm, w), lambda i, cb=cb: (i, cb)))
        elif kind == "full":
            in_specs.append(pl.BlockSpec(a.shape, lambda i, nd=a.ndim: (0,) * nd))
        elif kind == "prev8":
            in_specs.append(pl.BlockSpec((SUBLANES, w), lambda i, cb=cb: (jnp.maximum(i * r8 - 1, 0), cb)))
        else:
            last = rows // SUBLANES - 1
            in_specs.append(pl.BlockSpec((SUBLANES, w), lambda i, cb=cb: (jnp.minimum((i + 1) * r8, last), cb)))
    out_shape = [jax.ShapeDtypeStruct((rows, w), dt) for w, dt in outs]
    out_specs = [pl.BlockSpec((bm, w), lambda i: (i, 0)) for w, _ in outs]
    out_shape += [jax.ShapeDtypeStruct(s, F32) for s in accs]
    out_specs += [pl.BlockSpec(s, lambda i: (0, 0)) for s in accs]
    n_in, n_out = len(ins), len(outs)

    def body(*refs):
        i = pl.program_id(0)
        res = fn(i, *[r[...] for r in refs[:n_in]])
        if not isinstance(res, (tuple, list)):
            res = (res,)
        for r, v in zip(refs[n_in:n_in + n_out], res[:n_out]):
            r[...] = v.astype(r.dtype)
        if accs:
            @pl.when(i == 0)
            def _():
                for r in refs[n_in + n_out:]:
                    r[...] = jnp.zeros(r.shape, F32)
            for r, v in zip(refs[n_in + n_out:], res[n_out:]):
                r[...] += v

    res = pl.pallas_call(
        body, name=name, grid=(steps,), in_specs=in_specs, out_specs=out_specs, out_shape=out_shape,
        compiler_params=_params(("arbitrary",)))(*arrays)
    return res


def _tile(n, cap):
    for t in (1408, 1152, 1024, 896, 768, 640, 512, 384, 256, 128):
        if t <= cap and n % t == 0:
            return t
    return n


def _mm(a, b, mode, name, out_dtype=F32, add=None):
    if mode == "tn":
        s, m = a.shape
        n = b.shape[1]
        bm, bn, bk = _tile(m, 1408), _tile(n, 1408), min(s, 1024)
        nk = s // bk

        def body(a_ref, b_ref, o_ref, acc_ref):
            k = pl.program_id(2)

            @pl.when(k == 0)
            def _():
                acc_ref[...] = jnp.zeros(acc_ref.shape, F32)
            acc_ref[...] += _dot(a_ref[...].astype(BF16), b_ref[...].astype(BF16), TN)

            @pl.when(k == nk - 1)
            def _():
                o_ref[...] = acc_ref[...].astype(o_ref.dtype)

        return pl.pallas_call(
            body, name=name, grid=(m // bm, n // bn, nk),
            in_specs=[pl.BlockSpec((bk, bm), lambda i, j, k: (k, i)), pl.BlockSpec((bk, bn), lambda i, j, k: (k, j))],
            out_specs=pl.BlockSpec((bm, bn), lambda i, j, k: (i, j)),
            out_shape=jax.ShapeDtypeStruct((m, n), out_dtype),
            scratch_shapes=[pltpu.VMEM((bm, bn), F32)],
            compiler_params=_params(("parallel", "parallel", "arbitrary")))(a, b)

    m, k = a.shape
    n = b.shape[1] if mode == "nn" else b.shape[0]
    out_bytes = jnp.dtype(out_dtype).itemsize

    def vmem_need(bm, bn):
        need = 2 * bm * k * a.dtype.itemsize + 2 * bn * k * b.dtype.itemsize + bm * bn * (2 * out_bytes + 4)
        return need + (2 * bm * bn * 4 if add is not None else 0)

    bm, bn = min(m, 512), _tile(n, 512)
    for cand in ((2048, 1408), (2048, 1024), (2048, 512), (1024, 1408), (1024, 1024), (1024, 512)):
        tm, tn = min(m, cand[0]), _tile(n, cand[1])
        if m % tm == 0 and vmem_need(tm, tn) <= MM_VMEM_BUDGET:
            bm, bn = tm, tn
            break
    dims = NT if mode == "nt" else (((1,), (0,)), ((), ()))
    b_spec = (pl.BlockSpec((k, bn), lambda i, j: (0, j)) if mode == "nn" else pl.BlockSpec((bn, k), lambda i, j: (j, 0)))
    in_specs = [pl.BlockSpec((bm, k), lambda i, j: (i, 0)), b_spec]
    args = [a, b]
    if add is not None:
        in_specs.append(pl.BlockSpec((bm, bn), lambda i, j: (i, j)))
        args.append(add)

    def body(a_ref, b_ref, *rest):
        o_ref = rest[-1]
        acc = _dot(a_ref[...].astype(BF16), b_ref[...].astype(BF16), dims)
        if add is not None:
            acc = acc + rest[0][...]
        o_ref[...] = acc.astype(o_ref.dtype)

    return pl.pallas_call(
        body, name=name, grid=(m // bm, n // bn), in_specs=in_specs,
        out_specs=pl.BlockSpec((bm, bn), lambda i, j: (i, j)),
        out_shape=jax.ShapeDtypeStruct((m, n), out_dtype),
        compiler_params=_params(("parallel", "parallel")))(*args)


def _sigmoid(x):
    return 0.5 * jnp.tanh(0.5 * x) + 0.5


def _softplus(x):
    return jnp.maximum(x, 0.0) + jnp.log(1.0 + jnp.exp(-jnp.abs(x)))


def _silu_and_grad(x):
    s = _sigmoid(x)
    return x * s, s * (1.0 + x * (1.0 - s))


def _rms_stats(x):
    r = lax.rsqrt(jnp.mean(x * x, axis=-1, keepdims=True) + EPS)
    return r, x * r


def _rms_fwd(x, g, name, out_dtype=BF16, bm=512):
    def fn(i, x, g):
        _, xn = _rms_stats(x)
        return xn * g
    return _rowcall(fn, name, x.shape[0], bm, [_row(x), _full(g)], [(x.shape[1], out_dtype)])[0]


def _rms_bwd_math(x, g, dy):
    r, xn = _rms_stats(x)
    dxn = dy * g
    dx = r * (dxn - xn * jnp.mean(dxn * xn, axis=-1, keepdims=True))
    return dx, dy * xn


def _rms_bwd(x, g, dy, res, name, bm=256):
    d = x.shape[1]

    def fn(i, x, g, dy, *res_):
        dx, dg = _rms_bwd_math(x, g, dy.astype(F32))
        if res_:
            dx = dx + res_[0]
        return dx, dx, _fold8(dg)
    ins = [_row(x), _full(g), _row(dy)] + ([_row(res)] if res is not None else [])
    return _rowcall(fn, name, x.shape[0], bm, ins, [(d, F32), (d, BF16)], [(SUBLANES, d)])


def _final_loss(x, g, target, name, bm=256):
    d = x.shape[1]

    def fn(i, x, g, t):
        r, xn = _rms_stats(x)
        err = xn * g - t
        dy = err * (1.0 / d)
        dxn = dy * g
        dx = r * (dxn - xn * jnp.mean(dxn * xn, axis=-1, keepdims=True))
        e2 = _fold8(err * err)
        lp = e2[:, 0:LANES]
        for c in range(1, d // LANES):
            lp = lp + e2[:, c * LANES:(c + 1) * LANES]
        return dx, dx, lp * (0.5 / d), _fold8(dy * xn)
    return _rowcall(fn, name, x.shape[0], bm, [_row(x), _full(g), _row(target)], [(d, F32), (d, BF16)],
                    [(SUBLANES, LANES), (SUBLANES, d)])


FF_TILE = 256


def _ff_interleave(w):
    lead = w.shape[:-1]
    w = w.reshape(lead + (2, D_FF // FF_TILE, FF_TILE))
    return jnp.swapaxes(w, -3, -2).reshape(lead + (2 * D_FF,))


def _ff_deinterleave(w):
    lead = w.shape[:-1]
    w = w.reshape(lead + (D_FF // FF_TILE, 2, FF_TILE))
    return jnp.swapaxes(w, -3, -2).reshape(lead + (2 * D_FF,))


def _gate_up_fwd(h, w_gu, name, bm=2048):
    rows, k = h.shape
    bm = min(bm, rows)

    def body(a_ref, b_ref, gu_ref, act_ref):
        acc = _dot(a_ref[...], b_ref[...])
        gu_ref[...] = acc.astype(gu_ref.dtype)
        act_ref[...] = (_silu_and_grad(acc[:, :FF_TILE])[0] * acc[:, FF_TILE:]).astype(act_ref.dtype)

    return pl.pallas_call(
        body, name=name, grid=(rows // bm, D_FF // FF_TILE),
        in_specs=[pl.BlockSpec((bm, k), lambda i, j: (i, 0)), pl.BlockSpec((k, 2 * FF_TILE), lambda i, j: (0, j))],
        out_specs=[pl.BlockSpec((bm, 2 * FF_TILE), lambda i, j: (i, j)), pl.BlockSpec((bm, FF_TILE), lambda i, j: (i, j))],
        out_shape=[jax.ShapeDtypeStruct((rows, 2 * D_FF), BF16), jax.ShapeDtypeStruct((rows, D_FF), BF16)],
        compiler_params=_params(("parallel", "parallel")))(h, w_gu)


def _down_bwd(dx, w_down, gu, name, bm=2048):
    rows, k = dx.shape
    bm = min(bm, rows)

    def body(a_ref, b_ref, gu_ref, o_ref):
        da = _dot(a_ref[...], b_ref[...], NT)
        gu = gu_ref[...].astype(F32)
        s, ds = _silu_and_grad(gu[:, :FF_TILE])
        o_ref[:, :FF_TILE] = (da * gu[:, FF_TILE:] * ds).astype(o_ref.dtype)
        o_ref[:, FF_TILE:] = (da * s).astype(o_ref.dtype)

    return pl.pallas_call(
        body, name=name, grid=(rows // bm, D_FF // FF_TILE),
        in_specs=[pl.BlockSpec((bm, k), lambda i, j: (i, 0)), pl.BlockSpec((FF_TILE, k), lambda i, j: (j, 0)),
                  pl.BlockSpec((bm, 2 * FF_TILE), lambda i, j: (i, j))],
        out_specs=pl.BlockSpec((bm, 2 * FF_TILE), lambda i, j: (i, j)),
        out_shape=jax.ShapeDtypeStruct((rows, 2 * D_FF), BF16),
        compiler_params=_params(("parallel", "parallel")))(dx, w_down, gu)


def _rope_apply(x, tabs, sign):
    cos, ta, tb = tabs
    outs = []
    for c in range(x.shape[1] // LANES):
        xc = x[:, c * LANES:(c + 1) * LANES]
        if sign > 0:
            o = xc * cos + pltpu.roll(xc, LANES - 8, 1) * ta + pltpu.roll(xc, 8, 1) * tb
        else:
            o = xc * cos + pltpu.roll(xc * ta, 8, 1) + pltpu.roll(xc * tb, LANES - 8, 1)
        outs.append(o)
    return outs[0] if len(outs) == 1 else jnp.concatenate(outs, axis=1)


def _rope(x, w, cb, tabs, sign, name, out_dtype, bm=512):
    def fn(i, x, c, a, b):
        return _rope_apply(x.astype(F32), (c, a, b), sign)
    return _rowcall(fn, name, x.shape[0], bm, [_row(x, w, cb)] + [_row(t) for t in tabs], [(w, out_dtype)])[0]


def _shift_down(x, prev8, s, first):
    xs = pltpu.roll(x, s, 0)
    rp = pltpu.roll(prev8, s, 0) * jnp.where(first, 0.0, 1.0)
    rid = lax.broadcasted_iota(jnp.int32, rp.shape, 0)
    top = jnp.where(rid < s, rp, xs[0:SUBLANES])
    return jnp.concatenate([top, xs[SUBLANES:]], axis=0)


def _shift_up(x, next8, s, last):
    n = x.shape[0]
    xs = pltpu.roll(x, n - s, 0)
    rn = pltpu.roll(next8, SUBLANES - s, 0) * jnp.where(last, 0.0, 1.0)
    rid = lax.broadcasted_iota(jnp.int32, rn.shape, 0)
    bot = jnp.where(rid >= SUBLANES - s, rn, xs[n - SUBLANES:])
    return jnp.concatenate([xs[:n - SUBLANES], bot], axis=0)


def _conv_fwd(x, prev8, w, first):
    acc = x * w[CONV_K - 1:CONV_K]
    shifted = []
    for s in range(1, CONV_K):
        xs = _shift_down(x, prev8, s, first)
        shifted.append(xs)
        acc = acc + xs * w[CONV_K - 1 - s:CONV_K - s]
    return acc, shifted


def _l2n(x):
    outs, rs = [], []
    for h in range(x.shape[1] // LANES):
        xh = x[:, h * LANES:(h + 1) * LANES]
        r = lax.rsqrt(jnp.sum(xh * xh, axis=-1, keepdims=True) + EPS)
        outs.append(xh * r)
        rs.append(r)
    return jnp.concatenate(outs, axis=1), rs


def _split3(x):
    hi = x.astype(BF16)
    r = x - hi.astype(F32)
    mid = r.astype(BF16)
    return hi, mid, (r - mid.astype(F32)).astype(BF16)


def _gate_math(ba, a_row, dt_row):
    al = ba + dt_row
    ea = jnp.exp(a_row)
    return _sigmoid(ba), al, ea, -ea * _softplus(al)


def _spread(x, sel):
    return sum(_dot(part, sel) for part in _split3(x))


def _gather_heads(x, sel):
    return sum(_dot(part, sel, NT) for part in _split3(x)) * (1.0 / LANES)


def _cumsum_chunks(x, reverse=False):
    n = x.shape[0]
    rid = lax.broadcasted_iota(jnp.int32, x.shape, 0) % CHUNK
    s = 1
    while s < CHUNK:
        if reverse:
            x = x + jnp.where(rid < CHUNK - s, pltpu.roll(x, n - s, 0), 0.0)
        else:
            x = x + jnp.where(rid >= s, pltpu.roll(x, s, 0), 0.0)
        s *= 2
    return x


def _gdn_pre_fwd(proj, conv_w, sel_b, sel_a, a_row, dt_row, name, bm=256):
    rows = proj.shape[0]
    w3 = 3 * GDN_W

    def fn(i, x, p8, ba, w, sel_b, sel_a, a_row, dt_row):
        conv, _ = _conv_fwd(x, p8, w, i == 0)
        act = _silu_and_grad(conv)[0]
        qk, _ = _l2n(act[:, :2 * GDN_W])
        beta, _, _, g = _gate_math(ba, a_row, dt_row)
        return (qk[:, :GDN_W], qk[:, GDN_W:], act[:, 2 * GDN_W:], _spread(_cumsum_chunks(g), sel_a),
                _spread(beta, sel_b))
    ins = [_row(proj, w3, 0), _prev8(proj, w3, 0), _row(proj, LANES, (GDN_IN_PAD - LANES) // LANES),
           _full(conv_w), _full(sel_b), _full(sel_a), _full(a_row), _full(dt_row)]
    return _rowcall(fn, name, rows, bm, ins, [(GDN_W, F32)] * 5)


def _gdn_pre_bwd(proj, conv_w, sel_b, sel_a, a_row, dt_row, dq, dk, dv, dgc, dbeta, name, bm=128):
    rows = proj.shape[0]
    w3 = 3 * GDN_W

    def fn(i, x, p8, ba, w, sel_b, sel_a, a_row, dt_row, dq, dk, dv, dgc, dbeta):
        conv, shifted = _conv_fwd(x, p8, w, i == 0)
        act, dact = _silu_and_grad(conv)
        qk, rs = _l2n(act[:, :2 * GDN_W])
        dqk = jnp.concatenate([dq, dk], axis=1)
        parts = []
        for h in range(2 * GDN_HEADS):
            sl = slice(h * LANES, (h + 1) * LANES)
            y, dy = qk[:, sl], dqk[:, sl]
            parts.append(rs[h] * (dy - y * jnp.sum(y * dy, axis=-1, keepdims=True)))
        dconv = jnp.concatenate(parts + [dv], axis=1) * dact
        dws = [_fold8(dconv * xs) for xs in reversed(shifted)] + [_fold8(dconv * x)]
        beta, al, ea, g = _gate_math(ba, a_row, dt_row)
        dg = _cumsum_chunks(_gather_heads(dgc, sel_a), reverse=True)
        dbl = _gather_heads(dbeta, sel_b) * beta * (1.0 - beta)
        dal = dg * (-ea) * _sigmoid(al)
        return (dconv, dbl + dal) + tuple(dws) + (_fold8(dg * g), _fold8(dal))
    ins = [_row(proj, w3, 0), _prev8(proj, w3, 0), _row(proj, LANES, (GDN_IN_PAD - LANES) // LANES),
           _full(conv_w), _full(sel_b), _full(sel_a), _full(a_row), _full(dt_row),
           _row(dq), _row(dk), _row(dv), _row(dgc), _row(dbeta)]
    return _rowcall(fn, name, rows, bm, ins, [(w3, F32), (LANES, BF16)],
                    [(SUBLANES, w3)] * CONV_K + [(SUBLANES, LANES)] * 2)


def _conv_bwd_input(dconv, conv_w, name, bm=256):
    rows, w3 = dconv.shape
    steps = rows // min(bm, rows)

    def fn(i, dc, n8, w):
        acc = dc * w[CONV_K - 1:CONV_K]
        for s in range(1, CONV_K):
            acc = acc + _shift_up(dc, n8, s, i == steps - 1) * w[CONV_K - 1 - s:CONV_K - s]
        return acc
    return _rowcall(fn, name, rows, bm, [_row(dconv), _next8(dconv, w3, 0), _full(conv_w)], [(w3, BF16)])[0]


def _gdn_post_fwd(o, proj, ng, name, bm=512):
    def fn(i, o, z, ng):
        outs = []
        for h in range(GDN_HEADS):
            sl = slice(h * LANES, (h + 1) * LANES)
            _, on = _rms_stats(o[:, sl])
            outs.append(on * ng * _silu_and_grad(z[:, sl])[0])
        return jnp.concatenate(outs, axis=1)
    return _rowcall(fn, name, o.shape[0], bm, [_row(o), _row(proj, GDN_W, 3), _full(ng)], [(GDN_W, BF16)])[0]


def _gdn_post_bwd(o, proj, ng, dcat, name, bm=256):
    def fn(i, o, z, ng, dm):
        dm = dm.astype(F32)
        dos, dzs = [], []
        dng = jnp.zeros((SUBLANES, LANES), F32)
        for h in range(GDN_HEADS):
            sl = slice(h * LANES, (h + 1) * LANES)
            s, ds = _silu_and_grad(z[:, sl])
            r, on = _rms_stats(o[:, sl])
            dzs.append(dm[:, sl] * on * ng * ds)
            dy = dm[:, sl] * s
            dxn = dy * ng
            dos.append(r * (dxn - on * jnp.mean(dxn * on, axis=-1, keepdims=True)))
            dng = dng + _fold8(dy * on)
        return jnp.concatenate(dos, axis=1), jnp.concatenate(dzs, axis=1), dng
    return _rowcall(fn, name, o.shape[0], bm, [_row(o), _row(proj, GDN_W, 3), _full(ng), _row(dcat, GDN_W, 0)],
                    [(GDN_W, F32), (GDN_W, BF16)], [(SUBLANES, LANES)])


def _bdot(a, b, mode="nn"):
    lc, rc = {"nn": (2, 1), "nt": (2, 2), "tn": (1, 1)}[mode]
    return lax.dot_general(a, b, (((lc,), (rc,)), ((0,), (0,))), preferred_element_type=F32)


def _bdot3(a, b, mode="nn"):
    ah, bh = a.astype(BF16), b.astype(BF16)
    al, bl = (a - ah.astype(F32)).astype(BF16), (b - bh.astype(F32)).astype(BF16)
    return _bdot(ah, bh, mode) + _bdot(ah, bl, mode) + _bdot(al, bh, mode)


GDN_CB = 4


def _gdn_chunk(q, k, v, gc, beta, t=None):
    c = CHUNK
    nb = q.shape[0]
    row = lax.broadcasted_iota(jnp.int32, (c, c), 0)
    col = lax.broadcasted_iota(jnp.int32, (c, c), 1)
    tril, strict = row >= col, row > col
    lane0 = (lax.broadcasted_iota(jnp.int32, (nb, c, LANES), 2) == 0).astype(BF16)
    gc_row = sum(_bdot(lane0, part, "nt") for part in _split3(gc))
    dm = jnp.exp(jnp.where(tril, gc[:, :, :c] - gc_row, -1e30))
    eg = jnp.exp(gc)
    gcl = gc[:, c - 1:c, :]
    ekg = jnp.exp(gcl - gc)
    egl = jnp.exp(gcl)
    qs = q * (GDN_DK ** -0.5)
    kb = k * beta
    kk = _bdot(kb, k, "nt")
    a = jnp.where(strict, kk * dm, 0.0)
    vb = v * beta
    kbg = kb * eg
    qk = _bdot(qs, k, "nt")
    p = jnp.where(tril, qk * dm, 0.0)
    out = dict(tril=tril, strict=strict, dm=dm, eg=eg, ekg=ekg, egl=egl, qs=qs, kb=kb, kk=kk, a=a,
               vb=vb, kbg=kbg, qk=qk, p=p, qg=qs * eg, kg=k * ekg)
    if t is None:
        y = -a
        t = (row == col).astype(F32) + y
        for _ in range(5):
            y = _bdot3(y, y)
            t = t + _bdot3(t, y)
        out.update(u=_bdot3(t, vb), w=_bdot3(t, kbg))
    out["t"] = t
    return out


def _gdn_stack(ref):
    return jnp.stack([ref[c * CHUNK:(c + 1) * CHUNK, h * LANES:(h + 1) * LANES]
                      for c in range(GDN_CB) for h in range(GDN_HEADS)])


def _gdn_unstack(x, ref):
    for c in range(GDN_CB):
        for h in range(GDN_HEADS):
            ref[c * CHUNK:(c + 1) * CHUNK, h * LANES:(h + 1) * LANES] = x[c * GDN_HEADS + h]


def _gdn_fwd(q, k, v, gc, beta, name):
    rows = q.shape[0]
    n_chunks = rows // CHUNK
    steps = n_chunks // GDN_CB
    blk = pl.BlockSpec((GDN_CB * CHUNK, GDN_W), lambda n: (n, 0))
    st = pl.BlockSpec((GDN_HEADS, GDN_CB, GDN_DK, LANES), lambda n: (0, n, 0, 0))
    tinv = pl.BlockSpec((GDN_CB, GDN_HEADS, CHUNK, CHUNK), lambda n: (n, 0, 0, 0))

    def body(q_ref, k_ref, v_ref, g_ref, b_ref, o_ref, st_ref, t_ref, w_ref, vn_ref, s_ref):
        @pl.when(pl.program_id(0) == 0)
        def _():
            s_ref[...] = jnp.zeros(s_ref.shape, F32)
        c = _gdn_chunk(*[_gdn_stack(r) for r in (q_ref, k_ref, v_ref, g_ref, b_ref)])
        _gdn_unstack(c["w"], w_ref)
        s = s_ref[...]
        for i in range(GDN_CB):
            hs = slice(i * GDN_HEADS, (i + 1) * GDN_HEADS)
            rs = slice(i * CHUNK, (i + 1) * CHUNK)
            st_ref[:, i] = s
            vn = c["u"][hs] - _bdot(c["w"][hs], s)
            o = _bdot(c["qg"][hs], s) + _bdot(c["p"][hs], vn)
            s = s * c["egl"][hs] + _bdot(c["kg"][hs], vn, "tn")
            t_ref[i] = c["t"][hs]
            for h in range(GDN_HEADS):
                o_ref[rs, h * LANES:(h + 1) * LANES] = o[h]
                vn_ref[rs, h * LANES:(h + 1) * LANES] = vn[h]
        s_ref[...] = s

    f = jax.ShapeDtypeStruct((rows, GDN_W), F32)
    return pl.pallas_call(
        body, name=name, grid=(steps,), in_specs=[blk] * 5, out_specs=[blk, st, tinv, blk, blk],
        out_shape=[f, jax.ShapeDtypeStruct((GDN_HEADS, n_chunks, GDN_DK, LANES), F32),
                   jax.ShapeDtypeStruct((n_chunks, GDN_HEADS, CHUNK, CHUNK), F32), f, f],
        scratch_shapes=[pltpu.VMEM((GDN_HEADS, GDN_DK, LANES), F32)],
        compiler_params=_params(("arbitrary",)))(q, k, v, gc, beta)


def _gdn_bwd(q, k, v, gc, beta, states, tinv, w, vn, do, name):
    rows = q.shape[0]
    n_chunks = rows // CHUNK
    steps = n_chunks // GDN_CB
    blk = pl.BlockSpec((GDN_CB * CHUNK, GDN_W), lambda n: (steps - 1 - n, 0))
    st = pl.BlockSpec((GDN_HEADS, GDN_CB, GDN_DK, LANES), lambda n: (0, steps - 1 - n, 0, 0))
    ti = pl.BlockSpec((GDN_CB, GDN_HEADS, CHUNK, CHUNK), lambda n: (steps - 1 - n, 0, 0, 0))
    nbatch = GDN_CB * GDN_HEADS

    def lanesum(x):
        return jnp.broadcast_to(jnp.sum(x, axis=-1, keepdims=True), x.shape)

    def body(q_ref, k_ref, v_ref, g_ref, b_ref, st_ref, t_ref, w_ref, vn_ref, do_ref,
             dq_ref, dk_ref, dv_ref, dg_ref, db_ref, ds_ref):
        @pl.when(pl.program_id(0) == 0)
        def _():
            ds_ref[...] = jnp.zeros(ds_ref.shape, F32)
        q, k, v, gc, beta, w, vn, do = [_gdn_stack(r) for r in (q_ref, k_ref, v_ref, g_ref, b_ref, w_ref, vn_ref, do_ref)]
        t = t_ref[...].reshape(nbatch, CHUNK, CHUNK)
        s = jnp.stack([st_ref[h, i] for i in range(GDN_CB) for h in range(GDN_HEADS)])
        c = _gdn_chunk(q, k, v, gc, beta, t)
        tril, strict, dm = c["tril"], c["strict"], c["dm"]
        dsn = ds_ref[...]
        dvn_c, dkg_c, dgl_c = [None] * GDN_CB, [None] * GDN_CB, [None] * GDN_CB
        for i in reversed(range(GDN_CB)):
            hs = slice(i * GDN_HEADS, (i + 1) * GDN_HEADS)
            dvn_c[i] = _bdot(c["p"][hs], do[hs], "tn") + _bdot(c["kg"][hs], dsn)
            dkg_c[i] = _bdot(vn[hs], dsn, "nt")
            dgl_c[i] = jnp.sum(jnp.sum(s[hs] * dsn, axis=2, keepdims=True), axis=1, keepdims=True) * c["egl"][hs]
            dsn = dsn * c["egl"][hs] + _bdot(c["qg"][hs], do[hs], "tn") - _bdot(w[hs], dvn_c[i], "tn")
        ds_ref[...] = dsn
        dvn, dkg, dgl = jnp.concatenate(dvn_c), jnp.concatenate(dkg_c), jnp.concatenate(dgl_c)
        dp = jnp.where(tril, _bdot(do, vn, "nt"), 0.0)
        dqg = _bdot(do, s, "nt")
        dw = -_bdot(dvn, s, "nt")
        dvb = _bdot3(t, dvn, "tn")
        dkbg = _bdot3(t, dw, "tn")
        dt = _bdot(dvn, c["vb"], "nt") + _bdot(dw, c["kbg"], "nt")
        da = jnp.where(strict, -_bdot3(_bdot3(t, dt, "tn"), t, "nt"), 0.0)
        dkk = da * dm
        dqk = dp * dm
        dkb = _bdot(dkk, k) + dkbg * c["eg"]
        dk = _bdot(dkk, c["kb"], "tn") + _bdot(dqk, c["qs"], "tn") + dkg * c["ekg"] + dkb * beta
        dqs = _bdot(dqk, k) + dqg * c["eg"]
        e = da * c["a"] + dp * c["p"]
        ones = jnp.ones((nbatch, CHUNK, LANES), BF16)
        col_sums = sum(_bdot(part, ones, "tn") for part in _split3(e))
        kg_term = lanesum(dkg * c["kg"])
        dgc = (jnp.broadcast_to(jnp.sum(e, axis=-1, keepdims=True), (nbatch, CHUNK, LANES)) - col_sums
               + lanesum(dqg * c["qg"]) - kg_term + lanesum(dkbg * c["kbg"]))
        dgcl = jnp.sum(kg_term, axis=1, keepdims=True) + dgl
        last = lax.broadcasted_iota(jnp.int32, (CHUNK, LANES), 0) == CHUNK - 1
        _gdn_unstack(dqs * (GDN_DK ** -0.5), dq_ref)
        _gdn_unstack(dk, dk_ref)
        _gdn_unstack(dvb * beta, dv_ref)
        _gdn_unstack(dgc + jnp.where(last, dgcl, 0.0), dg_ref)
        _gdn_unstack(lanesum(dvb * v) + lanesum(dkb * k), db_ref)

    return pl.pallas_call(
        body, name=name, grid=(steps,), in_specs=[blk] * 5 + [st, ti, blk, blk, blk], out_specs=[blk] * 5,
        out_shape=[jax.ShapeDtypeStruct((rows, GDN_W), F32)] * 5,
        scratch_shapes=[pltpu.VMEM((GDN_HEADS, GDN_DK, LANES), F32)],
        compiler_params=_params(("arbitrary",)))(q, k, v, gc, beta, states, tinv, w, vn, do)


def _swa_masks(first):
    r = lax.broadcasted_iota(jnp.int32, (SWA_BLOCK, 2 * SWA_BLOCK), 0)
    c = lax.broadcasted_iota(jnp.int32, (SWA_BLOCK, 2 * SWA_BLOCK), 1)
    band = (c > r) & (c <= r + SWA_BLOCK)
    return band & (jnp.logical_not(first) | (c >= SWA_BLOCK))


def _swa_stack(ref, j):
    lane = lax.broadcasted_iota(jnp.int32, (1, LANES), 1)
    parts = []
    for g in range(SWA_GROUP):
        ch = j * (SWA_GROUP // 2) + g // 2
        keep = (lane < SWA_DH) if g % 2 == 0 else (lane >= SWA_DH)
        parts.append(ref[:, ch * LANES:(ch + 1) * LANES] * keep.astype(ref.dtype))
    return jnp.concatenate(parts, axis=0)


def _swa_unstack(x2, j, out_ref):
    low = lax.broadcasted_iota(jnp.int32, (SWA_BLOCK, LANES), 1) < SWA_DH
    for c3 in range(SWA_GROUP // 2):
        even = x2[(2 * c3) * SWA_BLOCK:(2 * c3 + 1) * SWA_BLOCK]
        odd = x2[(2 * c3 + 1) * SWA_BLOCK:(2 * c3 + 2) * SWA_BLOCK]
        ch = j * (SWA_GROUP // 2) + c3
        out_ref[:, ch * LANES:(ch + 1) * LANES] = jnp.where(low, even, odd).astype(out_ref.dtype)


def _swa_probs(s, sink, mask):
    s = jnp.where(mask, s, -1e30)
    m = jnp.maximum(jnp.max(s, axis=-1, keepdims=True), sink)
    p = jnp.where(mask, jnp.exp(s - m), 0.0)
    es = jnp.exp(sink - m)
    inv = 1.0 / (jnp.sum(p, axis=-1, keepdims=True) + es)
    return p * inv, es * inv


def _swa_scores(q_ref, kc_ref, kp_ref, sink_ref, j, mask):
    sl = slice(j * LANES, (j + 1) * LANES)
    qst = _swa_stack(q_ref, j)
    kw = jnp.concatenate([kp_ref[:, sl], kc_ref[:, sl]], axis=0)
    s = _dot(qst, kw, NT) * (SWA_DH ** -0.5)
    ps = [_swa_probs(s[g * SWA_BLOCK:(g + 1) * SWA_BLOCK], sink_ref[j * SWA_GROUP + g], mask)
          for g in range(SWA_GROUP)]
    return qst, kw, ps


def _swa_fwd(q, k2, v2, sinks, name):
    rows = q.shape[0]
    nb = rows // SWA_BLOCK
    w = SWA_HEADS * SWA_DH
    kvw = SWA_KV_HEADS * LANES
    cur = pl.BlockSpec((SWA_BLOCK, w), lambda i: (i, 0))
    kcur = pl.BlockSpec((SWA_BLOCK, kvw), lambda i: (i, 0))
    kprev = pl.BlockSpec((SWA_BLOCK, kvw), lambda i: (jnp.maximum(i - 1, 0), 0))

    def body(sink_ref, q_ref, kc_ref, kp_ref, vc_ref, vp_ref, o_ref):
        mask = _swa_masks(pl.program_id(0) == 0)
        for j in range(SWA_KV_HEADS):
            sl = slice(j * LANES, (j + 1) * LANES)
            _, _, ps = _swa_scores(q_ref, kc_ref, kp_ref, sink_ref, j, mask)
            vw = jnp.concatenate([vp_ref[:, sl], vc_ref[:, sl]], axis=0)
            pst = jnp.concatenate([p.astype(BF16) for p, _ in ps], axis=0)
            _swa_unstack(_dot(pst, vw), j, o_ref)

    return pl.pallas_call(
        body, name=name, grid=(nb,),
        in_specs=[pl.BlockSpec(memory_space=pltpu.SMEM), cur, kcur, kprev, kcur, kprev], out_specs=cur,
        out_shape=jax.ShapeDtypeStruct((rows, w), BF16),
        compiler_params=_params(("arbitrary",)))(sinks, q, k2, k2, v2, v2)


def _swa_bwd(q, k2, v2, sinks, dcat, name):
    rows = q.shape[0]
    nb = rows // SWA_BLOCK
    w = SWA_HEADS * SWA_DH
    kvw = SWA_KV_HEADS * LANES
    cur = pl.BlockSpec((SWA_BLOCK, w), lambda i: (jnp.minimum(i, nb - 1), 0))
    kcur = pl.BlockSpec((SWA_BLOCK, kvw), lambda i: (jnp.minimum(i, nb - 1), 0))
    kprev = pl.BlockSpec((SWA_BLOCK, kvw), lambda i: (jnp.clip(i - 1, 0, nb - 1), 0))
    late = pl.BlockSpec((SWA_BLOCK, kvw), lambda i: (jnp.maximum(i - 1, 0), 0))
    acc_spec = pl.BlockSpec((SUBLANES, LANES), lambda i: (0, 0))

    def body(sink_ref, q_ref, kc_ref, kp_ref, vc_ref, vp_ref, do_ref, dq_ref, dk_ref, dv_ref, dsk_ref,
             ck_ref, cv_ref):
        i = pl.program_id(0)

        @pl.when(i == 0)
        def _():
            ck_ref[...] = jnp.zeros(ck_ref.shape, F32)
            cv_ref[...] = jnp.zeros(cv_ref.shape, F32)
            dsk_ref[...] = jnp.zeros(dsk_ref.shape, F32)

        @pl.when(i == nb)
        def _():
            dk_ref[...] = ck_ref[...]
            dv_ref[...] = cv_ref[...]

        @pl.when(i < nb)
        def _():
            mask = _swa_masks(i == 0)
            lane = lax.broadcasted_iota(jnp.int32, (SUBLANES, LANES), 1)
            dsk = jnp.zeros((SUBLANES, LANES), F32)
            for j in range(SWA_KV_HEADS):
                sl = slice(j * LANES, (j + 1) * LANES)
                qst, kw, ps = _swa_scores(q_ref, kc_ref, kp_ref, sink_ref, j, mask)
                vw = jnp.concatenate([vp_ref[:, sl], vc_ref[:, sl]], axis=0)
                dost = _swa_stack(do_ref, j)
                dpr = _dot(dost, vw, NT)
                dss = []
                for g in range(SWA_GROUP):
                    p, sink_p = ps[g]
                    dpg = dpr[g * SWA_BLOCK:(g + 1) * SWA_BLOCK]
                    delta = jnp.sum(p * dpg, axis=-1, keepdims=True)
                    dss.append((p * (dpg - delta)).astype(BF16))
                    dsg = jnp.sum(-sink_p * delta, axis=0, keepdims=True)
                    dsk = dsk + jnp.where(lane == j * SWA_GROUP + g, dsg, 0.0)
                dsst = jnp.concatenate(dss, axis=0)
                pst = jnp.concatenate([p.astype(BF16) for p, _ in ps], axis=0)
                _swa_unstack(_dot(dsst, kw) * (SWA_DH ** -0.5), j, dq_ref)
                dk = _dot(dsst, qst, TN) * (SWA_DH ** -0.5)
                dv = _dot(pst, dost, TN)
                dk = dk + pltpu.roll(dk, SWA_DH, 1)
                dv = dv + pltpu.roll(dv, SWA_DH, 1)
                dk_ref[:, sl] = ck_ref[:, sl] + dk[:SWA_BLOCK]
                dv_ref[:, sl] = cv_ref[:, sl] + dv[:SWA_BLOCK]
                ck_ref[:, sl] = dk[SWA_BLOCK:]
                cv_ref[:, sl] = dv[SWA_BLOCK:]
            dsk_ref[...] += dsk

    f = jax.ShapeDtypeStruct((rows, kvw), F32)
    return pl.pallas_call(
        body, name=name, grid=(nb + 1,),
        in_specs=[pl.BlockSpec(memory_space=pltpu.SMEM), cur, kcur, kprev, kcur, kprev, cur],
        out_specs=[cur, late, late, acc_spec],
        out_shape=[jax.ShapeDtypeStruct((rows, w), F32), f, f, jax.ShapeDtypeStruct((SUBLANES, LANES), F32)],
        scratch_shapes=[pltpu.VMEM((SWA_BLOCK, kvw), F32), pltpu.VMEM((SWA_BLOCK, kvw), F32)],
        compiler_params=_params(("arbitrary",)))(sinks, q, k2, k2, v2, v2, dcat)


def _mem_probs(mq, kbd):
    s = _dot(mq.astype(BF16), kbd) * (MEM_DH ** -0.5)
    ps = []
    for h in range(MEM_HEADS):
        sh = s[:, h * MEM_LEN:(h + 1) * MEM_LEN]
        e = jnp.exp(sh - jnp.max(sh, axis=-1, keepdims=True))
        ps.append(e / jnp.sum(e, axis=-1, keepdims=True))
    return ps


def _mem_fwd(proj, cb, kbd, vbd, name, bm=512):
    def fn(i, mq, kbd, vbd):
        p = jnp.concatenate(_mem_probs(mq, kbd), axis=1)
        return _dot(p.astype(BF16), vbd)
    return _rowcall(fn, name, proj.shape[0], bm, [_row(proj, MEM_W, cb), _full(kbd), _full(vbd)], [(MEM_W, BF16)])[0]


def _mem_bwd(proj, cb, kbd, vbd, dcat, name, bm=512):
    def fn(i, mq, kbd, vbd, do):
        ps = _mem_probs(mq, kbd)
        dp = _dot(do, vbd, NT)
        dss = []
        for h in range(MEM_HEADS):
            dph = dp[:, h * MEM_LEN:(h + 1) * MEM_LEN]
            dss.append(ps[h] * (dph - jnp.sum(ps[h] * dph, axis=-1, keepdims=True)))
        ds = (jnp.concatenate(dss, axis=1) * (MEM_DH ** -0.5)).astype(BF16)
        p = jnp.concatenate(ps, axis=1).astype(BF16)
        return _dot(ds, kbd, NT), _dot(mq.astype(BF16), ds, TN), _dot(p, do, TN)
    return _rowcall(fn, name, proj.shape[0], bm, [_row(proj, MEM_W, cb), _full(kbd), _full(vbd), _row(dcat, MEM_W, 3)],
                    [(MEM_W, BF16)], [(MEM_W, MEM_HEADS * MEM_LEN), (MEM_HEADS * MEM_LEN, MEM_W)])


def _mem_expand(mkv):
    feat_head = jnp.arange(MEM_W) // MEM_DH
    slot_head = jnp.arange(MEM_HEADS * MEM_LEN) // MEM_LEN
    on = feat_head[:, None] == slot_head[None, :]
    kbd = jnp.where(on, jnp.tile(mkv[:, :MEM_W].T, (1, MEM_HEADS)), 0.0)
    vbd = jnp.where(on.T, jnp.tile(mkv[:, MEM_W:], (MEM_HEADS, 1)), 0.0)
    return kbd.astype(BF16), vbd.astype(BF16)


def _mem_collapse(dkbd, dvbd):
    dk = [dkbd[h * MEM_DH:(h + 1) * MEM_DH, h * MEM_LEN:(h + 1) * MEM_LEN].T for h in range(MEM_HEADS)]
    dv = [dvbd[h * MEM_LEN:(h + 1) * MEM_LEN, h * MEM_DH:(h + 1) * MEM_DH] for h in range(MEM_HEADS)]
    return jnp.concatenate(dk + dv, axis=1)


def _adamw_math(w, g, m, v):
    m = ADAM_B1 * m + (1.0 - ADAM_B1) * g
    v = ADAM_B2 * v + (1.0 - ADAM_B2) * (g * g)
    m_hat = m / (1.0 - ADAM_B1 ** ADAM_STEP)
    v_hat = v / (1.0 - ADAM_B2 ** ADAM_STEP)
    return -ADAM_LR * (m_hat / (jnp.sqrt(v_hat) + ADAM_EPS) + ADAM_WD * w), m, v


def _adamw(w, g, m, v, name, bm=512):
    d = w.shape[1]
    return _rowcall(lambda i, *a: _adamw_math(*a), name, w.shape[0], bm, [_row(w), _row(g), _row(m), _row(v)], [(d, F32)] * 3)


def _adamw_halves(w, g_own, g_other, my_core, m, v, name):
    layers, rows, n = w.shape
    r = rows // 2
    bm = LANES if r % LANES == 0 else r
    per_half = r // bm
    nat = pl.BlockSpec((None, bm, n), lambda l, c, i: (l, c * per_half + i, 0))
    own = pl.BlockSpec((None, bm, n), lambda l, c, i: (l, i, 0))
    other = pl.BlockSpec((None, None, bm, n), lambda l, c, i: (0, l, i, 0))

    def body(core_ref, w_ref, own_ref, other_ref, m_ref, v_ref, go_ref, d_ref, mo_ref, vo_ref):
        g = jnp.where(core_ref[0] == pl.program_id(1), own_ref[...], other_ref[...])
        go_ref[...] = g
        d_ref[...], mo_ref[...], vo_ref[...] = _adamw_math(w_ref[...], g, m_ref[...], v_ref[...])

    return pl.pallas_call(
        body, name=name, grid=(layers, 2, per_half),
        in_specs=[pl.BlockSpec(memory_space=pltpu.SMEM), nat, own, other, nat, nat], out_specs=[nat] * 4,
        out_shape=[jax.ShapeDtypeStruct(w.shape, F32)] * 4,
        compiler_params=_params(("parallel", "parallel", "parallel")))(my_core, w, g_own, g_other, m, v)


def _sum_slots(buf, name, bm=128):
    n, rows, w = buf.shape
    bm = min(bm, rows)
    assert rows % bm == 0

    def body(b_ref, o_ref):
        acc = b_ref[0].astype(F32)
        for s in range(1, n):
            acc = acc + b_ref[s].astype(F32)
        o_ref[...] = acc

    return pl.pallas_call(
        body, name=name, grid=(rows // bm,), in_specs=[pl.BlockSpec((n, bm, w), lambda i: (0, i, 0))],
        out_specs=pl.BlockSpec((bm, w), lambda i: (i, 0)), out_shape=jax.ShapeDtypeStruct((rows, w), F32),
        compiler_params=_params(("parallel",)))(buf)


def _exchange(srcs, same, masks, name, keep_own=True):
    slots = N_DEV if len(masks) == N_DEV - 1 else (2 if keep_own else 1)
    n_arr, n_peer = len(srcs), len(masks)
    shapes = [s.shape if sm else s.shape[1:] for s, sm in zip(srcs, same)]

    def body(*refs):
        src_refs, out_refs = refs[:n_arr], refs[n_arr:2 * n_arr]
        send_sems, recv_sems, local_sems = refs[2 * n_arr:]
        x, y, c = lax.axis_index("x"), lax.axis_index("y"), lax.axis_index("c")
        me = 4 * x + 2 * y + c

        def flip(v, bit):
            return 1 - v if bit else v

        def slot_of(dev):
            return dev if slots == N_DEV else (dev % 2 if slots == 2 else 0)

        def piece(a, p):
            return src_refs[a] if same[a] else src_refs[a].at[p]

        local = []
        if keep_own:
            local = [pltpu.make_async_copy(piece(a, me), out_refs[a].at[slot_of(me)], local_sems.at[a])
                     for a in range(n_arr)]
        for cp in local:
            cp.start()
        copies = []
        for idx, k in enumerate(masks):
            peer = (flip(x, k & 4), flip(y, k & 2), flip(c, k & 1))
            peer_id = 4 * peer[0] + 2 * peer[1] + peer[2]
            for a in range(n_arr):
                sem = idx * n_arr + a
                cp = pltpu.make_async_remote_copy(
                    src_ref=piece(a, peer_id), dst_ref=out_refs[a].at[slot_of(me)],
                    send_sem=send_sems.at[sem], recv_sem=recv_sems.at[sem], device_id=peer, device_id_type=MESH)
                cp.start()
                copies.append((cp, pltpu.make_async_remote_copy(
                    src_ref=piece(a, peer_id), dst_ref=out_refs[a].at[slot_of(peer_id)],
                    send_sem=send_sems.at[sem], recv_sem=recv_sems.at[sem], device_id=peer, device_id_type=MESH)))
        for cp, landing in copies:
            cp.wait_send()
            landing.wait_recv()
        for cp in local:
            cp.wait()

    any_spec = pl.BlockSpec(memory_space=pl.ANY)
    n_sem = n_arr * n_peer
    return pl.pallas_call(
        body, name=name, in_specs=[any_spec] * n_arr, out_specs=[any_spec] * n_arr,
        out_shape=[jax.ShapeDtypeStruct((slots,) + tuple(sh), s.dtype) for sh, s in zip(shapes, srcs)],
        scratch_shapes=[pltpu.SemaphoreType.DMA((n_sem,)), pltpu.SemaphoreType.DMA((n_sem,)),
                        pltpu.SemaphoreType.DMA((n_arr,))],
        )(*srcs)


ALL_PEERS = tuple(range(1, N_DEV))
SIBLING = (1,)


def _send_to_sibling(srcs, name):
    n_arr = len(srcs)

    def body(*refs):
        src_refs, out_refs = refs[:n_arr], refs[n_arr:2 * n_arr]
        send_sems, recv_sems = refs[2 * n_arr:]
        x, y, c = lax.axis_index("x"), lax.axis_index("y"), lax.axis_index("c")
        copies = [pltpu.make_async_remote_copy(
            src_ref=src_refs[a].at[1 - c], dst_ref=out_refs[a], send_sem=send_sems.at[a], recv_sem=recv_sems.at[a],
            device_id=(x, y, 1 - c), device_id_type=MESH) for a in range(n_arr)]
        for cp in copies:
            cp.start()
        for cp in copies:
            cp.wait()

    any_spec = pl.BlockSpec(memory_space=pl.ANY)
    return pl.pallas_call(
        body, name=name, in_specs=[any_spec] * n_arr, out_specs=[any_spec] * n_arr,
        out_shape=[jax.ShapeDtypeStruct(s.shape[1:], s.dtype) for s in srcs],
        scratch_shapes=[pltpu.SemaphoreType.DMA((n_arr,)), pltpu.SemaphoreType.DMA((n_arr,))],
        )(*srcs)


def _add_core_parts(mine, theirs, my_core, name, bm=128):
    _, chips, rows, n = mine.shape
    bm = min(bm, rows)
    assert rows % bm == 0

    def body(core_ref, p0_ref, p1_ref, t_ref, o_ref):
        own = jnp.where(core_ref[0] == 0, p0_ref[...], p1_ref[...])
        o_ref[...] = (own.astype(F32) + t_ref[...].astype(F32)).astype(o_ref.dtype)

    part = lambda k: pl.BlockSpec((None, None, bm, n), lambda s, i, k=k: (k, s, i, 0))
    flat = pl.BlockSpec((None, bm, n), lambda s, i: (s, i, 0))
    return pl.pallas_call(
        body, name=name, grid=(chips, rows // bm),
        in_specs=[pl.BlockSpec(memory_space=pltpu.SMEM), part(0), part(1), flat], out_specs=flat,
        out_shape=jax.ShapeDtypeStruct(theirs.shape, mine.dtype),
        compiler_params=_params(("parallel", "parallel")))(my_core, mine, mine, theirs)


def _scatter_by_chip(srcs, small, name):
    n_arr = len(srcs)

    def body(*refs):
        src_refs, small_ref = refs[:n_arr], refs[n_arr]
        out_refs, small_out = refs[n_arr + 1:2 * n_arr + 1], refs[2 * n_arr + 1]
        send_sems, recv_sems, local_sems = refs[2 * n_arr + 2:]
        x, y, c = lax.axis_index("x"), lax.axis_index("y"), lax.axis_index("c")
        my_chip, me = 2 * x + y, 4 * x + 2 * y + c
        chips = [(1 - x, y), (x, 1 - y), (1 - x, 1 - y)]
        local = [pltpu.make_async_copy(src_refs[a].at[my_chip], out_refs[a].at[my_chip], local_sems.at[a])
                 for a in range(n_arr)]
        local.append(pltpu.make_async_copy(small_ref, small_out.at[me], local_sems.at[n_arr]))
        for cp in local:
            cp.start()
        copies = []
        for j, (px, py) in enumerate(chips):
            their_chip = 2 * px + py
            for a in range(n_arr):
                sem = j * n_arr + a
                cp = pltpu.make_async_remote_copy(
                    src_ref=src_refs[a].at[their_chip], dst_ref=out_refs[a].at[my_chip],
                    send_sem=send_sems.at[sem], recv_sem=recv_sems.at[sem], device_id=(px, py, c), device_id_type=MESH)
                cp.start()
                copies.append((cp, pltpu.make_async_remote_copy(
                    src_ref=src_refs[a].at[their_chip], dst_ref=out_refs[a].at[their_chip],
                    send_sem=send_sems.at[sem], recv_sem=recv_sems.at[sem], device_id=(px, py, c), device_id_type=MESH)))
        for idx, k in enumerate(ALL_PEERS):
            peer = (1 - x if k & 4 else x, 1 - y if k & 2 else y, 1 - c if k & 1 else c)
            peer_id = 4 * peer[0] + 2 * peer[1] + peer[2]
            sem = 3 * n_arr + idx
            cp = pltpu.make_async_remote_copy(
                src_ref=small_ref, dst_ref=small_out.at[me], send_sem=send_sems.at[sem], recv_sem=recv_sems.at[sem],
                device_id=peer, device_id_type=MESH)
            cp.start()
            copies.append((cp, pltpu.make_async_remote_copy(
                src_ref=small_ref, dst_ref=small_out.at[peer_id], send_sem=send_sems.at[sem], recv_sem=recv_sems.at[sem],
                device_id=peer, device_id_type=MESH)))
        for cp, landing in copies:
            cp.wait_send()
            landing.wait_recv()
        for cp in local:
            cp.wait()

    any_spec = pl.BlockSpec(memory_space=pl.ANY)
    n_sem = 3 * n_arr + len(ALL_PEERS)
    return pl.pallas_call(
        body, name=name, in_specs=[any_spec] * (n_arr + 1), out_specs=[any_spec] * (n_arr + 1),
        out_shape=[jax.ShapeDtypeStruct(s.shape, s.dtype) for s in srcs]
        + [jax.ShapeDtypeStruct((N_DEV,) + small.shape, small.dtype)],
        scratch_shapes=[pltpu.SemaphoreType.DMA((n_sem,)), pltpu.SemaphoreType.DMA((n_sem,)),
                        pltpu.SemaphoreType.DMA((n_arr + 1,))],
        )(*srcs, small)


def _gather_two_level(srcs, name):
    n_arr = len(srcs)

    def body(*refs):
        src_refs, out_refs = refs[:n_arr], refs[n_arr:2 * n_arr]
        send_sems, recv_sems, local_sems = refs[2 * n_arr:]
        x, y, c = lax.axis_index("x"), lax.axis_index("y"), lax.axis_index("c")
        sibling = (x, y, 1 - c)
        chips = [(1 - x, y), (x, 1 - y), (1 - x, 1 - y)]

        def slot(px, py, pc):
            return 4 * px + 2 * py + pc

        def copy(a, k, block, to, own=False):
            return pltpu.make_async_remote_copy(
                src_ref=src_refs[a] if own else out_refs[a].at[slot(*block)], dst_ref=out_refs[a].at[slot(*block)],
                send_sem=send_sems.at[a * 7 + k], recv_sem=recv_sems.at[a * 7 + k], device_id=to, device_id_type=MESH)

        local = [pltpu.make_async_copy(src_refs[a], out_refs[a].at[slot(x, y, c)], local_sems.at[a]) for a in range(n_arr)]
        for cp in local:
            cp.start()
        started = []
        for a in range(n_arr):
            started.append(copy(a, 0, (x, y, c), sibling, own=True))
            started += [copy(a, 1 + j, (x, y, c), (*chip, c), own=True) for j, chip in enumerate(chips)]
        for cp in started:
            cp.start()
        for j, chip in enumerate(chips):
            for a in range(n_arr):
                copy(a, 1 + j, (*chip, c), (x, y, c)).wait_recv()
                passed = copy(a, 4 + j, (*chip, c), sibling)
                passed.start()
                started.append(passed)
        for a in range(n_arr):
            copy(a, 0, sibling, (x, y, c)).wait_recv()
            for j, chip in enumerate(chips):
                copy(a, 4 + j, (*chip, 1 - c), (x, y, c)).wait_recv()
        for cp in started:
            cp.wait_send()
        for cp in local:
            cp.wait()

    any_spec = pl.BlockSpec(memory_space=pl.ANY)
    return pl.pallas_call(
        body, name=name, in_specs=[any_spec] * n_arr, out_specs=[any_spec] * n_arr,
        out_shape=[jax.ShapeDtypeStruct((N_DEV,) + s.shape, s.dtype) for s in srcs],
        scratch_shapes=[pltpu.SemaphoreType.DMA((7 * n_arr,)), pltpu.SemaphoreType.DMA((7 * n_arr,)),
                        pltpu.SemaphoreType.DMA((n_arr,))],
        )(*srcs)


def _pack(arrays, rows):
    flat = jnp.concatenate([a.reshape(-1) for a in arrays])
    return jnp.pad(flat, (0, rows * D_MODEL - flat.shape[0])).reshape(rows, D_MODEL)


def _unpack(buf, shapes):
    flat = buf.reshape(-1)
    out, off = [], 0
    for s in shapes:
        n = math.prod(s)
        out.append(flat[off:off + n].reshape(s))
        off += n
    return out


def _rows_for(shapes, mult):
    n = sum(math.prod(s) for s in shapes)
    rows = -(-n // D_MODEL)
    return -(-rows // mult) * mult


SHARD_AXIS = dict(w_mem_kv=1, w_out=1, w_gate_up=2, w_down=1, gdn_w_in=2, swa_w_q=1, w_kv=0, gdn_conv=2)
HALF_AXIS = dict(w_mem_kv=1, w_out=1, w_gate_up=1, w_down=1, gdn_w_in=1, swa_w_q=1, w_kv=0)


def _my_half(shard, name, c):
    ax = HALF_AXIS[name]
    h = shard.shape[ax] // 2
    return lax.dynamic_slice_in_dim(shard, c * h, h, axis=ax)


def _piece_layout(name, half_shape):
    dims, pos = [], {}
    for i, d in enumerate(half_shape):
        if i == SHARD_AXIS[name]:
            pos["chip"] = len(dims)
            dims.append(4)
        if i == HALF_AXIS[name]:
            pos["core"] = len(dims)
            dims.append(2)
        pos[i] = len(dims)
        dims.append(d)
    return dims, [pos["chip"], pos["core"]] + [pos[i] for i in range(len(half_shape))]


def _full_shape(name, half_shape):
    return tuple(d * (4 if i == SHARD_AXIS[name] else 1) * (2 if i == HALF_AXIS[name] else 1)
                 for i, d in enumerate(half_shape))


def _assemble(pieces, name):
    half_shape = pieces.shape[1:]
    if half_shape[-1] % LANES:
        chips = [jnp.concatenate([pieces[2 * s], pieces[2 * s + 1]], axis=HALF_AXIS[name]) for s in range(4)]
        return jnp.concatenate(chips, axis=SHARD_AXIS[name])
    dims, perm = _piece_layout(name, half_shape)
    inverse = [perm.index(i) for i in range(len(perm))]
    return pieces.reshape((4, 2) + half_shape).transpose(inverse).reshape(_full_shape(name, half_shape))


def _to_pieces(full, name, half_shape):
    if half_shape[-1] % LANES:
        ns, nh = half_shape[SHARD_AXIS[name]], half_shape[HALF_AXIS[name]]
        parts = [lax.slice_in_dim(lax.slice_in_dim(full, s * ns, (s + 1) * ns, axis=SHARD_AXIS[name]),
                                  c * nh, (c + 1) * nh, axis=HALF_AXIS[name])
                 for c in range(2) for s in range(4)]
        return jnp.stack(parts).reshape((2, 4) + tuple(half_shape))
    dims, perm = _piece_layout(name, half_shape)
    return full.reshape(dims).transpose([perm[1], perm[0]] + perm[2:])


def _from_halves(halves, name):
    ax = HALF_AXIS[name]
    s = jnp.moveaxis(halves, 0, ax)
    return s.reshape(s.shape[:ax] + (2 * s.shape[ax + 1],) + s.shape[ax + 2:])


def _rope_tables(positions):
    half = ROT_DIM // 2
    inv = ROPE_THETA ** (-jnp.arange(0, ROT_DIM, 2, dtype=F32) / ROT_DIM)
    ang = positions.astype(F32)[:, None] * inv
    cos, sin = jnp.cos(ang), jnp.sin(ang)
    rows = positions.shape[0]
    one = jnp.ones((rows, SWA_DH - ROT_DIM), F32)
    zero = jnp.zeros((rows, SWA_DH - ROT_DIM), F32)
    zh = jnp.zeros((rows, half), F32)
    c64 = jnp.concatenate([cos, cos, one], axis=1)
    a64 = jnp.concatenate([-sin, zh, zero], axis=1)
    b64 = jnp.concatenate([zh, sin, zero], axis=1)
    return tuple(jnp.concatenate([t, t], axis=1) for t in (c64, a64, b64))


def _pair_heads(t):
    return jnp.concatenate([t[:, :SWA_DH], t[:, :SWA_DH], t[:, SWA_DH:], t[:, SWA_DH:]], axis=1)


def _unpair_heads(t):
    return jnp.concatenate([t[:, :SWA_DH], t[:, LANES:LANES + SWA_DH]], axis=1)


def _gdn_in_pad(w):
    o2 = 4 * GDN_W
    pad = jnp.zeros(w.shape[:-1] + (GDN_IN_PAD - GDN_IN,), w.dtype)
    return jnp.concatenate([w[..., :o2], w[..., o2 + 2 * GDN_HEADS:], w[..., o2:o2 + 2 * GDN_HEADS], pad], axis=-1)


def _gdn_in_unpad(w):
    o2 = 4 * GDN_W
    return jnp.concatenate([w[..., :o2], w[..., o2 + MEM_W:o2 + MEM_W + 2 * GDN_HEADS], w[..., o2:o2 + MEM_W]], axis=-1)


def _head_rows(v):
    return jnp.zeros((1, LANES), F32).at[0, GDN_HEADS:2 * GDN_HEADS].set(v.astype(F32))


def _selectors():
    lane = jnp.arange(LANES)[:, None]
    head = (jnp.arange(GDN_W) // LANES)[None, :]
    return (lane == head).astype(BF16), (lane == head + GDN_HEADS).astype(BF16)


def _local_step(x, mem, positions, target, w):
    rows = x.shape[0]
    tabs = _rope_tables(positions)
    sel_b, sel_a = _selectors()
    row2 = lambda v: v.reshape(1, -1).astype(F32)

    w_gu = _ff_interleave(w["w_gate_up"])
    mem_n = _rms_fwd(mem, row2(w["ln_mem"]), "mem_norm")
    saved = []
    kt = vt = None
    for l in range(DEPTH):
        s = dict(x0=x)
        h = _rms_fwd(x, row2(w["ln_mix"][l]), f"norm_mix{l}")
        mkv = _mm(mem_n, w["w_mem_kv"][l], "nn", f"mem_kv{l}")
        kbd, vbd = _mem_expand(mkv)
        if l < N_A:
            proj = _mm(h, w["gdn_w_in"][l], "nn", f"gdn_in{l}")
            a_row, dt_row = _head_rows(w["gdn_A_log"][l]), _head_rows(w["gdn_dt_bias"][l])
            q, k, v, gc, beta = _gdn_pre_fwd(proj, w["gdn_conv"][l], sel_b, sel_a, a_row, dt_row, f"gdn_pre{l}")
            o, states, tinv, gw, vn = _gdn_fwd(q, k, v, gc, beta, f"gdn_scan{l}")
            mix = _gdn_post_fwd(o, proj, row2(w["gdn_norm"][l]), f"gdn_post{l}")
            mq_cb = (3 * GDN_W + GDN_W) // MEM_W
            s.update(q=q, k=k, v=v, gc=gc, beta=beta, o=o, states=states, tinv=tinv, gw=gw, vn=vn,
                     a_row=a_row, dt_row=dt_row)
        else:
            proj = _mm(h, w["swa_w_q"][l - N_A], "nn", f"swa_in{l}")
            qr = _rope(proj, SWA_HEADS * SWA_DH, 0, tabs, 1, f"rope_q{l}", BF16)
            mix = _swa_fwd(qr, kt, vt, w["swa_sinks"][l - N_A], f"swa{l}")
            mq_cb = (SWA_HEADS * SWA_DH) // MEM_W
            s.update(qr=qr)
        mem_o = _mem_fwd(proj, mq_cb, kbd, vbd, f"mem_attn{l}")
        cat = jnp.concatenate([mix, mem_o], axis=1)
        x1 = _mm(cat, w["w_out"][l], "nn", f"out_proj{l}", add=x)
        h2 = _rms_fwd(x1, row2(w["ln_ffn"][l]), f"norm_ffn{l}")
        gu, act = _gate_up_fwd(h2, w_gu[l], f"gate_up{l}")
        x = _mm(act, w["w_down"][l], "nn", f"down{l}", add=x1)
        s.update(h=h, proj=proj, kbd=kbd, vbd=vbd, mq_cb=mq_cb, cat=cat, x1=x1, h2=h2, gu=gu, act=act)
        saved.append(s)
        if l == N_A - 1:
            x_kv = x
            h_kv = _rms_fwd(x, row2(w["ln_kv"]), "norm_kv")
            kv = _mm(h_kv, w["w_kv"], "nn", "kv_proj")
            kr = _rope(kv, LANES, 0, tabs, 1, "rope_k", F32)
            kt = _pair_heads(kr).astype(BF16)
            vt = _pair_heads(kv[:, LANES:]).astype(BF16)

    gr = {}
    dx, dxb, loss_part, dlnf = _final_loss(x, row2(w["ln_final"]), target, "final_loss")
    gr["ln_final"] = dlnf.sum(axis=0)
    dln_mix, dln_ffn = [None] * DEPTH, [None] * DEPTH
    dw_mem_kv, dw_out, dw_gu, dw_dn = [None] * DEPTH, [None] * DEPTH, [None] * DEPTH, [None] * DEPTH
    dgdn_in, dgdn_conv, dgdn_a, dgdn_dt, dgdn_norm = [None] * N_A, [None] * N_A, [None] * N_A, [None] * N_A, [None] * N_A
    dswa_q, dswa_sinks = [None] * N_B, [None] * N_B
    dmem_n = None
    dkt = dvt = None
    for l in reversed(range(DEPTH)):
        s = saved[l]
        if l == N_A - 1:
            dkr = _unpair_heads(dkt)
            dk = _rope(dkr, LANES, 0, tabs, -1, "rope_k_bwd", BF16)
            dkv = jnp.concatenate([dk, _unpair_heads(dvt).astype(BF16)], axis=1)
            dh_kv = _mm(dkv, w["w_kv"], "nt", "kv_proj_dx")
            gr["w_kv"] = _mm(h_kv, dkv, "tn", "kv_proj_dw", BF16)
            dx, dxb, dg = _rms_bwd(x_kv, row2(w["ln_kv"]), dh_kv, dx, "norm_kv_bwd")
            gr["ln_kv"] = dg.sum(axis=0)
        dgu = _down_bwd(dxb, w["w_down"][l], s["gu"], f"down_dx{l}")
        dw_dn[l] = _mm(s["act"], dxb, "tn", f"down_dw{l}", BF16)
        dh2 = _mm(dgu, w_gu[l], "nt", f"gate_up_dx{l}")
        dw_gu[l] = _mm(s["h2"], dgu, "tn", f"gate_up_dw{l}", BF16)
        dx, dxb, dg = _rms_bwd(s["x1"], row2(w["ln_ffn"][l]), dh2, dx, f"norm_ffn_bwd{l}")
        dln_ffn[l] = dg.sum(axis=0)
        dcat = _mm(dxb, w["w_out"][l], "nt", f"out_proj_dx{l}", BF16)
        dw_out[l] = _mm(s["cat"], dxb, "tn", f"out_proj_dw{l}", BF16)
        dmq, dkbd, dvbd = _mem_bwd(s["proj"], s["mq_cb"], s["kbd"], s["vbd"], dcat, f"mem_attn_bwd{l}")
        dmkv = _mem_collapse(dkbd, dvbd).astype(BF16)
        dw_mem_kv[l] = _mm(mem_n, dmkv, "tn", f"mem_kv_dw{l}", BF16)
        dmem_n = _mm(dmkv, w["w_mem_kv"][l], "nt", f"mem_kv_dx{l}", add=dmem_n)
        if l < N_A:
            do, dz, dng = _gdn_post_bwd(s["o"], s["proj"], row2(w["gdn_norm"][l]), dcat, f"gdn_post_bwd{l}")
            dq, dk, dv, dg_, dbeta = _gdn_bwd(s["q"], s["k"], s["v"], s["gc"], s["beta"], s["states"], s["tinv"], s["gw"],
                                              s["vn"], do, f"gdn_scan_bwd{l}")
            res = _gdn_pre_bwd(s["proj"], w["gdn_conv"][l], sel_b, sel_a, s["a_row"], s["dt_row"], dq, dk, dv, dg_, dbeta,
                               f"gdn_pre_bwd{l}")
            dconv, dba = res[0], res[1]
            dgdn_conv[l] = jnp.stack([r.sum(axis=0) for r in res[2:2 + CONV_K]])
            dgdn_a[l] = res[2 + CONV_K].sum(axis=0)[GDN_HEADS:2 * GDN_HEADS]
            dgdn_dt[l] = res[3 + CONV_K].sum(axis=0)[GDN_HEADS:2 * GDN_HEADS]
            dgdn_norm[l] = dng.sum(axis=0)
            dqkv = _conv_bwd_input(dconv, w["gdn_conv"][l], f"gdn_conv_bwd{l}")
            dproj = jnp.concatenate([dqkv, dz, dmq, dba], axis=1)
            dh = _mm(dproj, w["gdn_w_in"][l], "nt", f"gdn_in_dx{l}")
            dgdn_in[l] = _mm(s["h"], dproj, "tn", f"gdn_in_dw{l}", BF16)
        else:
            b = l - N_A
            dqr, dkt_l, dvt_l, dsk = _swa_bwd(s["qr"], kt, vt, w["swa_sinks"][b], dcat, f"swa_bwd{l}")
            dkt = dkt_l if dkt is None else dkt + dkt_l
            dvt = dvt_l if dvt is None else dvt + dvt_l
            dswa_sinks[b] = dsk[0, :SWA_HEADS]
            dq = _rope(dqr, SWA_HEADS * SWA_DH, 0, tabs, -1, f"rope_q_bwd{l}", BF16)
            dproj = jnp.concatenate([dq, dmq], axis=1)
            dh = _mm(dproj, w["swa_w_q"][b], "nt", f"swa_in_dx{l}")
            dswa_q[b] = _mm(s["h"], dproj, "tn", f"swa_in_dw{l}", BF16)
        dx, dxb, dg = _rms_bwd(s["x0"], row2(w["ln_mix"][l]), dh, dx, f"norm_mix_bwd{l}")
        dln_mix[l] = dg.sum(axis=0)
    _, _, dg = _rms_bwd(mem, row2(w["ln_mem"]), dmem_n, None, "mem_norm_bwd")
    gr["ln_mem"] = dg.sum(axis=0)
    gr.update(ln_mix=jnp.stack(dln_mix), ln_ffn=jnp.stack(dln_ffn), w_mem_kv=jnp.stack(dw_mem_kv), w_out=jnp.stack(dw_out),
              w_gate_up=_ff_deinterleave(jnp.stack(dw_gu)), w_down=jnp.stack(dw_dn), gdn_w_in=jnp.stack(dgdn_in), gdn_conv=jnp.stack(dgdn_conv),
              gdn_A_log=jnp.stack(dgdn_a), gdn_dt_bias=jnp.stack(dgdn_dt), gdn_norm=jnp.stack(dgdn_norm),
              swa_w_q=jnp.stack(dswa_q), swa_sinks=jnp.stack(dswa_sinks))
    return loss_part, dx, gr


def kernel(x, mem, positions, ln_mix, ln_ffn, ln_mem, w_mem_kv, w_out, w_gate_up, w_down, gdn_w_in, gdn_conv, gdn_A_log, gdn_dt_bias, gdn_norm, swa_w_q, swa_sinks, ln_kv, w_kv, ln_final, loss_target, m_ln_mix, m_ln_ffn, m_ln_mem, m_w_mem_kv, m_w_out, m_w_gate_up, m_w_down, m_gdn_w_in, m_gdn_conv, m_gdn_A_log, m_gdn_dt_bias, m_gdn_norm, m_swa_w_q, m_swa_sinks, m_ln_kv, m_w_kv, m_ln_final, v_ln_mix, v_ln_ffn, v_ln_mem, v_w_mem_kv, v_w_out, v_w_gate_up, v_w_down, v_gdn_w_in, v_gdn_conv, v_gdn_A_log, v_gdn_dt_bias, v_gdn_norm, v_swa_w_q, v_swa_sinks, v_ln_kv, v_w_kv, v_ln_final):
    given = dict(locals())
    wts = {n: given[n] for n in WEIGHTS}
    c = lax.axis_index("c")

    halves = [_my_half(wts[n].astype(BF16), n, c) for n in SHARDED]
    half_shapes = [h.shape for h in halves]
    conv_shape = wts["gdn_conv"].shape
    cpack = _pack([wts["gdn_conv"]], 16)
    conv_half = lax.dynamic_slice_in_dim(cpack, c * SUBLANES, SUBLANES, axis=0)
    got = _gather_two_level(halves + [conv_half], "gather_weights")
    full = {n: wts[n] for n in SMALL}
    for n, g in zip(SHARDED, got):
        full[n] = _assemble(g, n)
    conv_all = got[-1].reshape(4, 16, D_MODEL)
    full["gdn_conv"] = jnp.concatenate([_unpack(conv_all[s], [conv_shape])[0] for s in range(4)], axis=2)
    full["gdn_w_in"] = _gdn_in_pad(full["gdn_w_in"])

    loss_part, dx, gr = _local_step(x[0], mem[0], positions[0], loss_target[0], full)
    gr["gdn_w_in"] = _gdn_in_unpad(gr["gdn_w_in"])

    pieces = [_to_pieces(gr[n].astype(BF16), n, hs) for n, hs in zip(SHARDED, half_shapes)]
    small_shapes = [wts[n].shape for n in SMALL] + [conv_shape[:2] + (4 * conv_shape[2],), (SUBLANES, LANES)]
    rows_s = _rows_for(small_shapes, SUBLANES)
    spack = _pack([gr[n] for n in SMALL] + [gr["gdn_conv"], loss_part], rows_s)
    my_core = c.astype(jnp.int32).reshape(1)
    from_sibling = _send_to_sibling(pieces, "pair_grads")
    chip_parts = [_add_core_parts(p.reshape(2, 4, -1, p.shape[-1]), t.reshape(4, -1, t.shape[-1]), my_core, f"pair_sum_{n}")
                  for n, p, t in zip(SHARDED, pieces, from_sibling)]
    parts = _scatter_by_chip(chip_parts, spack, "scatter_grads")
    mine = [_sum_slots(p, f"sum_{n}").reshape(hs) for n, p, hs in zip(SHARDED, parts, half_shapes)]
    ssum = _unpack(_sum_slots(parts[-1], "sum_small"), small_shapes)
    theirs = _exchange(mine, [True] * len(mine), SIBLING, "swap_grad_halves", keep_own=False)
    g_all = dict(zip(SMALL, ssum[:len(SMALL)]))
    chip = 2 * lax.axis_index("x") + lax.axis_index("y")
    g_all["gdn_conv"] = lax.dynamic_slice_in_dim(ssum[len(SMALL)], chip * conv_shape[2], conv_shape[2], axis=2)
    loss = jnp.sum(ssum[-1])

    out = dict(grad=g_all, delta={}, new_m={}, new_v={})
    for n, own, other in zip(SHARDED, mine, theirs):
        as3d = lambda a: a.reshape((-1,) + a.shape[-2:])
        res = _adamw_halves(as3d(wts[n]), as3d(own), other.reshape((1, -1) + other.shape[-2:]), my_core,
                            as3d(given["m_" + n]), as3d(given["v_" + n]), f"adamw_{n}")
        for kind, r in zip(("grad", "delta", "new_m", "new_v"), res):
            out[kind][n] = r.reshape(wts[n].shape)
    small_names = SMALL + ("gdn_conv",)
    small_w_shapes = [wts[n].shape for n in small_names]
    rows_a = _rows_for(small_w_shapes, SUBLANES)
    res = _adamw(_pack([wts[n] for n in small_names], rows_a), _pack([g_all[n] for n in small_names], rows_a),
                 _pack([given["m_" + n] for n in small_names], rows_a),
                 _pack([given["v_" + n] for n in small_names], rows_a), "adamw_small")
    for kind, r in zip(("delta", "new_m", "new_v"), res):
        out[kind].update(zip(small_names, _unpack(r, small_w_shapes)))
    return (loss, dx[None], *[out["grad"][n] for n in WEIGHTS], *[out["delta"][n] for n in WEIGHTS],
            *[out["new_m"][n] for n in WEIGHTS], *[out["new_v"][n] for n in WEIGHTS])
```

```python
import functools
import math

import jax
import jax.numpy as jnp
from jax import lax
from jax.experimental import pallas as pl
from jax.experimental.pallas import tpu as pltpu

F32 = jnp.float32
BF16 = jnp.bfloat16
HI = lax.Precision.HIGHEST
MESH = pl.DeviceIdType.MESH

D_MODEL = 1024
DEPTH = 4
N_A = 2
N_B = 2
EPS = 1e-6
GDN_HEADS = 6
GDN_DK = 128
GDN_W = 768
CONV_K = 4
CHUNK = 64
SWA_HEADS = 12
SWA_KV_HEADS = 2
SWA_DH = 64
SWA_GROUP = 6
SWA_GW = SWA_GROUP * SWA_DH
SWA_BLOCK = 128
ROPE_THETA = 500000.0
ROT_DIM = 16
MEM_LEN = 256
MEM_HEADS = 4
MEM_DH = 64
MEM_W = 256
D_FF = 2816
GDN_IN = 3340
GDN_IN_PAD = 3456
ADAM_LR = 0.001
ADAM_B1 = 0.9
ADAM_B2 = 0.999
ADAM_EPS = 1e-08
ADAM_WD = 0.01
ADAM_STEP = 10

N_DEV = 8
LANES = 128
SUBLANES = 8
V7X_VMEM_LIMIT = 56 * 2**20
MM_VMEM_BUDGET = 44 * 2**20

SHARDED = ("w_mem_kv", "w_out", "w_gate_up", "w_down", "gdn_w_in", "swa_w_q", "w_kv")
SMALL = ("ln_mix", "ln_ffn", "ln_mem", "gdn_A_log", "gdn_dt_bias", "gdn_norm", "swa_sinks", "ln_kv", "ln_final")
WEIGHTS = ("ln_mix", "ln_ffn", "ln_mem", "w_mem_kv", "w_out", "w_gate_up", "w_down", "gdn_w_in", "gdn_conv",
           "gdn_A_log", "gdn_dt_bias", "gdn_norm", "swa_w_q", "swa_sinks", "ln_kv", "w_kv", "ln_final")


def _params(sem=None, **kw):
    return pltpu.CompilerParams(dimension_semantics=sem, vmem_limit_bytes=V7X_VMEM_LIMIT, **kw)


def _dot(a, b, dims=(((1,), (0,)), ((), ())), precision=None):
    return lax.dot_general(a, b, dims, precision=precision, preferred_element_type=F32)


NT = (((1,), (1,)), ((), ()))
TN = (((0,), (0,)), ((), ()))


def _fold8(v):
    r, w = v.shape
    return v.reshape(r // SUBLANES, SUBLANES, w).sum(axis=0)


def _row(a, w=None, cb=0):
    return ("row", a, a.shape[1] if w is None else w, cb)


def _full(a):
    return ("full", a, None, None)


def _prev8(a, w, cb=0):
    return ("prev8", a, w, cb)


def _next8(a, w, cb=0):
    return ("next8", a, w, cb)


def _rowcall(fn, name, rows, bm, ins, outs, accs=()):
    bm = min(bm, rows)
    assert rows % bm == 0 and bm % SUBLANES == 0
    steps = rows // bm
    r8 = bm // SUBLANES
    in_specs, arrays = [], []
    for kind, a, w, cb in ins:
        arrays.append(a)
        if kind == "row":
            in_specs.append(pl.BlockSpec((bm, w), lambda i, cb=cb: (i, cb)))
        elif kind == "full":
            in_specs.append(pl.BlockSpec(a.shape, lambda i, nd=a.ndim: (0,) * nd))
        elif kind == "prev8":
            in_specs.append(pl.BlockSpec((SUBLANES, w), lambda i, cb=cb: (jnp.maximum(i * r8 - 1, 0), cb)))
        else:
            last = rows // SUBLANES - 1
            in_specs.append(pl.BlockSpec((SUBLANES, w), lambda i, cb=cb: (jnp.minimum((i + 1) * r8, last), cb)))
    out_shape = [jax.ShapeDtypeStruct((rows, w), dt) for w, dt in outs]
    out_specs = [pl.BlockSpec((bm, w), lambda i: (i, 0)) for w, _ in outs]
    out_shape += [jax.ShapeDtypeStruct(s, F32) for s in accs]
    out_specs += [pl.BlockSpec(s, lambda i: (0, 0)) for s in accs]
    n_in, n_out = len(ins), len(outs)

    def body(*refs):
        i = pl.program_id(0)
        res = fn(i, *[r[...] for r in refs[:n_in]])
        if not isinstance(res, (tuple, list)):
            res = (res,)
        for r, v in zip(refs[n_in:n_in + n_out], res[:n_out]):
            r[...] = v.astype(r.dtype)
        if accs:
            @pl.when(i == 0)
            def _():
                for r in refs[n_in + n_out:]:
                    r[...] = jnp.zeros(r.shape, F32)
            for r, v in zip(refs[n_in + n_out:], res[n_out:]):
                r[...] += v

    res = pl.pallas_call(
        body, name=name, grid=(steps,), in_specs=in_specs, out_specs=out_specs, out_shape=out_shape,
        compiler_params=_params(("arbitrary",)))(*arrays)
    return res


def _tile(n, cap):
    for t in (1408, 1152, 1024, 896, 768, 640, 512, 384, 256, 128):
        if t <= cap and n % t == 0:
            return t
    return n


def _mm(a, b, mode, name, out_dtype=F32, add=None, layer=None, dst=None):
    if mode == "tn":
        s, m = a.shape
        n = b.shape[1]
        bm, bn, bk = _tile(m, 1408), _tile(n, 1408), min(s, 1024)
        nk = s // bk

        def body(a_ref, b_ref, *rest):
            o_ref, acc_ref = rest[-2:]
            k = pl.program_id(2)

            @pl.when(k == 0)
            def _():
                acc_ref[...] = jnp.zeros(acc_ref.shape, F32)
            acc_ref[...] += _dot(a_ref[...].astype(BF16), b_ref[...].astype(BF16), TN)

            @pl.when(k == nk - 1)
            def _():
                o_ref[...] = acc_ref[...].astype(o_ref.dtype)

        in_specs = [pl.BlockSpec((bk, bm), lambda i, j, k: (k, i)), pl.BlockSpec((bk, bn), lambda i, j, k: (k, j))]
        if dst is None:
            return pl.pallas_call(
                body, name=name, grid=(m // bm, n // bn, nk), in_specs=in_specs,
                out_specs=pl.BlockSpec((bm, bn), lambda i, j, k: (i, j)),
                out_shape=jax.ShapeDtypeStruct((m, n), out_dtype),
                scratch_shapes=[pltpu.VMEM((bm, bn), F32)],
                compiler_params=_params(("parallel", "parallel", "arbitrary")))(a, b)
        return pl.pallas_call(
            body, name=name, grid=(m // bm, n // bn, nk), in_specs=in_specs + [pl.BlockSpec(memory_space=pl.ANY)],
            out_specs=pl.BlockSpec((None, bm, bn), lambda i, j, k: (layer, i, j)),
            out_shape=jax.ShapeDtypeStruct(dst.shape, dst.dtype), input_output_aliases={2: 0},
            scratch_shapes=[pltpu.VMEM((bm, bn), F32)],
            compiler_params=_params(("parallel", "parallel", "arbitrary")))(a, b, dst)

    m, k = a.shape
    if layer is None:
        n = b.shape[1] if mode == "nn" else b.shape[0]
    else:
        n = b.shape[2] if mode == "nn" else b.shape[1]
    out_bytes = jnp.dtype(out_dtype).itemsize

    def vmem_need(bm, bn):
        need = 2 * bm * k * a.dtype.itemsize + 2 * bn * k * b.dtype.itemsize + bm * bn * (2 * out_bytes + 4)
        return need + (2 * bm * bn * 4 if add is not None else 0)

    bm, bn = min(m, 512), _tile(n, 512)
    for cand in ((2048, 1408), (2048, 1024), (2048, 512), (1024, 1408), (1024, 1024), (1024, 512)):
        tm, tn = min(m, cand[0]), _tile(n, cand[1])
        if m % tm == 0 and vmem_need(tm, tn) <= MM_VMEM_BUDGET:
            bm, bn = tm, tn
            break
    dims = NT if mode == "nt" else (((1,), (0,)), ((), ()))
    if layer is None:
        b_spec = (pl.BlockSpec((k, bn), lambda i, j: (0, j)) if mode == "nn" else pl.BlockSpec((bn, k), lambda i, j: (j, 0)))
    elif mode == "nn":
        b_spec = pl.BlockSpec((None, k, bn), lambda i, j: (layer, 0, j))
    else:
        b_spec = pl.BlockSpec((None, bn, k), lambda i, j: (layer, j, 0))
    in_specs = [pl.BlockSpec((bm, k), lambda i, j: (i, 0)), b_spec]
    args = [a, b]
    if add is not None:
        in_specs.append(pl.BlockSpec((bm, bn), lambda i, j: (i, j)))
        args.append(add)

    def body(a_ref, b_ref, *rest):
        o_ref = rest[-1]
        acc = _dot(a_ref[...].astype(BF16), b_ref[...].astype(BF16), dims)
        if add is not None:
            acc = acc + rest[0][...]
        o_ref[...] = acc.astype(o_ref.dtype)

    return pl.pallas_call(
        body, name=name, grid=(m // bm, n // bn), in_specs=in_specs,
        out_specs=pl.BlockSpec((bm, bn), lambda i, j: (i, j)),
        out_shape=jax.ShapeDtypeStruct((m, n), out_dtype),
        compiler_params=_params(("parallel", "parallel")))(*args)


def _sigmoid(x):
    return 0.5 * jnp.tanh(0.5 * x) + 0.5


def _softplus(x):
    return jnp.maximum(x, 0.0) + jnp.log(1.0 + jnp.exp(-jnp.abs(x)))


def _silu_and_grad(x):
    s = _sigmoid(x)
    return x * s, s * (1.0 + x * (1.0 - s))


def _rms_stats(x):
    r = lax.rsqrt(jnp.mean(x * x, axis=-1, keepdims=True) + EPS)
    return r, x * r


def _rms_fwd(x, g, name, out_dtype=BF16, bm=512):
    def fn(i, x, g):
        _, xn = _rms_stats(x)
        return xn * g
    return _rowcall(fn, name, x.shape[0], bm, [_row(x), _full(g)], [(x.shape[1], out_dtype)])[0]


def _rms_bwd_math(x, g, dy):
    r, xn = _rms_stats(x)
    dxn = dy * g
    dx = r * (dxn - xn * jnp.mean(dxn * xn, axis=-1, keepdims=True))
    return dx, dy * xn


def _rms_bwd(x, g, dy, res, name, bm=256):
    d = x.shape[1]

    def fn(i, x, g, dy, *res_):
        dx, dg = _rms_bwd_math(x, g, dy.astype(F32))
        if res_:
            dx = dx + res_[0]
        return dx, dx, _fold8(dg)
    ins = [_row(x), _full(g), _row(dy)] + ([_row(res)] if res is not None else [])
    return _rowcall(fn, name, x.shape[0], bm, ins, [(d, F32), (d, BF16)], [(SUBLANES, d)])


def _final_loss(x, g, target, name, bm=256):
    d = x.shape[1]

    def fn(i, x, g, t):
        r, xn = _rms_stats(x)
        err = xn * g - t
        dy = err * (1.0 / d)
        dxn = dy * g
        dx = r * (dxn - xn * jnp.mean(dxn * xn, axis=-1, keepdims=True))
        e2 = _fold8(err * err)
        lp = e2[:, 0:LANES]
        for c in range(1, d // LANES):
            lp = lp + e2[:, c * LANES:(c + 1) * LANES]
        return dx, dx, lp * (0.5 / d), _fold8(dy * xn)
    return _rowcall(fn, name, x.shape[0], bm, [_row(x), _full(g), _row(target)], [(d, F32), (d, BF16)],
                    [(SUBLANES, LANES), (SUBLANES, d)])


FF_TILE = 256


def _ff_interleave(w):
    lead = w.shape[:-1]
    w = w.reshape(lead + (2, D_FF // FF_TILE, FF_TILE))
    return jnp.swapaxes(w, -3, -2).reshape(lead + (2 * D_FF,))


def _ff_deinterleave(w):
    lead = w.shape[:-1]
    w = w.reshape(lead + (D_FF // FF_TILE, 2, FF_TILE))
    return jnp.swapaxes(w, -3, -2).reshape(lead + (2 * D_FF,))


def _gate_up_fwd(h, w_gu, layer, name, bm=2048):
    rows, k = h.shape
    bm = min(bm, rows)

    def body(a_ref, b_ref, gu_ref, act_ref):
        acc = _dot(a_ref[...], b_ref[...])
        gu_ref[...] = acc.astype(gu_ref.dtype)
        act_ref[...] = (_silu_and_grad(acc[:, :FF_TILE])[0] * acc[:, FF_TILE:]).astype(act_ref.dtype)

    return pl.pallas_call(
        body, name=name, grid=(rows // bm, D_FF // FF_TILE),
        in_specs=[pl.BlockSpec((bm, k), lambda i, j: (i, 0)), pl.BlockSpec((None, k, 2 * FF_TILE), lambda i, j: (layer, 0, j))],
        out_specs=[pl.BlockSpec((bm, 2 * FF_TILE), lambda i, j: (i, j)), pl.BlockSpec((bm, FF_TILE), lambda i, j: (i, j))],
        out_shape=[jax.ShapeDtypeStruct((rows, 2 * D_FF), BF16), jax.ShapeDtypeStruct((rows, D_FF), BF16)],
        compiler_params=_params(("parallel", "parallel")))(h, w_gu)


def _down_bwd(dx, w_down, layer, gu, name, bm=2048):
    rows, k = dx.shape
    bm = min(bm, rows)

    def body(a_ref, b_ref, gu_ref, o_ref):
        da = _dot(a_ref[...], b_ref[...], NT)
        gu = gu_ref[...].astype(F32)
        s, ds = _silu_and_grad(gu[:, :FF_TILE])
        o_ref[:, :FF_TILE] = (da * gu[:, FF_TILE:] * ds).astype(o_ref.dtype)
        o_ref[:, FF_TILE:] = (da * s).astype(o_ref.dtype)

    return pl.pallas_call(
        body, name=name, grid=(rows // bm, D_FF // FF_TILE),
        in_specs=[pl.BlockSpec((bm, k), lambda i, j: (i, 0)), pl.BlockSpec((None, FF_TILE, k), lambda i, j: (layer, j, 0)),
                  pl.BlockSpec((bm, 2 * FF_TILE), lambda i, j: (i, j))],
        out_specs=pl.BlockSpec((bm, 2 * FF_TILE), lambda i, j: (i, j)),
        out_shape=jax.ShapeDtypeStruct((rows, 2 * D_FF), BF16),
        compiler_params=_params(("parallel", "parallel")))(dx, w_down, gu)


def _rope_apply(x, tabs, sign):
    cos, ta, tb = tabs
    outs = []
    for c in range(x.shape[1] // LANES):
        xc = x[:, c * LANES:(c + 1) * LANES]
        if sign > 0:
            o = xc * cos + pltpu.roll(xc, LANES - 8, 1) * ta + pltpu.roll(xc, 8, 1) * tb
        else:
            o = xc * cos + pltpu.roll(xc * ta, 8, 1) + pltpu.roll(xc * tb, LANES - 8, 1)
        outs.append(o)
    return outs[0] if len(outs) == 1 else jnp.concatenate(outs, axis=1)


def _rope(x, w, cb, tabs, sign, name, out_dtype, bm=512):
    def fn(i, x, c, a, b):
        return _rope_apply(x.astype(F32), (c, a, b), sign)
    return _rowcall(fn, name, x.shape[0], bm, [_row(x, w, cb)] + [_row(t) for t in tabs], [(w, out_dtype)])[0]


def _shift_down(x, prev8, s, first):
    xs = pltpu.roll(x, s, 0)
    rp = pltpu.roll(prev8, s, 0) * jnp.where(first, 0.0, 1.0)
    rid = lax.broadcasted_iota(jnp.int32, rp.shape, 0)
    top = jnp.where(rid < s, rp, xs[0:SUBLANES])
    return jnp.concatenate([top, xs[SUBLANES:]], axis=0)


def _shift_up(x, next8, s, last):
    n = x.shape[0]
    xs = pltpu.roll(x, n - s, 0)
    rn = pltpu.roll(next8, SUBLANES - s, 0) * jnp.where(last, 0.0, 1.0)
    rid = lax.broadcasted_iota(jnp.int32, rn.shape, 0)
    bot = jnp.where(rid >= SUBLANES - s, rn, xs[n - SUBLANES:])
    return jnp.concatenate([xs[:n - SUBLANES], bot], axis=0)


def _conv_fwd(x, prev8, w, first):
    acc = x * w[CONV_K - 1:CONV_K]
    shifted = []
    for s in range(1, CONV_K):
        xs = _shift_down(x, prev8, s, first)
        shifted.append(xs)
        acc = acc + xs * w[CONV_K - 1 - s:CONV_K - s]
    return acc, shifted


def _l2n(x):
    outs, rs = [], []
    for h in range(x.shape[1] // LANES):
        xh = x[:, h * LANES:(h + 1) * LANES]
        r = lax.rsqrt(jnp.sum(xh * xh, axis=-1, keepdims=True) + EPS)
        outs.append(xh * r)
        rs.append(r)
    return jnp.concatenate(outs, axis=1), rs


def _split3(x):
    hi = x.astype(BF16)
    r = x - hi.astype(F32)
    mid = r.astype(BF16)
    return hi, mid, (r - mid.astype(F32)).astype(BF16)


def _gate_math(ba, a_row, dt_row):
    al = ba + dt_row
    ea = jnp.exp(a_row)
    return _sigmoid(ba), al, ea, -ea * _softplus(al)


def _spread(x, sel):
    return sum(_dot(part, sel) for part in _split3(x))


def _gather_heads(x, sel):
    return sum(_dot(part, sel, NT) for part in _split3(x)) * (1.0 / LANES)


def _cumsum_chunks(x, reverse=False):
    n = x.shape[0]
    rid = lax.broadcasted_iota(jnp.int32, x.shape, 0) % CHUNK
    s = 1
    while s < CHUNK:
        if reverse:
            x = x + jnp.where(rid < CHUNK - s, pltpu.roll(x, n - s, 0), 0.0)
        else:
            x = x + jnp.where(rid >= s, pltpu.roll(x, s, 0), 0.0)
        s *= 2
    return x


def _gdn_pre_fwd(proj, conv_w, sel_b, sel_a, a_row, dt_row, name, bm=256):
    rows = proj.shape[0]
    w3 = 3 * GDN_W

    def fn(i, x, p8, ba, w, sel_b, sel_a, a_row, dt_row):
        conv, _ = _conv_fwd(x, p8, w, i == 0)
        act = _silu_and_grad(conv)[0]
        qk, _ = _l2n(act[:, :2 * GDN_W])
        beta, _, _, g = _gate_math(ba, a_row, dt_row)
        return (qk[:, :GDN_W], qk[:, GDN_W:], act[:, 2 * GDN_W:], _spread(_cumsum_chunks(g), sel_a),
                _spread(beta, sel_b))
    ins = [_row(proj, w3, 0), _prev8(proj, w3, 0), _row(proj, LANES, (GDN_IN_PAD - LANES) // LANES),
           _full(conv_w), _full(sel_b), _full(sel_a), _full(a_row), _full(dt_row)]
    return _rowcall(fn, name, rows, bm, ins, [(GDN_W, F32)] * 5)


def _gdn_pre_bwd(proj, conv_w, sel_b, sel_a, a_row, dt_row, dq, dk, dv, dgc, dbeta, name, bm=128):
    rows = proj.shape[0]
    w3 = 3 * GDN_W

    def fn(i, x, p8, ba, w, sel_b, sel_a, a_row, dt_row, dq, dk, dv, dgc, dbeta):
        conv, shifted = _conv_fwd(x, p8, w, i == 0)
        act, dact = _silu_and_grad(conv)
        qk, rs = _l2n(act[:, :2 * GDN_W])
        dqk = jnp.concatenate([dq, dk], axis=1)
        parts = []
        for h in range(2 * GDN_HEADS):
            sl = slice(h * LANES, (h + 1) * LANES)
            y, dy = qk[:, sl], dqk[:, sl]
            parts.append(rs[h] * (dy - y * jnp.sum(y * dy, axis=-1, keepdims=True)))
        dconv = jnp.concatenate(parts + [dv], axis=1) * dact
        dws = [_fold8(dconv * xs) for xs in reversed(shifted)] + [_fold8(dconv * x)]
        beta, al, ea, g = _gate_math(ba, a_row, dt_row)
        dg = _cumsum_chunks(_gather_heads(dgc, sel_a), reverse=True)
        dbl = _gather_heads(dbeta, sel_b) * beta * (1.0 - beta)
        dal = dg * (-ea) * _sigmoid(al)
        return (dconv, dbl + dal) + tuple(dws) + (_fold8(dg * g), _fold8(dal))
    ins = [_row(proj, w3, 0), _prev8(proj, w3, 0), _row(proj, LANES, (GDN_IN_PAD - LANES) // LANES),
           _full(conv_w), _full(sel_b), _full(sel_a), _full(a_row), _full(dt_row),
           _row(dq), _row(dk), _row(dv), _row(dgc), _row(dbeta)]
    return _rowcall(fn, name, rows, bm, ins, [(w3, F32), (LANES, BF16)],
                    [(SUBLANES, w3)] * CONV_K + [(SUBLANES, LANES)] * 2)


def _conv_bwd_input(dconv, conv_w, name, bm=256):
    rows, w3 = dconv.shape
    steps = rows // min(bm, rows)

    def fn(i, dc, n8, w):
        acc = dc * w[CONV_K - 1:CONV_K]
        for s in range(1, CONV_K):
            acc = acc + _shift_up(dc, n8, s, i == steps - 1) * w[CONV_K - 1 - s:CONV_K - s]
        return acc
    return _rowcall(fn, name, rows, bm, [_row(dconv), _next8(dconv, w3, 0), _full(conv_w)], [(w3, BF16)])[0]


def _gdn_post_fwd(o, proj, ng, name, bm=512):
    def fn(i, o, z, ng):
        outs = []
        for h in range(GDN_HEADS):
            sl = slice(h * LANES, (h + 1) * LANES)
            _, on = _rms_stats(o[:, sl])
            outs.append(on * ng * _silu_and_grad(z[:, sl])[0])
        return jnp.concatenate(outs, axis=1)
    return _rowcall(fn, name, o.shape[0], bm, [_row(o), _row(proj, GDN_W, 3), _full(ng)], [(GDN_W, BF16)])[0]


def _gdn_post_bwd(o, proj, ng, dcat, name, bm=256):
    def fn(i, o, z, ng, dm):
        dm = dm.astype(F32)
        dos, dzs = [], []
        dng = jnp.zeros((SUBLANES, LANES), F32)
        for h in range(GDN_HEADS):
            sl = slice(h * LANES, (h + 1) * LANES)
            s, ds = _silu_and_grad(z[:, sl])
            r, on = _rms_stats(o[:, sl])
            dzs.append(dm[:, sl] * on * ng * ds)
            dy = dm[:, sl] * s
            dxn = dy * ng
            dos.append(r * (dxn - on * jnp.mean(dxn * on, axis=-1, keepdims=True)))
            dng = dng + _fold8(dy * on)
        return jnp.concatenate(dos, axis=1), jnp.concatenate(dzs, axis=1), dng
    return _rowcall(fn, name, o.shape[0], bm, [_row(o), _row(proj, GDN_W, 3), _full(ng), _row(dcat, GDN_W, 0)],
                    [(GDN_W, F32), (GDN_W, BF16)], [(SUBLANES, LANES)])


def _bdot(a, b, mode="nn"):
    lc, rc = {"nn": (2, 1), "nt": (2, 2), "tn": (1, 1)}[mode]
    return lax.dot_general(a, b, (((lc,), (rc,)), ((0,), (0,))), preferred_element_type=F32)


def _bdot3(a, b, mode="nn"):
    ah, bh = a.astype(BF16), b.astype(BF16)
    al, bl = (a - ah.astype(F32)).astype(BF16), (b - bh.astype(F32)).astype(BF16)
    return _bdot(ah, bh, mode) + _bdot(ah, bl, mode) + _bdot(al, bh, mode)


GDN_CB = 4


def _gdn_chunk(q, k, v, gc, beta, t=None):
    c = CHUNK
    nb = q.shape[0]
    row = lax.broadcasted_iota(jnp.int32, (c, c), 0)
    col = lax.broadcasted_iota(jnp.int32, (c, c), 1)
    tril, strict = row >= col, row > col
    lane0 = (lax.broadcasted_iota(jnp.int32, (nb, c, LANES), 2) == 0).astype(BF16)
    gc_row = sum(_bdot(lane0, part, "nt") for part in _split3(gc))
    dm = jnp.exp(jnp.where(tril, gc[:, :, :c] - gc_row, -1e30))
    eg = jnp.exp(gc)
    gcl = gc[:, c - 1:c, :]
    ekg = jnp.exp(gcl - gc)
    egl = jnp.exp(gcl)
    qs = q * (GDN_DK ** -0.5)
    kb = k * beta
    kk = _bdot(kb, k, "nt")
    a = jnp.where(strict, kk * dm, 0.0)
    vb = v * beta
    kbg = kb * eg
    qk = _bdot(qs, k, "nt")
    p = jnp.where(tril, qk * dm, 0.0)
    out = dict(tril=tril, strict=strict, dm=dm, eg=eg, ekg=ekg, egl=egl, qs=qs, kb=kb, kk=kk, a=a,
               vb=vb, kbg=kbg, qk=qk, p=p, qg=qs * eg, kg=k * ekg)
    if t is None:
        y = -a
        t = (row == col).astype(F32) + y
        for _ in range(5):
            y = _bdot3(y, y)
            t = t + _bdot3(t, y)
        out.update(u=_bdot3(t, vb), w=_bdot3(t, kbg))
    out["t"] = t
    return out


def _gdn_stack(ref):
    return jnp.stack([ref[c * CHUNK:(c + 1) * CHUNK, h * LANES:(h + 1) * LANES]
                      for c in range(GDN_CB) for h in range(GDN_HEADS)])


def _gdn_unstack(x, ref):
    for c in range(GDN_CB):
        for h in range(GDN_HEADS):
            ref[c * CHUNK:(c + 1) * CHUNK, h * LANES:(h + 1) * LANES] = x[c * GDN_HEADS + h]


def _gdn_fwd(q, k, v, gc, beta, name):
    rows = q.shape[0]
    n_chunks = rows // CHUNK
    steps = n_chunks // GDN_CB
    blk = pl.BlockSpec((GDN_CB * CHUNK, GDN_W), lambda n: (n, 0))
    st = pl.BlockSpec((GDN_HEADS, GDN_CB, GDN_DK, LANES), lambda n: (0, n, 0, 0))
    tinv = pl.BlockSpec((GDN_CB, GDN_HEADS, CHUNK, CHUNK), lambda n: (n, 0, 0, 0))

    def body(q_ref, k_ref, v_ref, g_ref, b_ref, o_ref, st_ref, t_ref, w_ref, vn_ref, s_ref):
        @pl.when(pl.program_id(0) == 0)
        def _():
            s_ref[...] = jnp.zeros(s_ref.shape, F32)
        c = _gdn_chunk(*[_gdn_stack(r) for r in (q_ref, k_ref, v_ref, g_ref, b_ref)])
        _gdn_unstack(c["w"], w_ref)
        s = s_ref[...]
        for i in range(GDN_CB):
            hs = slice(i * GDN_HEADS, (i + 1) * GDN_HEADS)
            rs = slice(i * CHUNK, (i + 1) * CHUNK)
            st_ref[:, i] = s
            vn = c["u"][hs] - _bdot(c["w"][hs], s)
            o = _bdot(c["qg"][hs], s) + _bdot(c["p"][hs], vn)
            s = s * c["egl"][hs] + _bdot(c["kg"][hs], vn, "tn")
            t_ref[i] = c["t"][hs]
            for h in range(GDN_HEADS):
                o_ref[rs, h * LANES:(h + 1) * LANES] = o[h]
                vn_ref[rs, h * LANES:(h + 1) * LANES] = vn[h]
        s_ref[...] = s

    f = jax.ShapeDtypeStruct((rows, GDN_W), F32)
    return pl.pallas_call(
        body, name=name, grid=(steps,), in_specs=[blk] * 5, out_specs=[blk, st, tinv, blk, blk],
        out_shape=[f, jax.ShapeDtypeStruct((GDN_HEADS, n_chunks, GDN_DK, LANES), F32),
                   jax.ShapeDtypeStruct((n_chunks, GDN_HEADS, CHUNK, CHUNK), F32), f, f],
        scratch_shapes=[pltpu.VMEM((GDN_HEADS, GDN_DK, LANES), F32)],
        compiler_params=_params(("arbitrary",)))(q, k, v, gc, beta)


def _gdn_bwd(q, k, v, gc, beta, states, tinv, w, vn, do, name):
    rows = q.shape[0]
    n_chunks = rows // CHUNK
    steps = n_chunks // GDN_CB
    blk = pl.BlockSpec((GDN_CB * CHUNK, GDN_W), lambda n: (steps - 1 - n, 0))
    st = pl.BlockSpec((GDN_HEADS, GDN_CB, GDN_DK, LANES), lambda n: (0, steps - 1 - n, 0, 0))
    ti = pl.BlockSpec((GDN_CB, GDN_HEADS, CHUNK, CHUNK), lambda n: (steps - 1 - n, 0, 0, 0))
    nbatch = GDN_CB * GDN_HEADS

    def lanesum(x):
        return jnp.broadcast_to(jnp.sum(x, axis=-1, keepdims=True), x.shape)

    def body(q_ref, k_ref, v_ref, g_ref, b_ref, st_ref, t_ref, w_ref, vn_ref, do_ref,
             dq_ref, dk_ref, dv_ref, dg_ref, db_ref, ds_ref):
        @pl.when(pl.program_id(0) == 0)
        def _():
            ds_ref[...] = jnp.zeros(ds_ref.shape, F32)
        q, k, v, gc, beta, w, vn, do = [_gdn_stack(r) for r in (q_ref, k_ref, v_ref, g_ref, b_ref, w_ref, vn_ref, do_ref)]
        t = t_ref[...].reshape(nbatch, CHUNK, CHUNK)
        s = jnp.stack([st_ref[h, i] for i in range(GDN_CB) for h in range(GDN_HEADS)])
        c = _gdn_chunk(q, k, v, gc, beta, t)
        tril, strict, dm = c["tril"], c["strict"], c["dm"]
        dsn = ds_ref[...]
        dvn_c, dkg_c, dgl_c = [None] * GDN_CB, [None] * GDN_CB, [None] * GDN_CB
        for i in reversed(range(GDN_CB)):
            hs = slice(i * GDN_HEADS, (i + 1) * GDN_HEADS)
            dvn_c[i] = _bdot(c["p"][hs], do[hs], "tn") + _bdot(c["kg"][hs], dsn)
            dkg_c[i] = _bdot(vn[hs], dsn, "nt")
            dgl_c[i] = jnp.sum(jnp.sum(s[hs] * dsn, axis=2, keepdims=True), axis=1, keepdims=True) * c["egl"][hs]
            dsn = dsn * c["egl"][hs] + _bdot(c["qg"][hs], do[hs], "tn") - _bdot(w[hs], dvn_c[i], "tn")
        ds_ref[...] = dsn
        dvn, dkg, dgl = jnp.concatenate(dvn_c), jnp.concatenate(dkg_c), jnp.concatenate(dgl_c)
        dp = jnp.where(tril, _bdot(do, vn, "nt"), 0.0)
        dqg = _bdot(do, s, "nt")
        dw = -_bdot(dvn, s, "nt")
        dvb = _bdot3(t, dvn, "tn")
        dkbg = _bdot3(t, dw, "tn")
        dt = _bdot(dvn, c["vb"], "nt") + _bdot(dw, c["kbg"], "nt")
        da = jnp.where(strict, -_bdot3(_bdot3(t, dt, "tn"), t, "nt"), 0.0)
        dkk = da * dm
        dqk = dp * dm
        dkb = _bdot(dkk, k) + dkbg * c["eg"]
        dk = _bdot(dkk, c["kb"], "tn") + _bdot(dqk, c["qs"], "tn") + dkg * c["ekg"] + dkb * beta
        dqs = _bdot(dqk, k) + dqg * c["eg"]
        e = da * c["a"] + dp * c["p"]
        ones = jnp.ones((nbatch, CHUNK, LANES), BF16)
        col_sums = sum(_bdot(part, ones, "tn") for part in _split3(e))
        kg_term = lanesum(dkg * c["kg"])
        dgc = (jnp.broadcast_to(jnp.sum(e, axis=-1, keepdims=True), (nbatch, CHUNK, LANES)) - col_sums
               + lanesum(dqg * c["qg"]) - kg_term + lanesum(dkbg * c["kbg"]))
        dgcl = jnp.sum(kg_term, axis=1, keepdims=True) + dgl
        last = lax.broadcasted_iota(jnp.int32, (CHUNK, LANES), 0) == CHUNK - 1
        _gdn_unstack(dqs * (GDN_DK ** -0.5), dq_ref)
        _gdn_unstack(dk, dk_ref)
        _gdn_unstack(dvb * beta, dv_ref)
        _gdn_unstack(dgc + jnp.where(last, dgcl, 0.0), dg_ref)
        _gdn_unstack(lanesum(dvb * v) + lanesum(dkb * k), db_ref)

    return pl.pallas_call(
        body, name=name, grid=(steps,), in_specs=[blk] * 5 + [st, ti, blk, blk, blk], out_specs=[blk] * 5,
        out_shape=[jax.ShapeDtypeStruct((rows, GDN_W), F32)] * 5,
        scratch_shapes=[pltpu.VMEM((GDN_HEADS, GDN_DK, LANES), F32)],
        compiler_params=_params(("arbitrary",)))(q, k, v, gc, beta, states, tinv, w, vn, do)


def _swa_masks(first):
    r = lax.broadcasted_iota(jnp.int32, (SWA_BLOCK, 2 * SWA_BLOCK), 0)
    c = lax.broadcasted_iota(jnp.int32, (SWA_BLOCK, 2 * SWA_BLOCK), 1)
    band = (c > r) & (c <= r + SWA_BLOCK)
    return band & (jnp.logical_not(first) | (c >= SWA_BLOCK))


def _swa_stack(ref, j):
    lane = lax.broadcasted_iota(jnp.int32, (1, LANES), 1)
    parts = []
    for g in range(SWA_GROUP):
        ch = j * (SWA_GROUP // 2) + g // 2
        keep = (lane < SWA_DH) if g % 2 == 0 else (lane >= SWA_DH)
        parts.append(ref[:, ch * LANES:(ch + 1) * LANES] * keep.astype(ref.dtype))
    return jnp.concatenate(parts, axis=0)


def _swa_unstack(x2, j, out_ref):
    low = lax.broadcasted_iota(jnp.int32, (SWA_BLOCK, LANES), 1) < SWA_DH
    for c3 in range(SWA_GROUP // 2):
        even = x2[(2 * c3) * SWA_BLOCK:(2 * c3 + 1) * SWA_BLOCK]
        odd = x2[(2 * c3 + 1) * SWA_BLOCK:(2 * c3 + 2) * SWA_BLOCK]
        ch = j * (SWA_GROUP // 2) + c3
        out_ref[:, ch * LANES:(ch + 1) * LANES] = jnp.where(low, even, odd).astype(out_ref.dtype)


def _swa_probs(s, sink, mask):
    s = jnp.where(mask, s, -1e30)
    m = jnp.maximum(jnp.max(s, axis=-1, keepdims=True), sink)
    p = jnp.where(mask, jnp.exp(s - m), 0.0)
    es = jnp.exp(sink - m)
    inv = 1.0 / (jnp.sum(p, axis=-1, keepdims=True) + es)
    return p * inv, es * inv


def _swa_scores(q_ref, kc_ref, kp_ref, sink_ref, j, mask):
    sl = slice(j * LANES, (j + 1) * LANES)
    qst = _swa_stack(q_ref, j)
    kw = jnp.concatenate([kp_ref[:, sl], kc_ref[:, sl]], axis=0)
    s = _dot(qst, kw, NT) * (SWA_DH ** -0.5)
    ps = [_swa_probs(s[g * SWA_BLOCK:(g + 1) * SWA_BLOCK], sink_ref[j * SWA_GROUP + g], mask)
          for g in range(SWA_GROUP)]
    return qst, kw, ps


def _swa_fwd(q, k2, v2, sinks, name):
    rows = q.shape[0]
    nb = rows // SWA_BLOCK
    w = SWA_HEADS * SWA_DH
    kvw = SWA_KV_HEADS * LANES
    cur = pl.BlockSpec((SWA_BLOCK, w), lambda i: (i, 0))
    kcur = pl.BlockSpec((SWA_BLOCK, kvw), lambda i: (i, 0))
    kprev = pl.BlockSpec((SWA_BLOCK, kvw), lambda i: (jnp.maximum(i - 1, 0), 0))

    def body(sink_ref, q_ref, kc_ref, kp_ref, vc_ref, vp_ref, o_ref):
        mask = _swa_masks(pl.program_id(0) == 0)
        for j in range(SWA_KV_HEADS):
            sl = slice(j * LANES, (j + 1) * LANES)
            _, _, ps = _swa_scores(q_ref, kc_ref, kp_ref, sink_ref, j, mask)
            vw = jnp.concatenate([vp_ref[:, sl], vc_ref[:, sl]], axis=0)
            pst = jnp.concatenate([p.astype(BF16) for p, _ in ps], axis=0)
            _swa_unstack(_dot(pst, vw), j, o_ref)

    return pl.pallas_call(
        body, name=name, grid=(nb,),
        in_specs=[pl.BlockSpec(memory_space=pltpu.SMEM), cur, kcur, kprev, kcur, kprev], out_specs=cur,
        out_shape=jax.ShapeDtypeStruct((rows, w), BF16),
        compiler_params=_params(("arbitrary",)))(sinks, q, k2, k2, v2, v2)


def _swa_bwd(q, k2, v2, sinks, dcat, name):
    rows = q.shape[0]
    nb = rows // SWA_BLOCK
    w = SWA_HEADS * SWA_DH
    kvw = SWA_KV_HEADS * LANES
    cur = pl.BlockSpec((SWA_BLOCK, w), lambda i: (jnp.minimum(i, nb - 1), 0))
    kcur = pl.BlockSpec((SWA_BLOCK, kvw), lambda i: (jnp.minimum(i, nb - 1), 0))
    kprev = pl.BlockSpec((SWA_BLOCK, kvw), lambda i: (jnp.clip(i - 1, 0, nb - 1), 0))
    late = pl.BlockSpec((SWA_BLOCK, kvw), lambda i: (jnp.maximum(i - 1, 0), 0))
    acc_spec = pl.BlockSpec((SUBLANES, LANES), lambda i: (0, 0))

    def body(sink_ref, q_ref, kc_ref, kp_ref, vc_ref, vp_ref, do_ref, dq_ref, dk_ref, dv_ref, dsk_ref,
             ck_ref, cv_ref):
        i = pl.program_id(0)

        @pl.when(i == 0)
        def _():
            ck_ref[...] = jnp.zeros(ck_ref.shape, F32)
            cv_ref[...] = jnp.zeros(cv_ref.shape, F32)
            dsk_ref[...] = jnp.zeros(dsk_ref.shape, F32)

        @pl.when(i == nb)
        def _():
            dk_ref[...] = ck_ref[...]
            dv_ref[...] = cv_ref[...]

        @pl.when(i < nb)
        def _():
            mask = _swa_masks(i == 0)
            lane = lax.broadcasted_iota(jnp.int32, (SUBLANES, LANES), 1)
            dsk = jnp.zeros((SUBLANES, LANES), F32)
            for j in range(SWA_KV_HEADS):
                sl = slice(j * LANES, (j + 1) * LANES)
                qst, kw, ps = _swa_scores(q_ref, kc_ref, kp_ref, sink_ref, j, mask)
                vw = jnp.concatenate([vp_ref[:, sl], vc_ref[:, sl]], axis=0)
                dost = _swa_stack(do_ref, j)
                dpr = _dot(dost, vw, NT)
                dss = []
                for g in range(SWA_GROUP):
                    p, sink_p = ps[g]
                    dpg = dpr[g * SWA_BLOCK:(g + 1) * SWA_BLOCK]
                    delta = jnp.sum(p * dpg, axis=-1, keepdims=True)
                    dss.append((p * (dpg - delta)).astype(BF16))
                    dsg = jnp.sum(-sink_p * delta, axis=0, keepdims=True)
                    dsk = dsk + jnp.where(lane == j * SWA_GROUP + g, dsg, 0.0)
                dsst = jnp.concatenate(dss, axis=0)
                pst = jnp.concatenate([p.astype(BF16) for p, _ in ps], axis=0)
                _swa_unstack(_dot(dsst, kw) * (SWA_DH ** -0.5), j, dq_ref)
                dk = _dot(dsst, qst, TN) * (SWA_DH ** -0.5)
                dv = _dot(pst, dost, TN)
                dk = dk + pltpu.roll(dk, SWA_DH, 1)
                dv = dv + pltpu.roll(dv, SWA_DH, 1)
                dk_ref[:, sl] = ck_ref[:, sl] + dk[:SWA_BLOCK]
                dv_ref[:, sl] = cv_ref[:, sl] + dv[:SWA_BLOCK]
                ck_ref[:, sl] = dk[SWA_BLOCK:]
                cv_ref[:, sl] = dv[SWA_BLOCK:]
            dsk_ref[...] += dsk

    f = jax.ShapeDtypeStruct((rows, kvw), F32)
    return pl.pallas_call(
        body, name=name, grid=(nb + 1,),
        in_specs=[pl.BlockSpec(memory_space=pltpu.SMEM), cur, kcur, kprev, kcur, kprev, cur],
        out_specs=[cur, late, late, acc_spec],
        out_shape=[jax.ShapeDtypeStruct((rows, w), F32), f, f, jax.ShapeDtypeStruct((SUBLANES, LANES), F32)],
        scratch_shapes=[pltpu.VMEM((SWA_BLOCK, kvw), F32), pltpu.VMEM((SWA_BLOCK, kvw), F32)],
        compiler_params=_params(("arbitrary",)))(sinks, q, k2, k2, v2, v2, dcat)


def _mem_probs(mq, kbd):
    s = _dot(mq.astype(BF16), kbd) * (MEM_DH ** -0.5)
    ps = []
    for h in range(MEM_HEADS):
        sh = s[:, h * MEM_LEN:(h + 1) * MEM_LEN]
        e = jnp.exp(sh - jnp.max(sh, axis=-1, keepdims=True))
        ps.append(e / jnp.sum(e, axis=-1, keepdims=True))
    return ps


def _mem_fwd(proj, cb, kbd, vbd, name, bm=512):
    def fn(i, mq, kbd, vbd):
        p = jnp.concatenate(_mem_probs(mq, kbd), axis=1)
        return _dot(p.astype(BF16), vbd)
    return _rowcall(fn, name, proj.shape[0], bm, [_row(proj, MEM_W, cb), _full(kbd), _full(vbd)], [(MEM_W, BF16)])[0]


def _mem_bwd(proj, cb, kbd, vbd, dcat, name, bm=512):
    def fn(i, mq, kbd, vbd, do):
        ps = _mem_probs(mq, kbd)
        dp = _dot(do, vbd, NT)
        dss = []
        for h in range(MEM_HEADS):
            dph = dp[:, h * MEM_LEN:(h + 1) * MEM_LEN]
            dss.append(ps[h] * (dph - jnp.sum(ps[h] * dph, axis=-1, keepdims=True)))
        ds = (jnp.concatenate(dss, axis=1) * (MEM_DH ** -0.5)).astype(BF16)
        p = jnp.concatenate(ps, axis=1).astype(BF16)
        return _dot(ds, kbd, NT), _dot(mq.astype(BF16), ds, TN), _dot(p, do, TN)
    return _rowcall(fn, name, proj.shape[0], bm, [_row(proj, MEM_W, cb), _full(kbd), _full(vbd), _row(dcat, MEM_W, 3)],
                    [(MEM_W, BF16)], [(MEM_W, MEM_HEADS * MEM_LEN), (MEM_HEADS * MEM_LEN, MEM_W)])


def _mem_expand(mkv):
    feat_head = jnp.arange(MEM_W) // MEM_DH
    slot_head = jnp.arange(MEM_HEADS * MEM_LEN) // MEM_LEN
    on = feat_head[:, None] == slot_head[None, :]
    kbd = jnp.where(on, jnp.tile(mkv[:, :MEM_W].T, (1, MEM_HEADS)), 0.0)
    vbd = jnp.where(on.T, jnp.tile(mkv[:, MEM_W:], (MEM_HEADS, 1)), 0.0)
    return kbd.astype(BF16), vbd.astype(BF16)


def _mem_collapse(dkbd, dvbd):
    dk = [dkbd[h * MEM_DH:(h + 1) * MEM_DH, h * MEM_LEN:(h + 1) * MEM_LEN].T for h in range(MEM_HEADS)]
    dv = [dvbd[h * MEM_LEN:(h + 1) * MEM_LEN, h * MEM_DH:(h + 1) * MEM_DH] for h in range(MEM_HEADS)]
    return jnp.concatenate(dk + dv, axis=1)


def _adamw_math(w, g, m, v):
    m = ADAM_B1 * m + (1.0 - ADAM_B1) * g
    v = ADAM_B2 * v + (1.0 - ADAM_B2) * (g * g)
    m_hat = m / (1.0 - ADAM_B1 ** ADAM_STEP)
    v_hat = v / (1.0 - ADAM_B2 ** ADAM_STEP)
    return -ADAM_LR * (m_hat / (jnp.sqrt(v_hat) + ADAM_EPS) + ADAM_WD * w), m, v


def _adamw(w, g, m, v, name, bm=512):
    d = w.shape[1]
    return _rowcall(lambda i, *a: _adamw_math(*a), name, w.shape[0], bm, [_row(w), _row(g), _row(m), _row(v)], [(d, F32)] * 3)


def _adamw_halves(w, g_own, g_other, my_core, m, v, name):
    layers, rows, n = w.shape
    r = rows // 2
    bm = LANES if r % LANES == 0 else r
    per_half = r // bm
    nat = pl.BlockSpec((None, bm, n), lambda l, c, i: (l, c * per_half + i, 0))
    own = pl.BlockSpec((None, bm, n), lambda l, c, i: (l, i, 0))
    other = pl.BlockSpec((None, None, bm, n), lambda l, c, i: (0, l, i, 0))

    def body(core_ref, w_ref, own_ref, other_ref, m_ref, v_ref, go_ref, d_ref, mo_ref, vo_ref):
        g = jnp.where(core_ref[0] == pl.program_id(1), own_ref[...], other_ref[...])
        go_ref[...] = g
        d_ref[...], mo_ref[...], vo_ref[...] = _adamw_math(w_ref[...], g, m_ref[...], v_ref[...])

    return pl.pallas_call(
        body, name=name, grid=(layers, 2, per_half),
        in_specs=[pl.BlockSpec(memory_space=pltpu.SMEM), nat, own, other, nat, nat], out_specs=[nat] * 4,
        out_shape=[jax.ShapeDtypeStruct(w.shape, F32)] * 4,
        compiler_params=_params(("parallel", "parallel", "parallel")))(my_core, w, g_own, g_other, m, v)


def _sum_slots(buf, name, bm=128):
    n, rows, w = buf.shape
    bm = min(bm, rows)
    assert rows % bm == 0

    def body(b_ref, o_ref):
        acc = b_ref[0].astype(F32)
        for s in range(1, n):
            acc = acc + b_ref[s].astype(F32)
        o_ref[...] = acc

    return pl.pallas_call(
        body, name=name, grid=(rows // bm,), in_specs=[pl.BlockSpec((n, bm, w), lambda i: (0, i, 0))],
        out_specs=pl.BlockSpec((bm, w), lambda i: (i, 0)), out_shape=jax.ShapeDtypeStruct((rows, w), F32),
        compiler_params=_params(("parallel",)))(buf)


def _exchange(srcs, same, masks, name, keep_own=True):
    slots = N_DEV if len(masks) == N_DEV - 1 else (2 if keep_own else 1)
    n_arr, n_peer = len(srcs), len(masks)
    shapes = [s.shape if sm else s.shape[1:] for s, sm in zip(srcs, same)]

    def body(*refs):
        src_refs, out_refs = refs[:n_arr], refs[n_arr:2 * n_arr]
        send_sems, recv_sems, local_sems = refs[2 * n_arr:]
        x, y, c = lax.axis_index("x"), lax.axis_index("y"), lax.axis_index("c")
        me = 4 * x + 2 * y + c

        def flip(v, bit):
            return 1 - v if bit else v

        def slot_of(dev):
            return dev if slots == N_DEV else (dev % 2 if slots == 2 else 0)

        def piece(a, p):
            return src_refs[a] if same[a] else src_refs[a].at[p]

        local = []
        if keep_own:
            local = [pltpu.make_async_copy(piece(a, me), out_refs[a].at[slot_of(me)], local_sems.at[a])
                     for a in range(n_arr)]
        for cp in local:
            cp.start()
        copies = []
        for idx, k in enumerate(masks):
            peer = (flip(x, k & 4), flip(y, k & 2), flip(c, k & 1))
            peer_id = 4 * peer[0] + 2 * peer[1] + peer[2]
            for a in range(n_arr):
                sem = idx * n_arr + a
                cp = pltpu.make_async_remote_copy(
                    src_ref=piece(a, peer_id), dst_ref=out_refs[a].at[slot_of(me)],
                    send_sem=send_sems.at[sem], recv_sem=recv_sems.at[sem], device_id=peer, device_id_type=MESH)
                cp.start()
                copies.append((cp, pltpu.make_async_remote_copy(
                    src_ref=piece(a, peer_id), dst_ref=out_refs[a].at[slot_of(peer_id)],
                    send_sem=send_sems.at[sem], recv_sem=recv_sems.at[sem], device_id=peer, device_id_type=MESH)))
        for cp, landing in copies:
            cp.wait_send()
            landing.wait_recv()
        for cp in local:
            cp.wait()

    any_spec = pl.BlockSpec(memory_space=pl.ANY)
    n_sem = n_arr * n_peer
    return pl.pallas_call(
        body, name=name, in_specs=[any_spec] * n_arr, out_specs=[any_spec] * n_arr,
        out_shape=[jax.ShapeDtypeStruct((slots,) + tuple(sh), s.dtype) for sh, s in zip(shapes, srcs)],
        scratch_shapes=[pltpu.SemaphoreType.DMA((n_sem,)), pltpu.SemaphoreType.DMA((n_sem,)),
                        pltpu.SemaphoreType.DMA((n_arr,))],
        )(*srcs)


ALL_PEERS = tuple(range(1, N_DEV))
SIBLING = (1,)


def _send_to_sibling(srcs, name):
    n_arr = len(srcs)

    def body(*refs):
        src_refs, out_refs = refs[:n_arr], refs[n_arr:2 * n_arr]
        send_sems, recv_sems = refs[2 * n_arr:]
        x, y, c = lax.axis_index("x"), lax.axis_index("y"), lax.axis_index("c")
        copies = [pltpu.make_async_remote_copy(
            src_ref=src_refs[a].at[1 - c], dst_ref=out_refs[a], send_sem=send_sems.at[a], recv_sem=recv_sems.at[a],
            device_id=(x, y, 1 - c), device_id_type=MESH) for a in range(n_arr)]
        for cp in copies:
            cp.start()
        for cp in copies:
            cp.wait()

    any_spec = pl.BlockSpec(memory_space=pl.ANY)
    return pl.pallas_call(
        body, name=name, in_specs=[any_spec] * n_arr, out_specs=[any_spec] * n_arr,
        out_shape=[jax.ShapeDtypeStruct(s.shape[1:], s.dtype) for s in srcs],
        scratch_shapes=[pltpu.SemaphoreType.DMA((n_arr,)), pltpu.SemaphoreType.DMA((n_arr,))],
        )(*srcs)


def _add_core_parts(mine, theirs, my_core, name, bm=128):
    _, chips, rows, n = mine.shape
    bm = min(bm, rows)
    assert rows % bm == 0

    def body(core_ref, p0_ref, p1_ref, t_ref, o_ref):
        own = jnp.where(core_ref[0] == 0, p0_ref[...], p1_ref[...])
        o_ref[...] = (own.astype(F32) + t_ref[...].astype(F32)).astype(o_ref.dtype)

    part = lambda k: pl.BlockSpec((None, None, bm, n), lambda s, i, k=k: (k, s, i, 0))
    flat = pl.BlockSpec((None, bm, n), lambda s, i: (s, i, 0))
    return pl.pallas_call(
        body, name=name, grid=(chips, rows // bm),
        in_specs=[pl.BlockSpec(memory_space=pltpu.SMEM), part(0), part(1), flat], out_specs=flat,
        out_shape=jax.ShapeDtypeStruct(theirs.shape, mine.dtype),
        compiler_params=_params(("parallel", "parallel")))(my_core, mine, mine, theirs)


def _scatter_by_chip(srcs, small, name):
    n_arr = len(srcs)

    def body(*refs):
        src_refs, small_ref = refs[:n_arr], refs[n_arr]
        out_refs, small_out = refs[n_arr + 1:2 * n_arr + 1], refs[2 * n_arr + 1]
        send_sems, recv_sems, local_sems = refs[2 * n_arr + 2:]
        x, y, c = lax.axis_index("x"), lax.axis_index("y"), lax.axis_index("c")
        my_chip, me = 2 * x + y, 4 * x + 2 * y + c
        chips = [(1 - x, y), (x, 1 - y), (1 - x, 1 - y)]
        local = [pltpu.make_async_copy(src_refs[a].at[my_chip], out_refs[a].at[my_chip], local_sems.at[a])
                 for a in range(n_arr)]
        local.append(pltpu.make_async_copy(small_ref, small_out.at[me], local_sems.at[n_arr]))
        for cp in local:
            cp.start()
        copies = []
        for j, (px, py) in enumerate(chips):
            their_chip = 2 * px + py
            for a in range(n_arr):
                sem = j * n_arr + a
                cp = pltpu.make_async_remote_copy(
                    src_ref=src_refs[a].at[their_chip], dst_ref=out_refs[a].at[my_chip],
                    send_sem=send_sems.at[sem], recv_sem=recv_sems.at[sem], device_id=(px, py, c), device_id_type=MESH)
                cp.start()
                copies.append((cp, pltpu.make_async_remote_copy(
                    src_ref=src_refs[a].at[their_chip], dst_ref=out_refs[a].at[their_chip],
                    send_sem=send_sems.at[sem], recv_sem=recv_sems.at[sem], device_id=(px, py, c), device_id_type=MESH)))
        for idx, k in enumerate(ALL_PEERS):
            peer = (1 - x if k & 4 else x, 1 - y if k & 2 else y, 1 - c if k & 1 else c)
            peer_id = 4 * peer[0] + 2 * peer[1] + peer[2]
            sem = 3 * n_arr + idx
            cp = pltpu.make_async_remote_copy(
                src_ref=small_ref, dst_ref=small_out.at[me], send_sem=send_sems.at[sem], recv_sem=recv_sems.at[sem],
                device_id=peer, device_id_type=MESH)
            cp.start()
            copies.append((cp, pltpu.make_async_remote_copy(
                src_ref=small_ref, dst_ref=small_out.at[peer_id], send_sem=send_sems.at[sem], recv_sem=recv_sems.at[sem],
                device_id=peer, device_id_type=MESH)))
        for cp, landing in copies:
            cp.wait_send()
            landing.wait_recv()
        for cp in local:
            cp.wait()

    any_spec = pl.BlockSpec(memory_space=pl.ANY)
    n_sem = 3 * n_arr + len(ALL_PEERS)
    return pl.pallas_call(
        body, name=name, in_specs=[any_spec] * (n_arr + 1), out_specs=[any_spec] * (n_arr + 1),
        out_shape=[jax.ShapeDtypeStruct(s.shape, s.dtype) for s in srcs]
        + [jax.ShapeDtypeStruct((N_DEV,) + small.shape, small.dtype)],
        scratch_shapes=[pltpu.SemaphoreType.DMA((n_sem,)), pltpu.SemaphoreType.DMA((n_sem,)),
                        pltpu.SemaphoreType.DMA((n_arr + 1,))],
        )(*srcs, small)


def _gather_two_level(srcs, name):
    n_arr = len(srcs)

    def body(*refs):
        src_refs, out_refs = refs[:n_arr], refs[n_arr:2 * n_arr]
        send_sems, recv_sems, local_sems = refs[2 * n_arr:]
        x, y, c = lax.axis_index("x"), lax.axis_index("y"), lax.axis_index("c")
        sibling = (x, y, 1 - c)
        chips = [(1 - x, y), (x, 1 - y), (1 - x, 1 - y)]

        def slot(px, py, pc):
            return 4 * px + 2 * py + pc

        def copy(a, k, block, to, own=False):
            return pltpu.make_async_remote_copy(
                src_ref=src_refs[a] if own else out_refs[a].at[slot(*block)], dst_ref=out_refs[a].at[slot(*block)],
                send_sem=send_sems.at[a * 7 + k], recv_sem=recv_sems.at[a * 7 + k], device_id=to, device_id_type=MESH)

        local = [pltpu.make_async_copy(src_refs[a], out_refs[a].at[slot(x, y, c)], local_sems.at[a]) for a in range(n_arr)]
        for cp in local:
            cp.start()
        started = []
        for a in range(n_arr):
            started.append(copy(a, 0, (x, y, c), sibling, own=True))
            started += [copy(a, 1 + j, (x, y, c), (*chip, c), own=True) for j, chip in enumerate(chips)]
        for cp in started:
            cp.start()
        for j, chip in enumerate(chips):
            for a in range(n_arr):
                copy(a, 1 + j, (*chip, c), (x, y, c)).wait_recv()
                passed = copy(a, 4 + j, (*chip, c), sibling)
                passed.start()
                started.append(passed)
        for a in range(n_arr):
            copy(a, 0, sibling, (x, y, c)).wait_recv()
            for j, chip in enumerate(chips):
                copy(a, 4 + j, (*chip, 1 - c), (x, y, c)).wait_recv()
        for cp in started:
            cp.wait_send()
        for cp in local:
            cp.wait()

    any_spec = pl.BlockSpec(memory_space=pl.ANY)
    return pl.pallas_call(
        body, name=name, in_specs=[any_spec] * n_arr, out_specs=[any_spec] * n_arr,
        out_shape=[jax.ShapeDtypeStruct((N_DEV,) + s.shape, s.dtype) for s in srcs],
        scratch_shapes=[pltpu.SemaphoreType.DMA((7 * n_arr,)), pltpu.SemaphoreType.DMA((7 * n_arr,)),
                        pltpu.SemaphoreType.DMA((n_arr,))],
        )(*srcs)


def _pack(arrays, rows):
    flat = jnp.concatenate([a.reshape(-1) for a in arrays])
    return jnp.pad(flat, (0, rows * D_MODEL - flat.shape[0])).reshape(rows, D_MODEL)


def _unpack(buf, shapes):
    flat = buf.reshape(-1)
    out, off = [], 0
    for s in shapes:
        n = math.prod(s)
        out.append(flat[off:off + n].reshape(s))
        off += n
    return out


def _rows_for(shapes, mult):
    n = sum(math.prod(s) for s in shapes)
    rows = -(-n // D_MODEL)
    return -(-rows // mult) * mult


SHARD_AXIS = dict(w_mem_kv=1, w_out=1, w_gate_up=2, w_down=1, gdn_w_in=2, swa_w_q=1, w_kv=0, gdn_conv=2)
HALF_AXIS = dict(w_mem_kv=1, w_out=1, w_gate_up=1, w_down=1, gdn_w_in=1, swa_w_q=1, w_kv=0)


def _my_half(shard, name, c):
    ax = HALF_AXIS[name]
    h = shard.shape[ax] // 2
    return lax.dynamic_slice_in_dim(shard, c * h, h, axis=ax)


def _piece_layout(name, half_shape):
    dims, pos = [], {}
    for i, d in enumerate(half_shape):
        if i == SHARD_AXIS[name]:
            pos["chip"] = len(dims)
            dims.append(4)
        if i == HALF_AXIS[name]:
            pos["core"] = len(dims)
            dims.append(2)
        pos[i] = len(dims)
        dims.append(d)
    return dims, [pos["chip"], pos["core"]] + [pos[i] for i in range(len(half_shape))]


def _full_shape(name, half_shape):
    return tuple(d * (4 if i == SHARD_AXIS[name] else 1) * (2 if i == HALF_AXIS[name] else 1)
                 for i, d in enumerate(half_shape))


def _assemble(pieces, name):
    half_shape = pieces.shape[1:]
    if half_shape[-1] % LANES:
        chips = [jnp.concatenate([pieces[2 * s], pieces[2 * s + 1]], axis=HALF_AXIS[name]) for s in range(4)]
        return jnp.concatenate(chips, axis=SHARD_AXIS[name])
    dims, perm = _piece_layout(name, half_shape)
    inverse = [perm.index(i) for i in range(len(perm))]
    return pieces.reshape((4, 2) + half_shape).transpose(inverse).reshape(_full_shape(name, half_shape))


def _to_pieces(full, name, half_shape):
    if half_shape[-1] % LANES:
        ns, nh = half_shape[SHARD_AXIS[name]], half_shape[HALF_AXIS[name]]
        parts = [lax.slice_in_dim(lax.slice_in_dim(full, s * ns, (s + 1) * ns, axis=SHARD_AXIS[name]),
                                  c * nh, (c + 1) * nh, axis=HALF_AXIS[name])
                 for c in range(2) for s in range(4)]
        return jnp.stack(parts).reshape((2, 4) + tuple(half_shape))
    dims, perm = _piece_layout(name, half_shape)
    return full.reshape(dims).transpose([perm[1], perm[0]] + perm[2:])


def _from_halves(halves, name):
    ax = HALF_AXIS[name]
    s = jnp.moveaxis(halves, 0, ax)
    return s.reshape(s.shape[:ax] + (2 * s.shape[ax + 1],) + s.shape[ax + 2:])


def _rope_tables(positions):
    half = ROT_DIM // 2
    inv = ROPE_THETA ** (-jnp.arange(0, ROT_DIM, 2, dtype=F32) / ROT_DIM)
    ang = positions.astype(F32)[:, None] * inv
    cos, sin = jnp.cos(ang), jnp.sin(ang)
    rows = positions.shape[0]
    one = jnp.ones((rows, SWA_DH - ROT_DIM), F32)
    zero = jnp.zeros((rows, SWA_DH - ROT_DIM), F32)
    zh = jnp.zeros((rows, half), F32)
    c64 = jnp.concatenate([cos, cos, one], axis=1)
    a64 = jnp.concatenate([-sin, zh, zero], axis=1)
    b64 = jnp.concatenate([zh, sin, zero], axis=1)
    return tuple(jnp.concatenate([t, t], axis=1) for t in (c64, a64, b64))


def _pair_heads(t):
    return jnp.concatenate([t[:, :SWA_DH], t[:, :SWA_DH], t[:, SWA_DH:], t[:, SWA_DH:]], axis=1)


def _unpair_heads(t):
    return jnp.concatenate([t[:, :SWA_DH], t[:, LANES:LANES + SWA_DH]], axis=1)


def _gdn_in_pad(w):
    o2 = 4 * GDN_W
    pad = jnp.zeros(w.shape[:-1] + (GDN_IN_PAD - GDN_IN,), w.dtype)
    return jnp.concatenate([w[..., :o2], w[..., o2 + 2 * GDN_HEADS:], w[..., o2:o2 + 2 * GDN_HEADS], pad], axis=-1)


def _gdn_in_unpad(w):
    o2 = 4 * GDN_W
    return jnp.concatenate([w[..., :o2], w[..., o2 + MEM_W:o2 + MEM_W + 2 * GDN_HEADS], w[..., o2:o2 + MEM_W]], axis=-1)


def _head_rows(v):
    return jnp.pad(v.astype(F32), (GDN_HEADS, LANES - 2 * GDN_HEADS))[None, :]


def _selectors():
    lane = jnp.arange(LANES)[:, None]
    head = (jnp.arange(GDN_W) // LANES)[None, :]
    return (lane == head).astype(BF16), (lane == head + GDN_HEADS).astype(BF16)


def _local_step(x, mem, positions, target, w):
    rows = x.shape[0]
    tabs = _rope_tables(positions)
    sel_b, sel_a = _selectors()
    row2 = lambda v: v.reshape(1, -1).astype(F32)

    w_gu = _ff_interleave(w["w_gate_up"])
    mem_n = _rms_fwd(mem, row2(w["ln_mem"]), "mem_norm")
    saved = []
    kt = vt = None
    for l in range(DEPTH):
        s = dict(x0=x)
        h = _rms_fwd(x, row2(w["ln_mix"][l]), f"norm_mix{l}")
        mkv = _mm(mem_n, w["w_mem_kv"], "nn", f"mem_kv{l}", layer=l)
        kbd, vbd = _mem_expand(mkv)
        if l < N_A:
            proj = _mm(h, w["gdn_w_in"], "nn", f"gdn_in{l}", layer=l)
            a_row, dt_row = _head_rows(w["gdn_A_log"][l]), _head_rows(w["gdn_dt_bias"][l])
            q, k, v, gc, beta = _gdn_pre_fwd(proj, w["gdn_conv"][l], sel_b, sel_a, a_row, dt_row, f"gdn_pre{l}")
            o, states, tinv, gw, vn = _gdn_fwd(q, k, v, gc, beta, f"gdn_scan{l}")
            mix = _gdn_post_fwd(o, proj, row2(w["gdn_norm"][l]), f"gdn_post{l}")
            mq_cb = (3 * GDN_W + GDN_W) // MEM_W
            s.update(q=q, k=k, v=v, gc=gc, beta=beta, o=o, states=states, tinv=tinv, gw=gw, vn=vn,
                     a_row=a_row, dt_row=dt_row)
        else:
            proj = _mm(h, w["swa_w_q"], "nn", f"swa_in{l}", layer=l - N_A)
            qr = _rope(proj, SWA_HEADS * SWA_DH, 0, tabs, 1, f"rope_q{l}", BF16)
            mix = _swa_fwd(qr, kt, vt, w["swa_sinks"][l - N_A], f"swa{l}")
            mq_cb = (SWA_HEADS * SWA_DH) // MEM_W
            s.update(qr=qr)
        mem_o = _mem_fwd(proj, mq_cb, kbd, vbd, f"mem_attn{l}")
        cat = jnp.concatenate([mix, mem_o], axis=1)
        x1 = _mm(cat, w["w_out"], "nn", f"out_proj{l}", add=x, layer=l)
        h2 = _rms_fwd(x1, row2(w["ln_ffn"][l]), f"norm_ffn{l}")
        gu, act = _gate_up_fwd(h2, w_gu, l, f"gate_up{l}")
        x = _mm(act, w["w_down"], "nn", f"down{l}", add=x1, layer=l)
        s.update(h=h, proj=proj, kbd=kbd, vbd=vbd, mq_cb=mq_cb, cat=cat, x1=x1, h2=h2, gu=gu, act=act)
        saved.append(s)
        if l == N_A - 1:
            x_kv = x
            h_kv = _rms_fwd(x, row2(w["ln_kv"]), "norm_kv")
            kv = _mm(h_kv, w["w_kv"], "nn", "kv_proj")
            kr = _rope(kv, LANES, 0, tabs, 1, "rope_k", F32)
            kt = _pair_heads(kr).astype(BF16)
            vt = _pair_heads(kv[:, LANES:]).astype(BF16)

    gr = {}
    dx, dxb, loss_part, dlnf = _final_loss(x, row2(w["ln_final"]), target, "final_loss")
    gr["ln_final"] = dlnf.sum(axis=0)
    dln_mix, dln_ffn = [None] * DEPTH, [None] * DEPTH
    dw_mem_kv, dw_out, dw_gu, dw_dn, dgdn_in, dswa_q = [
        lax.empty(w[n].shape, BF16) for n in ("w_mem_kv", "w_out", "w_gate_up", "w_down", "gdn_w_in", "swa_w_q")]
    dgdn_conv, dgdn_a, dgdn_dt, dgdn_norm = [None] * N_A, [None] * N_A, [None] * N_A, [None] * N_A
    dswa_sinks = [None] * N_B
    dmem_n = None
    dkt = dvt = None
    for l in reversed(range(DEPTH)):
        s = saved[l]
        if l == N_A - 1:
            dkr = _unpair_heads(dkt)
            dk = _rope(dkr, LANES, 0, tabs, -1, "rope_k_bwd", BF16)
            dkv = jnp.concatenate([dk, _unpair_heads(dvt).astype(BF16)], axis=1)
            dh_kv = _mm(dkv, w["w_kv"], "nt", "kv_proj_dx", BF16)
            gr["w_kv"] = _mm(h_kv, dkv, "tn", "kv_proj_dw", BF16)
            dx, dxb, dg = _rms_bwd(x_kv, row2(w["ln_kv"]), dh_kv, dx, "norm_kv_bwd")
            gr["ln_kv"] = dg.sum(axis=0)
        dgu = _down_bwd(dxb, w["w_down"], l, s["gu"], f"down_dx{l}")
        dw_dn = _mm(s["act"], dxb, "tn", f"down_dw{l}", dst=dw_dn, layer=l)
        dh2 = _mm(dgu, w_gu, "nt", f"gate_up_dx{l}", BF16, layer=l)
        dw_gu = _mm(s["h2"], dgu, "tn", f"gate_up_dw{l}", dst=dw_gu, layer=l)
        dx, dxb, dg = _rms_bwd(s["x1"], row2(w["ln_ffn"][l]), dh2, dx, f"norm_ffn_bwd{l}")
        dln_ffn[l] = dg.sum(axis=0)
        dcat = _mm(dxb, w["w_out"], "nt", f"out_proj_dx{l}", BF16, layer=l)
        dw_out = _mm(s["cat"], dxb, "tn", f"out_proj_dw{l}", dst=dw_out, layer=l)
        dmq, dkbd, dvbd = _mem_bwd(s["proj"], s["mq_cb"], s["kbd"], s["vbd"], dcat, f"mem_attn_bwd{l}")
        dmkv = _mem_collapse(dkbd, dvbd).astype(BF16)
        dw_mem_kv = _mm(mem_n, dmkv, "tn", f"mem_kv_dw{l}", dst=dw_mem_kv, layer=l)
        dmem_n = _mm(dmkv, w["w_mem_kv"], "nt", f"mem_kv_dx{l}", add=dmem_n, layer=l)
        if l < N_A:
            do, dz, dng = _gdn_post_bwd(s["o"], s["proj"], row2(w["gdn_norm"][l]), dcat, f"gdn_post_bwd{l}")
            dq, dk, dv, dg_, dbeta = _gdn_bwd(s["q"], s["k"], s["v"], s["gc"], s["beta"], s["states"], s["tinv"], s["gw"],
                                              s["vn"], do, f"gdn_scan_bwd{l}")
            res = _gdn_pre_bwd(s["proj"], w["gdn_conv"][l], sel_b, sel_a, s["a_row"], s["dt_row"], dq, dk, dv, dg_, dbeta,
                               f"gdn_pre_bwd{l}")
            dconv, dba = res[0], res[1]
            dgdn_conv[l] = jnp.stack([r.sum(axis=0) for r in res[2:2 + CONV_K]])
            dgdn_a[l] = res[2 + CONV_K].sum(axis=0)[GDN_HEADS:2 * GDN_HEADS]
            dgdn_dt[l] = res[3 + CONV_K].sum(axis=0)[GDN_HEADS:2 * GDN_HEADS]
            dgdn_norm[l] = dng.sum(axis=0)
            dqkv = _conv_bwd_input(dconv, w["gdn_conv"][l], f"gdn_conv_bwd{l}")
            dproj = jnp.concatenate([dqkv, dz, dmq, dba], axis=1)
            dh = _mm(dproj, w["gdn_w_in"], "nt", f"gdn_in_dx{l}", BF16, layer=l)
            dgdn_in = _mm(s["h"], dproj, "tn", f"gdn_in_dw{l}", dst=dgdn_in, layer=l)
        else:
            b = l - N_A
            dqr, dkt_l, dvt_l, dsk = _swa_bwd(s["qr"], kt, vt, w["swa_sinks"][b], dcat, f"swa_bwd{l}")
            dkt = dkt_l if dkt is None else dkt + dkt_l
            dvt = dvt_l if dvt is None else dvt + dvt_l
            dswa_sinks[b] = dsk[0, :SWA_HEADS]
            dq = _rope(dqr, SWA_HEADS * SWA_DH, 0, tabs, -1, f"rope_q_bwd{l}", BF16)
            dproj = jnp.concatenate([dq, dmq], axis=1)
            dh = _mm(dproj, w["swa_w_q"], "nt", f"swa_in_dx{l}", BF16, layer=b)
            dswa_q = _mm(s["h"], dproj, "tn", f"swa_in_dw{l}", dst=dswa_q, layer=b)
        dx, dxb, dg = _rms_bwd(s["x0"], row2(w["ln_mix"][l]), dh, dx, f"norm_mix_bwd{l}")
        dln_mix[l] = dg.sum(axis=0)
    _, _, dg = _rms_bwd(mem, row2(w["ln_mem"]), dmem_n, None, "mem_norm_bwd")
    gr["ln_mem"] = dg.sum(axis=0)
    gr.update(ln_mix=jnp.stack(dln_mix), ln_ffn=jnp.stack(dln_ffn), w_mem_kv=dw_mem_kv, w_out=dw_out,
              w_gate_up=_ff_deinterleave(dw_gu), w_down=dw_dn, gdn_w_in=dgdn_in, gdn_conv=jnp.stack(dgdn_conv),
              gdn_A_log=jnp.stack(dgdn_a), gdn_dt_bias=jnp.stack(dgdn_dt), gdn_norm=jnp.stack(dgdn_norm),
              swa_w_q=dswa_q, swa_sinks=jnp.stack(dswa_sinks))
    return loss_part, dx, gr


def kernel(x, mem, positions, ln_mix, ln_ffn, ln_mem, w_mem_kv, w_out, w_gate_up, w_down, gdn_w_in, gdn_conv, gdn_A_log, gdn_dt_bias, gdn_norm, swa_w_q, swa_sinks, ln_kv, w_kv, ln_final, loss_target, m_ln_mix, m_ln_ffn, m_ln_mem, m_w_mem_kv, m_w_out, m_w_gate_up, m_w_down, m_gdn_w_in, m_gdn_conv, m_gdn_A_log, m_gdn_dt_bias, m_gdn_norm, m_swa_w_q, m_swa_sinks, m_ln_kv, m_w_kv, m_ln_final, v_ln_mix, v_ln_ffn, v_ln_mem, v_w_mem_kv, v_w_out, v_w_gate_up, v_w_down, v_gdn_w_in, v_gdn_conv, v_gdn_A_log, v_gdn_dt_bias, v_gdn_norm, v_swa_w_q, v_swa_sinks, v_ln_kv, v_w_kv, v_ln_final):
    given = dict(locals())
    wts = {n: given[n] for n in WEIGHTS}
    c = lax.axis_index("c")

    halves = [_my_half(wts[n].astype(BF16), n, c) for n in SHARDED]
    half_shapes = [h.shape for h in halves]
    conv_shape = wts["gdn_conv"].shape
    cpack = _pack([wts["gdn_conv"]], 16)
    conv_half = lax.dynamic_slice_in_dim(cpack, c * SUBLANES, SUBLANES, axis=0)
    got = _gather_two_level(halves + [conv_half], "gather_weights")
    full = {n: wts[n] for n in SMALL}
    for n, g in zip(SHARDED, got):
        full[n] = _assemble(g, n)
    conv_all = got[-1].reshape(4, 16, D_MODEL)
    full["gdn_conv"] = jnp.concatenate([_unpack(conv_all[s], [conv_shape])[0] for s in range(4)], axis=2)
    full["gdn_w_in"] = _gdn_in_pad(full["gdn_w_in"])

    loss_part, dx, gr = _local_step(x[0], mem[0], positions[0], loss_target[0], full)
    gr["gdn_w_in"] = _gdn_in_unpad(gr["gdn_w_in"])

    pieces = [_to_pieces(gr[n].astype(BF16), n, hs) for n, hs in zip(SHARDED, half_shapes)]
    small_shapes = [wts[n].shape for n in SMALL] + [conv_shape[:2] + (4 * conv_shape[2],), (SUBLANES, LANES)]
    rows_s = _rows_for(small_shapes, SUBLANES)
    spack = _pack([gr[n] for n in SMALL] + [gr["gdn_conv"], loss_part], rows_s)
    my_core = c.astype(jnp.int32).reshape(1)
    from_sibling = _send_to_sibling(pieces, "pair_grads")
    chip_parts = [_add_core_parts(p.reshape(2, 4, -1, p.shape[-1]), t.reshape(4, -1, t.shape[-1]), my_core, f"pair_sum_{n}")
                  for n, p, t in zip(SHARDED, pieces, from_sibling)]
    parts = _scatter_by_chip(chip_parts, spack, "scatter_grads")
    mine = [_sum_slots(p, f"sum_{n}").reshape(hs) for n, p, hs in zip(SHARDED, parts, half_shapes)]
    ssum = _unpack(_sum_slots(parts[-1], "sum_small"), small_shapes)
    theirs = _exchange(mine, [True] * len(mine), SIBLING, "swap_grad_halves", keep_own=False)
    g_all = dict(zip(SMALL, ssum[:len(SMALL)]))
    chip = 2 * lax.axis_index("x") + lax.axis_index("y")
    g_all["gdn_conv"] = lax.dynamic_slice_in_dim(ssum[len(SMALL)], chip * conv_shape[2], conv_shape[2], axis=2)
    loss = jnp.sum(ssum[-1])

    out = dict(grad=g_all, delta={}, new_m={}, new_v={})
    for n, own, other in zip(SHARDED, mine, theirs):
        as3d = lambda a: a.reshape((-1,) + a.shape[-2:])
        res = _adamw_halves(as3d(wts[n]), as3d(own), other.reshape((1, -1) + other.shape[-2:]), my_core,
                            as3d(given["m_" + n]), as3d(given["v_" + n]), f"adamw_{n}")
        for kind, r in zip(("grad", "delta", "new_m", "new_v"), res):
            out[kind][n] = r.reshape(wts[n].shape)
    small_names = SMALL + ("gdn_conv",)
    small_w_shapes = [wts[n].shape for n in small_names]
    rows_a = _rows_for(small_w_shapes, SUBLANES)
    res = _adamw(_pack([wts[n] for n in small_names], rows_a), _pack([g_all[n] for n in small_names], rows_a),
                 _pack([given["m_" + n] for n in small_names], rows_a),
                 _pack([given["v_" + n] for n in small_names], rows_a), "adamw_small")
    for kind, r in zip(("delta", "new_m", "new_v"), res):
        out[kind].update(zip(small_names, _unpack(r, small_w_shapes)))
    return (loss, dx[None], *[out["grad"][n] for n in WEIGHTS], *[out["delta"][n] for n in WEIGHTS],
            *[out["new_m"][n] for n in WEIGHTS], *[out["new_v"][n] for n in WEIGHTS])
```

```python
import functools
import math

import jax
import jax.numpy as jnp
from jax import lax
from jax.experimental import pallas as pl
from jax.experimental.pallas import tpu as pltpu

F32 = jnp.float32
BF16 = jnp.bfloat16
HI = lax.Precision.HIGHEST
MESH = pl.DeviceIdType.MESH

D_MODEL = 1024
DEPTH = 4
N_A = 2
N_B = 2
EPS = 1e-6
GDN_HEADS = 6
GDN_DK = 128
GDN_W = 768
CONV_K = 4
CHUNK = 64
SWA_HEADS = 12
SWA_KV_HEADS = 2
SWA_DH = 64
SWA_GROUP = 6
SWA_GW = SWA_GROUP * SWA_DH
SWA_BLOCK = 128
ROPE_THETA = 500000.0
ROT_DIM = 16
MEM_LEN = 256
MEM_HEADS = 4
MEM_DH = 64
MEM_W = 256
D_FF = 2816
GDN_IN = 3340
GDN_IN_PAD = 3456
ADAM_LR = 0.001
ADAM_B1 = 0.9
ADAM_B2 = 0.999
ADAM_EPS = 1e-08
ADAM_WD = 0.01
ADAM_STEP = 10

N_DEV = 8
LANES = 128
SUBLANES = 8
V7X_VMEM_LIMIT = 56 * 2**20
MM_VMEM_BUDGET = 44 * 2**20

SHARDED = ("w_mem_kv", "w_out", "w_gate_up", "w_down", "gdn_w_in", "swa_w_q", "w_kv")
SMALL = ("ln_mix", "ln_ffn", "ln_mem", "gdn_A_log", "gdn_dt_bias", "gdn_norm", "swa_sinks", "ln_kv", "ln_final")
WEIGHTS = ("ln_mix", "ln_ffn", "ln_mem", "w_mem_kv", "w_out", "w_gate_up", "w_down", "gdn_w_in", "gdn_conv",
           "gdn_A_log", "gdn_dt_bias", "gdn_norm", "swa_w_q", "swa_sinks", "ln_kv", "w_kv", "ln_final")


def _params(sem=None, **kw):
    return pltpu.CompilerParams(dimension_semantics=sem, vmem_limit_bytes=V7X_VMEM_LIMIT, **kw)


def _dot(a, b, dims=(((1,), (0,)), ((), ())), precision=None):
    return lax.dot_general(a, b, dims, precision=precision, preferred_element_type=F32)


NT = (((1,), (1,)), ((), ()))
TN = (((0,), (0,)), ((), ()))


def _fold8(v):
    r, w = v.shape
    return v.reshape(r // SUBLANES, SUBLANES, w).sum(axis=0)


def _row(a, w=None, cb=0):
    return ("row", a, a.shape[1] if w is None else w, cb)


def _full(a):
    return ("full", a, None, None)


def _prev8(a, w, cb=0):
    return ("prev8", a, w, cb)


def _next8(a, w, cb=0):
    return ("next8", a, w, cb)


def _rowcall(fn, name, rows, bm, ins, outs, accs=()):
    bm = min(bm, rows)
    assert rows % bm == 0 and bm % SUBLANES == 0
    steps = rows // bm
    r8 = bm // SUBLANES
    in_specs, arrays = [], []
    for kind, a, w, cb in ins:
        arrays.append(a)
        if kind == "row":
            in_specs.append(pl.BlockSpec((bm, w), lambda i, cb=cb: (i, cb)))
        elif kind == "full":
            in_specs.append(pl.BlockSpec(a.shape, lambda i, nd=a.ndim: (0,) * nd))
        elif kind == "prev8":
            in_specs.append(pl.BlockSpec((SUBLANES, w), lambda i, cb=cb: (jnp.maximum(i * r8 - 1, 0), cb)))
        else:
            last = rows // SUBLANES - 1
            in_specs.append(pl.BlockSpec((SUBLANES, w), lambda i, cb=cb: (jnp.minimum((i + 1) * r8, last), cb)))
    out_shape = [jax.ShapeDtypeStruct((rows, w), dt) for w, dt in outs]
    out_specs = [pl.BlockSpec((bm, w), lambda i: (i, 0)) for w, _ in outs]
    out_shape += [jax.ShapeDtypeStruct(s, F32) for s in accs]
    out_specs += [pl.BlockSpec(s, lambda i: (0, 0)) for s in accs]
    n_in, n_out = len(ins), len(outs)

    def body(*refs):
        i = pl.program_id(0)
        res = fn(i, *[r[...] for r in refs[:n_in]])
        if not isinstance(res, (tuple, list)):
            res = (res,)
        for r, v in zip(refs[n_in:n_in + n_out], res[:n_out]):
            r[...] = v.astype(r.dtype)
        if accs:
            @pl.when(i == 0)
            def _():
                for r in refs[n_in + n_out:]:
                    r[...] = jnp.zeros(r.shape, F32)
            for r, v in zip(refs[n_in + n_out:], res[n_out:]):
                r[...] += v

    res = pl.pallas_call(
        body, name=name, grid=(steps,), in_specs=in_specs, out_specs=out_specs, out_shape=out_shape,
        compiler_params=_params(("arbitrary",)))(*arrays)
    return res


def _tile(n, cap):
    for t in (1408, 1152, 1024, 896, 768, 640, 512, 384, 256, 128):
        if t <= cap and n % t == 0:
            return t
    return n


def _mm(a, b, mode, name, out_dtype=F32, add=None, layer=None, dst=None):
    if mode == "tn":
        s, m = a.shape
        n = b.shape[1]
        bm, bn, bk = _tile(m, 1408), _tile(n, 1408), min(s, 1024)
        nk = s // bk

        def body(a_ref, b_ref, *rest):
            o_ref, acc_ref = rest[-2:]
            k = pl.program_id(2)

            @pl.when(k == 0)
            def _():
                acc_ref[...] = jnp.zeros(acc_ref.shape, F32)
            acc_ref[...] += _dot(a_ref[...].astype(BF16), b_ref[...].astype(BF16), TN)

            @pl.when(k == nk - 1)
            def _():
                o_ref[...] = acc_ref[...].astype(o_ref.dtype)

        in_specs = [pl.BlockSpec((bk, bm), lambda i, j, k: (k, i)), pl.BlockSpec((bk, bn), lambda i, j, k: (k, j))]
        if dst is None:
            return pl.pallas_call(
                body, name=name, grid=(m // bm, n // bn, nk), in_specs=in_specs,
                out_specs=pl.BlockSpec((bm, bn), lambda i, j, k: (i, j)),
                out_shape=jax.ShapeDtypeStruct((m, n), out_dtype),
                scratch_shapes=[pltpu.VMEM((bm, bn), F32)],
                compiler_params=_params(("parallel", "parallel", "arbitrary")))(a, b)
        return pl.pallas_call(
            body, name=name, grid=(m // bm, n // bn, nk), in_specs=in_specs + [pl.BlockSpec(memory_space=pl.ANY)],
            out_specs=pl.BlockSpec((None, bm, bn), lambda i, j, k: (layer, i, j)),
            out_shape=jax.ShapeDtypeStruct(dst.shape, dst.dtype), input_output_aliases={2: 0},
            scratch_shapes=[pltpu.VMEM((bm, bn), F32)],
            compiler_params=_params(("parallel", "parallel", "arbitrary")))(a, b, dst)

    m, k = a.shape
    if layer is None:
        n = b.shape[1] if mode == "nn" else b.shape[0]
    else:
        n = b.shape[2] if mode == "nn" else b.shape[1]
    out_bytes = jnp.dtype(out_dtype).itemsize

    def vmem_need(bm, bn):
        need = 2 * bm * k * a.dtype.itemsize + 2 * bn * k * b.dtype.itemsize + bm * bn * (2 * out_bytes + 4)
        return need + (2 * bm * bn * 4 if add is not None else 0)

    bm, bn = min(m, 512), _tile(n, 512)
    for cand in ((2048, 1408), (2048, 1024), (2048, 512), (1024, 1408), (1024, 1024), (1024, 512)):
        tm, tn = min(m, cand[0]), _tile(n, cand[1])
        if m % tm == 0 and vmem_need(tm, tn) <= MM_VMEM_BUDGET:
            bm, bn = tm, tn
            break
    dims = NT if mode == "nt" else (((1,), (0,)), ((), ()))
    if layer is None:
        b_spec = (pl.BlockSpec((k, bn), lambda i, j: (0, j)) if mode == "nn" else pl.BlockSpec((bn, k), lambda i, j: (j, 0)))
    elif mode == "nn":
        b_spec = pl.BlockSpec((None, k, bn), lambda i, j: (layer, 0, j))
    else:
        b_spec = pl.BlockSpec((None, bn, k), lambda i, j: (layer, j, 0))
    in_specs = [pl.BlockSpec((bm, k), lambda i, j: (i, 0)), b_spec]
    args = [a, b]
    if add is not None:
        in_specs.append(pl.BlockSpec((bm, bn), lambda i, j: (i, j)))
        args.append(add)

    def body(a_ref, b_ref, *rest):
        o_ref = rest[-1]
        acc = _dot(a_ref[...].astype(BF16), b_ref[...].astype(BF16), dims)
        if add is not None:
            acc = acc + rest[0][...]
        o_ref[...] = acc.astype(o_ref.dtype)

    return pl.pallas_call(
        body, name=name, grid=(m // bm, n // bn), in_specs=in_specs,
        out_specs=pl.BlockSpec((bm, bn), lambda i, j: (i, j)),
        out_shape=jax.ShapeDtypeStruct((m, n), out_dtype),
        compiler_params=_params(("parallel", "parallel")))(*args)


def _sigmoid(x):
    return 0.5 * jnp.tanh(0.5 * x) + 0.5


def _softplus(x):
    return jnp.maximum(x, 0.0) + jnp.log(1.0 + jnp.exp(-jnp.abs(x)))


def _silu_and_grad(x):
    s = _sigmoid(x)
    return x * s, s * (1.0 + x * (1.0 - s))


def _rms_stats(x):
    r = lax.rsqrt(jnp.mean(x * x, axis=-1, keepdims=True) + EPS)
    return r, x * r


def _rms_fwd(x, g, name, out_dtype=BF16, bm=512):
    def fn(i, x, g):
        _, xn = _rms_stats(x)
        return xn * g
    return _rowcall(fn, name, x.shape[0], bm, [_row(x), _full(g)], [(x.shape[1], out_dtype)])[0]


def _rms_bwd_math(x, g, dy):
    r, xn = _rms_stats(x)
    dxn = dy * g
    dx = r * (dxn - xn * jnp.mean(dxn * xn, axis=-1, keepdims=True))
    return dx, dy * xn


def _rms_bwd(x, g, dy, res, name, bm=256):
    d = x.shape[1]

    def fn(i, x, g, dy, *res_):
        dx, dg = _rms_bwd_math(x, g, dy.astype(F32))
        if res_:
            dx = dx + res_[0]
        return dx, dx, _fold8(dg)
    ins = [_row(x), _full(g), _row(dy)] + ([_row(res)] if res is not None else [])
    return _rowcall(fn, name, x.shape[0], bm, ins, [(d, F32), (d, BF16)], [(SUBLANES, d)])


def _final_loss(x, g, target, name, bm=256):
    d = x.shape[1]

    def fn(i, x, g, t):
        r, xn = _rms_stats(x)
        err = xn * g - t
        dy = err * (1.0 / d)
        dxn = dy * g
        dx = r * (dxn - xn * jnp.mean(dxn * xn, axis=-1, keepdims=True))
        e2 = _fold8(err * err)
        lp = e2[:, 0:LANES]
        for c in range(1, d // LANES):
            lp = lp + e2[:, c * LANES:(c + 1) * LANES]
        return dx, dx, lp * (0.5 / d), _fold8(dy * xn)
    return _rowcall(fn, name, x.shape[0], bm, [_row(x), _full(g), _row(target)], [(d, F32), (d, BF16)],
                    [(SUBLANES, LANES), (SUBLANES, d)])


FF_TILE = 256


def _move_col_tiles(w, src_tile, name):
    layers, rows, cols = w.shape

    def body(x_ref, o_ref):
        o_ref[...] = x_ref[...]

    return pl.pallas_call(
        body, name=name, grid=(layers, cols // FF_TILE),
        in_specs=[pl.BlockSpec((None, rows, FF_TILE), lambda l, j: (l, 0, src_tile(j)))],
        out_specs=pl.BlockSpec((None, rows, FF_TILE), lambda l, j: (l, 0, j)),
        out_shape=jax.ShapeDtypeStruct(w.shape, w.dtype),
        compiler_params=_params(("parallel", "parallel")))(w)


def _ff_interleave(w, name):
    half = D_FF // FF_TILE
    return _move_col_tiles(w, lambda j: (j % 2) * half + j // 2, name)


def _ff_deinterleave(w, name):
    half = D_FF // FF_TILE
    return _move_col_tiles(w, lambda j: jnp.where(j < half, 2 * j, 2 * (j - half) + 1), name)


def _gate_up_fwd(h, w_gu, layer, name, bm=2048):
    rows, k = h.shape
    bm = min(bm, rows)

    def body(a_ref, b_ref, gu_ref, act_ref):
        acc = _dot(a_ref[...], b_ref[...])
        gu_ref[...] = acc.astype(gu_ref.dtype)
        act_ref[...] = (_silu_and_grad(acc[:, :FF_TILE])[0] * acc[:, FF_TILE:]).astype(act_ref.dtype)

    return pl.pallas_call(
        body, name=name, grid=(rows // bm, D_FF // FF_TILE),
        in_specs=[pl.BlockSpec((bm, k), lambda i, j: (i, 0)), pl.BlockSpec((None, k, 2 * FF_TILE), lambda i, j: (layer, 0, j))],
        out_specs=[pl.BlockSpec((bm, 2 * FF_TILE), lambda i, j: (i, j)), pl.BlockSpec((bm, FF_TILE), lambda i, j: (i, j))],
        out_shape=[jax.ShapeDtypeStruct((rows, 2 * D_FF), BF16), jax.ShapeDtypeStruct((rows, D_FF), BF16)],
        compiler_params=_params(("parallel", "parallel")))(h, w_gu)


def _down_bwd(dx, w_down, layer, gu, name, bm=2048):
    rows, k = dx.shape
    bm = min(bm, rows)

    def body(a_ref, b_ref, gu_ref, o_ref):
        da = _dot(a_ref[...], b_ref[...], NT)
        gu = gu_ref[...].astype(F32)
        s, ds = _silu_and_grad(gu[:, :FF_TILE])
        o_ref[:, :FF_TILE] = (da * gu[:, FF_TILE:] * ds).astype(o_ref.dtype)
        o_ref[:, FF_TILE:] = (da * s).astype(o_ref.dtype)

    return pl.pallas_call(
        body, name=name, grid=(rows // bm, D_FF // FF_TILE),
        in_specs=[pl.BlockSpec((bm, k), lambda i, j: (i, 0)), pl.BlockSpec((None, FF_TILE, k), lambda i, j: (layer, j, 0)),
                  pl.BlockSpec((bm, 2 * FF_TILE), lambda i, j: (i, j))],
        out_specs=pl.BlockSpec((bm, 2 * FF_TILE), lambda i, j: (i, j)),
        out_shape=jax.ShapeDtypeStruct((rows, 2 * D_FF), BF16),
        compiler_params=_params(("parallel", "parallel")))(dx, w_down, gu)


def _rope_apply(x, tabs, sign):
    cos, ta, tb = tabs
    outs = []
    for c in range(x.shape[1] // LANES):
        xc = x[:, c * LANES:(c + 1) * LANES]
        if sign > 0:
            o = xc * cos + pltpu.roll(xc, LANES - 8, 1) * ta + pltpu.roll(xc, 8, 1) * tb
        else:
            o = xc * cos + pltpu.roll(xc * ta, 8, 1) + pltpu.roll(xc * tb, LANES - 8, 1)
        outs.append(o)
    return outs[0] if len(outs) == 1 else jnp.concatenate(outs, axis=1)


def _rope(x, w, cb, tabs, sign, name, out_dtype, bm=512):
    def fn(i, x, c, a, b):
        return _rope_apply(x.astype(F32), (c, a, b), sign)
    return _rowcall(fn, name, x.shape[0], bm, [_row(x, w, cb)] + [_row(t) for t in tabs], [(w, out_dtype)])[0]


def _shift_down(x, prev8, s, first):
    xs = pltpu.roll(x, s, 0)
    rp = pltpu.roll(prev8, s, 0) * jnp.where(first, 0.0, 1.0)
    rid = lax.broadcasted_iota(jnp.int32, rp.shape, 0)
    top = jnp.where(rid < s, rp, xs[0:SUBLANES])
    return jnp.concatenate([top, xs[SUBLANES:]], axis=0)


def _shift_up(x, next8, s, last):
    n = x.shape[0]
    xs = pltpu.roll(x, n - s, 0)
    rn = pltpu.roll(next8, SUBLANES - s, 0) * jnp.where(last, 0.0, 1.0)
    rid = lax.broadcasted_iota(jnp.int32, rn.shape, 0)
    bot = jnp.where(rid >= SUBLANES - s, rn, xs[n - SUBLANES:])
    return jnp.concatenate([xs[:n - SUBLANES], bot], axis=0)


def _conv_fwd(x, prev8, w, first):
    acc = x * w[CONV_K - 1:CONV_K]
    shifted = []
    for s in range(1, CONV_K):
        xs = _shift_down(x, prev8, s, first)
        shifted.append(xs)
        acc = acc + xs * w[CONV_K - 1 - s:CONV_K - s]
    return acc, shifted


def _l2n(x):
    outs, rs = [], []
    for h in range(x.shape[1] // LANES):
        xh = x[:, h * LANES:(h + 1) * LANES]
        r = lax.rsqrt(jnp.sum(xh * xh, axis=-1, keepdims=True) + EPS)
        outs.append(xh * r)
        rs.append(r)
    return jnp.concatenate(outs, axis=1), rs


def _split3(x):
    hi = x.astype(BF16)
    r = x - hi.astype(F32)
    mid = r.astype(BF16)
    return hi, mid, (r - mid.astype(F32)).astype(BF16)


def _gate_math(ba, a_row, dt_row):
    al = ba + dt_row
    ea = jnp.exp(a_row)
    return _sigmoid(ba), al, ea, -ea * _softplus(al)


def _spread(x, sel):
    return sum(_dot(part, sel) for part in _split3(x))


def _gather_heads(x, sel):
    return sum(_dot(part, sel, NT) for part in _split3(x)) * (1.0 / LANES)


def _cumsum_chunks(x, reverse=False):
    n = x.shape[0]
    rid = lax.broadcasted_iota(jnp.int32, x.shape, 0) % CHUNK
    s = 1
    while s < CHUNK:
        if reverse:
            x = x + jnp.where(rid < CHUNK - s, pltpu.roll(x, n - s, 0), 0.0)
        else:
            x = x + jnp.where(rid >= s, pltpu.roll(x, s, 0), 0.0)
        s *= 2
    return x


def _gdn_pre_fwd(proj, conv_w, sel_b, sel_a, a_row, dt_row, name, bm=256):
    rows = proj.shape[0]
    w3 = 3 * GDN_W

    def fn(i, x, p8, ba, w, sel_b, sel_a, a_row, dt_row):
        conv, _ = _conv_fwd(x, p8, w, i == 0)
        act = _silu_and_grad(conv)[0]
        qk, _ = _l2n(act[:, :2 * GDN_W])
        beta, _, _, g = _gate_math(ba, a_row, dt_row)
        return (qk[:, :GDN_W], qk[:, GDN_W:], act[:, 2 * GDN_W:], _spread(_cumsum_chunks(g), sel_a),
                _spread(beta, sel_b))
    ins = [_row(proj, w3, 0), _prev8(proj, w3, 0), _row(proj, LANES, (GDN_IN_PAD - LANES) // LANES),
           _full(conv_w), _full(sel_b), _full(sel_a), _full(a_row), _full(dt_row)]
    return _rowcall(fn, name, rows, bm, ins, [(GDN_W, F32)] * 5)


def _gdn_pre_bwd(proj, conv_w, sel_b, sel_a, a_row, dt_row, dq, dk, dv, dgc, dbeta, name, bm=128):
    rows = proj.shape[0]
    w3 = 3 * GDN_W

    def fn(i, x, p8, ba, w, sel_b, sel_a, a_row, dt_row, dq, dk, dv, dgc, dbeta):
        conv, shifted = _conv_fwd(x, p8, w, i == 0)
        act, dact = _silu_and_grad(conv)
        qk, rs = _l2n(act[:, :2 * GDN_W])
        dqk = jnp.concatenate([dq, dk], axis=1)
        parts = []
        for h in range(2 * GDN_HEADS):
            sl = slice(h * LANES, (h + 1) * LANES)
            y, dy = qk[:, sl], dqk[:, sl]
            parts.append(rs[h] * (dy - y * jnp.sum(y * dy, axis=-1, keepdims=True)))
        dconv = jnp.concatenate(parts + [dv], axis=1) * dact
        dws = [_fold8(dconv * xs) for xs in reversed(shifted)] + [_fold8(dconv * x)]
        beta, al, ea, g = _gate_math(ba, a_row, dt_row)
        dg = _cumsum_chunks(_gather_heads(dgc, sel_a), reverse=True)
        dbl = _gather_heads(dbeta, sel_b) * beta * (1.0 - beta)
        dal = dg * (-ea) * _sigmoid(al)
        return (dconv, dbl + dal) + tuple(dws) + (_fold8(dg * g), _fold8(dal))
    ins = [_row(proj, w3, 0), _prev8(proj, w3, 0), _row(proj, LANES, (GDN_IN_PAD - LANES) // LANES),
           _full(conv_w), _full(sel_b), _full(sel_a), _full(a_row), _full(dt_row),
           _row(dq), _row(dk), _row(dv), _row(dgc), _row(dbeta)]
    return _rowcall(fn, name, rows, bm, ins, [(w3, F32), (LANES, BF16)],
                    [(SUBLANES, w3)] * CONV_K + [(SUBLANES, LANES)] * 2)


def _conv_bwd_input(dconv, conv_w, name, bm=256):
    rows, w3 = dconv.shape
    steps = rows // min(bm, rows)

    def fn(i, dc, n8, w):
        acc = dc * w[CONV_K - 1:CONV_K]
        for s in range(1, CONV_K):
            acc = acc + _shift_up(dc, n8, s, i == steps - 1) * w[CONV_K - 1 - s:CONV_K - s]
        return acc
    return _rowcall(fn, name, rows, bm, [_row(dconv), _next8(dconv, w3, 0), _full(conv_w)], [(w3, BF16)])[0]


def _gdn_post_fwd(o, proj, ng, name, bm=512):
    def fn(i, o, z, ng):
        outs = []
        for h in range(GDN_HEADS):
            sl = slice(h * LANES, (h + 1) * LANES)
            _, on = _rms_stats(o[:, sl])
            outs.append(on * ng * _silu_and_grad(z[:, sl])[0])
        return jnp.concatenate(outs, axis=1)
    return _rowcall(fn, name, o.shape[0], bm, [_row(o), _row(proj, GDN_W, 3), _full(ng)], [(GDN_W, BF16)])[0]


def _gdn_post_bwd(o, proj, ng, dcat, name, bm=256):
    def fn(i, o, z, ng, dm):
        dm = dm.astype(F32)
        dos, dzs = [], []
        dng = jnp.zeros((SUBLANES, LANES), F32)
        for h in range(GDN_HEADS):
            sl = slice(h * LANES, (h + 1) * LANES)
            s, ds = _silu_and_grad(z[:, sl])
            r, on = _rms_stats(o[:, sl])
            dzs.append(dm[:, sl] * on * ng * ds)
            dy = dm[:, sl] * s
            dxn = dy * ng
            dos.append(r * (dxn - on * jnp.mean(dxn * on, axis=-1, keepdims=True)))
            dng = dng + _fold8(dy * on)
        return jnp.concatenate(dos, axis=1), jnp.concatenate(dzs, axis=1), dng
    return _rowcall(fn, name, o.shape[0], bm, [_row(o), _row(proj, GDN_W, 3), _full(ng), _row(dcat, GDN_W, 0)],
                    [(GDN_W, F32), (GDN_W, BF16)], [(SUBLANES, LANES)])


def _bdot(a, b, mode="nn"):
    lc, rc = {"nn": (2, 1), "nt": (2, 2), "tn": (1, 1)}[mode]
    return lax.dot_general(a, b, (((lc,), (rc,)), ((0,), (0,))), preferred_element_type=F32)


def _bdot3(a, b, mode="nn"):
    ah, bh = a.astype(BF16), b.astype(BF16)
    al, bl = (a - ah.astype(F32)).astype(BF16), (b - bh.astype(F32)).astype(BF16)
    return _bdot(ah, bh, mode) + _bdot(ah, bl, mode) + _bdot(al, bh, mode)


GDN_CB = 4


def _gdn_chunk(q, k, v, gc, beta, t=None):
    c = CHUNK
    nb = q.shape[0]
    row = lax.broadcasted_iota(jnp.int32, (c, c), 0)
    col = lax.broadcasted_iota(jnp.int32, (c, c), 1)
    tril, strict = row >= col, row > col
    lane0 = (lax.broadcasted_iota(jnp.int32, (nb, c, LANES), 2) == 0).astype(BF16)
    gc_row = sum(_bdot(lane0, part, "nt") for part in _split3(gc))
    dm = jnp.exp(jnp.where(tril, gc[:, :, :c] - gc_row, -1e30))
    eg = jnp.exp(gc)
    gcl = gc[:, c - 1:c, :]
    ekg = jnp.exp(gcl - gc)
    egl = jnp.exp(gcl)
    qs = q * (GDN_DK ** -0.5)
    kb = k * beta
    kk = _bdot(kb, k, "nt")
    a = jnp.where(strict, kk * dm, 0.0)
    vb = v * beta
    kbg = kb * eg
    qk = _bdot(qs, k, "nt")
    p = jnp.where(tril, qk * dm, 0.0)
    out = dict(tril=tril, strict=strict, dm=dm, eg=eg, ekg=ekg, egl=egl, qs=qs, kb=kb, kk=kk, a=a,
               vb=vb, kbg=kbg, qk=qk, p=p, qg=qs * eg, kg=k * ekg)
    if t is None:
        y = -a
        t = (row == col).astype(F32) + y
        for _ in range(5):
            y = _bdot3(y, y)
            t = t + _bdot3(t, y)
        out.update(u=_bdot3(t, vb), w=_bdot3(t, kbg))
    out["t"] = t
    return out


def _gdn_stack(ref):
    return jnp.stack([ref[c * CHUNK:(c + 1) * CHUNK, h * LANES:(h + 1) * LANES]
                      for c in range(GDN_CB) for h in range(GDN_HEADS)])


def _gdn_unstack(x, ref):
    for c in range(GDN_CB):
        for h in range(GDN_HEADS):
            ref[c * CHUNK:(c + 1) * CHUNK, h * LANES:(h + 1) * LANES] = x[c * GDN_HEADS + h]


def _gdn_fwd(q, k, v, gc, beta, name):
    rows = q.shape[0]
    n_chunks = rows // CHUNK
    steps = n_chunks // GDN_CB
    blk = pl.BlockSpec((GDN_CB * CHUNK, GDN_W), lambda n: (n, 0))
    st = pl.BlockSpec((GDN_HEADS, GDN_CB, GDN_DK, LANES), lambda n: (0, n, 0, 0))
    tinv = pl.BlockSpec((GDN_CB, GDN_HEADS, CHUNK, CHUNK), lambda n: (n, 0, 0, 0))

    def body(q_ref, k_ref, v_ref, g_ref, b_ref, o_ref, st_ref, t_ref, w_ref, vn_ref, s_ref):
        @pl.when(pl.program_id(0) == 0)
        def _():
            s_ref[...] = jnp.zeros(s_ref.shape, F32)
        c = _gdn_chunk(*[_gdn_stack(r) for r in (q_ref, k_ref, v_ref, g_ref, b_ref)])
        _gdn_unstack(c["w"], w_ref)
        s = s_ref[...]
        for i in range(GDN_CB):
            hs = slice(i * GDN_HEADS, (i + 1) * GDN_HEADS)
            rs = slice(i * CHUNK, (i + 1) * CHUNK)
            st_ref[:, i] = s
            vn = c["u"][hs] - _bdot(c["w"][hs], s)
            o = _bdot(c["qg"][hs], s) + _bdot(c["p"][hs], vn)
            s = s * c["egl"][hs] + _bdot(c["kg"][hs], vn, "tn")
            t_ref[i] = c["t"][hs]
            for h in range(GDN_HEADS):
                o_ref[rs, h * LANES:(h + 1) * LANES] = o[h]
                vn_ref[rs, h * LANES:(h + 1) * LANES] = vn[h]
        s_ref[...] = s

    f = jax.ShapeDtypeStruct((rows, GDN_W), F32)
    return pl.pallas_call(
        body, name=name, grid=(steps,), in_specs=[blk] * 5, out_specs=[blk, st, tinv, blk, blk],
        out_shape=[f, jax.ShapeDtypeStruct((GDN_HEADS, n_chunks, GDN_DK, LANES), F32),
                   jax.ShapeDtypeStruct((n_chunks, GDN_HEADS, CHUNK, CHUNK), F32), f, f],
        scratch_shapes=[pltpu.VMEM((GDN_HEADS, GDN_DK, LANES), F32)],
        compiler_params=_params(("arbitrary",)))(q, k, v, gc, beta)


def _gdn_bwd(q, k, v, gc, beta, states, tinv, w, vn, do, name):
    rows = q.shape[0]
    n_chunks = rows // CHUNK
    steps = n_chunks // GDN_CB
    blk = pl.BlockSpec((GDN_CB * CHUNK, GDN_W), lambda n: (steps - 1 - n, 0))
    st = pl.BlockSpec((GDN_HEADS, GDN_CB, GDN_DK, LANES), lambda n: (0, steps - 1 - n, 0, 0))
    ti = pl.BlockSpec((GDN_CB, GDN_HEADS, CHUNK, CHUNK), lambda n: (steps - 1 - n, 0, 0, 0))
    nbatch = GDN_CB * GDN_HEADS

    def lanesum(x):
        return jnp.broadcast_to(jnp.sum(x, axis=-1, keepdims=True), x.shape)

    def body(q_ref, k_ref, v_ref, g_ref, b_ref, st_ref, t_ref, w_ref, vn_ref, do_ref,
             dq_ref, dk_ref, dv_ref, dg_ref, db_ref, ds_ref):
        @pl.when(pl.program_id(0) == 0)
        def _():
            ds_ref[...] = jnp.zeros(ds_ref.shape, F32)
        q, k, v, gc, beta, w, vn, do = [_gdn_stack(r) for r in (q_ref, k_ref, v_ref, g_ref, b_ref, w_ref, vn_ref, do_ref)]
        t = t_ref[...].reshape(nbatch, CHUNK, CHUNK)
        s = jnp.stack([st_ref[h, i] for i in range(GDN_CB) for h in range(GDN_HEADS)])
        c = _gdn_chunk(q, k, v, gc, beta, t)
        tril, strict, dm = c["tril"], c["strict"], c["dm"]
        dsn = ds_ref[...]
        dvn_c, dkg_c, dgl_c = [None] * GDN_CB, [None] * GDN_CB, [None] * GDN_CB
        for i in reversed(range(GDN_CB)):
            hs = slice(i * GDN_HEADS, (i + 1) * GDN_HEADS)
            dvn_c[i] = _bdot(c["p"][hs], do[hs], "tn") + _bdot(c["kg"][hs], dsn)
            dkg_c[i] = _bdot(vn[hs], dsn, "nt")
            dgl_c[i] = jnp.sum(jnp.sum(s[hs] * dsn, axis=2, keepdims=True), axis=1, keepdims=True) * c["egl"][hs]
            dsn = dsn * c["egl"][hs] + _bdot(c["qg"][hs], do[hs], "tn") - _bdot(w[hs], dvn_c[i], "tn")
        ds_ref[...] = dsn
        dvn, dkg, dgl = jnp.concatenate(dvn_c), jnp.concatenate(dkg_c), jnp.concatenate(dgl_c)
        dp = jnp.where(tril, _bdot(do, vn, "nt"), 0.0)
        dqg = _bdot(do, s, "nt")
        dw = -_bdot(dvn, s, "nt")
        dvb = _bdot3(t, dvn, "tn")
        dkbg = _bdot3(t, dw, "tn")
        dt = _bdot(dvn, c["vb"], "nt") + _bdot(dw, c["kbg"], "nt")
        da = jnp.where(strict, -_bdot3(_bdot3(t, dt, "tn"), t, "nt"), 0.0)
        dkk = da * dm
        dqk = dp * dm
        dkb = _bdot(dkk, k) + dkbg * c["eg"]
        dk = _bdot(dkk, c["kb"], "tn") + _bdot(dqk, c["qs"], "tn") + dkg * c["ekg"] + dkb * beta
        dqs = _bdot(dqk, k) + dqg * c["eg"]
        e = da * c["a"] + dp * c["p"]
        ones = jnp.ones((nbatch, CHUNK, LANES), BF16)
        col_sums = sum(_bdot(part, ones, "tn") for part in _split3(e))
        kg_term = lanesum(dkg * c["kg"])
        dgc = (jnp.broadcast_to(jnp.sum(e, axis=-1, keepdims=True), (nbatch, CHUNK, LANES)) - col_sums
               + lanesum(dqg * c["qg"]) - kg_term + lanesum(dkbg * c["kbg"]))
        dgcl = jnp.sum(kg_term, axis=1, keepdims=True) + dgl
        last = lax.broadcasted_iota(jnp.int32, (CHUNK, LANES), 0) == CHUNK - 1
        _gdn_unstack(dqs * (GDN_DK ** -0.5), dq_ref)
        _gdn_unstack(dk, dk_ref)
        _gdn_unstack(dvb * beta, dv_ref)
        _gdn_unstack(dgc + jnp.where(last, dgcl, 0.0), dg_ref)
        _gdn_unstack(lanesum(dvb * v) + lanesum(dkb * k), db_ref)

    return pl.pallas_call(
        body, name=name, grid=(steps,), in_specs=[blk] * 5 + [st, ti, blk, blk, blk], out_specs=[blk] * 5,
        out_shape=[jax.ShapeDtypeStruct((rows, GDN_W), F32)] * 5,
        scratch_shapes=[pltpu.VMEM((GDN_HEADS, GDN_DK, LANES), F32)],
        compiler_params=_params(("arbitrary",)))(q, k, v, gc, beta, states, tinv, w, vn, do)


def _swa_masks(first):
    r = lax.broadcasted_iota(jnp.int32, (SWA_BLOCK, 2 * SWA_BLOCK), 0)
    c = lax.broadcasted_iota(jnp.int32, (SWA_BLOCK, 2 * SWA_BLOCK), 1)
    band = (c > r) & (c <= r + SWA_BLOCK)
    return band & (jnp.logical_not(first) | (c >= SWA_BLOCK))


def _swa_stack(ref, j):
    lane = lax.broadcasted_iota(jnp.int32, (1, LANES), 1)
    parts = []
    for g in range(SWA_GROUP):
        ch = j * (SWA_GROUP // 2) + g // 2
        keep = (lane < SWA_DH) if g % 2 == 0 else (lane >= SWA_DH)
        parts.append(ref[:, ch * LANES:(ch + 1) * LANES] * keep.astype(ref.dtype))
    return jnp.concatenate(parts, axis=0)


def _swa_unstack(x2, j, out_ref):
    low = lax.broadcasted_iota(jnp.int32, (SWA_BLOCK, LANES), 1) < SWA_DH
    for c3 in range(SWA_GROUP // 2):
        even = x2[(2 * c3) * SWA_BLOCK:(2 * c3 + 1) * SWA_BLOCK]
        odd = x2[(2 * c3 + 1) * SWA_BLOCK:(2 * c3 + 2) * SWA_BLOCK]
        ch = j * (SWA_GROUP // 2) + c3
        out_ref[:, ch * LANES:(ch + 1) * LANES] = jnp.where(low, even, odd).astype(out_ref.dtype)


def _swa_probs(s, sink, mask):
    s = jnp.where(mask, s, -1e30)
    m = jnp.maximum(jnp.max(s, axis=-1, keepdims=True), sink)
    p = jnp.where(mask, jnp.exp(s - m), 0.0)
    es = jnp.exp(sink - m)
    inv = 1.0 / (jnp.sum(p, axis=-1, keepdims=True) + es)
    return p * inv, es * inv


def _swa_scores(q_ref, kc_ref, kp_ref, sink_ref, j, mask):
    sl = slice(j * LANES, (j + 1) * LANES)
    qst = _swa_stack(q_ref, j)
    kw = jnp.concatenate([kp_ref[:, sl], kc_ref[:, sl]], axis=0)
    s = _dot(qst, kw, NT) * (SWA_DH ** -0.5)
    ps = [_swa_probs(s[g * SWA_BLOCK:(g + 1) * SWA_BLOCK], sink_ref[j * SWA_GROUP + g], mask)
          for g in range(SWA_GROUP)]
    return qst, kw, ps


def _swa_fwd(q, k2, v2, sinks, name):
    rows = q.shape[0]
    nb = rows // SWA_BLOCK
    w = SWA_HEADS * SWA_DH
    kvw = SWA_KV_HEADS * LANES
    cur = pl.BlockSpec((SWA_BLOCK, w), lambda i: (i, 0))
    kcur = pl.BlockSpec((SWA_BLOCK, kvw), lambda i: (i, 0))
    kprev = pl.BlockSpec((SWA_BLOCK, kvw), lambda i: (jnp.maximum(i - 1, 0), 0))

    def body(sink_ref, q_ref, kc_ref, kp_ref, vc_ref, vp_ref, o_ref):
        mask = _swa_masks(pl.program_id(0) == 0)
        for j in range(SWA_KV_HEADS):
            sl = slice(j * LANES, (j + 1) * LANES)
            _, _, ps = _swa_scores(q_ref, kc_ref, kp_ref, sink_ref, j, mask)
            vw = jnp.concatenate([vp_ref[:, sl], vc_ref[:, sl]], axis=0)
            pst = jnp.concatenate([p.astype(BF16) for p, _ in ps], axis=0)
            _swa_unstack(_dot(pst, vw), j, o_ref)

    return pl.pallas_call(
        body, name=name, grid=(nb,),
        in_specs=[pl.BlockSpec(memory_space=pltpu.SMEM), cur, kcur, kprev, kcur, kprev], out_specs=cur,
        out_shape=jax.ShapeDtypeStruct((rows, w), BF16),
        compiler_params=_params(("arbitrary",)))(sinks, q, k2, k2, v2, v2)


def _swa_bwd(q, k2, v2, sinks, dcat, name):
    rows = q.shape[0]
    nb = rows // SWA_BLOCK
    w = SWA_HEADS * SWA_DH
    kvw = SWA_KV_HEADS * LANES
    cur = pl.BlockSpec((SWA_BLOCK, w), lambda i: (jnp.minimum(i, nb - 1), 0))
    kcur = pl.BlockSpec((SWA_BLOCK, kvw), lambda i: (jnp.minimum(i, nb - 1), 0))
    kprev = pl.BlockSpec((SWA_BLOCK, kvw), lambda i: (jnp.clip(i - 1, 0, nb - 1), 0))
    late = pl.BlockSpec((SWA_BLOCK, kvw), lambda i: (jnp.maximum(i - 1, 0), 0))
    acc_spec = pl.BlockSpec((SUBLANES, LANES), lambda i: (0, 0))

    def body(sink_ref, q_ref, kc_ref, kp_ref, vc_ref, vp_ref, do_ref, dq_ref, dk_ref, dv_ref, dsk_ref,
             ck_ref, cv_ref):
        i = pl.program_id(0)

        @pl.when(i == 0)
        def _():
            ck_ref[...] = jnp.zeros(ck_ref.shape, F32)
            cv_ref[...] = jnp.zeros(cv_ref.shape, F32)
            dsk_ref[...] = jnp.zeros(dsk_ref.shape, F32)

        @pl.when(i == nb)
        def _():
            dk_ref[...] = ck_ref[...]
            dv_ref[...] = cv_ref[...]

        @pl.when(i < nb)
        def _():
            mask = _swa_masks(i == 0)
            lane = lax.broadcasted_iota(jnp.int32, (SUBLANES, LANES), 1)
            dsk = jnp.zeros((SUBLANES, LANES), F32)
            for j in range(SWA_KV_HEADS):
                sl = slice(j * LANES, (j + 1) * LANES)
                qst, kw, ps = _swa_scores(q_ref, kc_ref, kp_ref, sink_ref, j, mask)
                vw = jnp.concatenate([vp_ref[:, sl], vc_ref[:, sl]], axis=0)
                dost = _swa_stack(do_ref, j)
                dpr = _dot(dost, vw, NT)
                dss = []
                for g in range(SWA_GROUP):
                    p, sink_p = ps[g]
                    dpg = dpr[g * SWA_BLOCK:(g + 1) * SWA_BLOCK]
                    delta = jnp.sum(p * dpg, axis=-1, keepdims=True)
                    dss.append((p * (dpg - delta)).astype(BF16))
                    dsg = jnp.sum(-sink_p * delta, axis=0, keepdims=True)
                    dsk = dsk + jnp.where(lane == j * SWA_GROUP + g, dsg, 0.0)
                dsst = jnp.concatenate(dss, axis=0)
                pst = jnp.concatenate([p.astype(BF16) for p, _ in ps], axis=0)
                _swa_unstack(_dot(dsst, kw) * (SWA_DH ** -0.5), j, dq_ref)
                dk = _dot(dsst, qst, TN) * (SWA_DH ** -0.5)
                dv = _dot(pst, dost, TN)
                dk = dk + pltpu.roll(dk, SWA_DH, 1)
                dv = dv + pltpu.roll(dv, SWA_DH, 1)
                dk_ref[:, sl] = ck_ref[:, sl] + dk[:SWA_BLOCK]
                dv_ref[:, sl] = cv_ref[:, sl] + dv[:SWA_BLOCK]
                ck_ref[:, sl] = dk[SWA_BLOCK:]
                cv_ref[:, sl] = dv[SWA_BLOCK:]
            dsk_ref[...] += dsk

    f = jax.ShapeDtypeStruct((rows, kvw), F32)
    return pl.pallas_call(
        body, name=name, grid=(nb + 1,),
        in_specs=[pl.BlockSpec(memory_space=pltpu.SMEM), cur, kcur, kprev, kcur, kprev, cur],
        out_specs=[cur, late, late, acc_spec],
        out_shape=[jax.ShapeDtypeStruct((rows, w), F32), f, f, jax.ShapeDtypeStruct((SUBLANES, LANES), F32)],
        scratch_shapes=[pltpu.VMEM((SWA_BLOCK, kvw), F32), pltpu.VMEM((SWA_BLOCK, kvw), F32)],
        compiler_params=_params(("arbitrary",)))(sinks, q, k2, k2, v2, v2, dcat)


def _mem_probs(mq, kbd):
    s = _dot(mq.astype(BF16), kbd) * (MEM_DH ** -0.5)
    ps = []
    for h in range(MEM_HEADS):
        sh = s[:, h * MEM_LEN:(h + 1) * MEM_LEN]
        e = jnp.exp(sh - jnp.max(sh, axis=-1, keepdims=True))
        ps.append(e / jnp.sum(e, axis=-1, keepdims=True))
    return ps


def _mem_fwd(proj, cb, kbd, vbd, name, bm=512):
    def fn(i, mq, kbd, vbd):
        p = jnp.concatenate(_mem_probs(mq, kbd), axis=1)
        return _dot(p.astype(BF16), vbd)
    return _rowcall(fn, name, proj.shape[0], bm, [_row(proj, MEM_W, cb), _full(kbd), _full(vbd)], [(MEM_W, BF16)])[0]


def _mem_bwd(proj, cb, kbd, vbd, dcat, name, bm=512):
    def fn(i, mq, kbd, vbd, do):
        ps = _mem_probs(mq, kbd)
        dp = _dot(do, vbd, NT)
        dss = []
        for h in range(MEM_HEADS):
            dph = dp[:, h * MEM_LEN:(h + 1) * MEM_LEN]
            dss.append(ps[h] * (dph - jnp.sum(ps[h] * dph, axis=-1, keepdims=True)))
        ds = (jnp.concatenate(dss, axis=1) * (MEM_DH ** -0.5)).astype(BF16)
        p = jnp.concatenate(ps, axis=1).astype(BF16)
        return _dot(ds, kbd, NT), _dot(mq.astype(BF16), ds, TN), _dot(p, do, TN)
    return _rowcall(fn, name, proj.shape[0], bm, [_row(proj, MEM_W, cb), _full(kbd), _full(vbd), _row(dcat, MEM_W, 3)],
                    [(MEM_W, BF16)], [(MEM_W, MEM_HEADS * MEM_LEN), (MEM_HEADS * MEM_LEN, MEM_W)])


def _mem_expand(mkv):
    feat_head = jnp.arange(MEM_W) // MEM_DH
    slot_head = jnp.arange(MEM_HEADS * MEM_LEN) // MEM_LEN
    on = feat_head[:, None] == slot_head[None, :]
    kbd = jnp.where(on, jnp.tile(mkv[:, :MEM_W].T, (1, MEM_HEADS)), 0.0)
    vbd = jnp.where(on.T, jnp.tile(mkv[:, MEM_W:], (MEM_HEADS, 1)), 0.0)
    return kbd.astype(BF16), vbd.astype(BF16)


def _mem_collapse(dkbd, dvbd):
    dk = [dkbd[h * MEM_DH:(h + 1) * MEM_DH, h * MEM_LEN:(h + 1) * MEM_LEN].T for h in range(MEM_HEADS)]
    dv = [dvbd[h * MEM_LEN:(h + 1) * MEM_LEN, h * MEM_DH:(h + 1) * MEM_DH] for h in range(MEM_HEADS)]
    return jnp.concatenate(dk + dv, axis=1)


def _adamw_math(w, g, m, v):
    m = ADAM_B1 * m + (1.0 - ADAM_B1) * g
    v = ADAM_B2 * v + (1.0 - ADAM_B2) * (g * g)
    m_hat = m / (1.0 - ADAM_B1 ** ADAM_STEP)
    v_hat = v / (1.0 - ADAM_B2 ** ADAM_STEP)
    return -ADAM_LR * (m_hat / (jnp.sqrt(v_hat) + ADAM_EPS) + ADAM_WD * w), m, v


def _adamw(w, g, m, v, name, bm=512):
    d = w.shape[1]
    return _rowcall(lambda i, *a: _adamw_math(*a), name, w.shape[0], bm, [_row(w), _row(g), _row(m), _row(v)], [(d, F32)] * 3)


def _adamw_halves(w, g_own, g_other, my_core, m, v, name):
    layers, rows, n = w.shape
    r = rows // 2
    bm = LANES if r % LANES == 0 else r
    per_half = r // bm
    nat = pl.BlockSpec((None, bm, n), lambda l, c, i: (l, c * per_half + i, 0))
    own = pl.BlockSpec((None, bm, n), lambda l, c, i: (l, i, 0))
    other = pl.BlockSpec((None, None, bm, n), lambda l, c, i: (0, l, i, 0))

    def body(core_ref, w_ref, own_ref, other_ref, m_ref, v_ref, go_ref, d_ref, mo_ref, vo_ref):
        g = jnp.where(core_ref[0] == pl.program_id(1), own_ref[...], other_ref[...])
        go_ref[...] = g
        d_ref[...], mo_ref[...], vo_ref[...] = _adamw_math(w_ref[...], g, m_ref[...], v_ref[...])

    return pl.pallas_call(
        body, name=name, grid=(layers, 2, per_half),
        in_specs=[pl.BlockSpec(memory_space=pltpu.SMEM), nat, own, other, nat, nat], out_specs=[nat] * 4,
        out_shape=[jax.ShapeDtypeStruct(w.shape, F32)] * 4,
        compiler_params=_params(("parallel", "parallel", "parallel")))(my_core, w, g_own, g_other, m, v)


def _sum_slots(buf, name, bm=128):
    n, rows, w = buf.shape
    bm = min(bm, rows)
    assert rows % bm == 0

    def body(b_ref, o_ref):
        acc = b_ref[0].astype(F32)
        for s in range(1, n):
            acc = acc + b_ref[s].astype(F32)
        o_ref[...] = acc

    return pl.pallas_call(
        body, name=name, grid=(rows // bm,), in_specs=[pl.BlockSpec((n, bm, w), lambda i: (0, i, 0))],
        out_specs=pl.BlockSpec((bm, w), lambda i: (i, 0)), out_shape=jax.ShapeDtypeStruct((rows, w), F32),
        compiler_params=_params(("parallel",)))(buf)


def _exchange(srcs, same, masks, name, keep_own=True):
    slots = N_DEV if len(masks) == N_DEV - 1 else (2 if keep_own else 1)
    n_arr, n_peer = len(srcs), len(masks)
    shapes = [s.shape if sm else s.shape[1:] for s, sm in zip(srcs, same)]

    def body(*refs):
        src_refs, out_refs = refs[:n_arr], refs[n_arr:2 * n_arr]
        send_sems, recv_sems, local_sems = refs[2 * n_arr:]
        x, y, c = lax.axis_index("x"), lax.axis_index("y"), lax.axis_index("c")
        me = 4 * x + 2 * y + c

        def flip(v, bit):
            return 1 - v if bit else v

        def slot_of(dev):
            return dev if slots == N_DEV else (dev % 2 if slots == 2 else 0)

        def piece(a, p):
            return src_refs[a] if same[a] else src_refs[a].at[p]

        local = []
        if keep_own:
            local = [pltpu.make_async_copy(piece(a, me), out_refs[a].at[slot_of(me)], local_sems.at[a])
                     for a in range(n_arr)]
        for cp in local:
            cp.start()
        copies = []
        for idx, k in enumerate(masks):
            peer = (flip(x, k & 4), flip(y, k & 2), flip(c, k & 1))
            peer_id = 4 * peer[0] + 2 * peer[1] + peer[2]
            for a in range(n_arr):
                sem = idx * n_arr + a
                cp = pltpu.make_async_remote_copy(
                    src_ref=piece(a, peer_id), dst_ref=out_refs[a].at[slot_of(me)],
                    send_sem=send_sems.at[sem], recv_sem=recv_sems.at[sem], device_id=peer, device_id_type=MESH)
                cp.start()
                copies.append((cp, pltpu.make_async_remote_copy(
                    src_ref=piece(a, peer_id), dst_ref=out_refs[a].at[slot_of(peer_id)],
                    send_sem=send_sems.at[sem], recv_sem=recv_sems.at[sem], device_id=peer, device_id_type=MESH)))
        for cp, landing in copies:
            cp.wait_send()
            landing.wait_recv()
        for cp in local:
            cp.wait()

    any_spec = pl.BlockSpec(memory_space=pl.ANY)
    n_sem = n_arr * n_peer
    return pl.pallas_call(
        body, name=name, in_specs=[any_spec] * n_arr, out_specs=[any_spec] * n_arr,
        out_shape=[jax.ShapeDtypeStruct((slots,) + tuple(sh), s.dtype) for sh, s in zip(shapes, srcs)],
        scratch_shapes=[pltpu.SemaphoreType.DMA((n_sem,)), pltpu.SemaphoreType.DMA((n_sem,)),
                        pltpu.SemaphoreType.DMA((n_arr,))],
        )(*srcs)


ALL_PEERS = tuple(range(1, N_DEV))
SIBLING = (1,)


def _send_to_sibling(srcs, name):
    n_arr = len(srcs)

    def body(*refs):
        src_refs, out_refs = refs[:n_arr], refs[n_arr:2 * n_arr]
        send_sems, recv_sems = refs[2 * n_arr:]
        x, y, c = lax.axis_index("x"), lax.axis_index("y"), lax.axis_index("c")
        copies = [pltpu.make_async_remote_copy(
            src_ref=src_refs[a].at[1 - c], dst_ref=out_refs[a], send_sem=send_sems.at[a], recv_sem=recv_sems.at[a],
            device_id=(x, y, 1 - c), device_id_type=MESH) for a in range(n_arr)]
        for cp in copies:
            cp.start()
        for cp in copies:
            cp.wait()

    any_spec = pl.BlockSpec(memory_space=pl.ANY)
    return pl.pallas_call(
        body, name=name, in_specs=[any_spec] * n_arr, out_specs=[any_spec] * n_arr,
        out_shape=[jax.ShapeDtypeStruct(s.shape[1:], s.dtype) for s in srcs],
        scratch_shapes=[pltpu.SemaphoreType.DMA((n_arr,)), pltpu.SemaphoreType.DMA((n_arr,))],
        )(*srcs)


def _add_core_parts(mine, theirs, my_core, name, bm=128):
    _, chips, rows, n = mine.shape
    bm = min(bm, rows)
    assert rows % bm == 0

    def body(core_ref, p0_ref, p1_ref, t_ref, o_ref):
        own = jnp.where(core_ref[0] == 0, p0_ref[...], p1_ref[...])
        o_ref[...] = (own.astype(F32) + t_ref[...].astype(F32)).astype(o_ref.dtype)

    part = lambda k: pl.BlockSpec((None, None, bm, n), lambda s, i, k=k: (k, s, i, 0))
    flat = pl.BlockSpec((None, bm, n), lambda s, i: (s, i, 0))
    return pl.pallas_call(
        body, name=name, grid=(chips, rows // bm),
        in_specs=[pl.BlockSpec(memory_space=pltpu.SMEM), part(0), part(1), flat], out_specs=flat,
        out_shape=jax.ShapeDtypeStruct(theirs.shape, mine.dtype),
        compiler_params=_params(("parallel", "parallel")))(my_core, mine, mine, theirs)


def _scatter_by_chip(srcs, small, name):
    n_arr = len(srcs)

    def body(*refs):
        src_refs, small_ref = refs[:n_arr], refs[n_arr]
        out_refs, small_out = refs[n_arr + 1:2 * n_arr + 1], refs[2 * n_arr + 1]
        send_sems, recv_sems, local_sems = refs[2 * n_arr + 2:]
        x, y, c = lax.axis_index("x"), lax.axis_index("y"), lax.axis_index("c")
        my_chip, me = 2 * x + y, 4 * x + 2 * y + c
        chips = [(1 - x, y), (x, 1 - y), (1 - x, 1 - y)]
        local = [pltpu.make_async_copy(src_refs[a].at[my_chip], out_refs[a].at[my_chip], local_sems.at[a])
                 for a in range(n_arr)]
        local.append(pltpu.make_async_copy(small_ref, small_out.at[me], local_sems.at[n_arr]))
        for cp in local:
            cp.start()
        copies = []
        for j, (px, py) in enumerate(chips):
            their_chip = 2 * px + py
            for a in range(n_arr):
                sem = j * n_arr + a
                cp = pltpu.make_async_remote_copy(
                    src_ref=src_refs[a].at[their_chip], dst_ref=out_refs[a].at[my_chip],
                    send_sem=send_sems.at[sem], recv_sem=recv_sems.at[sem], device_id=(px, py, c), device_id_type=MESH)
                cp.start()
                copies.append((cp, pltpu.make_async_remote_copy(
                    src_ref=src_refs[a].at[their_chip], dst_ref=out_refs[a].at[their_chip],
                    send_sem=send_sems.at[sem], recv_sem=recv_sems.at[sem], device_id=(px, py, c), device_id_type=MESH)))
        for idx, k in enumerate(ALL_PEERS):
            peer = (1 - x if k & 4 else x, 1 - y if k & 2 else y, 1 - c if k & 1 else c)
            peer_id = 4 * peer[0] + 2 * peer[1] + peer[2]
            sem = 3 * n_arr + idx
            cp = pltpu.make_async_remote_copy(
                src_ref=small_ref, dst_ref=small_out.at[me], send_sem=send_sems.at[sem], recv_sem=recv_sems.at[sem],
                device_id=peer, device_id_type=MESH)
            cp.start()
            copies.append((cp, pltpu.make_async_remote_copy(
                src_ref=small_ref, dst_ref=small_out.at[peer_id], send_sem=send_sems.at[sem], recv_sem=recv_sems.at[sem],
                device_id=peer, device_id_type=MESH)))
        for cp, landing in copies:
            cp.wait_send()
            landing.wait_recv()
        for cp in local:
            cp.wait()

    any_spec = pl.BlockSpec(memory_space=pl.ANY)
    n_sem = 3 * n_arr + len(ALL_PEERS)
    return pl.pallas_call(
        body, name=name, in_specs=[any_spec] * (n_arr + 1), out_specs=[any_spec] * (n_arr + 1),
        out_shape=[jax.ShapeDtypeStruct(s.shape, s.dtype) for s in srcs]
        + [jax.ShapeDtypeStruct((N_DEV,) + small.shape, small.dtype)],
        scratch_shapes=[pltpu.SemaphoreType.DMA((n_sem,)), pltpu.SemaphoreType.DMA((n_sem,)),
                        pltpu.SemaphoreType.DMA((n_arr + 1,))],
        )(*srcs, small)


def _gather_two_level(srcs, name):
    n_arr = len(srcs)

    def body(*refs):
        src_refs, out_refs = refs[:n_arr], refs[n_arr:2 * n_arr]
        send_sems, recv_sems, local_sems = refs[2 * n_arr:]
        x, y, c = lax.axis_index("x"), lax.axis_index("y"), lax.axis_index("c")
        sibling = (x, y, 1 - c)
        chips = [(1 - x, y), (x, 1 - y), (1 - x, 1 - y)]

        def slot(px, py, pc):
            return 4 * px + 2 * py + pc

        def copy(a, k, block, to, own=False):
            return pltpu.make_async_remote_copy(
                src_ref=src_refs[a] if own else out_refs[a].at[slot(*block)], dst_ref=out_refs[a].at[slot(*block)],
                send_sem=send_sems.at[a * 7 + k], recv_sem=recv_sems.at[a * 7 + k], device_id=to, device_id_type=MESH)

        local = [pltpu.make_async_copy(src_refs[a], out_refs[a].at[slot(x, y, c)], local_sems.at[a]) for a in range(n_arr)]
        for cp in local:
            cp.start()
        started = []
        for a in range(n_arr):
            started.append(copy(a, 0, (x, y, c), sibling, own=True))
            started += [copy(a, 1 + j, (x, y, c), (*chip, c), own=True) for j, chip in enumerate(chips)]
        for cp in started:
            cp.start()
        for j, chip in enumerate(chips):
            for a in range(n_arr):
                copy(a, 1 + j, (*chip, c), (x, y, c)).wait_recv()
                passed = copy(a, 4 + j, (*chip, c), sibling)
                passed.start()
                started.append(passed)
        for a in range(n_arr):
            copy(a, 0, sibling, (x, y, c)).wait_recv()
            for j, chip in enumerate(chips):
                copy(a, 4 + j, (*chip, 1 - c), (x, y, c)).wait_recv()
        for cp in started:
            cp.wait_send()
        for cp in local:
            cp.wait()

    any_spec = pl.BlockSpec(memory_space=pl.ANY)
    return pl.pallas_call(
        body, name=name, in_specs=[any_spec] * n_arr, out_specs=[any_spec] * n_arr,
        out_shape=[jax.ShapeDtypeStruct((N_DEV,) + s.shape, s.dtype) for s in srcs],
        scratch_shapes=[pltpu.SemaphoreType.DMA((7 * n_arr,)), pltpu.SemaphoreType.DMA((7 * n_arr,)),
                        pltpu.SemaphoreType.DMA((n_arr,))],
        )(*srcs)


def _pack(arrays, rows):
    flat = jnp.concatenate([a.reshape(-1) for a in arrays])
    return jnp.pad(flat, (0, rows * D_MODEL - flat.shape[0])).reshape(rows, D_MODEL)


def _unpack(buf, shapes):
    flat = buf.reshape(-1)
    out, off = [], 0
    for s in shapes:
        n = math.prod(s)
        out.append(flat[off:off + n].reshape(s))
        off += n
    return out


def _rows_for(shapes, mult):
    n = sum(math.prod(s) for s in shapes)
    rows = -(-n // D_MODEL)
    return -(-rows // mult) * mult


SHARD_AXIS = dict(w_mem_kv=1, w_out=1, w_gate_up=2, w_down=1, gdn_w_in=2, swa_w_q=1, w_kv=0, gdn_conv=2)
HALF_AXIS = dict(w_mem_kv=1, w_out=1, w_gate_up=1, w_down=1, gdn_w_in=1, swa_w_q=1, w_kv=0)


def _my_half(shard, name, c):
    ax = HALF_AXIS[name]
    h = shard.shape[ax] // 2
    return lax.dynamic_slice_in_dim(shard, c * h, h, axis=ax)


def _piece_layout(name, half_shape):
    dims, pos = [], {}
    for i, d in enumerate(half_shape):
        if i == SHARD_AXIS[name]:
            pos["chip"] = len(dims)
            dims.append(4)
        if i == HALF_AXIS[name]:
            pos["core"] = len(dims)
            dims.append(2)
        pos[i] = len(dims)
        dims.append(d)
    return dims, [pos["chip"], pos["core"]] + [pos[i] for i in range(len(half_shape))]


def _full_shape(name, half_shape):
    return tuple(d * (4 if i == SHARD_AXIS[name] else 1) * (2 if i == HALF_AXIS[name] else 1)
                 for i, d in enumerate(half_shape))


def _assemble(pieces, name):
    half_shape = pieces.shape[1:]
    if half_shape[-1] % LANES:
        chips = [jnp.concatenate([pieces[2 * s], pieces[2 * s + 1]], axis=HALF_AXIS[name]) for s in range(4)]
        return jnp.concatenate(chips, axis=SHARD_AXIS[name])
    dims, perm = _piece_layout(name, half_shape)
    inverse = [perm.index(i) for i in range(len(perm))]
    return pieces.reshape((4, 2) + half_shape).transpose(inverse).reshape(_full_shape(name, half_shape))


def _to_pieces(full, name, half_shape):
    if half_shape[-1] % LANES:
        ns, nh = half_shape[SHARD_AXIS[name]], half_shape[HALF_AXIS[name]]
        parts = [lax.slice_in_dim(lax.slice_in_dim(full, s * ns, (s + 1) * ns, axis=SHARD_AXIS[name]),
                                  c * nh, (c + 1) * nh, axis=HALF_AXIS[name])
                 for c in range(2) for s in range(4)]
        return jnp.stack(parts).reshape((2, 4) + tuple(half_shape))
    dims, perm = _piece_layout(name, half_shape)
    return full.reshape(dims).transpose([perm[1], perm[0]] + perm[2:])


def _from_halves(halves, name):
    ax = HALF_AXIS[name]
    s = jnp.moveaxis(halves, 0, ax)
    return s.reshape(s.shape[:ax] + (2 * s.shape[ax + 1],) + s.shape[ax + 2:])


def _rope_tables(positions):
    half = ROT_DIM // 2
    inv = ROPE_THETA ** (-jnp.arange(0, ROT_DIM, 2, dtype=F32) / ROT_DIM)
    ang = positions.astype(F32)[:, None] * inv
    cos, sin = jnp.cos(ang), jnp.sin(ang)
    rows = positions.shape[0]
    one = jnp.ones((rows, SWA_DH - ROT_DIM), F32)
    zero = jnp.zeros((rows, SWA_DH - ROT_DIM), F32)
    zh = jnp.zeros((rows, half), F32)
    c64 = jnp.concatenate([cos, cos, one], axis=1)
    a64 = jnp.concatenate([-sin, zh, zero], axis=1)
    b64 = jnp.concatenate([zh, sin, zero], axis=1)
    return tuple(jnp.concatenate([t, t], axis=1) for t in (c64, a64, b64))


def _pair_heads(t):
    return jnp.concatenate([t[:, :SWA_DH], t[:, :SWA_DH], t[:, SWA_DH:], t[:, SWA_DH:]], axis=1)


def _unpair_heads(t):
    return jnp.concatenate([t[:, :SWA_DH], t[:, LANES:LANES + SWA_DH]], axis=1)


def _gdn_in_pad(w):
    o2 = 4 * GDN_W
    pad = jnp.zeros(w.shape[:-1] + (GDN_IN_PAD - GDN_IN,), w.dtype)
    return jnp.concatenate([w[..., :o2], w[..., o2 + 2 * GDN_HEADS:], w[..., o2:o2 + 2 * GDN_HEADS], pad], axis=-1)


def _gdn_in_unpad(w):
    o2 = 4 * GDN_W
    return jnp.concatenate([w[..., :o2], w[..., o2 + MEM_W:o2 + MEM_W + 2 * GDN_HEADS], w[..., o2:o2 + MEM_W]], axis=-1)


def _head_rows(v):
    return jnp.pad(v.astype(F32), (GDN_HEADS, LANES - 2 * GDN_HEADS))[None, :]


def _selectors():
    lane = jnp.arange(LANES)[:, None]
    head = (jnp.arange(GDN_W) // LANES)[None, :]
    return (lane == head).astype(BF16), (lane == head + GDN_HEADS).astype(BF16)


def _local_step(x, mem, positions, target, w):
    rows = x.shape[0]
    tabs = _rope_tables(positions)
    sel_b, sel_a = _selectors()
    row2 = lambda v: v.reshape(1, -1).astype(F32)

    w_gu = _ff_interleave(w["w_gate_up"], "interleave_gate_up")
    mem_n = _rms_fwd(mem, row2(w["ln_mem"]), "mem_norm")
    saved = []
    kt = vt = None
    for l in range(DEPTH):
        s = dict(x0=x)
        h = _rms_fwd(x, row2(w["ln_mix"][l]), f"norm_mix{l}")
        mkv = _mm(mem_n, w["w_mem_kv"], "nn", f"mem_kv{l}", layer=l)
        kbd, vbd = _mem_expand(mkv)
        if l < N_A:
            proj = _mm(h, w["gdn_w_in"], "nn", f"gdn_in{l}", layer=l)
            a_row, dt_row = _head_rows(w["gdn_A_log"][l]), _head_rows(w["gdn_dt_bias"][l])
            q, k, v, gc, beta = _gdn_pre_fwd(proj, w["gdn_conv"][l], sel_b, sel_a, a_row, dt_row, f"gdn_pre{l}")
            o, states, tinv, gw, vn = _gdn_fwd(q, k, v, gc, beta, f"gdn_scan{l}")
            mix = _gdn_post_fwd(o, proj, row2(w["gdn_norm"][l]), f"gdn_post{l}")
            mq_cb = (3 * GDN_W + GDN_W) // MEM_W
            s.update(q=q, k=k, v=v, gc=gc, beta=beta, o=o, states=states, tinv=tinv, gw=gw, vn=vn,
                     a_row=a_row, dt_row=dt_row)
        else:
            proj = _mm(h, w["swa_w_q"], "nn", f"swa_in{l}", layer=l - N_A)
            qr = _rope(proj, SWA_HEADS * SWA_DH, 0, tabs, 1, f"rope_q{l}", BF16)
            mix = _swa_fwd(qr, kt, vt, w["swa_sinks"][l - N_A], f"swa{l}")
            mq_cb = (SWA_HEADS * SWA_DH) // MEM_W
            s.update(qr=qr)
        mem_o = _mem_fwd(proj, mq_cb, kbd, vbd, f"mem_attn{l}")
        cat = jnp.concatenate([mix, mem_o], axis=1)
        x1 = _mm(cat, w["w_out"], "nn", f"out_proj{l}", add=x, layer=l)
        h2 = _rms_fwd(x1, row2(w["ln_ffn"][l]), f"norm_ffn{l}")
        gu, act = _gate_up_fwd(h2, w_gu, l, f"gate_up{l}")
        x = _mm(act, w["w_down"], "nn", f"down{l}", add=x1, layer=l)
        s.update(h=h, proj=proj, kbd=kbd, vbd=vbd, mq_cb=mq_cb, cat=cat, x1=x1, h2=h2, gu=gu, act=act)
        saved.append(s)
        if l == N_A - 1:
            x_kv = x
            h_kv = _rms_fwd(x, row2(w["ln_kv"]), "norm_kv")
            kv = _mm(h_kv, w["w_kv"], "nn", "kv_proj")
            kr = _rope(kv, LANES, 0, tabs, 1, "rope_k", F32)
            kt = _pair_heads(kr).astype(BF16)
            vt = _pair_heads(kv[:, LANES:]).astype(BF16)

    gr = {}
    dx, dxb, loss_part, dlnf = _final_loss(x, row2(w["ln_final"]), target, "final_loss")
    gr["ln_final"] = dlnf.sum(axis=0)
    dln_mix, dln_ffn = [None] * DEPTH, [None] * DEPTH
    dw_mem_kv, dw_out, dw_gu, dw_dn, dgdn_in, dswa_q = [
        lax.empty(w[n].shape, BF16) for n in ("w_mem_kv", "w_out", "w_gate_up", "w_down", "gdn_w_in", "swa_w_q")]
    dgdn_conv, dgdn_a, dgdn_dt, dgdn_norm = [None] * N_A, [None] * N_A, [None] * N_A, [None] * N_A
    dswa_sinks = [None] * N_B
    dmem_n = None
    dkt = dvt = None
    for l in reversed(range(DEPTH)):
        s = saved[l]
        if l == N_A - 1:
            dkr = _unpair_heads(dkt)
            dk = _rope(dkr, LANES, 0, tabs, -1, "rope_k_bwd", BF16)
            dkv = jnp.concatenate([dk, _unpair_heads(dvt).astype(BF16)], axis=1)
            dh_kv = _mm(dkv, w["w_kv"], "nt", "kv_proj_dx", BF16)
            gr["w_kv"] = _mm(h_kv, dkv, "tn", "kv_proj_dw", BF16)
            dx, dxb, dg = _rms_bwd(x_kv, row2(w["ln_kv"]), dh_kv, dx, "norm_kv_bwd")
            gr["ln_kv"] = dg.sum(axis=0)
        dgu = _down_bwd(dxb, w["w_down"], l, s["gu"], f"down_dx{l}")
        dw_dn = _mm(s["act"], dxb, "tn", f"down_dw{l}", dst=dw_dn, layer=l)
        dh2 = _mm(dgu, w_gu, "nt", f"gate_up_dx{l}", BF16, layer=l)
        dw_gu = _mm(s["h2"], dgu, "tn", f"gate_up_dw{l}", dst=dw_gu, layer=l)
        dx, dxb, dg = _rms_bwd(s["x1"], row2(w["ln_ffn"][l]), dh2, dx, f"norm_ffn_bwd{l}")
        dln_ffn[l] = dg.sum(axis=0)
        dcat = _mm(dxb, w["w_out"], "nt", f"out_proj_dx{l}", BF16, layer=l)
        dw_out = _mm(s["cat"], dxb, "tn", f"out_proj_dw{l}", dst=dw_out, layer=l)
        dmq, dkbd, dvbd = _mem_bwd(s["proj"], s["mq_cb"], s["kbd"], s["vbd"], dcat, f"mem_attn_bwd{l}")
        dmkv = _mem_collapse(dkbd, dvbd).astype(BF16)
        dw_mem_kv = _mm(mem_n, dmkv, "tn", f"mem_kv_dw{l}", dst=dw_mem_kv, layer=l)
        dmem_n = _mm(dmkv, w["w_mem_kv"], "nt", f"mem_kv_dx{l}", add=dmem_n, layer=l)
        if l < N_A:
            do, dz, dng = _gdn_post_bwd(s["o"], s["proj"], row2(w["gdn_norm"][l]), dcat, f"gdn_post_bwd{l}")
            dq, dk, dv, dg_, dbeta = _gdn_bwd(s["q"], s["k"], s["v"], s["gc"], s["beta"], s["states"], s["tinv"], s["gw"],
                                              s["vn"], do, f"gdn_scan_bwd{l}")
            res = _gdn_pre_bwd(s["proj"], w["gdn_conv"][l], sel_b, sel_a, s["a_row"], s["dt_row"], dq, dk, dv, dg_, dbeta,
                               f"gdn_pre_bwd{l}")
            dconv, dba = res[0], res[1]
            dgdn_conv[l] = jnp.stack([r.sum(axis=0) for r in res[2:2 + CONV_K]])
            dgdn_a[l] = res[2 + CONV_K].sum(axis=0)[GDN_HEADS:2 * GDN_HEADS]
            dgdn_dt[l] = res[3 + CONV_K].sum(axis=0)[GDN_HEADS:2 * GDN_HEADS]
            dgdn_norm[l] = dng.sum(axis=0)
            dqkv = _conv_bwd_input(dconv, w["gdn_conv"][l], f"gdn_conv_bwd{l}")
            dproj = jnp.concatenate([dqkv, dz, dmq, dba], axis=1)
            dh = _mm(dproj, w["gdn_w_in"], "nt", f"gdn_in_dx{l}", BF16, layer=l)
            dgdn_in = _mm(s["h"], dproj, "tn", f"gdn_in_dw{l}", dst=dgdn_in, layer=l)
        else:
            b = l - N_A
            dqr, dkt_l, dvt_l, dsk = _swa_bwd(s["qr"], kt, vt, w["swa_sinks"][b], dcat, f"swa_bwd{l}")
            dkt = dkt_l if dkt is None else dkt + dkt_l
            dvt = dvt_l if dvt is None else dvt + dvt_l
            dswa_sinks[b] = dsk[0, :SWA_HEADS]
            dq = _rope(dqr, SWA_HEADS * SWA_DH, 0, tabs, -1, f"rope_q_bwd{l}", BF16)
            dproj = jnp.concatenate([dq, dmq], axis=1)
            dh = _mm(dproj, w["swa_w_q"], "nt", f"swa_in_dx{l}", BF16, layer=b)
            dswa_q = _mm(s["h"], dproj, "tn", f"swa_in_dw{l}", dst=dswa_q, layer=b)
        dx, dxb, dg = _rms_bwd(s["x0"], row2(w["ln_mix"][l]), dh, dx, f"norm_mix_bwd{l}")
        dln_mix[l] = dg.sum(axis=0)
    _, _, dg = _rms_bwd(mem, row2(w["ln_mem"]), dmem_n, None, "mem_norm_bwd")
    gr["ln_mem"] = dg.sum(axis=0)
    gr.update(ln_mix=jnp.stack(dln_mix), ln_ffn=jnp.stack(dln_ffn), w_mem_kv=dw_mem_kv, w_out=dw_out,
              w_gate_up=_ff_deinterleave(dw_gu, "deinterleave_gate_up"), w_down=dw_dn, gdn_w_in=dgdn_in, gdn_conv=jnp.stack(dgdn_conv),
              gdn_A_log=jnp.stack(dgdn_a), gdn_dt_bias=jnp.stack(dgdn_dt), gdn_norm=jnp.stack(dgdn_norm),
              swa_w_q=dswa_q, swa_sinks=jnp.stack(dswa_sinks))
    return loss_part, dx, gr


def kernel(x, mem, positions, ln_mix, ln_ffn, ln_mem, w_mem_kv, w_out, w_gate_up, w_down, gdn_w_in, gdn_conv, gdn_A_log, gdn_dt_bias, gdn_norm, swa_w_q, swa_sinks, ln_kv, w_kv, ln_final, loss_target, m_ln_mix, m_ln_ffn, m_ln_mem, m_w_mem_kv, m_w_out, m_w_gate_up, m_w_down, m_gdn_w_in, m_gdn_conv, m_gdn_A_log, m_gdn_dt_bias, m_gdn_norm, m_swa_w_q, m_swa_sinks, m_ln_kv, m_w_kv, m_ln_final, v_ln_mix, v_ln_ffn, v_ln_mem, v_w_mem_kv, v_w_out, v_w_gate_up, v_w_down, v_gdn_w_in, v_gdn_conv, v_gdn_A_log, v_gdn_dt_bias, v_gdn_norm, v_swa_w_q, v_swa_sinks, v_ln_kv, v_w_kv, v_ln_final):
    given = dict(locals())
    wts = {n: given[n] for n in WEIGHTS}
    c = lax.axis_index("c")

    halves = [_my_half(wts[n].astype(BF16), n, c) for n in SHARDED]
    half_shapes = [h.shape for h in halves]
    conv_shape = wts["gdn_conv"].shape
    cpack = _pack([wts["gdn_conv"]], 16)
    conv_half = lax.dynamic_slice_in_dim(cpack, c * SUBLANES, SUBLANES, axis=0)
    got = _gather_two_level(halves + [conv_half], "gather_weights")
    full = {n: wts[n] for n in SMALL}
    for n, g in zip(SHARDED, got):
        full[n] = _assemble(g, n)
    conv_all = got[-1].reshape(4, 16, D_MODEL)
    full["gdn_conv"] = jnp.concatenate([_unpack(conv_all[s], [conv_shape])[0] for s in range(4)], axis=2)
    full["gdn_w_in"] = _gdn_in_pad(full["gdn_w_in"])

    loss_part, dx, gr = _local_step(x[0], mem[0], positions[0], loss_target[0], full)
    gr["gdn_w_in"] = _gdn_in_unpad(gr["gdn_w_in"])

    pieces = [_to_pieces(gr[n].astype(BF16), n, hs) for n, hs in zip(SHARDED, half_shapes)]
    small_shapes = [wts[n].shape for n in SMALL] + [conv_shape[:2] + (4 * conv_shape[2],), (SUBLANES, LANES)]
    rows_s = _rows_for(small_shapes, SUBLANES)
    spack = _pack([gr[n] for n in SMALL] + [gr["gdn_conv"], loss_part], rows_s)
    my_core = c.astype(jnp.int32).reshape(1)
    from_sibling = _send_to_sibling(pieces, "pair_grads")
    chip_parts = [_add_core_parts(p.reshape(2, 4, -1, p.shape[-1]), t.reshape(4, -1, t.shape[-1]), my_core, f"pair_sum_{n}")
                  for n, p, t in zip(SHARDED, pieces, from_sibling)]
    parts = _scatter_by_chip(chip_parts, spack, "scatter_grads")
    mine = [_sum_slots(p, f"sum_{n}").reshape(hs) for n, p, hs in zip(SHARDED, parts, half_shapes)]
    ssum = _unpack(_sum_slots(parts[-1], "sum_small"), small_shapes)
    theirs = _exchange(mine, [True] * len(mine), SIBLING, "swap_grad_halves", keep_own=False)
    g_all = dict(zip(SMALL, ssum[:len(SMALL)]))
    chip = 2 * lax.axis_index("x") + lax.axis_index("y")
    g_all["gdn_conv"] = lax.dynamic_slice_in_dim(ssum[len(SMALL)], chip * conv_shape[2], conv_shape[2], axis=2)
    loss = jnp.sum(ssum[-1])

    out = dict(grad=g_all, delta={}, new_m={}, new_v={})
    for n, own, other in zip(SHARDED, mine, theirs):
        as3d = lambda a: a.reshape((-1,) + a.shape[-2:])
        res = _adamw_halves(as3d(wts[n]), as3d(own), other.reshape((1, -1) + other.shape[-2:]), my_core,
                            as3d(given["m_" + n]), as3d(given["v_" + n]), f"adamw_{n}")
        for kind, r in zip(("grad", "delta", "new_m", "new_v"), res):
            out[kind][n] = r.reshape(wts[n].shape)
    small_names = SMALL + ("gdn_conv",)
    small_w_shapes = [wts[n].shape for n in small_names]
    rows_a = _rows_for(small_w_shapes, SUBLANES)
    res = _adamw(_pack([wts[n] for n in small_names], rows_a), _pack([g_all[n] for n in small_names], rows_a),
                 _pack([given["m_" + n] for n in small_names], rows_a),
                 _pack([given["v_" + n] for n in small_names], rows_a), "adamw_small")
    for kind, r in zip(("delta", "new_m", "new_v"), res):
        out[kind].update(zip(small_names, _unpack(r, small_w_shapes)))
    return (loss, dx[None], *[out["grad"][n] for n in WEIGHTS], *[out["delta"][n] for n in WEIGHTS],
            *[out["new_m"][n] for n in WEIGHTS], *[out["new_v"][n] for n in WEIGHTS])
```

```python
import math

import jax
import jax.numpy as jnp
from jax import lax
from jax.experimental import pallas as pl
from jax.experimental.pallas import tpu as pltpu

F32 = jnp.float32
BF16 = jnp.bfloat16
MESH = pl.DeviceIdType.MESH

D_MODEL = 1024
DEPTH = 4
N_A = 2
N_B = 2
EPS = 1e-6
GDN_HEADS = 6
GDN_DK = 128
GDN_W = 768
CONV_K = 4
CHUNK = 64
SWA_HEADS = 12
SWA_KV_HEADS = 2
SWA_DH = 64
SWA_GROUP = 6
SWA_BLOCK = 128
ROPE_THETA = 500000.0
ROT_DIM = 16
MEM_LEN = 256
MEM_HEADS = 4
MEM_DH = 64
MEM_W = 256
D_FF = 2816
GDN_IN = 3340
GDN_IN_PAD = 3456
ADAM_LR = 0.001
ADAM_B1 = 0.9
ADAM_B2 = 0.999
ADAM_EPS = 1e-08
ADAM_WD = 0.01
ADAM_STEP = 10

N_DEV = 8
LANES = 128
SUBLANES = 8
V7X_VMEM_LIMIT = 56 * 2**20
MM_VMEM_BUDGET = 44 * 2**20

SHARDED = ("w_mem_kv", "w_out", "w_gate_up", "w_down", "gdn_w_in", "swa_w_q", "w_kv")
SMALL = ("ln_mix", "ln_ffn", "ln_mem", "gdn_A_log", "gdn_dt_bias", "gdn_norm", "swa_sinks", "ln_kv", "ln_final")
WEIGHTS = ("ln_mix", "ln_ffn", "ln_mem", "w_mem_kv", "w_out", "w_gate_up", "w_down", "gdn_w_in", "gdn_conv",
           "gdn_A_log", "gdn_dt_bias", "gdn_norm", "swa_w_q", "swa_sinks", "ln_kv", "w_kv", "ln_final")


def _params(sem=None, **kw):
    return pltpu.CompilerParams(dimension_semantics=sem, vmem_limit_bytes=V7X_VMEM_LIMIT, **kw)


def _dot(a, b, dims=(((1,), (0,)), ((), ())), precision=None):
    return lax.dot_general(a, b, dims, precision=precision, preferred_element_type=F32)


NT = (((1,), (1,)), ((), ()))
TN = (((0,), (0,)), ((), ()))


def _fold8(v):
    r, w = v.shape
    return v.reshape(r // SUBLANES, SUBLANES, w).sum(axis=0)


def _row(a, w=None, cb=0):
    return ("row", a, a.shape[1] if w is None else w, cb)


def _full(a):
    return ("full", a, None, None)


def _prev8(a, w, cb=0):
    return ("prev8", a, w, cb)


def _next8(a, w, cb=0):
    return ("next8", a, w, cb)


def _rowcall(fn, name, rows, bm, ins, outs, accs=()):
    bm = min(bm, rows)
    assert rows % bm == 0 and bm % SUBLANES == 0
    steps = rows // bm
    r8 = bm // SUBLANES
    in_specs, arrays = [], []
    for kind, a, w, cb in ins:
        arrays.append(a)
        if kind == "row":
            in_specs.append(pl.BlockSpec((bm, w), lambda i, cb=cb: (i, cb)))
        elif kind == "full":
            in_specs.append(pl.BlockSpec(a.shape, lambda i, nd=a.ndim: (0,) * nd))
        elif kind == "prev8":
            in_specs.append(pl.BlockSpec((SUBLANES, w), lambda i, cb=cb: (jnp.maximum(i * r8 - 1, 0), cb)))
        else:
            last = rows // SUBLANES - 1
            in_specs.append(pl.BlockSpec((SUBLANES, w), lambda i, cb=cb: (jnp.minimum((i + 1) * r8, last), cb)))
    out_shape = [jax.ShapeDtypeStruct((rows, w), dt) for w, dt in outs]
    out_specs = [pl.BlockSpec((bm, w), lambda i: (i, 0)) for w, _ in outs]
    out_shape += [jax.ShapeDtypeStruct(s, F32) for s in accs]
    out_specs += [pl.BlockSpec(s, lambda i: (0, 0)) for s in accs]
    n_in, n_out = len(ins), len(outs)

    def body(*refs):
        i = pl.program_id(0)
        res = fn(i, *[r[...] for r in refs[:n_in]])
        if not isinstance(res, (tuple, list)):
            res = (res,)
        for r, v in zip(refs[n_in:n_in + n_out], res[:n_out]):
            r[...] = v.astype(r.dtype)
        if accs:
            @pl.when(i == 0)
            def _():
                for r in refs[n_in + n_out:]:
                    r[...] = jnp.zeros(r.shape, F32)
            for r, v in zip(refs[n_in + n_out:], res[n_out:]):
                r[...] += v

    res = pl.pallas_call(
        body, name=name, grid=(steps,), in_specs=in_specs, out_specs=out_specs, out_shape=out_shape,
        compiler_params=_params(("arbitrary",)))(*arrays)
    return res


def _tile(n, cap):
    for t in (1408, 1152, 1024, 896, 768, 640, 512, 384, 256, 128):
        if t <= cap and n % t == 0:
            return t
    return n


def _mm(a, b, mode, name, out_dtype=F32, add=None, layer=None, dst=None, norm_g=None):
    if mode == "tn":
        s, m = a.shape
        n = b.shape[1]
        bm, bn, bk = _tile(m, 1408), _tile(n, 1408), min(s, 1024)
        nk = s // bk

        def body(a_ref, b_ref, *rest):
            o_ref, acc_ref = rest[-2:]
            k = pl.program_id(2)

            @pl.when(k == 0)
            def _():
                acc_ref[...] = jnp.zeros(acc_ref.shape, F32)
            acc_ref[...] += _dot(a_ref[...].astype(BF16), b_ref[...].astype(BF16), TN)

            @pl.when(k == nk - 1)
            def _():
                o_ref[...] = acc_ref[...].astype(o_ref.dtype)

        in_specs = [pl.BlockSpec((bk, bm), lambda i, j, k: (k, i)), pl.BlockSpec((bk, bn), lambda i, j, k: (k, j))]
        if dst is None:
            return pl.pallas_call(
                body, name=name, grid=(m // bm, n // bn, nk), in_specs=in_specs,
                out_specs=pl.BlockSpec((bm, bn), lambda i, j, k: (i, j)),
                out_shape=jax.ShapeDtypeStruct((m, n), out_dtype),
                scratch_shapes=[pltpu.VMEM((bm, bn), F32)],
                compiler_params=_params(("parallel", "parallel", "arbitrary")))(a, b)
        return pl.pallas_call(
            body, name=name, grid=(m // bm, n // bn, nk), in_specs=in_specs + [pl.BlockSpec(memory_space=pl.ANY)],
            out_specs=pl.BlockSpec((None, bm, bn), lambda i, j, k: (layer, i, j)),
            out_shape=jax.ShapeDtypeStruct(dst.shape, dst.dtype), input_output_aliases={2: 0},
            scratch_shapes=[pltpu.VMEM((bm, bn), F32)],
            compiler_params=_params(("parallel", "parallel", "arbitrary")))(a, b, dst)

    m, k = a.shape
    if layer is None:
        n = b.shape[1] if mode == "nn" else b.shape[0]
    else:
        n = b.shape[2] if mode == "nn" else b.shape[1]
    out_bytes = jnp.dtype(out_dtype).itemsize

    def vmem_need(bm, bn):
        need = 2 * bm * k * a.dtype.itemsize + 2 * bn * k * b.dtype.itemsize + bm * bn * (2 * out_bytes + 4)
        need += 2 * bm * bn * 4 if add is not None else 0
        return need + (2 * bm * bn * 2 + bm * bn * 4 if norm_g is not None else 0)

    bm, bn = min(m, 512), _tile(n, 512)
    for cand in ((2048, 1408), (2048, 1024), (2048, 512), (1024, 1408), (1024, 1024), (1024, 512), (512, 1408), (512, 1024)):
        tm, tn = min(m, cand[0]), _tile(n, cand[1])
        if norm_g is not None and tn != n:
            continue
        if m % tm == 0 and vmem_need(tm, tn) <= MM_VMEM_BUDGET:
            bm, bn = tm, tn
            break
    assert norm_g is None or bn == n
    dims = NT if mode == "nt" else (((1,), (0,)), ((), ()))
    if layer is None:
        b_spec = (pl.BlockSpec((k, bn), lambda i, j: (0, j)) if mode == "nn" else pl.BlockSpec((bn, k), lambda i, j: (j, 0)))
    elif mode == "nn":
        b_spec = pl.BlockSpec((None, k, bn), lambda i, j: (layer, 0, j))
    else:
        b_spec = pl.BlockSpec((None, bn, k), lambda i, j: (layer, j, 0))
    in_specs = [pl.BlockSpec((bm, k), lambda i, j: (i, 0)), b_spec]
    args = [a, b]
    if add is not None:
        in_specs.append(pl.BlockSpec((bm, bn), lambda i, j: (i, j)))
        args.append(add)
    if norm_g is not None:
        in_specs.append(pl.BlockSpec((1, n), lambda i, j: (0, 0)))
        args.append(norm_g)
    n_out = 1 if norm_g is None else 2

    def body(a_ref, b_ref, *rest):
        acc = _dot(a_ref[...].astype(BF16), b_ref[...].astype(BF16), dims)
        if add is not None:
            acc = acc + rest[0][...]
        rest[-n_out][...] = acc.astype(rest[-n_out].dtype)
        if norm_g is not None:
            rest[-1][...] = (_rms_stats(acc)[1] * rest[-3][...]).astype(rest[-1].dtype)

    tile = pl.BlockSpec((bm, bn), lambda i, j: (i, j))
    res = pl.pallas_call(
        body, name=name, grid=(m // bm, n // bn), in_specs=in_specs,
        out_specs=[tile] * n_out,
        out_shape=[jax.ShapeDtypeStruct((m, n), out_dtype)] + [jax.ShapeDtypeStruct((m, n), BF16)] * (n_out - 1),
        compiler_params=_params(("parallel", "parallel")))(*args)
    return res[0] if norm_g is None else res


def _sigmoid(x):
    return 0.5 * jnp.tanh(0.5 * x) + 0.5


def _softplus(x):
    return jnp.maximum(x, 0.0) + jnp.log(1.0 + jnp.exp(-jnp.abs(x)))


def _silu_and_grad(x):
    s = _sigmoid(x)
    return x * s, s * (1.0 + x * (1.0 - s))


def _rms_stats(x):
    r = lax.rsqrt(jnp.mean(x * x, axis=-1, keepdims=True) + EPS)
    return r, x * r


def _rms_fwd(x, g, name, out_dtype=BF16, bm=512):
    def fn(i, x, g):
        _, xn = _rms_stats(x)
        return xn * g
    return _rowcall(fn, name, x.shape[0], bm, [_row(x), _full(g)], [(x.shape[1], out_dtype)])[0]


def _rms_bwd_math(x, g, dy):
    r, xn = _rms_stats(x)
    dxn = dy * g
    dx = r * (dxn - xn * jnp.mean(dxn * xn, axis=-1, keepdims=True))
    return dx, dy * xn


def _rms_bwd(x, g, dy, res, name, bm=256):
    d = x.shape[1]

    def fn(i, x, g, dy, *res_):
        dx, dg = _rms_bwd_math(x, g, dy.astype(F32))
        if res_:
            dx = dx + res_[0]
        return dx, dx, _fold8(dg)
    ins = [_row(x), _full(g), _row(dy)] + ([_row(res)] if res is not None else [])
    return _rowcall(fn, name, x.shape[0], bm, ins, [(d, F32), (d, BF16)], [(SUBLANES, d)])


def _final_loss(x, g, target, name, bm=256):
    d = x.shape[1]

    def fn(i, x, g, t):
        r, xn = _rms_stats(x)
        err = xn * g - t
        dy = err * (1.0 / d)
        dxn = dy * g
        dx = r * (dxn - xn * jnp.mean(dxn * xn, axis=-1, keepdims=True))
        e2 = _fold8(err * err)
        lp = e2[:, 0:LANES]
        for c in range(1, d // LANES):
            lp = lp + e2[:, c * LANES:(c + 1) * LANES]
        return dx, dx, lp * (0.5 / d), _fold8(dy * xn)
    return _rowcall(fn, name, x.shape[0], bm, [_row(x), _full(g), _row(target)], [(d, F32), (d, BF16)],
                    [(SUBLANES, LANES), (SUBLANES, d)])


FF_TILE = 256


def _move_col_tiles(w, src_tile, name):
    layers, rows, cols = w.shape

    def body(x_ref, o_ref):
        o_ref[...] = x_ref[...]

    return pl.pallas_call(
        body, name=name, grid=(layers, cols // FF_TILE),
        in_specs=[pl.BlockSpec((None, rows, FF_TILE), lambda l, j: (l, 0, src_tile(j)))],
        out_specs=pl.BlockSpec((None, rows, FF_TILE), lambda l, j: (l, 0, j)),
        out_shape=jax.ShapeDtypeStruct(w.shape, w.dtype),
        compiler_params=_params(("parallel", "parallel")))(w)


def _ff_interleave(w, name):
    half = D_FF // FF_TILE
    return _move_col_tiles(w, lambda j: (j % 2) * half + j // 2, name)


def _ff_deinterleave(w, name):
    half = D_FF // FF_TILE
    return _move_col_tiles(w, lambda j: jnp.where(j < half, 2 * j, 2 * (j - half) + 1), name)


def _gate_up_fwd(h, w_gu, layer, name, bm=2048):
    rows, k = h.shape
    bm = min(bm, rows)

    def body(a_ref, b_ref, gu_ref, act_ref):
        acc = _dot(a_ref[...], b_ref[...])
        gu_ref[...] = acc.astype(gu_ref.dtype)
        act_ref[...] = (_silu_and_grad(acc[:, :FF_TILE])[0] * acc[:, FF_TILE:]).astype(act_ref.dtype)

    return pl.pallas_call(
        body, name=name, grid=(rows // bm, D_FF // FF_TILE),
        in_specs=[pl.BlockSpec((bm, k), lambda i, j: (i, 0)), pl.BlockSpec((None, k, 2 * FF_TILE), lambda i, j: (layer, 0, j))],
        out_specs=[pl.BlockSpec((bm, 2 * FF_TILE), lambda i, j: (i, j)), pl.BlockSpec((bm, FF_TILE), lambda i, j: (i, j))],
        out_shape=[jax.ShapeDtypeStruct((rows, 2 * D_FF), BF16), jax.ShapeDtypeStruct((rows, D_FF), BF16)],
        compiler_params=_params(("parallel", "parallel")))(h, w_gu)


def _down_bwd(dx, w_down, layer, gu, name, bm=2048):
    rows, k = dx.shape
    bm = min(bm, rows)

    def body(a_ref, b_ref, gu_ref, o_ref):
        da = _dot(a_ref[...], b_ref[...], NT)
        gu = gu_ref[...].astype(F32)
        s, ds = _silu_and_grad(gu[:, :FF_TILE])
        o_ref[:, :FF_TILE] = (da * gu[:, FF_TILE:] * ds).astype(o_ref.dtype)
        o_ref[:, FF_TILE:] = (da * s).astype(o_ref.dtype)

    return pl.pallas_call(
        body, name=name, grid=(rows // bm, D_FF // FF_TILE),
        in_specs=[pl.BlockSpec((bm, k), lambda i, j: (i, 0)), pl.BlockSpec((None, FF_TILE, k), lambda i, j: (layer, j, 0)),
                  pl.BlockSpec((bm, 2 * FF_TILE), lambda i, j: (i, j))],
        out_specs=pl.BlockSpec((bm, 2 * FF_TILE), lambda i, j: (i, j)),
        out_shape=jax.ShapeDtypeStruct((rows, 2 * D_FF), BF16),
        compiler_params=_params(("parallel", "parallel")))(dx, w_down, gu)


def _rope_apply(x, tabs, sign):
    cos, ta, tb = tabs
    outs = []
    for c in range(x.shape[1] // LANES):
        xc = x[:, c * LANES:(c + 1) * LANES]
        if sign > 0:
            o = xc * cos + pltpu.roll(xc, LANES - 8, 1) * ta + pltpu.roll(xc, 8, 1) * tb
        else:
            o = xc * cos + pltpu.roll(xc * ta, 8, 1) + pltpu.roll(xc * tb, LANES - 8, 1)
        outs.append(o)
    return outs[0] if len(outs) == 1 else jnp.concatenate(outs, axis=1)


def _rope(x, w, cb, tabs, sign, name, out_dtype, bm=512):
    def fn(i, x, c, a, b):
        return _rope_apply(x.astype(F32), (c, a, b), sign)
    return _rowcall(fn, name, x.shape[0], bm, [_row(x, w, cb)] + [_row(t) for t in tabs], [(w, out_dtype)])[0]


def _shift_down(x, prev8, s, first):
    xs = pltpu.roll(x, s, 0)
    rp = pltpu.roll(prev8, s, 0) * jnp.where(first, 0.0, 1.0)
    rid = lax.broadcasted_iota(jnp.int32, rp.shape, 0)
    top = jnp.where(rid < s, rp, xs[0:SUBLANES])
    return jnp.concatenate([top, xs[SUBLANES:]], axis=0)


def _shift_up(x, next8, s, last):
    n = x.shape[0]
    xs = pltpu.roll(x, n - s, 0)
    rn = pltpu.roll(next8, SUBLANES - s, 0) * jnp.where(last, 0.0, 1.0)
    rid = lax.broadcasted_iota(jnp.int32, rn.shape, 0)
    bot = jnp.where(rid >= SUBLANES - s, rn, xs[n - SUBLANES:])
    return jnp.concatenate([xs[:n - SUBLANES], bot], axis=0)


def _conv_fwd(x, prev8, w, first):
    acc = x * w[CONV_K - 1:CONV_K]
    shifted = []
    for s in range(1, CONV_K):
        xs = _shift_down(x, prev8, s, first)
        shifted.append(xs)
        acc = acc + xs * w[CONV_K - 1 - s:CONV_K - s]
    return acc, shifted


def _l2n(x):
    outs, rs = [], []
    for h in range(x.shape[1] // LANES):
        xh = x[:, h * LANES:(h + 1) * LANES]
        r = lax.rsqrt(jnp.sum(xh * xh, axis=-1, keepdims=True) + EPS)
        outs.append(xh * r)
        rs.append(r)
    return jnp.concatenate(outs, axis=1), rs


def _split3(x):
    hi = x.astype(BF16)
    r = x - hi.astype(F32)
    mid = r.astype(BF16)
    return hi, mid, (r - mid.astype(F32)).astype(BF16)


def _gate_math(ba, a_row, dt_row):
    al = ba + dt_row
    ea = jnp.exp(a_row)
    return _sigmoid(ba), al, ea, -ea * _softplus(al)


def _spread(x, sel):
    return sum(_dot(part, sel) for part in _split3(x))


def _gather_heads(x, sel):
    return sum(_dot(part, sel, NT) for part in _split3(x)) * (1.0 / LANES)


def _cumsum_chunks(x, reverse=False):
    n = x.shape[0]
    rid = lax.broadcasted_iota(jnp.int32, x.shape, 0) % CHUNK
    s = 1
    while s < CHUNK:
        if reverse:
            x = x + jnp.where(rid < CHUNK - s, pltpu.roll(x, n - s, 0), 0.0)
        else:
            x = x + jnp.where(rid >= s, pltpu.roll(x, s, 0), 0.0)
        s *= 2
    return x


def _gdn_pre_fwd(proj, conv_w, sel_b, sel_a, a_row, dt_row, name, bm=256):
    rows = proj.shape[0]
    w3 = 3 * GDN_W

    def fn(i, x, p8, ba, w, sel_b, sel_a, a_row, dt_row):
        conv, _ = _conv_fwd(x, p8, w, i == 0)
        act = _silu_and_grad(conv)[0]
        qk, _ = _l2n(act[:, :2 * GDN_W])
        beta, _, _, g = _gate_math(ba, a_row, dt_row)
        return (qk[:, :GDN_W], qk[:, GDN_W:], act[:, 2 * GDN_W:], _spread(_cumsum_chunks(g), sel_a),
                _spread(beta, sel_b))
    ins = [_row(proj, w3, 0), _prev8(proj, w3, 0), _row(proj, LANES, (GDN_IN_PAD - LANES) // LANES),
           _full(conv_w), _full(sel_b), _full(sel_a), _full(a_row), _full(dt_row)]
    return _rowcall(fn, name, rows, bm, ins, [(GDN_W, F32)] * 5)


def _gdn_pre_bwd(proj, conv_w, sel_b, sel_a, a_row, dt_row, dq, dk, dv, dgc, dbeta, name, bm=128):
    rows = proj.shape[0]
    w3 = 3 * GDN_W

    def fn(i, x, p8, ba, w, sel_b, sel_a, a_row, dt_row, dq, dk, dv, dgc, dbeta):
        conv, shifted = _conv_fwd(x, p8, w, i == 0)
        act, dact = _silu_and_grad(conv)
        qk, rs = _l2n(act[:, :2 * GDN_W])
        dqk = jnp.concatenate([dq, dk], axis=1)
        parts = []
        for h in range(2 * GDN_HEADS):
            sl = slice(h * LANES, (h + 1) * LANES)
            y, dy = qk[:, sl], dqk[:, sl]
            parts.append(rs[h] * (dy - y * jnp.sum(y * dy, axis=-1, keepdims=True)))
        dconv = jnp.concatenate(parts + [dv], axis=1) * dact
        dws = [_fold8(dconv * xs) for xs in reversed(shifted)] + [_fold8(dconv * x)]
        beta, al, ea, g = _gate_math(ba, a_row, dt_row)
        dg = _cumsum_chunks(_gather_heads(dgc, sel_a), reverse=True)
        dbl = _gather_heads(dbeta, sel_b) * beta * (1.0 - beta)
        dal = dg * (-ea) * _sigmoid(al)
        return (dconv, dbl + dal) + tuple(dws) + (_fold8(dg * g), _fold8(dal))
    ins = [_row(proj, w3, 0), _prev8(proj, w3, 0), _row(proj, LANES, (GDN_IN_PAD - LANES) // LANES),
           _full(conv_w), _full(sel_b), _full(sel_a), _full(a_row), _full(dt_row),
           _row(dq), _row(dk), _row(dv), _row(dgc), _row(dbeta)]
    return _rowcall(fn, name, rows, bm, ins, [(w3, F32), (LANES, BF16)],
                    [(SUBLANES, w3)] * CONV_K + [(SUBLANES, LANES)] * 2)


def _conv_bwd_input(dconv, conv_w, name, bm=256):
    rows, w3 = dconv.shape
    steps = rows // min(bm, rows)

    def fn(i, dc, n8, w):
        acc = dc * w[CONV_K - 1:CONV_K]
        for s in range(1, CONV_K):
            acc = acc + _shift_up(dc, n8, s, i == steps - 1) * w[CONV_K - 1 - s:CONV_K - s]
        return acc
    return _rowcall(fn, name, rows, bm, [_row(dconv), _next8(dconv, w3, 0), _full(conv_w)], [(w3, BF16)])[0]


def _gdn_post_fwd(o, proj, ng, name, bm=512):
    def fn(i, o, z, ng):
        outs = []
        for h in range(GDN_HEADS):
            sl = slice(h * LANES, (h + 1) * LANES)
            _, on = _rms_stats(o[:, sl])
            outs.append(on * ng * _silu_and_grad(z[:, sl])[0])
        return jnp.concatenate(outs, axis=1)
    return _rowcall(fn, name, o.shape[0], bm, [_row(o), _row(proj, GDN_W, 3), _full(ng)], [(GDN_W, BF16)])[0]


def _gdn_post_bwd(o, proj, ng, dcat, name, bm=256):
    def fn(i, o, z, ng, dm):
        dm = dm.astype(F32)
        dos, dzs = [], []
        dng = jnp.zeros((SUBLANES, LANES), F32)
        for h in range(GDN_HEADS):
            sl = slice(h * LANES, (h + 1) * LANES)
            s, ds = _silu_and_grad(z[:, sl])
            r, on = _rms_stats(o[:, sl])
            dzs.append(dm[:, sl] * on * ng * ds)
            dy = dm[:, sl] * s
            dxn = dy * ng
            dos.append(r * (dxn - on * jnp.mean(dxn * on, axis=-1, keepdims=True)))
            dng = dng + _fold8(dy * on)
        return jnp.concatenate(dos, axis=1), jnp.concatenate(dzs, axis=1), dng
    return _rowcall(fn, name, o.shape[0], bm, [_row(o), _row(proj, GDN_W, 3), _full(ng), _row(dcat, GDN_W, 0)],
                    [(GDN_W, F32), (GDN_W, BF16)], [(SUBLANES, LANES)])


def _bdot(a, b, mode="nn"):
    lc, rc = {"nn": (2, 1), "nt": (2, 2), "tn": (1, 1)}[mode]
    return lax.dot_general(a, b, (((lc,), (rc,)), ((0,), (0,))), preferred_element_type=F32)


def _bdot3(a, b, mode="nn"):
    ah, bh = a.astype(BF16), b.astype(BF16)
    al, bl = (a - ah.astype(F32)).astype(BF16), (b - bh.astype(F32)).astype(BF16)
    return _bdot(ah, bh, mode) + _bdot(ah, bl, mode) + _bdot(al, bh, mode)


GDN_CB = 4


def _gdn_chunk(q, k, v, gc, beta, t=None):
    c = CHUNK
    nb = q.shape[0]
    row = lax.broadcasted_iota(jnp.int32, (c, c), 0)
    col = lax.broadcasted_iota(jnp.int32, (c, c), 1)
    tril, strict = row >= col, row > col
    lane0 = (lax.broadcasted_iota(jnp.int32, (nb, c, LANES), 2) == 0).astype(BF16)
    gc_row = sum(_bdot(lane0, part, "nt") for part in _split3(gc))
    dm = jnp.exp(jnp.where(tril, gc[:, :, :c] - gc_row, -1e30))
    eg = jnp.exp(gc)
    gcl = gc[:, c - 1:c, :]
    ekg = jnp.exp(gcl - gc)
    egl = jnp.exp(gcl)
    qs = q * (GDN_DK ** -0.5)
    kb = k * beta
    kk = _bdot(kb, k, "nt")
    a = jnp.where(strict, kk * dm, 0.0)
    vb = v * beta
    kbg = kb * eg
    qk = _bdot(qs, k, "nt")
    p = jnp.where(tril, qk * dm, 0.0)
    out = dict(tril=tril, strict=strict, dm=dm, eg=eg, ekg=ekg, egl=egl, qs=qs, kb=kb, kk=kk, a=a,
               vb=vb, kbg=kbg, qk=qk, p=p, qg=qs * eg, kg=k * ekg)
    if t is None:
        y = -a
        t = (row == col).astype(F32) + y
        for _ in range(5):
            y = _bdot3(y, y)
            t = t + _bdot3(t, y)
        out.update(u=_bdot3(t, vb), w=_bdot3(t, kbg))
    out["t"] = t
    return out


def _gdn_stack(ref):
    return jnp.stack([ref[c * CHUNK:(c + 1) * CHUNK, h * LANES:(h + 1) * LANES]
                      for c in range(GDN_CB) for h in range(GDN_HEADS)])


def _gdn_unstack(x, ref):
    for c in range(GDN_CB):
        for h in range(GDN_HEADS):
            ref[c * CHUNK:(c + 1) * CHUNK, h * LANES:(h + 1) * LANES] = x[c * GDN_HEADS + h]


def _gdn_fwd(q, k, v, gc, beta, name):
    rows = q.shape[0]
    n_chunks = rows // CHUNK
    steps = n_chunks // GDN_CB
    blk = pl.BlockSpec((GDN_CB * CHUNK, GDN_W), lambda n: (n, 0))
    st = pl.BlockSpec((GDN_HEADS, GDN_CB, GDN_DK, LANES), lambda n: (0, n, 0, 0))
    tinv = pl.BlockSpec((GDN_CB, GDN_HEADS, CHUNK, CHUNK), lambda n: (n, 0, 0, 0))

    def body(q_ref, k_ref, v_ref, g_ref, b_ref, o_ref, st_ref, t_ref, w_ref, vn_ref, s_ref):
        @pl.when(pl.program_id(0) == 0)
        def _():
            s_ref[...] = jnp.zeros(s_ref.shape, F32)
        c = _gdn_chunk(*[_gdn_stack(r) for r in (q_ref, k_ref, v_ref, g_ref, b_ref)])
        _gdn_unstack(c["w"], w_ref)
        s = s_ref[...]
        for i in range(GDN_CB):
            hs = slice(i * GDN_HEADS, (i + 1) * GDN_HEADS)
            rs = slice(i * CHUNK, (i + 1) * CHUNK)
            st_ref[:, i] = s
            vn = c["u"][hs] - _bdot(c["w"][hs], s)
            o = _bdot(c["qg"][hs], s) + _bdot(c["p"][hs], vn)
            s = s * c["egl"][hs] + _bdot(c["kg"][hs], vn, "tn")
            t_ref[i] = c["t"][hs]
            for h in range(GDN_HEADS):
                o_ref[rs, h * LANES:(h + 1) * LANES] = o[h]
                vn_ref[rs, h * LANES:(h + 1) * LANES] = vn[h]
        s_ref[...] = s

    f = jax.ShapeDtypeStruct((rows, GDN_W), F32)
    return pl.pallas_call(
        body, name=name, grid=(steps,), in_specs=[blk] * 5, out_specs=[blk, st, tinv, blk, blk],
        out_shape=[f, jax.ShapeDtypeStruct((GDN_HEADS, n_chunks, GDN_DK, LANES), F32),
                   jax.ShapeDtypeStruct((n_chunks, GDN_HEADS, CHUNK, CHUNK), F32), f, f],
        scratch_shapes=[pltpu.VMEM((GDN_HEADS, GDN_DK, LANES), F32)],
        compiler_params=_params(("arbitrary",)))(q, k, v, gc, beta)


def _gdn_bwd(q, k, v, gc, beta, states, tinv, w, vn, do, name):
    rows = q.shape[0]
    n_chunks = rows // CHUNK
    steps = n_chunks // GDN_CB
    blk = pl.BlockSpec((GDN_CB * CHUNK, GDN_W), lambda n: (steps - 1 - n, 0))
    st = pl.BlockSpec((GDN_HEADS, GDN_CB, GDN_DK, LANES), lambda n: (0, steps - 1 - n, 0, 0))
    ti = pl.BlockSpec((GDN_CB, GDN_HEADS, CHUNK, CHUNK), lambda n: (steps - 1 - n, 0, 0, 0))
    nbatch = GDN_CB * GDN_HEADS

    def lanesum(x):
        return jnp.broadcast_to(jnp.sum(x, axis=-1, keepdims=True), x.shape)

    def body(q_ref, k_ref, v_ref, g_ref, b_ref, st_ref, t_ref, w_ref, vn_ref, do_ref,
             dq_ref, dk_ref, dv_ref, dg_ref, db_ref, ds_ref):
        @pl.when(pl.program_id(0) == 0)
        def _():
            ds_ref[...] = jnp.zeros(ds_ref.shape, F32)
        q, k, v, gc, beta, w, vn, do = [_gdn_stack(r) for r in (q_ref, k_ref, v_ref, g_ref, b_ref, w_ref, vn_ref, do_ref)]
        t = t_ref[...].reshape(nbatch, CHUNK, CHUNK)
        s = jnp.stack([st_ref[h, i] for i in range(GDN_CB) for h in range(GDN_HEADS)])
        c = _gdn_chunk(q, k, v, gc, beta, t)
        tril, strict, dm = c["tril"], c["strict"], c["dm"]
        dsn = ds_ref[...]
        dvn_c, dkg_c, dgl_c = [None] * GDN_CB, [None] * GDN_CB, [None] * GDN_CB
        for i in reversed(range(GDN_CB)):
            hs = slice(i * GDN_HEADS, (i + 1) * GDN_HEADS)
            dvn_c[i] = _bdot(c["p"][hs], do[hs], "tn") + _bdot(c["kg"][hs], dsn)
            dkg_c[i] = _bdot(vn[hs], dsn, "nt")
            dgl_c[i] = jnp.sum(jnp.sum(s[hs] * dsn, axis=2, keepdims=True), axis=1, keepdims=True) * c["egl"][hs]
            dsn = dsn * c["egl"][hs] + _bdot(c["qg"][hs], do[hs], "tn") - _bdot(w[hs], dvn_c[i], "tn")
        ds_ref[...] = dsn
        dvn, dkg, dgl = jnp.concatenate(dvn_c), jnp.concatenate(dkg_c), jnp.concatenate(dgl_c)
        dp = jnp.where(tril, _bdot(do, vn, "nt"), 0.0)
        dqg = _bdot(do, s, "nt")
        dw = -_bdot(dvn, s, "nt")
        dvb = _bdot3(t, dvn, "tn")
        dkbg = _bdot3(t, dw, "tn")
        dt = _bdot(dvn, c["vb"], "nt") + _bdot(dw, c["kbg"], "nt")
        da = jnp.where(strict, -_bdot3(_bdot3(t, dt, "tn"), t, "nt"), 0.0)
        dkk = da * dm
        dqk = dp * dm
        dkb = _bdot(dkk, k) + dkbg * c["eg"]
        dk = _bdot(dkk, c["kb"], "tn") + _bdot(dqk, c["qs"], "tn") + dkg * c["ekg"] + dkb * beta
        dqs = _bdot(dqk, k) + dqg * c["eg"]
        e = da * c["a"] + dp * c["p"]
        ones = jnp.ones((nbatch, CHUNK, LANES), BF16)
        col_sums = sum(_bdot(part, ones, "tn") for part in _split3(e))
        kg_term = lanesum(dkg * c["kg"])
        dgc = (jnp.broadcast_to(jnp.sum(e, axis=-1, keepdims=True), (nbatch, CHUNK, LANES)) - col_sums
               + lanesum(dqg * c["qg"]) - kg_term + lanesum(dkbg * c["kbg"]))
        dgcl = jnp.sum(kg_term, axis=1, keepdims=True) + dgl
        last = lax.broadcasted_iota(jnp.int32, (CHUNK, LANES), 0) == CHUNK - 1
        _gdn_unstack(dqs * (GDN_DK ** -0.5), dq_ref)
        _gdn_unstack(dk, dk_ref)
        _gdn_unstack(dvb * beta, dv_ref)
        _gdn_unstack(dgc + jnp.where(last, dgcl, 0.0), dg_ref)
        _gdn_unstack(lanesum(dvb * v) + lanesum(dkb * k), db_ref)

    return pl.pallas_call(
        body, name=name, grid=(steps,), in_specs=[blk] * 5 + [st, ti, blk, blk, blk], out_specs=[blk] * 5,
        out_shape=[jax.ShapeDtypeStruct((rows, GDN_W), F32)] * 5,
        scratch_shapes=[pltpu.VMEM((GDN_HEADS, GDN_DK, LANES), F32)],
        compiler_params=_params(("arbitrary",)))(q, k, v, gc, beta, states, tinv, w, vn, do)


def _swa_masks(first):
    r = lax.broadcasted_iota(jnp.int32, (SWA_BLOCK, 2 * SWA_BLOCK), 0)
    c = lax.broadcasted_iota(jnp.int32, (SWA_BLOCK, 2 * SWA_BLOCK), 1)
    band = (c > r) & (c <= r + SWA_BLOCK)
    return band & (jnp.logical_not(first) | (c >= SWA_BLOCK))


def _swa_stack(ref, j):
    lane = lax.broadcasted_iota(jnp.int32, (1, LANES), 1)
    parts = []
    for g in range(SWA_GROUP):
        ch = j * (SWA_GROUP // 2) + g // 2
        keep = (lane < SWA_DH) if g % 2 == 0 else (lane >= SWA_DH)
        parts.append(ref[:, ch * LANES:(ch + 1) * LANES] * keep.astype(ref.dtype))
    return jnp.concatenate(parts, axis=0)


def _swa_unstack(x2, j, out_ref):
    low = lax.broadcasted_iota(jnp.int32, (SWA_BLOCK, LANES), 1) < SWA_DH
    for c3 in range(SWA_GROUP // 2):
        even = x2[(2 * c3) * SWA_BLOCK:(2 * c3 + 1) * SWA_BLOCK]
        odd = x2[(2 * c3 + 1) * SWA_BLOCK:(2 * c3 + 2) * SWA_BLOCK]
        ch = j * (SWA_GROUP // 2) + c3
        out_ref[:, ch * LANES:(ch + 1) * LANES] = jnp.where(low, even, odd).astype(out_ref.dtype)


def _swa_probs(s, sink, mask):
    s = jnp.where(mask, s, -1e30)
    m = jnp.maximum(jnp.max(s, axis=-1, keepdims=True), sink)
    p = jnp.where(mask, jnp.exp(s - m), 0.0)
    es = jnp.exp(sink - m)
    inv = 1.0 / (jnp.sum(p, axis=-1, keepdims=True) + es)
    return p * inv, es * inv


def _swa_scores(q_ref, kc_ref, kp_ref, sink_ref, j, mask):
    sl = slice(j * LANES, (j + 1) * LANES)
    qst = _swa_stack(q_ref, j)
    kw = jnp.concatenate([kp_ref[:, sl], kc_ref[:, sl]], axis=0)
    s = _dot(qst, kw, NT) * (SWA_DH ** -0.5)
    ps = [_swa_probs(s[g * SWA_BLOCK:(g + 1) * SWA_BLOCK], sink_ref[j * SWA_GROUP + g], mask)
          for g in range(SWA_GROUP)]
    return qst, kw, ps


def _swa_fwd(q, k2, v2, sinks, name):
    rows = q.shape[0]
    nb = rows // SWA_BLOCK
    w = SWA_HEADS * SWA_DH
    kvw = SWA_KV_HEADS * LANES
    cur = pl.BlockSpec((SWA_BLOCK, w), lambda i: (i, 0))
    kcur = pl.BlockSpec((SWA_BLOCK, kvw), lambda i: (i, 0))
    kprev = pl.BlockSpec((SWA_BLOCK, kvw), lambda i: (jnp.maximum(i - 1, 0), 0))

    def body(sink_ref, q_ref, kc_ref, kp_ref, vc_ref, vp_ref, o_ref):
        mask = _swa_masks(pl.program_id(0) == 0)
        for j in range(SWA_KV_HEADS):
            sl = slice(j * LANES, (j + 1) * LANES)
            _, _, ps = _swa_scores(q_ref, kc_ref, kp_ref, sink_ref, j, mask)
            vw = jnp.concatenate([vp_ref[:, sl], vc_ref[:, sl]], axis=0)
            pst = jnp.concatenate([p.astype(BF16) for p, _ in ps], axis=0)
            _swa_unstack(_dot(pst, vw), j, o_ref)

    return pl.pallas_call(
        body, name=name, grid=(nb,),
        in_specs=[pl.BlockSpec(memory_space=pltpu.SMEM), cur, kcur, kprev, kcur, kprev], out_specs=cur,
        out_shape=jax.ShapeDtypeStruct((rows, w), BF16),
        compiler_params=_params(("arbitrary",)))(sinks, q, k2, k2, v2, v2)


def _swa_bwd(q, k2, v2, sinks, dcat, name):
    rows = q.shape[0]
    nb = rows // SWA_BLOCK
    w = SWA_HEADS * SWA_DH
    kvw = SWA_KV_HEADS * LANES
    cur = pl.BlockSpec((SWA_BLOCK, w), lambda i: (jnp.minimum(i, nb - 1), 0))
    kcur = pl.BlockSpec((SWA_BLOCK, kvw), lambda i: (jnp.minimum(i, nb - 1), 0))
    kprev = pl.BlockSpec((SWA_BLOCK, kvw), lambda i: (jnp.clip(i - 1, 0, nb - 1), 0))
    late = pl.BlockSpec((SWA_BLOCK, kvw), lambda i: (jnp.maximum(i - 1, 0), 0))
    acc_spec = pl.BlockSpec((SUBLANES, LANES), lambda i: (0, 0))

    def body(sink_ref, q_ref, kc_ref, kp_ref, vc_ref, vp_ref, do_ref, dq_ref, dk_ref, dv_ref, dsk_ref,
             ck_ref, cv_ref):
        i = pl.program_id(0)

        @pl.when(i == 0)
        def _():
            ck_ref[...] = jnp.zeros(ck_ref.shape, F32)
            cv_ref[...] = jnp.zeros(cv_ref.shape, F32)
            dsk_ref[...] = jnp.zeros(dsk_ref.shape, F32)

        @pl.when(i == nb)
        def _():
            dk_ref[...] = ck_ref[...]
            dv_ref[...] = cv_ref[...]

        @pl.when(i < nb)
        def _():
            mask = _swa_masks(i == 0)
            lane = lax.broadcasted_iota(jnp.int32, (SUBLANES, LANES), 1)
            dsk = jnp.zeros((SUBLANES, LANES), F32)
            for j in range(SWA_KV_HEADS):
                sl = slice(j * LANES, (j + 1) * LANES)
                qst, kw, ps = _swa_scores(q_ref, kc_ref, kp_ref, sink_ref, j, mask)
                vw = jnp.concatenate([vp_ref[:, sl], vc_ref[:, sl]], axis=0)
                dost = _swa_stack(do_ref, j)
                dpr = _dot(dost, vw, NT)
                dss = []
                for g in range(SWA_GROUP):
                    p, sink_p = ps[g]
                    dpg = dpr[g * SWA_BLOCK:(g + 1) * SWA_BLOCK]
                    delta = jnp.sum(p * dpg, axis=-1, keepdims=True)
                    dss.append((p * (dpg - delta)).astype(BF16))
                    dsg = jnp.sum(-sink_p * delta, axis=0, keepdims=True)
                    dsk = dsk + jnp.where(lane == j * SWA_GROUP + g, dsg, 0.0)
                dsst = jnp.concatenate(dss, axis=0)
                pst = jnp.concatenate([p.astype(BF16) for p, _ in ps], axis=0)
                _swa_unstack(_dot(dsst, kw) * (SWA_DH ** -0.5), j, dq_ref)
                dk = _dot(dsst, qst, TN) * (SWA_DH ** -0.5)
                dv = _dot(pst, dost, TN)
                dk = dk + pltpu.roll(dk, SWA_DH, 1)
                dv = dv + pltpu.roll(dv, SWA_DH, 1)
                dk_ref[:, sl] = ck_ref[:, sl] + dk[:SWA_BLOCK]
                dv_ref[:, sl] = cv_ref[:, sl] + dv[:SWA_BLOCK]
                ck_ref[:, sl] = dk[SWA_BLOCK:]
                cv_ref[:, sl] = dv[SWA_BLOCK:]
            dsk_ref[...] += dsk

    f = jax.ShapeDtypeStruct((rows, kvw), F32)
    return pl.pallas_call(
        body, name=name, grid=(nb + 1,),
        in_specs=[pl.BlockSpec(memory_space=pltpu.SMEM), cur, kcur, kprev, kcur, kprev, cur],
        out_specs=[cur, late, late, acc_spec],
        out_shape=[jax.ShapeDtypeStruct((rows, w), F32), f, f, jax.ShapeDtypeStruct((SUBLANES, LANES), F32)],
        scratch_shapes=[pltpu.VMEM((SWA_BLOCK, kvw), F32), pltpu.VMEM((SWA_BLOCK, kvw), F32)],
        compiler_params=_params(("arbitrary",)))(sinks, q, k2, k2, v2, v2, dcat)


def _mem_probs(mq, kbd):
    s = _dot(mq.astype(BF16), kbd) * (MEM_DH ** -0.5)
    ps = []
    for h in range(MEM_HEADS):
        sh = s[:, h * MEM_LEN:(h + 1) * MEM_LEN]
        e = jnp.exp(sh - jnp.max(sh, axis=-1, keepdims=True))
        ps.append(e / jnp.sum(e, axis=-1, keepdims=True))
    return ps


def _mem_fwd(proj, cb, kbd, vbd, name, bm=512):
    def fn(i, mq, kbd, vbd):
        p = jnp.concatenate(_mem_probs(mq, kbd), axis=1)
        return _dot(p.astype(BF16), vbd)
    return _rowcall(fn, name, proj.shape[0], bm, [_row(proj, MEM_W, cb), _full(kbd), _full(vbd)], [(MEM_W, BF16)])[0]


def _mem_bwd(proj, cb, kbd, vbd, dcat, name, bm=512):
    def fn(i, mq, kbd, vbd, do):
        ps = _mem_probs(mq, kbd)
        dp = _dot(do, vbd, NT)
        dss = []
        for h in range(MEM_HEADS):
            dph = dp[:, h * MEM_LEN:(h + 1) * MEM_LEN]
            dss.append(ps[h] * (dph - jnp.sum(ps[h] * dph, axis=-1, keepdims=True)))
        ds = (jnp.concatenate(dss, axis=1) * (MEM_DH ** -0.5)).astype(BF16)
        p = jnp.concatenate(ps, axis=1).astype(BF16)
        return _dot(ds, kbd, NT), _dot(mq.astype(BF16), ds, TN), _dot(p, do, TN)
    return _rowcall(fn, name, proj.shape[0], bm, [_row(proj, MEM_W, cb), _full(kbd), _full(vbd), _row(dcat, MEM_W, 3)],
                    [(MEM_W, BF16)], [(MEM_W, MEM_HEADS * MEM_LEN), (MEM_HEADS * MEM_LEN, MEM_W)])


def _mem_expand(mkv):
    feat_head = jnp.arange(MEM_W) // MEM_DH
    slot_head = jnp.arange(MEM_HEADS * MEM_LEN) // MEM_LEN
    on = feat_head[:, None] == slot_head[None, :]
    kbd = jnp.where(on, jnp.tile(mkv[:, :MEM_W].T, (1, MEM_HEADS)), 0.0)
    vbd = jnp.where(on.T, jnp.tile(mkv[:, MEM_W:], (MEM_HEADS, 1)), 0.0)
    return kbd.astype(BF16), vbd.astype(BF16)


def _mem_collapse(dkbd, dvbd):
    dk = [dkbd[h * MEM_DH:(h + 1) * MEM_DH, h * MEM_LEN:(h + 1) * MEM_LEN].T for h in range(MEM_HEADS)]
    dv = [dvbd[h * MEM_LEN:(h + 1) * MEM_LEN, h * MEM_DH:(h + 1) * MEM_DH] for h in range(MEM_HEADS)]
    return jnp.concatenate(dk + dv, axis=1)


def _adamw_math(w, g, m, v):
    m = ADAM_B1 * m + (1.0 - ADAM_B1) * g
    v = ADAM_B2 * v + (1.0 - ADAM_B2) * (g * g)
    m_hat = m / (1.0 - ADAM_B1 ** ADAM_STEP)
    v_hat = v / (1.0 - ADAM_B2 ** ADAM_STEP)
    return -ADAM_LR * (m_hat / (jnp.sqrt(v_hat) + ADAM_EPS) + ADAM_WD * w), m, v


def _adamw(w, g, m, v, name, bm=512):
    d = w.shape[1]
    return _rowcall(lambda i, *a: _adamw_math(*a), name, w.shape[0], bm, [_row(w), _row(g), _row(m), _row(v)], [(d, F32)] * 3)


def _adamw_halves(w, g_own, g_other, my_core, m, v, name):
    layers, rows, n = w.shape
    r = rows // 2
    bm = LANES if r % LANES == 0 else r
    per_half = r // bm
    nat = pl.BlockSpec((None, bm, n), lambda l, c, i: (l, c * per_half + i, 0))
    own = pl.BlockSpec((None, bm, n), lambda l, c, i: (l, i, 0))
    other = pl.BlockSpec((None, None, bm, n), lambda l, c, i: (0, l, i, 0))

    def body(core_ref, w_ref, own_ref, other_ref, m_ref, v_ref, go_ref, d_ref, mo_ref, vo_ref):
        g = jnp.where(core_ref[0] == pl.program_id(1), own_ref[...], other_ref[...])
        go_ref[...] = g
        d_ref[...], mo_ref[...], vo_ref[...] = _adamw_math(w_ref[...], g, m_ref[...], v_ref[...])

    return pl.pallas_call(
        body, name=name, grid=(layers, 2, per_half),
        in_specs=[pl.BlockSpec(memory_space=pltpu.SMEM), nat, own, other, nat, nat], out_specs=[nat] * 4,
        out_shape=[jax.ShapeDtypeStruct(w.shape, F32)] * 4,
        compiler_params=_params(("parallel", "parallel", "parallel")))(my_core, w, g_own, g_other, m, v)


def _sum_slots(buf, name, bm=128):
    n, rows, w = buf.shape
    bm = min(bm, rows)
    assert rows % bm == 0

    def body(b_ref, o_ref):
        acc = b_ref[0].astype(F32)
        for s in range(1, n):
            acc = acc + b_ref[s].astype(F32)
        o_ref[...] = acc

    return pl.pallas_call(
        body, name=name, grid=(rows // bm,), in_specs=[pl.BlockSpec((n, bm, w), lambda i: (0, i, 0))],
        out_specs=pl.BlockSpec((bm, w), lambda i: (i, 0)), out_shape=jax.ShapeDtypeStruct((rows, w), F32),
        compiler_params=_params(("parallel",)))(buf)


def _exchange(srcs, same, masks, name, keep_own=True):
    slots = N_DEV if len(masks) == N_DEV - 1 else (2 if keep_own else 1)
    n_arr, n_peer = len(srcs), len(masks)
    shapes = [s.shape if sm else s.shape[1:] for s, sm in zip(srcs, same)]

    def body(*refs):
        src_refs, out_refs = refs[:n_arr], refs[n_arr:2 * n_arr]
        send_sems, recv_sems, local_sems = refs[2 * n_arr:]
        x, y, c = lax.axis_index("x"), lax.axis_index("y"), lax.axis_index("c")
        me = 4 * x + 2 * y + c

        def flip(v, bit):
            return 1 - v if bit else v

        def slot_of(dev):
            return dev if slots == N_DEV else (dev % 2 if slots == 2 else 0)

        def piece(a, p):
            return src_refs[a] if same[a] else src_refs[a].at[p]

        local = []
        if keep_own:
            local = [pltpu.make_async_copy(piece(a, me), out_refs[a].at[slot_of(me)], local_sems.at[a])
                     for a in range(n_arr)]
        for cp in local:
            cp.start()
        copies = []
        for idx, k in enumerate(masks):
            peer = (flip(x, k & 4), flip(y, k & 2), flip(c, k & 1))
            peer_id = 4 * peer[0] + 2 * peer[1] + peer[2]
            for a in range(n_arr):
                sem = idx * n_arr + a
                cp = pltpu.make_async_remote_copy(
                    src_ref=piece(a, peer_id), dst_ref=out_refs[a].at[slot_of(me)],
                    send_sem=send_sems.at[sem], recv_sem=recv_sems.at[sem], device_id=peer, device_id_type=MESH)
                cp.start()
                copies.append((cp, pltpu.make_async_remote_copy(
                    src_ref=piece(a, peer_id), dst_ref=out_refs[a].at[slot_of(peer_id)],
                    send_sem=send_sems.at[sem], recv_sem=recv_sems.at[sem], device_id=peer, device_id_type=MESH)))
        for cp, landing in copies:
            cp.wait_send()
            landing.wait_recv()
        for cp in local:
            cp.wait()

    any_spec = pl.BlockSpec(memory_space=pl.ANY)
    n_sem = n_arr * n_peer
    return pl.pallas_call(
        body, name=name, in_specs=[any_spec] * n_arr, out_specs=[any_spec] * n_arr,
        out_shape=[jax.ShapeDtypeStruct((slots,) + tuple(sh), s.dtype) for sh, s in zip(shapes, srcs)],
        scratch_shapes=[pltpu.SemaphoreType.DMA((n_sem,)), pltpu.SemaphoreType.DMA((n_sem,)),
                        pltpu.SemaphoreType.DMA((n_arr,))],
        )(*srcs)


ALL_PEERS = tuple(range(1, N_DEV))
SIBLING = (1,)


def _send_to_sibling(srcs, name):
    n_arr = len(srcs)

    def body(*refs):
        src_refs, out_refs = refs[:n_arr], refs[n_arr:2 * n_arr]
        send_sems, recv_sems = refs[2 * n_arr:]
        x, y, c = lax.axis_index("x"), lax.axis_index("y"), lax.axis_index("c")
        copies = [pltpu.make_async_remote_copy(
            src_ref=src_refs[a].at[1 - c], dst_ref=out_refs[a], send_sem=send_sems.at[a], recv_sem=recv_sems.at[a],
            device_id=(x, y, 1 - c), device_id_type=MESH) for a in range(n_arr)]
        for cp in copies:
            cp.start()
        for cp in copies:
            cp.wait()

    any_spec = pl.BlockSpec(memory_space=pl.ANY)
    return pl.pallas_call(
        body, name=name, in_specs=[any_spec] * n_arr, out_specs=[any_spec] * n_arr,
        out_shape=[jax.ShapeDtypeStruct(s.shape[1:], s.dtype) for s in srcs],
        scratch_shapes=[pltpu.SemaphoreType.DMA((n_arr,)), pltpu.SemaphoreType.DMA((n_arr,))],
        )(*srcs)


def _add_core_parts(mine, theirs, my_core, name, bm=128):
    _, chips, rows, n = mine.shape
    bm = min(bm, rows)
    assert rows % bm == 0

    def body(core_ref, p0_ref, p1_ref, t_ref, o_ref):
        own = jnp.where(core_ref[0] == 0, p0_ref[...], p1_ref[...])
        o_ref[...] = (own.astype(F32) + t_ref[...].astype(F32)).astype(o_ref.dtype)

    part = lambda k: pl.BlockSpec((None, None, bm, n), lambda s, i, k=k: (k, s, i, 0))
    flat = pl.BlockSpec((None, bm, n), lambda s, i: (s, i, 0))
    return pl.pallas_call(
        body, name=name, grid=(chips, rows // bm),
        in_specs=[pl.BlockSpec(memory_space=pltpu.SMEM), part(0), part(1), flat], out_specs=flat,
        out_shape=jax.ShapeDtypeStruct(theirs.shape, mine.dtype),
        compiler_params=_params(("parallel", "parallel")))(my_core, mine, mine, theirs)


def _scatter_by_chip(srcs, small, name):
    n_arr = len(srcs)

    def body(*refs):
        src_refs, small_ref = refs[:n_arr], refs[n_arr]
        out_refs, small_out = refs[n_arr + 1:2 * n_arr + 1], refs[2 * n_arr + 1]
        send_sems, recv_sems, local_sems = refs[2 * n_arr + 2:]
        x, y, c = lax.axis_index("x"), lax.axis_index("y"), lax.axis_index("c")
        my_chip, me = 2 * x + y, 4 * x + 2 * y + c
        chips = [(1 - x, y), (x, 1 - y), (1 - x, 1 - y)]
        local = [pltpu.make_async_copy(src_refs[a].at[my_chip], out_refs[a].at[my_chip], local_sems.at[a])
                 for a in range(n_arr)]
        local.append(pltpu.make_async_copy(small_ref, small_out.at[me], local_sems.at[n_arr]))
        for cp in local:
            cp.start()
        copies = []
        for j, (px, py) in enumerate(chips):
            their_chip = 2 * px + py
            for a in range(n_arr):
                sem = j * n_arr + a
                cp = pltpu.make_async_remote_copy(
                    src_ref=src_refs[a].at[their_chip], dst_ref=out_refs[a].at[my_chip],
                    send_sem=send_sems.at[sem], recv_sem=recv_sems.at[sem], device_id=(px, py, c), device_id_type=MESH)
                cp.start()
                copies.append((cp, pltpu.make_async_remote_copy(
                    src_ref=src_refs[a].at[their_chip], dst_ref=out_refs[a].at[their_chip],
                    send_sem=send_sems.at[sem], recv_sem=recv_sems.at[sem], device_id=(px, py, c), device_id_type=MESH)))
        for idx, k in enumerate(ALL_PEERS):
            peer = (1 - x if k & 4 else x, 1 - y if k & 2 else y, 1 - c if k & 1 else c)
            peer_id = 4 * peer[0] + 2 * peer[1] + peer[2]
            sem = 3 * n_arr + idx
            cp = pltpu.make_async_remote_copy(
                src_ref=small_ref, dst_ref=small_out.at[me], send_sem=send_sems.at[sem], recv_sem=recv_sems.at[sem],
                device_id=peer, device_id_type=MESH)
            cp.start()
            copies.append((cp, pltpu.make_async_remote_copy(
                src_ref=small_ref, dst_ref=small_out.at[peer_id], send_sem=send_sems.at[sem], recv_sem=recv_sems.at[sem],
                device_id=peer, device_id_type=MESH)))
        for cp, landing in copies:
            cp.wait_send()
            landing.wait_recv()
        for cp in local:
            cp.wait()

    any_spec = pl.BlockSpec(memory_space=pl.ANY)
    n_sem = 3 * n_arr + len(ALL_PEERS)
    return pl.pallas_call(
        body, name=name, in_specs=[any_spec] * (n_arr + 1), out_specs=[any_spec] * (n_arr + 1),
        out_shape=[jax.ShapeDtypeStruct(s.shape, s.dtype) for s in srcs]
        + [jax.ShapeDtypeStruct((N_DEV,) + small.shape, small.dtype)],
        scratch_shapes=[pltpu.SemaphoreType.DMA((n_sem,)), pltpu.SemaphoreType.DMA((n_sem,)),
                        pltpu.SemaphoreType.DMA((n_arr + 1,))],
        )(*srcs, small)


def _gather_two_level(srcs, name):
    n_arr = len(srcs)

    def body(*refs):
        src_refs, out_refs = refs[:n_arr], refs[n_arr:2 * n_arr]
        send_sems, recv_sems, local_sems = refs[2 * n_arr:]
        x, y, c = lax.axis_index("x"), lax.axis_index("y"), lax.axis_index("c")
        sibling = (x, y, 1 - c)
        chips = [(1 - x, y), (x, 1 - y), (1 - x, 1 - y)]

        def slot(px, py, pc):
            return 4 * px + 2 * py + pc

        def copy(a, k, block, to, own=False):
            return pltpu.make_async_remote_copy(
                src_ref=src_refs[a] if own else out_refs[a].at[slot(*block)], dst_ref=out_refs[a].at[slot(*block)],
                send_sem=send_sems.at[a * 7 + k], recv_sem=recv_sems.at[a * 7 + k], device_id=to, device_id_type=MESH)

        local = [pltpu.make_async_copy(src_refs[a], out_refs[a].at[slot(x, y, c)], local_sems.at[a]) for a in range(n_arr)]
        for cp in local:
            cp.start()
        started = []
        for a in range(n_arr):
            started.append(copy(a, 0, (x, y, c), sibling, own=True))
            started += [copy(a, 1 + j, (x, y, c), (*chip, c), own=True) for j, chip in enumerate(chips)]
        for cp in started:
            cp.start()
        for j, chip in enumerate(chips):
            for a in range(n_arr):
                copy(a, 1 + j, (*chip, c), (x, y, c)).wait_recv()
                passed = copy(a, 4 + j, (*chip, c), sibling)
                passed.start()
                started.append(passed)
        for a in range(n_arr):
            copy(a, 0, sibling, (x, y, c)).wait_recv()
            for j, chip in enumerate(chips):
                copy(a, 4 + j, (*chip, 1 - c), (x, y, c)).wait_recv()
        for cp in started:
            cp.wait_send()
        for cp in local:
            cp.wait()

    any_spec = pl.BlockSpec(memory_space=pl.ANY)
    return pl.pallas_call(
        body, name=name, in_specs=[any_spec] * n_arr, out_specs=[any_spec] * n_arr,
        out_shape=[jax.ShapeDtypeStruct((N_DEV,) + s.shape, s.dtype) for s in srcs],
        scratch_shapes=[pltpu.SemaphoreType.DMA((7 * n_arr,)), pltpu.SemaphoreType.DMA((7 * n_arr,)),
                        pltpu.SemaphoreType.DMA((n_arr,))],
        )(*srcs)


def _pack(arrays, rows):
    flat = jnp.concatenate([a.reshape(-1) for a in arrays])
    return jnp.pad(flat, (0, rows * D_MODEL - flat.shape[0])).reshape(rows, D_MODEL)


def _unpack(buf, shapes):
    flat = buf.reshape(-1)
    out, off = [], 0
    for s in shapes:
        n = math.prod(s)
        out.append(flat[off:off + n].reshape(s))
        off += n
    return out


def _rows_for(shapes, mult):
    n = sum(math.prod(s) for s in shapes)
    rows = -(-n // D_MODEL)
    return -(-rows // mult) * mult


SHARD_AXIS = dict(w_mem_kv=1, w_out=1, w_gate_up=2, w_down=1, gdn_w_in=2, swa_w_q=1, w_kv=0, gdn_conv=2)
HALF_AXIS = dict(w_mem_kv=1, w_out=1, w_gate_up=1, w_down=1, gdn_w_in=1, swa_w_q=1, w_kv=0)


def _my_half(shard, name, c):
    ax = HALF_AXIS[name]
    h = shard.shape[ax] // 2
    return lax.dynamic_slice_in_dim(shard, c * h, h, axis=ax)


def _piece_layout(name, half_shape):
    dims, pos = [], {}
    for i, d in enumerate(half_shape):
        if i == SHARD_AXIS[name]:
            pos["chip"] = len(dims)
            dims.append(4)
        if i == HALF_AXIS[name]:
            pos["core"] = len(dims)
            dims.append(2)
        pos[i] = len(dims)
        dims.append(d)
    return dims, [pos["chip"], pos["core"]] + [pos[i] for i in range(len(half_shape))]


def _full_shape(name, half_shape):
    return tuple(d * (4 if i == SHARD_AXIS[name] else 1) * (2 if i == HALF_AXIS[name] else 1)
                 for i, d in enumerate(half_shape))


def _assemble(pieces, name):
    half_shape = pieces.shape[1:]
    if half_shape[-1] % LANES:
        chips = [jnp.concatenate([pieces[2 * s], pieces[2 * s + 1]], axis=HALF_AXIS[name]) for s in range(4)]
        return jnp.concatenate(chips, axis=SHARD_AXIS[name])
    dims, perm = _piece_layout(name, half_shape)
    inverse = [perm.index(i) for i in range(len(perm))]
    return pieces.reshape((4, 2) + half_shape).transpose(inverse).reshape(_full_shape(name, half_shape))


def _to_pieces(full, name, half_shape):
    if half_shape[-1] % LANES:
        ns, nh = half_shape[SHARD_AXIS[name]], half_shape[HALF_AXIS[name]]
        parts = [lax.slice_in_dim(lax.slice_in_dim(full, s * ns, (s + 1) * ns, axis=SHARD_AXIS[name]),
                                  c * nh, (c + 1) * nh, axis=HALF_AXIS[name])
                 for c in range(2) for s in range(4)]
        return jnp.stack(parts).reshape((2, 4) + tuple(half_shape))
    dims, perm = _piece_layout(name, half_shape)
    return full.reshape(dims).transpose([perm[1], perm[0]] + perm[2:])


def _rope_tables(positions):
    half = ROT_DIM // 2
    inv = ROPE_THETA ** (-jnp.arange(0, ROT_DIM, 2, dtype=F32) / ROT_DIM)
    ang = positions.astype(F32)[:, None] * inv
    cos, sin = jnp.cos(ang), jnp.sin(ang)
    rows = positions.shape[0]
    one = jnp.ones((rows, SWA_DH - ROT_DIM), F32)
    zero = jnp.zeros((rows, SWA_DH - ROT_DIM), F32)
    zh = jnp.zeros((rows, half), F32)
    c64 = jnp.concatenate([cos, cos, one], axis=1)
    a64 = jnp.concatenate([-sin, zh, zero], axis=1)
    b64 = jnp.concatenate([zh, sin, zero], axis=1)
    return tuple(jnp.concatenate([t, t], axis=1) for t in (c64, a64, b64))


def _pair_heads(t):
    return jnp.concatenate([t[:, :SWA_DH], t[:, :SWA_DH], t[:, SWA_DH:], t[:, SWA_DH:]], axis=1)


def _unpair_heads(t):
    return jnp.concatenate([t[:, :SWA_DH], t[:, LANES:LANES + SWA_DH]], axis=1)


def _gdn_in_pad(w):
    o2 = 4 * GDN_W
    pad = jnp.zeros(w.shape[:-1] + (GDN_IN_PAD - GDN_IN,), w.dtype)
    return jnp.concatenate([w[..., :o2], w[..., o2 + 2 * GDN_HEADS:], w[..., o2:o2 + 2 * GDN_HEADS], pad], axis=-1)


def _gdn_in_unpad(w):
    o2 = 4 * GDN_W
    return jnp.concatenate([w[..., :o2], w[..., o2 + MEM_W:o2 + MEM_W + 2 * GDN_HEADS], w[..., o2:o2 + MEM_W]], axis=-1)


def _head_rows(v):
    return jnp.pad(v.astype(F32), (GDN_HEADS, LANES - 2 * GDN_HEADS))[None, :]


def _selectors():
    lane = jnp.arange(LANES)[:, None]
    head = (jnp.arange(GDN_W) // LANES)[None, :]
    return (lane == head).astype(BF16), (lane == head + GDN_HEADS).astype(BF16)


def _local_step(x, mem, positions, target, w):
    rows = x.shape[0]
    tabs = _rope_tables(positions)
    sel_b, sel_a = _selectors()
    row2 = lambda v: v.reshape(1, -1).astype(F32)

    w_gu = _ff_interleave(w["w_gate_up"], "interleave_gate_up")
    mem_n = _rms_fwd(mem, row2(w["ln_mem"]), "mem_norm")
    saved = []
    kt = vt = None
    for l in range(DEPTH):
        s = dict(x0=x)
        h = _rms_fwd(x, row2(w["ln_mix"][l]), f"norm_mix{l}") if l == 0 else h_next
        mkv =_mm(mem_n, w["w_mem_kv"], "nn", f"mem_kv{l}", layer=l)
        kbd, vbd = _mem_expand(mkv)
        if l < N_A:
            proj = _mm(h, w["gdn_w_in"], "nn", f"gdn_in{l}", layer=l)
            a_row, dt_row = _head_rows(w["gdn_A_log"][l]), _head_rows(w["gdn_dt_bias"][l])
            q, k, v, gc, beta = _gdn_pre_fwd(proj, w["gdn_conv"][l], sel_b, sel_a, a_row, dt_row, f"gdn_pre{l}")
            o, states, tinv, gw, vn = _gdn_fwd(q, k, v, gc, beta, f"gdn_scan{l}")
            mix = _gdn_post_fwd(o, proj, row2(w["gdn_norm"][l]), f"gdn_post{l}")
            mq_cb = (3 * GDN_W + GDN_W) // MEM_W
            s.update(q=q, k=k, v=v, gc=gc, beta=beta, o=o, states=states, tinv=tinv, gw=gw, vn=vn,
                     a_row=a_row, dt_row=dt_row)
        else:
            proj = _mm(h, w["swa_w_q"], "nn", f"swa_in{l}", layer=l - N_A)
            qr = _rope(proj, SWA_HEADS * SWA_DH, 0, tabs, 1, f"rope_q{l}", BF16)
            mix = _swa_fwd(qr, kt, vt, w["swa_sinks"][l - N_A], f"swa{l}")
            mq_cb = (SWA_HEADS * SWA_DH) // MEM_W
            s.update(qr=qr)
        mem_o = _mem_fwd(proj, mq_cb, kbd, vbd, f"mem_attn{l}")
        cat = jnp.concatenate([mix, mem_o], axis=1)
        x1, h2 = _mm(cat, w["w_out"], "nn", f"out_proj{l}", add=x, layer=l, norm_g=row2(w["ln_ffn"][l]))
        gu, act = _gate_up_fwd(h2, w_gu, l, f"gate_up{l}")
        if l + 1 < DEPTH:
            x, h_next = _mm(act, w["w_down"], "nn", f"down{l}", add=x1, layer=l, norm_g=row2(w["ln_mix"][l + 1]))
        else:
            x = _mm(act, w["w_down"], "nn", f"down{l}", add=x1, layer=l)
        s.update(h=h, proj=proj, kbd=kbd, vbd=vbd, mq_cb=mq_cb, cat=cat, x1=x1, h2=h2, gu=gu, act=act)
        saved.append(s)
        if l == N_A - 1:
            x_kv = x
            h_kv = _rms_fwd(x, row2(w["ln_kv"]), "norm_kv")
            kv = _mm(h_kv, w["w_kv"], "nn", "kv_proj")
            kr = _rope(kv, LANES, 0, tabs, 1, "rope_k", F32)
            kt = _pair_heads(kr).astype(BF16)
            vt = _pair_heads(kv[:, LANES:]).astype(BF16)

    gr = {}
    dx, dxb, loss_part, dlnf = _final_loss(x, row2(w["ln_final"]), target, "final_loss")
    gr["ln_final"] = dlnf.sum(axis=0)
    dln_mix, dln_ffn = [None] * DEPTH, [None] * DEPTH
    dw_mem_kv, dw_out, dw_gu, dw_dn, dgdn_in, dswa_q = [
        lax.empty(w[n].shape, BF16) for n in ("w_mem_kv", "w_out", "w_gate_up", "w_down", "gdn_w_in", "swa_w_q")]
    dgdn_conv, dgdn_a, dgdn_dt, dgdn_norm = [None] * N_A, [None] * N_A, [None] * N_A, [None] * N_A
    dswa_sinks = [None] * N_B
    dmem_n = None
    dkt = dvt = None
    for l in reversed(range(DEPTH)):
        s = saved[l]
        if l == N_A - 1:
            dkr = _unpair_heads(dkt)
            dk = _rope(dkr, LANES, 0, tabs, -1, "rope_k_bwd", BF16)
            dkv = jnp.concatenate([dk, _unpair_heads(dvt).astype(BF16)], axis=1)
            dh_kv = _mm(dkv, w["w_kv"], "nt", "kv_proj_dx", BF16)
            gr["w_kv"] = _mm(h_kv, dkv, "tn", "kv_proj_dw", BF16)
            dx, dxb, dg = _rms_bwd(x_kv, row2(w["ln_kv"]), dh_kv, dx, "norm_kv_bwd")
            gr["ln_kv"] = dg.sum(axis=0)
        dgu = _down_bwd(dxb, w["w_down"], l, s["gu"], f"down_dx{l}")
        dw_dn = _mm(s["act"], dxb, "tn", f"down_dw{l}", dst=dw_dn, layer=l)
        dh2 = _mm(dgu, w_gu, "nt", f"gate_up_dx{l}", BF16, layer=l)
        dw_gu = _mm(s["h2"], dgu, "tn", f"gate_up_dw{l}", dst=dw_gu, layer=l)
        dx, dxb, dg = _rms_bwd(s["x1"], row2(w["ln_ffn"][l]), dh2, dx, f"norm_ffn_bwd{l}")
        dln_ffn[l] = dg.sum(axis=0)
        dcat = _mm(dxb, w["w_out"], "nt", f"out_proj_dx{l}", BF16, layer=l)
        dw_out = _mm(s["cat"], dxb, "tn", f"out_proj_dw{l}", dst=dw_out, layer=l)
        dmq, dkbd, dvbd = _mem_bwd(s["proj"], s["mq_cb"], s["kbd"], s["vbd"], dcat, f"mem_attn_bwd{l}")
        dmkv = _mem_collapse(dkbd, dvbd).astype(BF16)
        dw_mem_kv = _mm(mem_n, dmkv, "tn", f"mem_kv_dw{l}", dst=dw_mem_kv, layer=l)
        dmem_n = _mm(dmkv, w["w_mem_kv"], "nt", f"mem_kv_dx{l}", add=dmem_n, layer=l)
        if l < N_A:
            do, dz, dng = _gdn_post_bwd(s["o"], s["proj"], row2(w["gdn_norm"][l]), dcat, f"gdn_post_bwd{l}")
            dq, dk, dv, dg_, dbeta = _gdn_bwd(s["q"], s["k"], s["v"], s["gc"], s["beta"], s["states"], s["tinv"], s["gw"],
                                              s["vn"], do, f"gdn_scan_bwd{l}")
            res = _gdn_pre_bwd(s["proj"], w["gdn_conv"][l], sel_b, sel_a, s["a_row"], s["dt_row"], dq, dk, dv, dg_, dbeta,
                               f"gdn_pre_bwd{l}")
            dconv, dba = res[0], res[1]
            dgdn_conv[l] = jnp.stack([r.sum(axis=0) for r in res[2:2 + CONV_K]])
            dgdn_a[l] = res[2 + CONV_K].sum(axis=0)[GDN_HEADS:2 * GDN_HEADS]
            dgdn_dt[l] = res[3 + CONV_K].sum(axis=0)[GDN_HEADS:2 * GDN_HEADS]
            dgdn_norm[l] = dng.sum(axis=0)
            dqkv = _conv_bwd_input(dconv, w["gdn_conv"][l], f"gdn_conv_bwd{l}")
            dproj = jnp.concatenate([dqkv, dz, dmq, dba], axis=1)
            dh = _mm(dproj, w["gdn_w_in"], "nt", f"gdn_in_dx{l}", BF16, layer=l)
            dgdn_in = _mm(s["h"], dproj, "tn", f"gdn_in_dw{l}", dst=dgdn_in, layer=l)
        else:
            b = l - N_A
            dqr, dkt_l, dvt_l, dsk = _swa_bwd(s["qr"], kt, vt, w["swa_sinks"][b], dcat, f"swa_bwd{l}")
            dkt = dkt_l if dkt is None else dkt + dkt_l
            dvt = dvt_l if dvt is None else dvt + dvt_l
            dswa_sinks[b] = dsk[0, :SWA_HEADS]
            dq = _rope(dqr, SWA_HEADS * SWA_DH, 0, tabs, -1, f"rope_q_bwd{l}", BF16)
            dproj = jnp.concatenate([dq, dmq], axis=1)
            dh = _mm(dproj, w["swa_w_q"], "nt", f"swa_in_dx{l}", BF16, layer=b)
            dswa_q = _mm(s["h"], dproj, "tn", f"swa_in_dw{l}", dst=dswa_q, layer=b)
        dx, dxb, dg = _rms_bwd(s["x0"], row2(w["ln_mix"][l]), dh, dx, f"norm_mix_bwd{l}")
        dln_mix[l] = dg.sum(axis=0)
    _, _, dg = _rms_bwd(mem, row2(w["ln_mem"]), dmem_n, None, "mem_norm_bwd")
    gr["ln_mem"] = dg.sum(axis=0)
    gr.update(ln_mix=jnp.stack(dln_mix), ln_ffn=jnp.stack(dln_ffn), w_mem_kv=dw_mem_kv, w_out=dw_out,
              w_gate_up=_ff_deinterleave(dw_gu, "deinterleave_gate_up"), w_down=dw_dn, gdn_w_in=dgdn_in, gdn_conv=jnp.stack(dgdn_conv),
              gdn_A_log=jnp.stack(dgdn_a), gdn_dt_bias=jnp.stack(dgdn_dt), gdn_norm=jnp.stack(dgdn_norm),
              swa_w_q=dswa_q, swa_sinks=jnp.stack(dswa_sinks))
    return loss_part, dx, gr


def kernel(x, mem, positions, ln_mix, ln_ffn, ln_mem, w_mem_kv, w_out, w_gate_up, w_down, gdn_w_in, gdn_conv, gdn_A_log, gdn_dt_bias, gdn_norm, swa_w_q, swa_sinks, ln_kv, w_kv, ln_final, loss_target, m_ln_mix, m_ln_ffn, m_ln_mem, m_w_mem_kv, m_w_out, m_w_gate_up, m_w_down, m_gdn_w_in, m_gdn_conv, m_gdn_A_log, m_gdn_dt_bias, m_gdn_norm, m_swa_w_q, m_swa_sinks, m_ln_kv, m_w_kv, m_ln_final, v_ln_mix, v_ln_ffn, v_ln_mem, v_w_mem_kv, v_w_out, v_w_gate_up, v_w_down, v_gdn_w_in, v_gdn_conv, v_gdn_A_log, v_gdn_dt_bias, v_gdn_norm, v_swa_w_q, v_swa_sinks, v_ln_kv, v_w_kv, v_ln_final):
    given = dict(locals())
    wts = {n: given[n] for n in WEIGHTS}
    c = lax.axis_index("c")

    halves = [_my_half(wts[n].astype(BF16), n, c) for n in SHARDED]
    half_shapes = [h.shape for h in halves]
    conv_shape = wts["gdn_conv"].shape
    cpack = _pack([wts["gdn_conv"]], 16)
    conv_half = lax.dynamic_slice_in_dim(cpack, c * SUBLANES, SUBLANES, axis=0)
    got = _gather_two_level(halves + [conv_half], "gather_weights")
    full = {n: wts[n] for n in SMALL}
    for n, g in zip(SHARDED, got):
        full[n] = _assemble(g, n)
    conv_all = got[-1].reshape(4, 16, D_MODEL)
    full["gdn_conv"] = jnp.concatenate([_unpack(conv_all[s], [conv_shape])[0] for s in range(4)], axis=2)
    full["gdn_w_in"] = _gdn_in_pad(full["gdn_w_in"])

    loss_part, dx, gr = _local_step(x[0], mem[0], positions[0], loss_target[0], full)
    gr["gdn_w_in"] = _gdn_in_unpad(gr["gdn_w_in"])

    pieces = [_to_pieces(gr[n].astype(BF16), n, hs) for n, hs in zip(SHARDED, half_shapes)]
    small_shapes = [wts[n].shape for n in SMALL] + [conv_shape[:2] + (4 * conv_shape[2],), (SUBLANES, LANES)]
    rows_s = _rows_for(small_shapes, SUBLANES)
    spack = _pack([gr[n] for n in SMALL] + [gr["gdn_conv"], loss_part], rows_s)
    my_core = c.astype(jnp.int32).reshape(1)
    from_sibling = _send_to_sibling(pieces, "pair_grads")
    chip_parts = [_add_core_parts(p.reshape(2, 4, -1, p.shape[-1]), t.reshape(4, -1, t.shape[-1]), my_core, f"pair_sum_{n}")
                  for n, p, t in zip(SHARDED, pieces, from_sibling)]
    parts = _scatter_by_chip(chip_parts, spack, "scatter_grads")
    mine = [_sum_slots(p, f"sum_{n}").reshape(hs) for n, p, hs in zip(SHARDED, parts, half_shapes)]
    ssum = _unpack(_sum_slots(parts[-1], "sum_small"), small_shapes)
    theirs = _exchange(mine, [True] * len(mine), SIBLING, "swap_grad_halves", keep_own=False)
    g_all = dict(zip(SMALL, ssum[:len(SMALL)]))
    chip = 2 * lax.axis_index("x") + lax.axis_index("y")
    g_all["gdn_conv"] = lax.dynamic_slice_in_dim(ssum[len(SMALL)], chip * conv_shape[2], conv_shape[2], axis=2)
    loss = jnp.sum(ssum[-1])

    out = dict(grad=g_all, delta={}, new_m={}, new_v={})
    for n, own, other in zip(SHARDED, mine, theirs):
        as3d = lambda a: a.reshape((-1,) + a.shape[-2:])
        res = _adamw_halves(as3d(wts[n]), as3d(own), other.reshape((1, -1) + other.shape[-2:]), my_core,
                            as3d(given["m_" + n]), as3d(given["v_" + n]), f"adamw_{n}")
        for kind, r in zip(("grad", "delta", "new_m", "new_v"), res):
            out[kind][n] = r.reshape(wts[n].shape)
    small_names = SMALL + ("gdn_conv",)
    small_w_shapes = [wts[n].shape for n in small_names]
    rows_a = _rows_for(small_w_shapes, SUBLANES)
    res = _adamw(_pack([wts[n] for n in small_names], rows_a), _pack([g_all[n] for n in small_names], rows_a),
                 _pack([given["m_" + n] for n in small_names], rows_a),
                 _pack([given["v_" + n] for n in small_names], rows_a), "adamw_small")
    for kind, r in zip(("delta", "new_m", "new_v"), res):
        out[kind].update(zip(small_names, _unpack(r, small_w_shapes)))
    return (loss, dx[None], *[out["grad"][n] for n in WEIGHTS], *[out["delta"][n] for n in WEIGHTS],
            *[out["new_m"][n] for n in WEIGHTS], *[out["new_v"][n] for n in WEIGHTS])
```

```python
import math

import jax
import jax.numpy as jnp
from jax import lax
from jax.experimental import pallas as pl
from jax.experimental.pallas import tpu as pltpu

F32 = jnp.float32
BF16 = jnp.bfloat16
MESH = pl.DeviceIdType.MESH

D_MODEL = 1024
DEPTH = 4
N_A = 2
N_B = 2
EPS = 1e-6
GDN_HEADS = 6
GDN_DK = 128
GDN_W = 768
CONV_K = 4
CHUNK = 64
SWA_HEADS = 12
SWA_KV_HEADS = 2
SWA_DH = 64
SWA_GROUP = 6
SWA_BLOCK = 128
ROPE_THETA = 500000.0
ROT_DIM = 16
MEM_LEN = 256
MEM_HEADS = 4
MEM_DH = 64
MEM_W = 256
D_FF = 2816
GDN_IN = 3340
GDN_IN_PAD = 3456
ADAM_LR = 0.001
ADAM_B1 = 0.9
ADAM_B2 = 0.999
ADAM_EPS = 1e-08
ADAM_WD = 0.01
ADAM_STEP = 10

N_DEV = 8
LANES = 128
SUBLANES = 8
V7X_VMEM_LIMIT = 56 * 2**20
MM_VMEM_BUDGET = 44 * 2**20

SHARDED = ("w_mem_kv", "w_out", "w_gate_up", "w_down", "gdn_w_in", "swa_w_q", "w_kv")
SMALL = ("ln_mix", "ln_ffn", "ln_mem", "gdn_A_log", "gdn_dt_bias", "gdn_norm", "swa_sinks", "ln_kv", "ln_final")
WEIGHTS = ("ln_mix", "ln_ffn", "ln_mem", "w_mem_kv", "w_out", "w_gate_up", "w_down", "gdn_w_in", "gdn_conv",
           "gdn_A_log", "gdn_dt_bias", "gdn_norm", "swa_w_q", "swa_sinks", "ln_kv", "w_kv", "ln_final")


def _params(sem=None, **kw):
    return pltpu.CompilerParams(dimension_semantics=sem, vmem_limit_bytes=V7X_VMEM_LIMIT, **kw)


def _dot(a, b, dims=(((1,), (0,)), ((), ())), precision=None):
    return lax.dot_general(a, b, dims, precision=precision, preferred_element_type=F32)


NT = (((1,), (1,)), ((), ()))
TN = (((0,), (0,)), ((), ()))


def _fold8(v):
    r, w = v.shape
    return v.reshape(r // SUBLANES, SUBLANES, w).sum(axis=0)


def _row(a, w=None, cb=0):
    return ("row", a, a.shape[1] if w is None else w, cb)


def _full(a):
    return ("full", a, None, None)


def _prev8(a, w, cb=0):
    return ("prev8", a, w, cb)


def _next8(a, w, cb=0):
    return ("next8", a, w, cb)


def _rowcall(fn, name, rows, bm, ins, outs, accs=()):
    bm = min(bm, rows)
    assert rows % bm == 0 and bm % SUBLANES == 0
    steps = rows // bm
    r8 = bm // SUBLANES
    in_specs, arrays = [], []
    for kind, a, w, cb in ins:
        arrays.append(a)
        if kind == "row":
            in_specs.append(pl.BlockSpec((bm, w), lambda i, cb=cb: (i, cb)))
        elif kind == "full":
            in_specs.append(pl.BlockSpec(a.shape, lambda i, nd=a.ndim: (0,) * nd))
        elif kind == "prev8":
            in_specs.append(pl.BlockSpec((SUBLANES, w), lambda i, cb=cb: (jnp.maximum(i * r8 - 1, 0), cb)))
        else:
            last = rows // SUBLANES - 1
            in_specs.append(pl.BlockSpec((SUBLANES, w), lambda i, cb=cb: (jnp.minimum((i + 1) * r8, last), cb)))
    out_shape = [jax.ShapeDtypeStruct((rows, w), dt) for w, dt in outs]
    out_specs = [pl.BlockSpec((bm, w), lambda i: (i, 0)) for w, _ in outs]
    out_shape += [jax.ShapeDtypeStruct(s, F32) for s in accs]
    out_specs += [pl.BlockSpec(s, lambda i: (0, 0)) for s in accs]
    n_in, n_out = len(ins), len(outs)

    def body(*refs):
        i = pl.program_id(0)
        res = fn(i, *[r[...] for r in refs[:n_in]])
        if not isinstance(res, (tuple, list)):
            res = (res,)
        for r, v in zip(refs[n_in:n_in + n_out], res[:n_out]):
            r[...] = v.astype(r.dtype)
        if accs:
            @pl.when(i == 0)
            def _():
                for r in refs[n_in + n_out:]:
                    r[...] = jnp.zeros(r.shape, F32)
            for r, v in zip(refs[n_in + n_out:], res[n_out:]):
                r[...] += v

    res = pl.pallas_call(
        body, name=name, grid=(steps,), in_specs=in_specs, out_specs=out_specs, out_shape=out_shape,
        compiler_params=_params(("arbitrary",)))(*arrays)
    return res


def _tile(n, cap):
    for t in (1408, 1152, 1024, 896, 768, 640, 512, 384, 256, 128):
        if t <= cap and n % t == 0:
            return t
    return n


def _mm(a, b, mode, name, out_dtype=F32, add=None, layer=None, dst=None, norm_g=None):
    if mode == "tn":
        s, m = a.shape
        n = b.shape[1]
        bm, bn, bk = _tile(m, 1408), _tile(n, 1408), min(s, 1024)
        nk = s // bk

        def body(a_ref, b_ref, *rest):
            o_ref, acc_ref = rest[-2:]
            k = pl.program_id(2)

            @pl.when(k == 0)
            def _():
                acc_ref[...] = jnp.zeros(acc_ref.shape, F32)
            acc_ref[...] += _dot(a_ref[...].astype(BF16), b_ref[...].astype(BF16), TN)

            @pl.when(k == nk - 1)
            def _():
                o_ref[...] = acc_ref[...].astype(o_ref.dtype)

        in_specs = [pl.BlockSpec((bk, bm), lambda i, j, k: (k, i)), pl.BlockSpec((bk, bn), lambda i, j, k: (k, j))]
        if dst is None:
            return pl.pallas_call(
                body, name=name, grid=(m // bm, n // bn, nk), in_specs=in_specs,
                out_specs=pl.BlockSpec((bm, bn), lambda i, j, k: (i, j)),
                out_shape=jax.ShapeDtypeStruct((m, n), out_dtype),
                scratch_shapes=[pltpu.VMEM((bm, bn), F32)],
                compiler_params=_params(("parallel", "parallel", "arbitrary")))(a, b)
        return pl.pallas_call(
            body, name=name, grid=(m // bm, n // bn, nk), in_specs=in_specs + [pl.BlockSpec(memory_space=pl.ANY)],
            out_specs=pl.BlockSpec((None, bm, bn), lambda i, j, k: (layer, i, j)),
            out_shape=jax.ShapeDtypeStruct(dst.shape, dst.dtype), input_output_aliases={2: 0},
            scratch_shapes=[pltpu.VMEM((bm, bn), F32)],
            compiler_params=_params(("parallel", "parallel", "arbitrary")))(a, b, dst)

    m, k = a.shape
    if layer is None:
        n = b.shape[1] if mode == "nn" else b.shape[0]
    else:
        n = b.shape[2] if mode == "nn" else b.shape[1]
    out_bytes = jnp.dtype(out_dtype).itemsize

    def vmem_need(bm, bn):
        need = 2 * bm * k * a.dtype.itemsize + 2 * bn * k * b.dtype.itemsize + bm * bn * (2 * out_bytes + 4)
        need += 2 * bm * bn * 4 if add is not None else 0
        return need + (2 * bm * bn * 2 + bm * bn * 4 if norm_g is not None else 0)

    bm, bn = min(m, 512), _tile(n, 512)
    for cand in ((2048, 1408), (2048, 1024), (2048, 512), (1024, 1408), (1024, 1024), (1024, 512), (512, 1408), (512, 1024)):
        tm, tn = min(m, cand[0]), _tile(n, cand[1])
        if norm_g is not None and tn != n:
            continue
        if m % tm == 0 and vmem_need(tm, tn) <= MM_VMEM_BUDGET:
            bm, bn = tm, tn
            break
    assert norm_g is None or bn == n
    dims = NT if mode == "nt" else (((1,), (0,)), ((), ()))
    if layer is None:
        b_spec = (pl.BlockSpec((k, bn), lambda i, j: (0, j)) if mode == "nn" else pl.BlockSpec((bn, k), lambda i, j: (j, 0)))
    elif mode == "nn":
        b_spec = pl.BlockSpec((None, k, bn), lambda i, j: (layer, 0, j))
    else:
        b_spec = pl.BlockSpec((None, bn, k), lambda i, j: (layer, j, 0))
    in_specs = [pl.BlockSpec((bm, k), lambda i, j: (i, 0)), b_spec]
    args = [a, b]
    if add is not None:
        in_specs.append(pl.BlockSpec((bm, bn), lambda i, j: (i, j)))
        args.append(add)
    if norm_g is not None:
        in_specs.append(pl.BlockSpec((1, n), lambda i, j: (0, 0)))
        args.append(norm_g)
    n_out = 1 if norm_g is None else 2

    def body(a_ref, b_ref, *rest):
        acc = _dot(a_ref[...].astype(BF16), b_ref[...].astype(BF16), dims)
        if add is not None:
            acc = acc + rest[0][...]
        rest[-n_out][...] = acc.astype(rest[-n_out].dtype)
        if norm_g is not None:
            rest[-1][...] = (_rms_stats(acc)[1] * rest[-3][...]).astype(rest[-1].dtype)

    tile = pl.BlockSpec((bm, bn), lambda i, j: (i, j))
    res = pl.pallas_call(
        body, name=name, grid=(m // bm, n // bn), in_specs=in_specs,
        out_specs=[tile] * n_out,
        out_shape=[jax.ShapeDtypeStruct((m, n), out_dtype)] + [jax.ShapeDtypeStruct((m, n), BF16)] * (n_out - 1),
        compiler_params=_params(("parallel", "parallel")))(*args)
    return res[0] if norm_g is None else res


def _sigmoid(x):
    return 0.5 * jnp.tanh(0.5 * x) + 0.5


def _softplus(x):
    return jnp.maximum(x, 0.0) + jnp.log(1.0 + jnp.exp(-jnp.abs(x)))


def _silu_and_grad(x):
    s = _sigmoid(x)
    return x * s, s * (1.0 + x * (1.0 - s))


def _rms_stats(x):
    r = lax.rsqrt(jnp.mean(x * x, axis=-1, keepdims=True) + EPS)
    return r, x * r


def _rms_fwd(x, g, name, out_dtype=BF16, bm=512):
    def fn(i, x, g):
        _, xn = _rms_stats(x)
        return xn * g
    return _rowcall(fn, name, x.shape[0], bm, [_row(x), _full(g)], [(x.shape[1], out_dtype)])[0]


def _rms_bwd_math(x, g, dy):
    r, xn = _rms_stats(x)
    dxn = dy * g
    dx = r * (dxn - xn * jnp.mean(dxn * xn, axis=-1, keepdims=True))
    return dx, dy * xn


def _rms_bwd(x, g, dy, res, name, bm=256):
    d = x.shape[1]

    def fn(i, x, g, dy, *res_):
        dx, dg = _rms_bwd_math(x, g, dy.astype(F32))
        if res_:
            dx = dx + res_[0]
        return dx, dx, _fold8(dg)
    ins = [_row(x), _full(g), _row(dy)] + ([_row(res)] if res is not None else [])
    return _rowcall(fn, name, x.shape[0], bm, ins, [(d, F32), (d, BF16)], [(SUBLANES, d)])


def _final_loss(x, g, target, name, bm=256):
    d = x.shape[1]

    def fn(i, x, g, t):
        r, xn = _rms_stats(x)
        err = xn * g - t
        dy = err * (1.0 / d)
        dxn = dy * g
        dx = r * (dxn - xn * jnp.mean(dxn * xn, axis=-1, keepdims=True))
        e2 = _fold8(err * err)
        lp = e2[:, 0:LANES]
        for c in range(1, d // LANES):
            lp = lp + e2[:, c * LANES:(c + 1) * LANES]
        return dx, dx, lp * (0.5 / d), _fold8(dy * xn)
    return _rowcall(fn, name, x.shape[0], bm, [_row(x), _full(g), _row(target)], [(d, F32), (d, BF16)],
                    [(SUBLANES, LANES), (SUBLANES, d)])


FF_TILE = 256


def _move_col_tiles(w, src_tile, name):
    layers, rows, cols = w.shape

    def body(x_ref, o_ref):
        o_ref[...] = x_ref[...]

    return pl.pallas_call(
        body, name=name, grid=(layers, cols // FF_TILE),
        in_specs=[pl.BlockSpec((None, rows, FF_TILE), lambda l, j: (l, 0, src_tile(j)))],
        out_specs=pl.BlockSpec((None, rows, FF_TILE), lambda l, j: (l, 0, j)),
        out_shape=jax.ShapeDtypeStruct(w.shape, w.dtype),
        compiler_params=_params(("parallel", "parallel")))(w)


def _ff_interleave(w, name):
    half = D_FF // FF_TILE
    return _move_col_tiles(w, lambda j: (j % 2) * half + j // 2, name)


def _ff_deinterleave(w, name):
    half = D_FF // FF_TILE
    return _move_col_tiles(w, lambda j: jnp.where(j < half, 2 * j, 2 * (j - half) + 1), name)


def _gate_up_fwd(h, w_gu, layer, name, bm=2048):
    rows, k = h.shape
    bm = min(bm, rows)

    def body(a_ref, b_ref, gu_ref, act_ref):
        acc = _dot(a_ref[...], b_ref[...])
        gu_ref[...] = acc.astype(gu_ref.dtype)
        act_ref[...] = (_silu_and_grad(acc[:, :FF_TILE])[0] * acc[:, FF_TILE:]).astype(act_ref.dtype)

    return pl.pallas_call(
        body, name=name, grid=(rows // bm, D_FF // FF_TILE),
        in_specs=[pl.BlockSpec((bm, k), lambda i, j: (i, 0)), pl.BlockSpec((None, k, 2 * FF_TILE), lambda i, j: (layer, 0, j))],
        out_specs=[pl.BlockSpec((bm, 2 * FF_TILE), lambda i, j: (i, j)), pl.BlockSpec((bm, FF_TILE), lambda i, j: (i, j))],
        out_shape=[jax.ShapeDtypeStruct((rows, 2 * D_FF), BF16), jax.ShapeDtypeStruct((rows, D_FF), BF16)],
        compiler_params=_params(("parallel", "parallel")))(h, w_gu)


def _down_bwd(dx, w_down, layer, gu, name, bm=2048):
    rows, k = dx.shape
    bm = min(bm, rows)

    def body(a_ref, b_ref, gu_ref, o_ref):
        da = _dot(a_ref[...], b_ref[...], NT)
        gu = gu_ref[...].astype(F32)
        s, ds = _silu_and_grad(gu[:, :FF_TILE])
        o_ref[:, :FF_TILE] = (da * gu[:, FF_TILE:] * ds).astype(o_ref.dtype)
        o_ref[:, FF_TILE:] = (da * s).astype(o_ref.dtype)

    return pl.pallas_call(
        body, name=name, grid=(rows // bm, D_FF // FF_TILE),
        in_specs=[pl.BlockSpec((bm, k), lambda i, j: (i, 0)), pl.BlockSpec((None, FF_TILE, k), lambda i, j: (layer, j, 0)),
                  pl.BlockSpec((bm, 2 * FF_TILE), lambda i, j: (i, j))],
        out_specs=pl.BlockSpec((bm, 2 * FF_TILE), lambda i, j: (i, j)),
        out_shape=jax.ShapeDtypeStruct((rows, 2 * D_FF), BF16),
        compiler_params=_params(("parallel", "parallel")))(dx, w_down, gu)


def _rope_apply(x, tabs, sign):
    cos, ta, tb = tabs
    outs = []
    for c in range(x.shape[1] // LANES):
        xc = x[:, c * LANES:(c + 1) * LANES]
        if sign > 0:
            o = xc * cos + pltpu.roll(xc, LANES - 8, 1) * ta + pltpu.roll(xc, 8, 1) * tb
        else:
            o = xc * cos + pltpu.roll(xc * ta, 8, 1) + pltpu.roll(xc * tb, LANES - 8, 1)
        outs.append(o)
    return outs[0] if len(outs) == 1 else jnp.concatenate(outs, axis=1)


def _rope(x, w, cb, tabs, sign, name, out_dtype, bm=512):
    def fn(i, x, c, a, b):
        return _rope_apply(x.astype(F32), (c, a, b), sign)
    return _rowcall(fn, name, x.shape[0], bm, [_row(x, w, cb)] + [_row(t) for t in tabs], [(w, out_dtype)])[0]


def _shift_down(x, prev8, s, first):
    xs = pltpu.roll(x, s, 0)
    rp = pltpu.roll(prev8, s, 0) * jnp.where(first, 0.0, 1.0)
    rid = lax.broadcasted_iota(jnp.int32, rp.shape, 0)
    top = jnp.where(rid < s, rp, xs[0:SUBLANES])
    return jnp.concatenate([top, xs[SUBLANES:]], axis=0)


def _shift_up(x, next8, s, last):
    n = x.shape[0]
    xs = pltpu.roll(x, n - s, 0)
    rn = pltpu.roll(next8, SUBLANES - s, 0) * jnp.where(last, 0.0, 1.0)
    rid = lax.broadcasted_iota(jnp.int32, rn.shape, 0)
    bot = jnp.where(rid >= SUBLANES - s, rn, xs[n - SUBLANES:])
    return jnp.concatenate([xs[:n - SUBLANES], bot], axis=0)


def _conv_fwd(x, prev8, w, first):
    acc = x * w[CONV_K - 1:CONV_K]
    shifted = []
    for s in range(1, CONV_K):
        xs = _shift_down(x, prev8, s, first)
        shifted.append(xs)
        acc = acc + xs * w[CONV_K - 1 - s:CONV_K - s]
    return acc, shifted


def _l2n(x):
    outs, rs = [], []
    for h in range(x.shape[1] // LANES):
        xh = x[:, h * LANES:(h + 1) * LANES]
        r = lax.rsqrt(jnp.sum(xh * xh, axis=-1, keepdims=True) + EPS)
        outs.append(xh * r)
        rs.append(r)
    return jnp.concatenate(outs, axis=1), rs


def _split3(x):
    hi = x.astype(BF16)
    r = x - hi.astype(F32)
    mid = r.astype(BF16)
    return hi, mid, (r - mid.astype(F32)).astype(BF16)


def _gate_math(ba, a_row, dt_row):
    al = ba + dt_row
    ea = jnp.exp(a_row)
    return _sigmoid(ba), al, ea, -ea * _softplus(al)


def _spread(x, sel):
    return sum(_dot(part, sel) for part in _split3(x))


def _gather_heads(x, sel):
    return sum(_dot(part, sel, NT) for part in _split3(x)) * (1.0 / LANES)


def _cumsum_chunks(x, reverse=False):
    n = x.shape[0]
    rid = lax.broadcasted_iota(jnp.int32, x.shape, 0) % CHUNK
    s = 1
    while s < CHUNK:
        if reverse:
            x = x + jnp.where(rid < CHUNK - s, pltpu.roll(x, n - s, 0), 0.0)
        else:
            x = x + jnp.where(rid >= s, pltpu.roll(x, s, 0), 0.0)
        s *= 2
    return x


def _gdn_pre_fwd(proj, conv_w, sel_b, sel_a, a_row, dt_row, name, bm=256):
    rows = proj.shape[0]
    w3 = 3 * GDN_W

    def fn(i, x, p8, ba, w, sel_b, sel_a, a_row, dt_row):
        conv, _ = _conv_fwd(x, p8, w, i == 0)
        act = _silu_and_grad(conv)[0]
        qk, _ = _l2n(act[:, :2 * GDN_W])
        beta, _, _, g = _gate_math(ba, a_row, dt_row)
        return (qk[:, :GDN_W], qk[:, GDN_W:], act[:, 2 * GDN_W:], _spread(_cumsum_chunks(g), sel_a),
                _spread(beta, sel_b))
    ins = [_row(proj, w3, 0), _prev8(proj, w3, 0), _row(proj, LANES, (GDN_IN_PAD - LANES) // LANES),
           _full(conv_w), _full(sel_b), _full(sel_a), _full(a_row), _full(dt_row)]
    return _rowcall(fn, name, rows, bm, ins, [(GDN_W, F32)] * 5)


def _gdn_pre_bwd(proj, conv_w, sel_b, sel_a, a_row, dt_row, dq, dk, dv, dgc, dbeta, name, bm=128):
    rows = proj.shape[0]
    w3 = 3 * GDN_W

    def fn(i, x, p8, ba, w, sel_b, sel_a, a_row, dt_row, dq, dk, dv, dgc, dbeta):
        conv, shifted = _conv_fwd(x, p8, w, i == 0)
        act, dact = _silu_and_grad(conv)
        qk, rs = _l2n(act[:, :2 * GDN_W])
        dqk = jnp.concatenate([dq, dk], axis=1)
        parts = []
        for h in range(2 * GDN_HEADS):
            sl = slice(h * LANES, (h + 1) * LANES)
            y, dy = qk[:, sl], dqk[:, sl]
            parts.append(rs[h] * (dy - y * jnp.sum(y * dy, axis=-1, keepdims=True)))
        dconv = jnp.concatenate(parts + [dv], axis=1) * dact
        dws = [_fold8(dconv * xs) for xs in reversed(shifted)] + [_fold8(dconv * x)]
        beta, al, ea, g = _gate_math(ba, a_row, dt_row)
        dg = _cumsum_chunks(_gather_heads(dgc, sel_a), reverse=True)
        dbl = _gather_heads(dbeta, sel_b) * beta * (1.0 - beta)
        dal = dg * (-ea) * _sigmoid(al)
        return (dconv, dbl + dal) + tuple(dws) + (_fold8(dg * g), _fold8(dal))
    ins = [_row(proj, w3, 0), _prev8(proj, w3, 0), _row(proj, LANES, (GDN_IN_PAD - LANES) // LANES),
           _full(conv_w), _full(sel_b), _full(sel_a), _full(a_row), _full(dt_row),
           _row(dq), _row(dk), _row(dv), _row(dgc), _row(dbeta)]
    return _rowcall(fn, name, rows, bm, ins, [(w3, F32), (LANES, BF16)],
                    [(SUBLANES, w3)] * CONV_K + [(SUBLANES, LANES)] * 2)


def _conv_bwd_input(dconv, conv_w, name, bm=256):
    rows, w3 = dconv.shape
    steps = rows // min(bm, rows)

    def fn(i, dc, n8, w):
        acc = dc * w[CONV_K - 1:CONV_K]
        for s in range(1, CONV_K):
            acc = acc + _shift_up(dc, n8, s, i == steps - 1) * w[CONV_K - 1 - s:CONV_K - s]
        return acc
    return _rowcall(fn, name, rows, bm, [_row(dconv), _next8(dconv, w3, 0), _full(conv_w)], [(w3, BF16)])[0]


def _gdn_post_fwd(o, proj, ng, name, bm=512):
    def fn(i, o, z, ng):
        outs = []
        for h in range(GDN_HEADS):
            sl = slice(h * LANES, (h + 1) * LANES)
            _, on = _rms_stats(o[:, sl])
            outs.append(on * ng * _silu_and_grad(z[:, sl])[0])
        return jnp.concatenate(outs, axis=1)
    return _rowcall(fn, name, o.shape[0], bm, [_row(o), _row(proj, GDN_W, 3), _full(ng)], [(GDN_W, BF16)])[0]


def _gdn_post_bwd(o, proj, ng, dcat, name, bm=256):
    def fn(i, o, z, ng, dm):
        dm = dm.astype(F32)
        dos, dzs = [], []
        dng = jnp.zeros((SUBLANES, LANES), F32)
        for h in range(GDN_HEADS):
            sl = slice(h * LANES, (h + 1) * LANES)
            s, ds = _silu_and_grad(z[:, sl])
            r, on = _rms_stats(o[:, sl])
            dzs.append(dm[:, sl] * on * ng * ds)
            dy = dm[:, sl] * s
            dxn = dy * ng
            dos.append(r * (dxn - on * jnp.mean(dxn * on, axis=-1, keepdims=True)))
            dng = dng + _fold8(dy * on)
        return jnp.concatenate(dos, axis=1), jnp.concatenate(dzs, axis=1), dng
    return _rowcall(fn, name, o.shape[0], bm, [_row(o), _row(proj, GDN_W, 3), _full(ng), _row(dcat, GDN_W, 0)],
                    [(GDN_W, F32), (GDN_W, BF16)], [(SUBLANES, LANES)])


def _bdot(a, b, mode="nn"):
    lc, rc = {"nn": (2, 1), "nt": (2, 2), "tn": (1, 1)}[mode]
    return lax.dot_general(a.astype(BF16), b.astype(BF16), (((lc,), (rc,)), ((0,), (0,))), preferred_element_type=F32)


def _bdot3(a, b, mode="nn"):
    ah, bh = a.astype(BF16), b.astype(BF16)
    al, bl = (a - ah.astype(F32)).astype(BF16), (b - bh.astype(F32)).astype(BF16)
    return _bdot(ah, bh, mode) + _bdot(ah, bl, mode) + _bdot(al, bh, mode)


GDN_CB = 4


def _gdn_chunk(q, k, v, gc, beta, t=None):
    c = CHUNK
    nb = q.shape[0]
    row = lax.broadcasted_iota(jnp.int32, (c, c), 0)
    col = lax.broadcasted_iota(jnp.int32, (c, c), 1)
    tril, strict = row >= col, row > col
    lane0 = (lax.broadcasted_iota(jnp.int32, (nb, c, LANES), 2) == 0).astype(BF16)
    gc_row = sum(_bdot(lane0, part, "nt") for part in _split3(gc))
    dm = jnp.exp(jnp.where(tril, gc[:, :, :c] - gc_row, -1e30))
    eg = jnp.exp(gc)
    gcl = gc[:, c - 1:c, :]
    ekg = jnp.exp(gcl - gc)
    egl = jnp.exp(gcl)
    qs = q * (GDN_DK ** -0.5)
    kb = k * beta
    kk = _bdot(kb, k, "nt")
    a = jnp.where(strict, kk * dm, 0.0)
    vb = v * beta
    kbg = kb * eg
    qk = _bdot(qs, k, "nt")
    p = jnp.where(tril, qk * dm, 0.0)
    out = dict(tril=tril, strict=strict, dm=dm, eg=eg, ekg=ekg, egl=egl, qs=qs, kb=kb, kk=kk, a=a,
               vb=vb, kbg=kbg, qk=qk, p=p, qg=qs * eg, kg=k * ekg)
    if t is None:
        y = -a
        t = (row == col).astype(F32) + y
        for _ in range(5):
            y = _bdot3(y, y)
            t = t + _bdot3(t, y)
        out.update(u=_bdot3(t, vb), w=_bdot3(t, kbg))
    out["t"] = t
    return out


def _gdn_stack(ref):
    return jnp.stack([ref[c * CHUNK:(c + 1) * CHUNK, h * LANES:(h + 1) * LANES]
                      for c in range(GDN_CB) for h in range(GDN_HEADS)])


def _gdn_unstack(x, ref):
    for c in range(GDN_CB):
        for h in range(GDN_HEADS):
            ref[c * CHUNK:(c + 1) * CHUNK, h * LANES:(h + 1) * LANES] = x[c * GDN_HEADS + h]


def _gdn_fwd(q, k, v, gc, beta, name):
    rows = q.shape[0]
    n_chunks = rows // CHUNK
    steps = n_chunks // GDN_CB
    blk = pl.BlockSpec((GDN_CB * CHUNK, GDN_W), lambda n: (n, 0))
    st = pl.BlockSpec((GDN_HEADS, GDN_CB, GDN_DK, LANES), lambda n: (0, n, 0, 0))
    tinv = pl.BlockSpec((GDN_CB, GDN_HEADS, CHUNK, CHUNK), lambda n: (n, 0, 0, 0))

    def body(q_ref, k_ref, v_ref, g_ref, b_ref, o_ref, st_ref, t_ref, w_ref, vn_ref, s_ref):
        @pl.when(pl.program_id(0) == 0)
        def _():
            s_ref[...] = jnp.zeros(s_ref.shape, F32)
        c = _gdn_chunk(*[_gdn_stack(r) for r in (q_ref, k_ref, v_ref, g_ref, b_ref)])
        _gdn_unstack(c["w"], w_ref)
        s = s_ref[...]
        for i in range(GDN_CB):
            hs = slice(i * GDN_HEADS, (i + 1) * GDN_HEADS)
            rs = slice(i * CHUNK, (i + 1) * CHUNK)
            st_ref[:, i] = s
            vn = c["u"][hs] - _bdot(c["w"][hs], s)
            o = _bdot(c["qg"][hs], s) + _bdot(c["p"][hs], vn)
            s = s * c["egl"][hs] + _bdot(c["kg"][hs], vn, "tn")
            t_ref[i] = c["t"][hs]
            for h in range(GDN_HEADS):
                o_ref[rs, h * LANES:(h + 1) * LANES] = o[h]
                vn_ref[rs, h * LANES:(h + 1) * LANES] = vn[h]
        s_ref[...] = s

    f = jax.ShapeDtypeStruct((rows, GDN_W), F32)
    return pl.pallas_call(
        body, name=name, grid=(steps,), in_specs=[blk] * 5, out_specs=[blk, st, tinv, blk, blk],
        out_shape=[f, jax.ShapeDtypeStruct((GDN_HEADS, n_chunks, GDN_DK, LANES), F32),
                   jax.ShapeDtypeStruct((n_chunks, GDN_HEADS, CHUNK, CHUNK), F32), f, f],
        scratch_shapes=[pltpu.VMEM((GDN_HEADS, GDN_DK, LANES), F32)],
        compiler_params=_params(("arbitrary",)))(q, k, v, gc, beta)


def _gdn_bwd(q, k, v, gc, beta, states, tinv, w, vn, do, name):
    rows = q.shape[0]
    n_chunks = rows // CHUNK
    steps = n_chunks // GDN_CB
    blk = pl.BlockSpec((GDN_CB * CHUNK, GDN_W), lambda n: (steps - 1 - n, 0))
    st = pl.BlockSpec((GDN_HEADS, GDN_CB, GDN_DK, LANES), lambda n: (0, steps - 1 - n, 0, 0))
    ti = pl.BlockSpec((GDN_CB, GDN_HEADS, CHUNK, CHUNK), lambda n: (steps - 1 - n, 0, 0, 0))
    nbatch = GDN_CB * GDN_HEADS

    def lanesum(x):
        return jnp.broadcast_to(jnp.sum(x, axis=-1, keepdims=True), x.shape)

    def body(q_ref, k_ref, v_ref, g_ref, b_ref, st_ref, t_ref, w_ref, vn_ref, do_ref,
             dq_ref, dk_ref, dv_ref, dg_ref, db_ref, ds_ref):
        @pl.when(pl.program_id(0) == 0)
        def _():
            ds_ref[...] = jnp.zeros(ds_ref.shape, F32)
        q, k, v, gc, beta, w, vn, do = [_gdn_stack(r) for r in (q_ref, k_ref, v_ref, g_ref, b_ref, w_ref, vn_ref, do_ref)]
        t = t_ref[...].reshape(nbatch, CHUNK, CHUNK)
        s = jnp.stack([st_ref[h, i] for i in range(GDN_CB) for h in range(GDN_HEADS)])
        c = _gdn_chunk(q, k, v, gc, beta, t)
        tril, strict, dm = c["tril"], c["strict"], c["dm"]
        dsn = ds_ref[...]
        dvn_c, dkg_c, dgl_c = [None] * GDN_CB, [None] * GDN_CB, [None] * GDN_CB
        for i in reversed(range(GDN_CB)):
            hs = slice(i * GDN_HEADS, (i + 1) * GDN_HEADS)
            dvn_c[i] = _bdot(c["p"][hs], do[hs], "tn") + _bdot(c["kg"][hs], dsn)
            dkg_c[i] = _bdot(vn[hs], dsn, "nt")
            dgl_c[i] = jnp.sum(jnp.sum(s[hs] * dsn, axis=2, keepdims=True), axis=1, keepdims=True) * c["egl"][hs]
            dsn = dsn * c["egl"][hs] + _bdot(c["qg"][hs], do[hs], "tn") - _bdot(w[hs], dvn_c[i], "tn")
        ds_ref[...] = dsn
        dvn, dkg, dgl = jnp.concatenate(dvn_c), jnp.concatenate(dkg_c), jnp.concatenate(dgl_c)
        dp = jnp.where(tril, _bdot(do, vn, "nt"), 0.0)
        dqg = _bdot(do, s, "nt")
        dw = -_bdot(dvn, s, "nt")
        dvb = _bdot3(t, dvn, "tn")
        dkbg = _bdot3(t, dw, "tn")
        dt = _bdot(dvn, c["vb"], "nt") + _bdot(dw, c["kbg"], "nt")
        da = jnp.where(strict, -_bdot3(_bdot3(t, dt, "tn"), t, "nt"), 0.0)
        dkk = da * dm
        dqk = dp * dm
        dkb = _bdot(dkk, k) + dkbg * c["eg"]
        dk = _bdot(dkk, c["kb"], "tn") + _bdot(dqk, c["qs"], "tn") + dkg * c["ekg"] + dkb * beta
        dqs = _bdot(dqk, k) + dqg * c["eg"]
        e = da * c["a"] + dp * c["p"]
        ones = jnp.ones((nbatch, CHUNK, LANES), BF16)
        col_sums = sum(_bdot(part, ones, "tn") for part in _split3(e))
        kg_term = lanesum(dkg * c["kg"])
        dgc = (jnp.broadcast_to(jnp.sum(e, axis=-1, keepdims=True), (nbatch, CHUNK, LANES)) - col_sums
               + lanesum(dqg * c["qg"]) - kg_term + lanesum(dkbg * c["kbg"]))
        dgcl = jnp.sum(kg_term, axis=1, keepdims=True) + dgl
        last = lax.broadcasted_iota(jnp.int32, (CHUNK, LANES), 0) == CHUNK - 1
        _gdn_unstack(dqs * (GDN_DK ** -0.5), dq_ref)
        _gdn_unstack(dk, dk_ref)
        _gdn_unstack(dvb * beta, dv_ref)
        _gdn_unstack(dgc + jnp.where(last, dgcl, 0.0), dg_ref)
        _gdn_unstack(lanesum(dvb * v) + lanesum(dkb * k), db_ref)

    return pl.pallas_call(
        body, name=name, grid=(steps,), in_specs=[blk] * 5 + [st, ti, blk, blk, blk], out_specs=[blk] * 5,
        out_shape=[jax.ShapeDtypeStruct((rows, GDN_W), F32)] * 5,
        scratch_shapes=[pltpu.VMEM((GDN_HEADS, GDN_DK, LANES), F32)],
        compiler_params=_params(("arbitrary",)))(q, k, v, gc, beta, states, tinv, w, vn, do)


def _swa_masks(first):
    r = lax.broadcasted_iota(jnp.int32, (SWA_BLOCK, 2 * SWA_BLOCK), 0)
    c = lax.broadcasted_iota(jnp.int32, (SWA_BLOCK, 2 * SWA_BLOCK), 1)
    band = (c > r) & (c <= r + SWA_BLOCK)
    return band & (jnp.logical_not(first) | (c >= SWA_BLOCK))


def _swa_stack(ref, j):
    lane = lax.broadcasted_iota(jnp.int32, (1, LANES), 1)
    parts = []
    for g in range(SWA_GROUP):
        ch = j * (SWA_GROUP // 2) + g // 2
        keep = (lane < SWA_DH) if g % 2 == 0 else (lane >= SWA_DH)
        parts.append(ref[:, ch * LANES:(ch + 1) * LANES] * keep.astype(ref.dtype))
    return jnp.concatenate(parts, axis=0)


def _swa_unstack(x2, j, out_ref):
    low = lax.broadcasted_iota(jnp.int32, (SWA_BLOCK, LANES), 1) < SWA_DH
    for c3 in range(SWA_GROUP // 2):
        even = x2[(2 * c3) * SWA_BLOCK:(2 * c3 + 1) * SWA_BLOCK]
        odd = x2[(2 * c3 + 1) * SWA_BLOCK:(2 * c3 + 2) * SWA_BLOCK]
        ch = j * (SWA_GROUP // 2) + c3
        out_ref[:, ch * LANES:(ch + 1) * LANES] = jnp.where(low, even, odd).astype(out_ref.dtype)


def _swa_probs(s, sink, mask):
    s = jnp.where(mask, s, -1e30)
    m = jnp.maximum(jnp.max(s, axis=-1, keepdims=True), sink)
    p = jnp.where(mask, jnp.exp(s - m), 0.0)
    es = jnp.exp(sink - m)
    inv = 1.0 / (jnp.sum(p, axis=-1, keepdims=True) + es)
    return p * inv, es * inv


def _swa_scores(q_ref, kc_ref, kp_ref, sink_ref, j, mask):
    sl = slice(j * LANES, (j + 1) * LANES)
    qst = _swa_stack(q_ref, j)
    kw = jnp.concatenate([kp_ref[:, sl], kc_ref[:, sl]], axis=0)
    s = _dot(qst, kw, NT) * (SWA_DH ** -0.5)
    ps = [_swa_probs(s[g * SWA_BLOCK:(g + 1) * SWA_BLOCK], sink_ref[j * SWA_GROUP + g], mask)
          for g in range(SWA_GROUP)]
    return qst, kw, ps


def _swa_fwd(q, k2, v2, sinks, name):
    rows = q.shape[0]
    nb = rows // SWA_BLOCK
    w = SWA_HEADS * SWA_DH
    kvw = SWA_KV_HEADS * LANES
    cur = pl.BlockSpec((SWA_BLOCK, w), lambda i: (i, 0))
    kcur = pl.BlockSpec((SWA_BLOCK, kvw), lambda i: (i, 0))
    kprev = pl.BlockSpec((SWA_BLOCK, kvw), lambda i: (jnp.maximum(i - 1, 0), 0))

    def body(sink_ref, q_ref, kc_ref, kp_ref, vc_ref, vp_ref, o_ref):
        mask = _swa_masks(pl.program_id(0) == 0)
        for j in range(SWA_KV_HEADS):
            sl = slice(j * LANES, (j + 1) * LANES)
            _, _, ps = _swa_scores(q_ref, kc_ref, kp_ref, sink_ref, j, mask)
            vw = jnp.concatenate([vp_ref[:, sl], vc_ref[:, sl]], axis=0)
            pst = jnp.concatenate([p.astype(BF16) for p, _ in ps], axis=0)
            _swa_unstack(_dot(pst, vw), j, o_ref)

    return pl.pallas_call(
        body, name=name, grid=(nb,),
        in_specs=[pl.BlockSpec(memory_space=pltpu.SMEM), cur, kcur, kprev, kcur, kprev], out_specs=cur,
        out_shape=jax.ShapeDtypeStruct((rows, w), BF16),
        compiler_params=_params(("arbitrary",)))(sinks, q, k2, k2, v2, v2)


def _swa_bwd(q, k2, v2, sinks, dcat, name):
    rows = q.shape[0]
    nb = rows // SWA_BLOCK
    w = SWA_HEADS * SWA_DH
    kvw = SWA_KV_HEADS * LANES
    cur = pl.BlockSpec((SWA_BLOCK, w), lambda i: (jnp.minimum(i, nb - 1), 0))
    kcur = pl.BlockSpec((SWA_BLOCK, kvw), lambda i: (jnp.minimum(i, nb - 1), 0))
    kprev = pl.BlockSpec((SWA_BLOCK, kvw), lambda i: (jnp.clip(i - 1, 0, nb - 1), 0))
    late = pl.BlockSpec((SWA_BLOCK, kvw), lambda i: (jnp.maximum(i - 1, 0), 0))
    acc_spec = pl.BlockSpec((SUBLANES, LANES), lambda i: (0, 0))

    def body(sink_ref, q_ref, kc_ref, kp_ref, vc_ref, vp_ref, do_ref, dq_ref, dk_ref, dv_ref, dsk_ref,
             ck_ref, cv_ref):
        i = pl.program_id(0)

        @pl.when(i == 0)
        def _():
            ck_ref[...] = jnp.zeros(ck_ref.shape, F32)
            cv_ref[...] = jnp.zeros(cv_ref.shape, F32)
            dsk_ref[...] = jnp.zeros(dsk_ref.shape, F32)

        @pl.when(i == nb)
        def _():
            dk_ref[...] = ck_ref[...]
            dv_ref[...] = cv_ref[...]

        @pl.when(i < nb)
        def _():
            mask = _swa_masks(i == 0)
            lane = lax.broadcasted_iota(jnp.int32, (SUBLANES, LANES), 1)
            dsk = jnp.zeros((SUBLANES, LANES), F32)
            for j in range(SWA_KV_HEADS):
                sl = slice(j * LANES, (j + 1) * LANES)
                qst, kw, ps = _swa_scores(q_ref, kc_ref, kp_ref, sink_ref, j, mask)
                vw = jnp.concatenate([vp_ref[:, sl], vc_ref[:, sl]], axis=0)
                dost = _swa_stack(do_ref, j)
                dpr = _dot(dost, vw, NT)
                dss = []
                for g in range(SWA_GROUP):
                    p, sink_p = ps[g]
                    dpg = dpr[g * SWA_BLOCK:(g + 1) * SWA_BLOCK]
                    delta = jnp.sum(p * dpg, axis=-1, keepdims=True)
                    dss.append((p * (dpg - delta)).astype(BF16))
                    dsg = jnp.sum(-sink_p * delta, axis=0, keepdims=True)
                    dsk = dsk + jnp.where(lane == j * SWA_GROUP + g, dsg, 0.0)
                dsst = jnp.concatenate(dss, axis=0)
                pst = jnp.concatenate([p.astype(BF16) for p, _ in ps], axis=0)
                _swa_unstack(_dot(dsst, kw) * (SWA_DH ** -0.5), j, dq_ref)
                dk = _dot(dsst, qst, TN) * (SWA_DH ** -0.5)
                dv = _dot(pst, dost, TN)
                dk = dk + pltpu.roll(dk, SWA_DH, 1)
                dv = dv + pltpu.roll(dv, SWA_DH, 1)
                dk_ref[:, sl] = ck_ref[:, sl] + dk[:SWA_BLOCK]
                dv_ref[:, sl] = cv_ref[:, sl] + dv[:SWA_BLOCK]
                ck_ref[:, sl] = dk[SWA_BLOCK:]
                cv_ref[:, sl] = dv[SWA_BLOCK:]
            dsk_ref[...] += dsk

    f = jax.ShapeDtypeStruct((rows, kvw), F32)
    return pl.pallas_call(
        body, name=name, grid=(nb + 1,),
        in_specs=[pl.BlockSpec(memory_space=pltpu.SMEM), cur, kcur, kprev, kcur, kprev, cur],
        out_specs=[cur, late, late, acc_spec],
        out_shape=[jax.ShapeDtypeStruct((rows, w), F32), f, f, jax.ShapeDtypeStruct((SUBLANES, LANES), F32)],
        scratch_shapes=[pltpu.VMEM((SWA_BLOCK, kvw), F32), pltpu.VMEM((SWA_BLOCK, kvw), F32)],
        compiler_params=_params(("arbitrary",)))(sinks, q, k2, k2, v2, v2, dcat)


def _mem_probs(mq, kbd):
    s = _dot(mq.astype(BF16), kbd) * (MEM_DH ** -0.5)
    ps = []
    for h in range(MEM_HEADS):
        sh = s[:, h * MEM_LEN:(h + 1) * MEM_LEN]
        e = jnp.exp(sh - jnp.max(sh, axis=-1, keepdims=True))
        ps.append(e / jnp.sum(e, axis=-1, keepdims=True))
    return ps


def _mem_fwd(proj, cb, kbd, vbd, name, bm=512):
    def fn(i, mq, kbd, vbd):
        p = jnp.concatenate(_mem_probs(mq, kbd), axis=1)
        return _dot(p.astype(BF16), vbd)
    return _rowcall(fn, name, proj.shape[0], bm, [_row(proj, MEM_W, cb), _full(kbd), _full(vbd)], [(MEM_W, BF16)])[0]


def _mem_bwd(proj, cb, kbd, vbd, dcat, name, bm=512):
    def fn(i, mq, kbd, vbd, do):
        ps = _mem_probs(mq, kbd)
        dp = _dot(do, vbd, NT)
        dss = []
        for h in range(MEM_HEADS):
            dph = dp[:, h * MEM_LEN:(h + 1) * MEM_LEN]
            dss.append(ps[h] * (dph - jnp.sum(ps[h] * dph, axis=-1, keepdims=True)))
        ds = (jnp.concatenate(dss, axis=1) * (MEM_DH ** -0.5)).astype(BF16)
        p = jnp.concatenate(ps, axis=1).astype(BF16)
        return _dot(ds, kbd, NT), _dot(mq.astype(BF16), ds, TN), _dot(p, do, TN)
    return _rowcall(fn, name, proj.shape[0], bm, [_row(proj, MEM_W, cb), _full(kbd), _full(vbd), _row(dcat, MEM_W, 3)],
                    [(MEM_W, BF16)], [(MEM_W, MEM_HEADS * MEM_LEN), (MEM_HEADS * MEM_LEN, MEM_W)])


def _mem_expand(mkv):
    feat_head = jnp.arange(MEM_W) // MEM_DH
    slot_head = jnp.arange(MEM_HEADS * MEM_LEN) // MEM_LEN
    on = feat_head[:, None] == slot_head[None, :]
    kbd = jnp.where(on, jnp.tile(mkv[:, :MEM_W].T, (1, MEM_HEADS)), 0.0)
    vbd = jnp.where(on.T, jnp.tile(mkv[:, MEM_W:], (MEM_HEADS, 1)), 0.0)
    return kbd.astype(BF16), vbd.astype(BF16)


def _mem_collapse(dkbd, dvbd):
    dk = [dkbd[h * MEM_DH:(h + 1) * MEM_DH, h * MEM_LEN:(h + 1) * MEM_LEN].T for h in range(MEM_HEADS)]
    dv = [dvbd[h * MEM_LEN:(h + 1) * MEM_LEN, h * MEM_DH:(h + 1) * MEM_DH] for h in range(MEM_HEADS)]
    return jnp.concatenate(dk + dv, axis=1)


def _adamw_math(w, g, m, v):
    m = ADAM_B1 * m + (1.0 - ADAM_B1) * g
    v = ADAM_B2 * v + (1.0 - ADAM_B2) * (g * g)
    m_hat = m / (1.0 - ADAM_B1 ** ADAM_STEP)
    v_hat = v / (1.0 - ADAM_B2 ** ADAM_STEP)
    return -ADAM_LR * (m_hat / (jnp.sqrt(v_hat) + ADAM_EPS) + ADAM_WD * w), m, v


def _adamw(w, g, m, v, name, bm=512):
    d = w.shape[1]
    return _rowcall(lambda i, *a: _adamw_math(*a), name, w.shape[0], bm, [_row(w), _row(g), _row(m), _row(v)], [(d, F32)] * 3)


def _adamw_halves(w, g_own, g_other, my_core, m, v, name):
    layers, rows, n = w.shape
    r = rows // 2
    bm = LANES if r % LANES == 0 else r
    per_half = r // bm
    nat = pl.BlockSpec((None, bm, n), lambda l, c, i: (l, c * per_half + i, 0))
    own = pl.BlockSpec((None, bm, n), lambda l, c, i: (l, i, 0))
    other = pl.BlockSpec((None, None, bm, n), lambda l, c, i: (0, l, i, 0))

    def body(core_ref, w_ref, own_ref, other_ref, m_ref, v_ref, go_ref, d_ref, mo_ref, vo_ref):
        g = jnp.where(core_ref[0] == pl.program_id(1), own_ref[...], other_ref[...])
        go_ref[...] = g
        d_ref[...], mo_ref[...], vo_ref[...] = _adamw_math(w_ref[...], g, m_ref[...], v_ref[...])

    return pl.pallas_call(
        body, name=name, grid=(layers, 2, per_half),
        in_specs=[pl.BlockSpec(memory_space=pltpu.SMEM), nat, own, other, nat, nat], out_specs=[nat] * 4,
        out_shape=[jax.ShapeDtypeStruct(w.shape, F32)] * 4,
        compiler_params=_params(("parallel", "parallel", "parallel")))(my_core, w, g_own, g_other, m, v)


def _sum_slots(buf, name, bm=128):
    n, rows, w = buf.shape
    bm = min(bm, rows)
    assert rows % bm == 0

    def body(b_ref, o_ref):
        acc = b_ref[0].astype(F32)
        for s in range(1, n):
            acc = acc + b_ref[s].astype(F32)
        o_ref[...] = acc

    return pl.pallas_call(
        body, name=name, grid=(rows // bm,), in_specs=[pl.BlockSpec((n, bm, w), lambda i: (0, i, 0))],
        out_specs=pl.BlockSpec((bm, w), lambda i: (i, 0)), out_shape=jax.ShapeDtypeStruct((rows, w), F32),
        compiler_params=_params(("parallel",)))(buf)


def _exchange(srcs, same, masks, name, keep_own=True):
    slots = N_DEV if len(masks) == N_DEV - 1 else (2 if keep_own else 1)
    n_arr, n_peer = len(srcs), len(masks)
    shapes = [s.shape if sm else s.shape[1:] for s, sm in zip(srcs, same)]

    def body(*refs):
        src_refs, out_refs = refs[:n_arr], refs[n_arr:2 * n_arr]
        send_sems, recv_sems, local_sems = refs[2 * n_arr:]
        x, y, c = lax.axis_index("x"), lax.axis_index("y"), lax.axis_index("c")
        me = 4 * x + 2 * y + c

        def flip(v, bit):
            return 1 - v if bit else v

        def slot_of(dev):
            return dev if slots == N_DEV else (dev % 2 if slots == 2 else 0)

        def piece(a, p):
            return src_refs[a] if same[a] else src_refs[a].at[p]

        local = []
        if keep_own:
            local = [pltpu.make_async_copy(piece(a, me), out_refs[a].at[slot_of(me)], local_sems.at[a])
                     for a in range(n_arr)]
        for cp in local:
            cp.start()
        copies = []
        for idx, k in enumerate(masks):
            peer = (flip(x, k & 4), flip(y, k & 2), flip(c, k & 1))
            peer_id = 4 * peer[0] + 2 * peer[1] + peer[2]
            for a in range(n_arr):
                sem = idx * n_arr + a
                cp = pltpu.make_async_remote_copy(
                    src_ref=piece(a, peer_id), dst_ref=out_refs[a].at[slot_of(me)],
                    send_sem=send_sems.at[sem], recv_sem=recv_sems.at[sem], device_id=peer, device_id_type=MESH)
                cp.start()
                copies.append((cp, pltpu.make_async_remote_copy(
                    src_ref=piece(a, peer_id), dst_ref=out_refs[a].at[slot_of(peer_id)],
                    send_sem=send_sems.at[sem], recv_sem=recv_sems.at[sem], device_id=peer, device_id_type=MESH)))
        for cp, landing in copies:
            cp.wait_send()
            landing.wait_recv()
        for cp in local:
            cp.wait()

    any_spec = pl.BlockSpec(memory_space=pl.ANY)
    n_sem = n_arr * n_peer
    return pl.pallas_call(
        body, name=name, in_specs=[any_spec] * n_arr, out_specs=[any_spec] * n_arr,
        out_shape=[jax.ShapeDtypeStruct((slots,) + tuple(sh), s.dtype) for sh, s in zip(shapes, srcs)],
        scratch_shapes=[pltpu.SemaphoreType.DMA((n_sem,)), pltpu.SemaphoreType.DMA((n_sem,)),
                        pltpu.SemaphoreType.DMA((n_arr,))],
        )(*srcs)


ALL_PEERS = tuple(range(1, N_DEV))
SIBLING = (1,)


def _send_to_sibling(srcs, name):
    n_arr = len(srcs)

    def body(*refs):
        src_refs, out_refs = refs[:n_arr], refs[n_arr:2 * n_arr]
        send_sems, recv_sems = refs[2 * n_arr:]
        x, y, c = lax.axis_index("x"), lax.axis_index("y"), lax.axis_index("c")
        copies = [pltpu.make_async_remote_copy(
            src_ref=src_refs[a].at[1 - c], dst_ref=out_refs[a], send_sem=send_sems.at[a], recv_sem=recv_sems.at[a],
            device_id=(x, y, 1 - c), device_id_type=MESH) for a in range(n_arr)]
        for cp in copies:
            cp.start()
        for cp in copies:
            cp.wait()

    any_spec = pl.BlockSpec(memory_space=pl.ANY)
    return pl.pallas_call(
        body, name=name, in_specs=[any_spec] * n_arr, out_specs=[any_spec] * n_arr,
        out_shape=[jax.ShapeDtypeStruct(s.shape[1:], s.dtype) for s in srcs],
        scratch_shapes=[pltpu.SemaphoreType.DMA((n_arr,)), pltpu.SemaphoreType.DMA((n_arr,))],
        )(*srcs)


def _add_core_parts(mine, theirs, my_core, name, bm=128):
    _, chips, rows, n = mine.shape
    bm = min(bm, rows)
    assert rows % bm == 0

    def body(core_ref, p0_ref, p1_ref, t_ref, o_ref):
        own = jnp.where(core_ref[0] == 0, p0_ref[...], p1_ref[...])
        o_ref[...] = (own.astype(F32) + t_ref[...].astype(F32)).astype(o_ref.dtype)

    part = lambda k: pl.BlockSpec((None, None, bm, n), lambda s, i, k=k: (k, s, i, 0))
    flat = pl.BlockSpec((None, bm, n), lambda s, i: (s, i, 0))
    return pl.pallas_call(
        body, name=name, grid=(chips, rows // bm),
        in_specs=[pl.BlockSpec(memory_space=pltpu.SMEM), part(0), part(1), flat], out_specs=flat,
        out_shape=jax.ShapeDtypeStruct(theirs.shape, mine.dtype),
        compiler_params=_params(("parallel", "parallel")))(my_core, mine, mine, theirs)


def _scatter_by_chip(srcs, small, name):
    n_arr = len(srcs)

    def body(*refs):
        src_refs, small_ref = refs[:n_arr], refs[n_arr]
        out_refs, small_out = refs[n_arr + 1:2 * n_arr + 1], refs[2 * n_arr + 1]
        send_sems, recv_sems, local_sems = refs[2 * n_arr + 2:]
        x, y, c = lax.axis_index("x"), lax.axis_index("y"), lax.axis_index("c")
        my_chip, me = 2 * x + y, 4 * x + 2 * y + c
        chips = [(1 - x, y), (x, 1 - y), (1 - x, 1 - y)]
        local = [pltpu.make_async_copy(src_refs[a].at[my_chip], out_refs[a].at[my_chip], local_sems.at[a])
                 for a in range(n_arr)]
        local.append(pltpu.make_async_copy(small_ref, small_out.at[me], local_sems.at[n_arr]))
        for cp in local:
            cp.start()
        copies = []
        for j, (px, py) in enumerate(chips):
            their_chip = 2 * px + py
            for a in range(n_arr):
                sem = j * n_arr + a
                cp = pltpu.make_async_remote_copy(
                    src_ref=src_refs[a].at[their_chip], dst_ref=out_refs[a].at[my_chip],
                    send_sem=send_sems.at[sem], recv_sem=recv_sems.at[sem], device_id=(px, py, c), device_id_type=MESH)
                cp.start()
                copies.append((cp, pltpu.make_async_remote_copy(
                    src_ref=src_refs[a].at[their_chip], dst_ref=out_refs[a].at[their_chip],
                    send_sem=send_sems.at[sem], recv_sem=recv_sems.at[sem], device_id=(px, py, c), device_id_type=MESH)))
        for idx, k in enumerate(ALL_PEERS):
            peer = (1 - x if k & 4 else x, 1 - y if k & 2 else y, 1 - c if k & 1 else c)
            peer_id = 4 * peer[0] + 2 * peer[1] + peer[2]
            sem = 3 * n_arr + idx
            cp = pltpu.make_async_remote_copy(
                src_ref=small_ref, dst_ref=small_out.at[me], send_sem=send_sems.at[sem], recv_sem=recv_sems.at[sem],
                device_id=peer, device_id_type=MESH)
            cp.start()
            copies.append((cp, pltpu.make_async_remote_copy(
                src_ref=small_ref, dst_ref=small_out.at[peer_id], send_sem=send_sems.at[sem], recv_sem=recv_sems.at[sem],
                device_id=peer, device_id_type=MESH)))
        for cp, landing in copies:
            cp.wait_send()
            landing.wait_recv()
        for cp in local:
            cp.wait()

    any_spec = pl.BlockSpec(memory_space=pl.ANY)
    n_sem = 3 * n_arr + len(ALL_PEERS)
    return pl.pallas_call(
        body, name=name, in_specs=[any_spec] * (n_arr + 1), out_specs=[any_spec] * (n_arr + 1),
        out_shape=[jax.ShapeDtypeStruct(s.shape, s.dtype) for s in srcs]
        + [jax.ShapeDtypeStruct((N_DEV,) + small.shape, small.dtype)],
        scratch_shapes=[pltpu.SemaphoreType.DMA((n_sem,)), pltpu.SemaphoreType.DMA((n_sem,)),
                        pltpu.SemaphoreType.DMA((n_arr + 1,))],
        )(*srcs, small)


def _gather_two_level(srcs, name):
    n_arr = len(srcs)

    def body(*refs):
        src_refs, out_refs = refs[:n_arr], refs[n_arr:2 * n_arr]
        send_sems, recv_sems, local_sems = refs[2 * n_arr:]
        x, y, c = lax.axis_index("x"), lax.axis_index("y"), lax.axis_index("c")
        sibling = (x, y, 1 - c)
        chips = [(1 - x, y), (x, 1 - y), (1 - x, 1 - y)]

        def slot(px, py, pc):
            return 4 * px + 2 * py + pc

        def copy(a, k, block, to, own=False):
            return pltpu.make_async_remote_copy(
                src_ref=src_refs[a] if own else out_refs[a].at[slot(*block)], dst_ref=out_refs[a].at[slot(*block)],
                send_sem=send_sems.at[a * 7 + k], recv_sem=recv_sems.at[a * 7 + k], device_id=to, device_id_type=MESH)

        local = [pltpu.make_async_copy(src_refs[a], out_refs[a].at[slot(x, y, c)], local_sems.at[a]) for a in range(n_arr)]
        for cp in local:
            cp.start()
        started = []
        for a in range(n_arr):
            started.append(copy(a, 0, (x, y, c), sibling, own=True))
            started += [copy(a, 1 + j, (x, y, c), (*chip, c), own=True) for j, chip in enumerate(chips)]
        for cp in started:
            cp.start()
        for j, chip in enumerate(chips):
            for a in range(n_arr):
                copy(a, 1 + j, (*chip, c), (x, y, c)).wait_recv()
                passed = copy(a, 4 + j, (*chip, c), sibling)
                passed.start()
                started.append(passed)
        for a in range(n_arr):
            copy(a, 0, sibling, (x, y, c)).wait_recv()
            for j, chip in enumerate(chips):
                copy(a, 4 + j, (*chip, 1 - c), (x, y, c)).wait_recv()
        for cp in started:
            cp.wait_send()
        for cp in local:
            cp.wait()

    any_spec = pl.BlockSpec(memory_space=pl.ANY)
    return pl.pallas_call(
        body, name=name, in_specs=[any_spec] * n_arr, out_specs=[any_spec] * n_arr,
        out_shape=[jax.ShapeDtypeStruct((N_DEV,) + s.shape, s.dtype) for s in srcs],
        scratch_shapes=[pltpu.SemaphoreType.DMA((7 * n_arr,)), pltpu.SemaphoreType.DMA((7 * n_arr,)),
                        pltpu.SemaphoreType.DMA((n_arr,))],
        )(*srcs)


def _pack(arrays, rows):
    flat = jnp.concatenate([a.reshape(-1) for a in arrays])
    return jnp.pad(flat, (0, rows * D_MODEL - flat.shape[0])).reshape(rows, D_MODEL)


def _unpack(buf, shapes):
    flat = buf.reshape(-1)
    out, off = [], 0
    for s in shapes:
        n = math.prod(s)
        out.append(flat[off:off + n].reshape(s))
        off += n
    return out


def _rows_for(shapes, mult):
    n = sum(math.prod(s) for s in shapes)
    rows = -(-n // D_MODEL)
    return -(-rows // mult) * mult


SHARD_AXIS = dict(w_mem_kv=1, w_out=1, w_gate_up=2, w_down=1, gdn_w_in=2, swa_w_q=1, w_kv=0, gdn_conv=2)
HALF_AXIS = dict(w_mem_kv=1, w_out=1, w_gate_up=1, w_down=1, gdn_w_in=1, swa_w_q=1, w_kv=0)


def _my_half(shard, name, c):
    ax = HALF_AXIS[name]
    h = shard.shape[ax] // 2
    return lax.dynamic_slice_in_dim(shard, c * h, h, axis=ax)


def _piece_layout(name, half_shape):
    dims, pos = [], {}
    for i, d in enumerate(half_shape):
        if i == SHARD_AXIS[name]:
            pos["chip"] = len(dims)
            dims.append(4)
        if i == HALF_AXIS[name]:
            pos["core"] = len(dims)
            dims.append(2)
        pos[i] = len(dims)
        dims.append(d)
    return dims, [pos["chip"], pos["core"]] + [pos[i] for i in range(len(half_shape))]


def _full_shape(name, half_shape):
    return tuple(d * (4 if i == SHARD_AXIS[name] else 1) * (2 if i == HALF_AXIS[name] else 1)
                 for i, d in enumerate(half_shape))


def _assemble(pieces, name):
    half_shape = pieces.shape[1:]
    if half_shape[-1] % LANES:
        chips = [jnp.concatenate([pieces[2 * s], pieces[2 * s + 1]], axis=HALF_AXIS[name]) for s in range(4)]
        return jnp.concatenate(chips, axis=SHARD_AXIS[name])
    dims, perm = _piece_layout(name, half_shape)
    inverse = [perm.index(i) for i in range(len(perm))]
    return pieces.reshape((4, 2) + half_shape).transpose(inverse).reshape(_full_shape(name, half_shape))


def _to_pieces(full, name, half_shape):
    if half_shape[-1] % LANES:
        ns, nh = half_shape[SHARD_AXIS[name]], half_shape[HALF_AXIS[name]]
        parts = [lax.slice_in_dim(lax.slice_in_dim(full, s * ns, (s + 1) * ns, axis=SHARD_AXIS[name]),
                                  c * nh, (c + 1) * nh, axis=HALF_AXIS[name])
                 for c in range(2) for s in range(4)]
        return jnp.stack(parts).reshape((2, 4) + tuple(half_shape))
    dims, perm = _piece_layout(name, half_shape)
    return full.reshape(dims).transpose([perm[1], perm[0]] + perm[2:])


def _rope_tables(positions):
    half = ROT_DIM // 2
    inv = ROPE_THETA ** (-jnp.arange(0, ROT_DIM, 2, dtype=F32) / ROT_DIM)
    ang = positions.astype(F32)[:, None] * inv
    cos, sin = jnp.cos(ang), jnp.sin(ang)
    rows = positions.shape[0]
    one = jnp.ones((rows, SWA_DH - ROT_DIM), F32)
    zero = jnp.zeros((rows, SWA_DH - ROT_DIM), F32)
    zh = jnp.zeros((rows, half), F32)
    c64 = jnp.concatenate([cos, cos, one], axis=1)
    a64 = jnp.concatenate([-sin, zh, zero], axis=1)
    b64 = jnp.concatenate([zh, sin, zero], axis=1)
    return tuple(jnp.concatenate([t, t], axis=1) for t in (c64, a64, b64))


def _pair_heads(t):
    return jnp.concatenate([t[:, :SWA_DH], t[:, :SWA_DH], t[:, SWA_DH:], t[:, SWA_DH:]], axis=1)


def _unpair_heads(t):
    return jnp.concatenate([t[:, :SWA_DH], t[:, LANES:LANES + SWA_DH]], axis=1)


def _gdn_in_pad(w):
    o2 = 4 * GDN_W
    pad = jnp.zeros(w.shape[:-1] + (GDN_IN_PAD - GDN_IN,), w.dtype)
    return jnp.concatenate([w[..., :o2], w[..., o2 + 2 * GDN_HEADS:], w[..., o2:o2 + 2 * GDN_HEADS], pad], axis=-1)


def _gdn_in_unpad(w):
    o2 = 4 * GDN_W
    return jnp.concatenate([w[..., :o2], w[..., o2 + MEM_W:o2 + MEM_W + 2 * GDN_HEADS], w[..., o2:o2 + MEM_W]], axis=-1)


def _head_rows(v):
    return jnp.pad(v.astype(F32), (GDN_HEADS, LANES - 2 * GDN_HEADS))[None, :]


def _selectors():
    lane = jnp.arange(LANES)[:, None]
    head = (jnp.arange(GDN_W) // LANES)[None, :]
    return (lane == head).astype(BF16), (lane == head + GDN_HEADS).astype(BF16)


def _local_step(x, mem, positions, target, w):
    rows = x.shape[0]
    tabs = _rope_tables(positions)
    sel_b, sel_a = _selectors()
    row2 = lambda v: v.reshape(1, -1).astype(F32)

    w_gu = _ff_interleave(w["w_gate_up"], "interleave_gate_up")
    mem_n = _rms_fwd(mem, row2(w["ln_mem"]), "mem_norm")
    saved = []
    kt = vt = None
    for l in range(DEPTH):
        s = dict(x0=x)
        h = _rms_fwd(x, row2(w["ln_mix"][l]), f"norm_mix{l}") if l == 0 else h_next
        mkv =_mm(mem_n, w["w_mem_kv"], "nn", f"mem_kv{l}", layer=l)
        kbd, vbd = _mem_expand(mkv)
        if l < N_A:
            proj = _mm(h, w["gdn_w_in"], "nn", f"gdn_in{l}", layer=l)
            a_row, dt_row = _head_rows(w["gdn_A_log"][l]), _head_rows(w["gdn_dt_bias"][l])
            q, k, v, gc, beta = _gdn_pre_fwd(proj, w["gdn_conv"][l], sel_b, sel_a, a_row, dt_row, f"gdn_pre{l}")
            o, states, tinv, gw, vn = _gdn_fwd(q, k, v, gc, beta, f"gdn_scan{l}")
            mix = _gdn_post_fwd(o, proj, row2(w["gdn_norm"][l]), f"gdn_post{l}")
            mq_cb = (3 * GDN_W + GDN_W) // MEM_W
            s.update(q=q, k=k, v=v, gc=gc, beta=beta, o=o, states=states, tinv=tinv, gw=gw, vn=vn,
                     a_row=a_row, dt_row=dt_row)
        else:
            proj = _mm(h, w["swa_w_q"], "nn", f"swa_in{l}", layer=l - N_A)
            qr = _rope(proj, SWA_HEADS * SWA_DH, 0, tabs, 1, f"rope_q{l}", BF16)
            mix = _swa_fwd(qr, kt, vt, w["swa_sinks"][l - N_A], f"swa{l}")
            mq_cb = (SWA_HEADS * SWA_DH) // MEM_W
            s.update(qr=qr)
        mem_o = _mem_fwd(proj, mq_cb, kbd, vbd, f"mem_attn{l}")
        cat = jnp.concatenate([mix, mem_o], axis=1)
        x1, h2 = _mm(cat, w["w_out"], "nn", f"out_proj{l}", add=x, layer=l, norm_g=row2(w["ln_ffn"][l]))
        gu, act = _gate_up_fwd(h2, w_gu, l, f"gate_up{l}")
        if l + 1 < DEPTH:
            x, h_next = _mm(act, w["w_down"], "nn", f"down{l}", add=x1, layer=l, norm_g=row2(w["ln_mix"][l + 1]))
        else:
            x = _mm(act, w["w_down"], "nn", f"down{l}", add=x1, layer=l)
        s.update(h=h, proj=proj, kbd=kbd, vbd=vbd, mq_cb=mq_cb, cat=cat, x1=x1, h2=h2, gu=gu, act=act)
        saved.append(s)
        if l == N_A - 1:
            x_kv = x
            h_kv = _rms_fwd(x, row2(w["ln_kv"]), "norm_kv")
            kv = _mm(h_kv, w["w_kv"], "nn", "kv_proj")
            kr = _rope(kv, LANES, 0, tabs, 1, "rope_k", F32)
            kt = _pair_heads(kr).astype(BF16)
            vt = _pair_heads(kv[:, LANES:]).astype(BF16)

    gr = {}
    dx, dxb, loss_part, dlnf = _final_loss(x, row2(w["ln_final"]), target, "final_loss")
    gr["ln_final"] = dlnf.sum(axis=0)
    dln_mix, dln_ffn = [None] * DEPTH, [None] * DEPTH
    dw_mem_kv, dw_out, dw_gu, dw_dn, dgdn_in, dswa_q = [
        lax.empty(w[n].shape, BF16) for n in ("w_mem_kv", "w_out", "w_gate_up", "w_down", "gdn_w_in", "swa_w_q")]
    dgdn_conv, dgdn_a, dgdn_dt, dgdn_norm = [None] * N_A, [None] * N_A, [None] * N_A, [None] * N_A
    dswa_sinks = [None] * N_B
    dmem_n = None
    dkt = dvt = None
    for l in reversed(range(DEPTH)):
        s = saved[l]
        if l == N_A - 1:
            dkr = _unpair_heads(dkt)
            dk = _rope(dkr, LANES, 0, tabs, -1, "rope_k_bwd", BF16)
            dkv = jnp.concatenate([dk, _unpair_heads(dvt).astype(BF16)], axis=1)
            dh_kv = _mm(dkv, w["w_kv"], "nt", "kv_proj_dx", BF16)
            gr["w_kv"] = _mm(h_kv, dkv, "tn", "kv_proj_dw", BF16)
            dx, dxb, dg = _rms_bwd(x_kv, row2(w["ln_kv"]), dh_kv, dx, "norm_kv_bwd")
            gr["ln_kv"] = dg.sum(axis=0)
        dgu = _down_bwd(dxb, w["w_down"], l, s["gu"], f"down_dx{l}")
        dw_dn = _mm(s["act"], dxb, "tn", f"down_dw{l}", dst=dw_dn, layer=l)
        dh2 = _mm(dgu, w_gu, "nt", f"gate_up_dx{l}", BF16, layer=l)
        dw_gu = _mm(s["h2"], dgu, "tn", f"gate_up_dw{l}", dst=dw_gu, layer=l)
        dx, dxb, dg = _rms_bwd(s["x1"], row2(w["ln_ffn"][l]), dh2, dx, f"norm_ffn_bwd{l}")
        dln_ffn[l] = dg.sum(axis=0)
        dcat = _mm(dxb, w["w_out"], "nt", f"out_proj_dx{l}", BF16, layer=l)
        dw_out = _mm(s["cat"], dxb, "tn", f"out_proj_dw{l}", dst=dw_out, layer=l)
        dmq, dkbd, dvbd = _mem_bwd(s["proj"], s["mq_cb"], s["kbd"], s["vbd"], dcat, f"mem_attn_bwd{l}")
        dmkv = _mem_collapse(dkbd, dvbd).astype(BF16)
        dw_mem_kv = _mm(mem_n, dmkv, "tn", f"mem_kv_dw{l}", dst=dw_mem_kv, layer=l)
        dmem_n = _mm(dmkv, w["w_mem_kv"], "nt", f"mem_kv_dx{l}", add=dmem_n, layer=l)
        if l < N_A:
            do, dz, dng = _gdn_post_bwd(s["o"], s["proj"], row2(w["gdn_norm"][l]), dcat, f"gdn_post_bwd{l}")
            dq, dk, dv, dg_, dbeta = _gdn_bwd(s["q"], s["k"], s["v"], s["gc"], s["beta"], s["states"], s["tinv"], s["gw"],
                                              s["vn"], do, f"gdn_scan_bwd{l}")
            res = _gdn_pre_bwd(s["proj"], w["gdn_conv"][l], sel_b, sel_a, s["a_row"], s["dt_row"], dq, dk, dv, dg_, dbeta,
                               f"gdn_pre_bwd{l}")
            dconv, dba = res[0], res[1]
            dgdn_conv[l] = jnp.stack([r.sum(axis=0) for r in res[2:2 + CONV_K]])
            dgdn_a[l] = res[2 + CONV_K].sum(axis=0)[GDN_HEADS:2 * GDN_HEADS]
            dgdn_dt[l] = res[3 + CONV_K].sum(axis=0)[GDN_HEADS:2 * GDN_HEADS]
            dgdn_norm[l] = dng.sum(axis=0)
            dqkv = _conv_bwd_input(dconv, w["gdn_conv"][l], f"gdn_conv_bwd{l}")
            dproj = jnp.concatenate([dqkv, dz, dmq, dba], axis=1)
            dh = _mm(dproj, w["gdn_w_in"], "nt", f"gdn_in_dx{l}", BF16, layer=l)
            dgdn_in = _mm(s["h"], dproj, "tn", f"gdn_in_dw{l}", dst=dgdn_in, layer=l)
        else:
            b = l - N_A
            dqr, dkt_l, dvt_l, dsk = _swa_bwd(s["qr"], kt, vt, w["swa_sinks"][b], dcat, f"swa_bwd{l}")
            dkt = dkt_l if dkt is None else dkt + dkt_l
            dvt = dvt_l if dvt is None else dvt + dvt_l
            dswa_sinks[b] = dsk[0, :SWA_HEADS]
            dq = _rope(dqr, SWA_HEADS * SWA_DH, 0, tabs, -1, f"rope_q_bwd{l}", BF16)
            dproj = jnp.concatenate([dq, dmq], axis=1)
            dh = _mm(dproj, w["swa_w_q"], "nt", f"swa_in_dx{l}", BF16, layer=b)
            dswa_q = _mm(s["h"], dproj, "tn", f"swa_in_dw{l}", dst=dswa_q, layer=b)
        dx, dxb, dg = _rms_bwd(s["x0"], row2(w["ln_mix"][l]), dh, dx, f"norm_mix_bwd{l}")
        dln_mix[l] = dg.sum(axis=0)
    _, _, dg = _rms_bwd(mem, row2(w["ln_mem"]), dmem_n, None, "mem_norm_bwd")
    gr["ln_mem"] = dg.sum(axis=0)
    gr.update(ln_mix=jnp.stack(dln_mix), ln_ffn=jnp.stack(dln_ffn), w_mem_kv=dw_mem_kv, w_out=dw_out,
              w_gate_up=_ff_deinterleave(dw_gu, "deinterleave_gate_up"), w_down=dw_dn, gdn_w_in=dgdn_in, gdn_conv=jnp.stack(dgdn_conv),
              gdn_A_log=jnp.stack(dgdn_a), gdn_dt_bias=jnp.stack(dgdn_dt), gdn_norm=jnp.stack(dgdn_norm),
              swa_w_q=dswa_q, swa_sinks=jnp.stack(dswa_sinks))
    return loss_part, dx, gr


def kernel(x, mem, positions, ln_mix, ln_ffn, ln_mem, w_mem_kv, w_out, w_gate_up, w_down, gdn_w_in, gdn_conv, gdn_A_log, gdn_dt_bias, gdn_norm, swa_w_q, swa_sinks, ln_kv, w_kv, ln_final, loss_target, m_ln_mix, m_ln_ffn, m_ln_mem, m_w_mem_kv, m_w_out, m_w_gate_up, m_w_down, m_gdn_w_in, m_gdn_conv, m_gdn_A_log, m_gdn_dt_bias, m_gdn_norm, m_swa_w_q, m_swa_sinks, m_ln_kv, m_w_kv, m_ln_final, v_ln_mix, v_ln_ffn, v_ln_mem, v_w_mem_kv, v_w_out, v_w_gate_up, v_w_down, v_gdn_w_in, v_gdn_conv, v_gdn_A_log, v_gdn_dt_bias, v_gdn_norm, v_swa_w_q, v_swa_sinks, v_ln_kv, v_w_kv, v_ln_final):
    given = dict(locals())
    wts = {n: given[n] for n in WEIGHTS}
    c = lax.axis_index("c")

    halves = [_my_half(wts[n].astype(BF16), n, c) for n in SHARDED]
    half_shapes = [h.shape for h in halves]
    conv_shape = wts["gdn_conv"].shape
    cpack = _pack([wts["gdn_conv"]], 16)
    conv_half = lax.dynamic_slice_in_dim(cpack, c * SUBLANES, SUBLANES, axis=0)
    got = _gather_two_level(halves + [conv_half], "gather_weights")
    full = {n: wts[n] for n in SMALL}
    for n, g in zip(SHARDED, got):
        full[n] = _assemble(g, n)
    conv_all = got[-1].reshape(4, 16, D_MODEL)
    full["gdn_conv"] = jnp.concatenate([_unpack(conv_all[s], [conv_shape])[0] for s in range(4)], axis=2)
    full["gdn_w_in"] = _gdn_in_pad(full["gdn_w_in"])

    loss_part, dx, gr = _local_step(x[0], mem[0], positions[0], loss_target[0], full)
    gr["gdn_w_in"] = _gdn_in_unpad(gr["gdn_w_in"])

    pieces = [_to_pieces(gr[n].astype(BF16), n, hs) for n, hs in zip(SHARDED, half_shapes)]
    small_shapes = [wts[n].shape for n in SMALL] + [conv_shape[:2] + (4 * conv_shape[2],), (SUBLANES, LANES)]
    rows_s = _rows_for(small_shapes, SUBLANES)
    spack = _pack([gr[n] for n in SMALL] + [gr["gdn_conv"], loss_part], rows_s)
    my_core = c.astype(jnp.int32).reshape(1)
    from_sibling = _send_to_sibling(pieces, "pair_grads")
    chip_parts = [_add_core_parts(p.reshape(2, 4, -1, p.shape[-1]), t.reshape(4, -1, t.shape[-1]), my_core, f"pair_sum_{n}")
                  for n, p, t in zip(SHARDED, pieces, from_sibling)]
    parts = _scatter_by_chip(chip_parts, spack, "scatter_grads")
    mine = [_sum_slots(p, f"sum_{n}").reshape(hs) for n, p, hs in zip(SHARDED, parts, half_shapes)]
    ssum = _unpack(_sum_slots(parts[-1], "sum_small"), small_shapes)
    theirs = _exchange(mine, [True] * len(mine), SIBLING, "swap_grad_halves", keep_own=False)
    g_all = dict(zip(SMALL, ssum[:len(SMALL)]))
    chip = 2 * lax.axis_index("x") + lax.axis_index("y")
    g_all["gdn_conv"] = lax.dynamic_slice_in_dim(ssum[len(SMALL)], chip * conv_shape[2], conv_shape[2], axis=2)
    loss = jnp.sum(ssum[-1])

    out = dict(grad=g_all, delta={}, new_m={}, new_v={})
    for n, own, other in zip(SHARDED, mine, theirs):
        as3d = lambda a: a.reshape((-1,) + a.shape[-2:])
        res = _adamw_halves(as3d(wts[n]), as3d(own), other.reshape((1, -1) + other.shape[-2:]), my_core,
                            as3d(given["m_" + n]), as3d(given["v_" + n]), f"adamw_{n}")
        for kind, r in zip(("grad", "delta", "new_m", "new_v"), res):
            out[kind][n] = r.reshape(wts[n].shape)
    small_names = SMALL + ("gdn_conv",)
    small_w_shapes = [wts[n].shape for n in small_names]
    rows_a = _rows_for(small_w_shapes, SUBLANES)
    res = _adamw(_pack([wts[n] for n in small_names], rows_a), _pack([g_all[n] for n in small_names], rows_a),
                 _pack([given["m_" + n] for n in small_names], rows_a),
                 _pack([given["v_" + n] for n in small_names], rows_a), "adamw_small")
    for kind, r in zip(("delta", "new_m", "new_v"), res):
        out[kind].update(zip(small_names, _unpack(r, small_w_shapes)))
    return (loss, dx[None], *[out["grad"][n] for n in WEIGHTS], *[out["delta"][n] for n in WEIGHTS],
            *[out["new_m"][n] for n in WEIGHTS], *[out["new_v"][n] for n in WEIGHTS])
```
